```python
import jax, jax.numpy as jnp
from jax import lax
import numpy as np

D_MODEL = 2048
BATCH = 8
SEQ = 8192
DEPTH = 1

D_CONV = D_MODEL // 2
CONV_GROUPS = 8
CONV_K = 31
D_SGU = D_MODEL // 2
SGU_GROUPS = 8
SGU_HEAD = D_SGU // SGU_GROUPS
CHUNK = 128
D_FF = 5632
FFN_K = 3
N_MOD = 6
D_IN = 2 * D_CONV + 2 * D_SGU + 2 * D_MODEL
EPS = 1e-6

kernel_name = "hybrid_conformer_gmlp_convffn_adaln"


def rms_norm(x, g):
    xf = x.astype(jnp.float32)
    y = xf * lax.rsqrt(jnp.mean(xf * xf, axis=-1, keepdims=True) + EPS)
    return (y * g.astype(jnp.float32)).astype(x.dtype)


def layer_norm(x, g, b):
    xf = x.astype(jnp.float32)
    mu = jnp.mean(xf, axis=-1, keepdims=True)
    var = jnp.mean(jnp.square(xf - mu), axis=-1, keepdims=True)
    y = (xf - mu) * lax.rsqrt(var + EPS)
    return (y * g.astype(jnp.float32) + b.astype(jnp.float32)).astype(x.dtype)


def causal_dwconv(x, w, b):
    k, ch = w.shape
    y = lax.conv_general_dilated(
        x, w.astype(x.dtype)[:, None, :], window_strides=(1,), padding=[(k - 1, 0)],
        dimension_numbers=("NWC", "WIO", "NWC"), feature_group_count=ch)
    return y + b.astype(x.dtype)


def _fwd_setup_inputs(seed: int = 0) -> dict:
    key = jax.random.key(seed)
    ks = jax.random.split(key, 24)
    L, D = DEPTH, D_MODEL
    n = lambda k, shape, s: jax.random.normal(k, shape, jnp.float32) * s
    return {
        "x": n(ks[0], (BATCH, SEQ, D), 1.0),
        "c": n(ks[1], (BATCH, D), 1.0),
        "w_ada": n(ks[2], (L, D, N_MOD * D), 0.5 * D ** -0.5),
        "b_ada": n(ks[3], (L, N_MOD * D), 0.01),
        "norm1_g": 1.0 + n(ks[4], (L, D), 0.02),
        "w_in": n(ks[5], (L, D, D_IN), D ** -0.5),
        "b_in": n(ks[6], (L, D_IN), 0.01),
        "conv_dw_w": n(ks[7], (L, CONV_K, D_CONV), CONV_K ** -0.5),
        "conv_dw_b": n(ks[8], (L, D_CONV), 0.01),
        "conv_ln_g": 1.0 + n(ks[9], (L, D_CONV), 0.02),
        "conv_ln_b": n(ks[10], (L, D_CONV), 0.01),
        "w_conv_out": n(ks[11], (L, D_CONV, D), D_CONV ** -0.5),
        "sgu_ln_g": 1.0 + n(ks[12], (L, D_SGU), 0.02),
        "sgu_ln_b": n(ks[13], (L, D_SGU), 0.01),
        "w_spatial": n(ks[14], (L, SGU_GROUPS, CHUNK, CHUNK), CHUNK ** -0.5),
        "b_spatial": 1.0 + n(ks[15], (L, SGU_GROUPS, CHUNK), 0.01),
        "w_sgu_out": n(ks[16], (L, D_SGU, D), D_SGU ** -0.5),
        "w_out": n(ks[17], (L, D, D), D ** -0.5),
        "norm2_g": 1.0 + n(ks[18], (L, D), 0.02),
        "w_up": n(ks[19], (L, D, 2 * D_FF), D ** -0.5),
        "ffn_dw_w": n(ks[20], (L, FFN_K, 2 * D_FF), FFN_K ** -0.5),
        "ffn_dw_b": n(ks[21], (L, 2 * D_FF), 0.01),
        "w_down": n(ks[22], (L, D_FF, D), D_FF ** -0.5),
        "final_g": 1.0 + n(ks[23], (D,), 0.02),
    }


def _fwd_reference(x, c, w_ada, b_ada, norm1_g, w_in, b_in, conv_dw_w, conv_dw_b, conv_ln_g,
              conv_ln_b, w_conv_out, sgu_ln_g, sgu_ln_b, w_spatial, b_spatial, w_sgu_out,
              w_out, norm2_g, w_up, ffn_dw_w, ffn_dw_b, w_down, final_g):
    B, S, D = x.shape
    n_chunks = S // CHUNK
    causal_mask = jnp.tril(jnp.ones((CHUNK, CHUNK), dtype=x.dtype))
    c_act = jax.nn.silu(c)
    for l in range(DEPTH):
        mod = (c_act @ w_ada[l] + b_ada[l])[:, None, :]
        shift1, scale1, gate1, shift2, scale2, gate2 = jnp.split(mod, N_MOD, axis=-1)

        h = rms_norm(x, norm1_g[l]) * (1.0 + scale1) + shift1
        proj = h @ w_in[l] + b_in[l]
        a_in, s_in, gates = jnp.split(proj, [2 * D_CONV, 2 * D_CONV + 2 * D_SGU], axis=-1)

        a_val, a_gate = jnp.split(a_in, 2, axis=-1)
        a = a_val * jax.nn.sigmoid(a_gate)
        a = causal_dwconv(a, conv_dw_w[l], conv_dw_b[l])
        a = jax.nn.silu(layer_norm(a, conv_ln_g[l], conv_ln_b[l]))
        y_a = a @ w_conv_out[l]

        z = jax.nn.gelu(s_in, approximate=False)
        u, v = jnp.split(z, 2, axis=-1)
        v = layer_norm(v, sgu_ln_g[l], sgu_ln_b[l])
        v = v.reshape(B, n_chunks, CHUNK, SGU_GROUPS, SGU_HEAD)
        ws = w_spatial[l] * causal_mask
        v = jnp.einsum("gts,bnsgc->bntgc", ws, v) + b_spatial[l].T[:, :, None]
        y_b = (u * v.reshape(B, S, D_SGU)) @ w_sgu_out[l]

        g_a, g_b = jnp.split(gates, 2, axis=-1)
        merged = jax.nn.sigmoid(g_a) * y_a + jax.nn.sigmoid(g_b) * y_b
        x = x + gate1 * (merged @ w_out[l])

        h = rms_norm(x, norm2_g[l]) * (1.0 + scale2) + shift2
        up = causal_dwconv(h @ w_up[l], ffn_dw_w[l], ffn_dw_b[l])
        val, gt = jnp.split(up, 2, axis=-1)
        x = x + gate2 * ((jax.nn.silu(gt) * val) @ w_down[l])

    return rms_norm(x, final_g)


import jax as _jax
import jax.numpy as _jnp

TWIN_FORMAT = 'train_step'
FWD_PARAMS = ['x', 'c', 'w_ada', 'b_ada', 'norm1_g', 'w_in', 'b_in', 'conv_dw_w', 'conv_dw_b', 'conv_ln_g', 'conv_ln_b', 'w_conv_out', 'sgu_ln_g', 'sgu_ln_b', 'w_spatial', 'b_spatial', 'w_sgu_out', 'w_out', 'norm2_g', 'w_up', 'ffn_dw_w', 'ffn_dw_b', 'w_down', 'final_g']
TWIN_WEIGHTS = ['w_ada', 'b_ada', 'norm1_g', 'w_in', 'b_in', 'conv_dw_w', 'conv_dw_b', 'conv_ln_g', 'conv_ln_b', 'w_conv_out', 'sgu_ln_g', 'sgu_ln_b', 'w_spatial', 'b_spatial', 'w_sgu_out', 'w_out', 'norm2_g', 'w_up', 'ffn_dw_w', 'ffn_dw_b', 'w_down', 'final_g']
TWIN_DIFF_INPUT = 'x'
TWIN_INPUTS = ['x', 'c', 'w_ada', 'b_ada', 'norm1_g', 'w_in', 'b_in', 'conv_dw_w', 'conv_dw_b', 'conv_ln_g', 'conv_ln_b', 'w_conv_out', 'sgu_ln_g', 'sgu_ln_b', 'w_spatial', 'b_spatial', 'w_sgu_out', 'w_out', 'norm2_g', 'w_up', 'ffn_dw_w', 'ffn_dw_b', 'w_down', 'final_g', 'loss_target', 'm_w_ada', 'm_b_ada', 'm_norm1_g', 'm_w_in', 'm_b_in', 'm_conv_dw_w', 'm_conv_dw_b', 'm_conv_ln_g', 'm_conv_ln_b', 'm_w_conv_out', 'm_sgu_ln_g', 'm_sgu_ln_b', 'm_w_spatial', 'm_b_spatial', 'm_w_sgu_out', 'm_w_out', 'm_norm2_g', 'm_w_up', 'm_ffn_dw_w', 'm_ffn_dw_b', 'm_w_down', 'm_final_g', 'v_w_ada', 'v_b_ada', 'v_norm1_g', 'v_w_in', 'v_b_in', 'v_conv_dw_w', 'v_conv_dw_b', 'v_conv_ln_g', 'v_conv_ln_b', 'v_w_conv_out', 'v_sgu_ln_g', 'v_sgu_ln_b', 'v_w_spatial', 'v_b_spatial', 'v_w_sgu_out', 'v_w_out', 'v_norm2_g', 'v_w_up', 'v_ffn_dw_w', 'v_ffn_dw_b', 'v_w_down', 'v_final_g']
TWIN_OUTPUTS = ['loss', 'grad_x', 'grad_w_ada', 'grad_b_ada', 'grad_norm1_g', 'grad_w_in', 'grad_b_in', 'grad_conv_dw_w', 'grad_conv_dw_b', 'grad_conv_ln_g', 'grad_conv_ln_b', 'grad_w_conv_out', 'grad_sgu_ln_g', 'grad_sgu_ln_b', 'grad_w_spatial', 'grad_b_spatial', 'grad_w_sgu_out', 'grad_w_out', 'grad_norm2_g', 'grad_w_up', 'grad_ffn_dw_w', 'grad_ffn_dw_b', 'grad_w_down', 'grad_final_g', 'delta_w_ada', 'delta_b_ada', 'delta_norm1_g', 'delta_w_in', 'delta_b_in', 'delta_conv_dw_w', 'delta_conv_dw_b', 'delta_conv_ln_g', 'delta_conv_ln_b', 'delta_w_conv_out', 'delta_sgu_ln_g', 'delta_sgu_ln_b', 'delta_w_spatial', 'delta_b_spatial', 'delta_w_sgu_out', 'delta_w_out', 'delta_norm2_g', 'delta_w_up', 'delta_ffn_dw_w', 'delta_ffn_dw_b', 'delta_w_down', 'delta_final_g', 'new_m_w_ada', 'new_m_b_ada', 'new_m_norm1_g', 'new_m_w_in', 'new_m_b_in', 'new_m_conv_dw_w', 'new_m_conv_dw_b', 'new_m_conv_ln_g', 'new_m_conv_ln_b', 'new_m_w_conv_out', 'new_m_sgu_ln_g', 'new_m_sgu_ln_b', 'new_m_w_spatial', 'new_m_b_spatial', 'new_m_w_sgu_out', 'new_m_w_out', 'new_m_norm2_g', 'new_m_w_up', 'new_m_ffn_dw_w', 'new_m_ffn_dw_b', 'new_m_w_down', 'new_m_final_g', 'new_v_w_ada', 'new_v_b_ada', 'new_v_norm1_g', 'new_v_w_in', 'new_v_b_in', 'new_v_conv_dw_w', 'new_v_conv_dw_b', 'new_v_conv_ln_g', 'new_v_conv_ln_b', 'new_v_w_conv_out', 'new_v_sgu_ln_g', 'new_v_sgu_ln_b', 'new_v_w_spatial', 'new_v_b_spatial', 'new_v_w_sgu_out', 'new_v_w_out', 'new_v_norm2_g', 'new_v_w_up', 'new_v_ffn_dw_w', 'new_v_ffn_dw_b', 'new_v_w_down', 'new_v_final_g']
TWIN_LEAF_KINDS = {'loss': 'loss', 'grad_x': 'grad_x', 'grad_w_ada': 'grad_w', 'grad_b_ada': 'grad_w', 'grad_norm1_g': 'grad_w', 'grad_w_in': 'grad_w', 'grad_b_in': 'grad_w', 'grad_conv_dw_w': 'grad_w', 'grad_conv_dw_b': 'grad_w', 'grad_conv_ln_g': 'grad_w', 'grad_conv_ln_b': 'grad_w', 'grad_w_conv_out': 'grad_w', 'grad_sgu_ln_g': 'grad_w', 'grad_sgu_ln_b': 'grad_w', 'grad_w_spatial': 'grad_w', 'grad_b_spatial': 'grad_w', 'grad_w_sgu_out': 'grad_w', 'grad_w_out': 'grad_w', 'grad_norm2_g': 'grad_w', 'grad_w_up': 'grad_w', 'grad_ffn_dw_w': 'grad_w', 'grad_ffn_dw_b': 'grad_w', 'grad_w_down': 'grad_w', 'grad_final_g': 'grad_w', 'delta_w_ada': 'delta_w', 'delta_b_ada': 'delta_w', 'delta_norm1_g': 'delta_w', 'delta_w_in': 'delta_w', 'delta_b_in': 'delta_w', 'delta_conv_dw_w': 'delta_w', 'delta_conv_dw_b': 'delta_w', 'delta_conv_ln_g': 'delta_w', 'delta_conv_ln_b': 'delta_w', 'delta_w_conv_out': 'delta_w', 'delta_sgu_ln_g': 'delta_w', 'delta_sgu_ln_b': 'delta_w', 'delta_w_spatial': 'delta_w', 'delta_b_spatial': 'delta_w', 'delta_w_sgu_out': 'delta_w', 'delta_w_out': 'delta_w', 'delta_norm2_g': 'delta_w', 'delta_w_up': 'delta_w', 'delta_ffn_dw_w': 'delta_w', 'delta_ffn_dw_b': 'delta_w', 'delta_w_down': 'delta_w', 'delta_final_g': 'delta_w', 'new_m_w_ada': 'new_m', 'new_m_b_ada': 'new_m', 'new_m_norm1_g': 'new_m', 'new_m_w_in': 'new_m', 'new_m_b_in': 'new_m', 'new_m_conv_dw_w': 'new_m', 'new_m_conv_dw_b': 'new_m', 'new_m_conv_ln_g': 'new_m', 'new_m_conv_ln_b': 'new_m', 'new_m_w_conv_out': 'new_m', 'new_m_sgu_ln_g': 'new_m', 'new_m_sgu_ln_b': 'new_m', 'new_m_w_spatial': 'new_m', 'new_m_b_spatial': 'new_m', 'new_m_w_sgu_out': 'new_m', 'new_m_w_out': 'new_m', 'new_m_norm2_g': 'new_m', 'new_m_w_up': 'new_m', 'new_m_ffn_dw_w': 'new_m', 'new_m_ffn_dw_b': 'new_m', 'new_m_w_down': 'new_m', 'new_m_final_g': 'new_m', 'new_v_w_ada': 'new_v', 'new_v_b_ada': 'new_v', 'new_v_norm1_g': 'new_v', 'new_v_w_in': 'new_v', 'new_v_b_in': 'new_v', 'new_v_conv_dw_w': 'new_v', 'new_v_conv_dw_b': 'new_v', 'new_v_conv_ln_g': 'new_v', 'new_v_conv_ln_b': 'new_v', 'new_v_w_conv_out': 'new_v', 'new_v_sgu_ln_g': 'new_v', 'new_v_sgu_ln_b': 'new_v', 'new_v_w_spatial': 'new_v', 'new_v_b_spatial': 'new_v', 'new_v_w_sgu_out': 'new_v', 'new_v_w_out': 'new_v', 'new_v_norm2_g': 'new_v', 'new_v_w_up': 'new_v', 'new_v_ffn_dw_w': 'new_v', 'new_v_ffn_dw_b': 'new_v', 'new_v_w_down': 'new_v', 'new_v_final_g': 'new_v'}


def _forward(args):
    return _fwd_reference(*[args[k] for k in FWD_PARAMS])


def _output_shape():
    def fwd():
        inp = _fwd_setup_inputs(0)
        return _fwd_reference(*[inp[k] for k in FWD_PARAMS])
    out = _jax.eval_shape(fwd)
    return out.shape, out.dtype

N_MICROBATCH = 1
ADAM_LR = 0.001
ADAM_B1 = 0.9
ADAM_B2 = 0.999
ADAM_EPS = 1e-08
ADAM_WD = 0.01
ADAM_STEP = 10
PER_EXAMPLE_BATCH_AXIS = {'x': 0, 'c': 0, 'loss_target': 0}
SHARED_INPUTS = []
_WEIGHT_DTYPES = {'w_ada': _jnp.float32, 'b_ada': _jnp.float32, 'norm1_g': _jnp.float32, 'w_in': _jnp.float32, 'b_in': _jnp.float32, 'conv_dw_w': _jnp.float32, 'conv_dw_b': _jnp.float32, 'conv_ln_g': _jnp.float32, 'conv_ln_b': _jnp.float32, 'w_conv_out': _jnp.float32, 'sgu_ln_g': _jnp.float32, 'sgu_ln_b': _jnp.float32, 'w_spatial': _jnp.float32, 'b_spatial': _jnp.float32, 'w_sgu_out': _jnp.float32, 'w_out': _jnp.float32, 'norm2_g': _jnp.float32, 'w_up': _jnp.float32, 'ffn_dw_w': _jnp.float32, 'ffn_dw_b': _jnp.float32, 'w_down': _jnp.float32, 'final_g': _jnp.float32}
MOMENT_SCALE = {'w_ada': 3.374559e-02, 'b_ada': 5.930594e-02, 'norm1_g': 2.751780e-02, 'w_in': 1.389258e-02, 'b_in': 1.301885e-02, 'conv_dw_w': 1.834973e-02, 'conv_dw_b': 3.369765e-02, 'conv_ln_g': 2.262035e-02, 'conv_ln_b': 1.970884e-02, 'w_conv_out': 1.251969e-02, 'sgu_ln_g': 1.462246e-02, 'sgu_ln_b': 1.444212e-02, 'w_spatial': 1.463530e-02, 'b_spatial': 2.124703e-02, 'w_sgu_out': 1.836665e-02, 'w_out': 2.236370e-02, 'norm2_g': 3.695892e-02, 'w_up': 1.655108e-02, 'ffn_dw_w': 1.676149e-02, 'ffn_dw_b': 1.491114e-02, 'w_down': 2.686771e-02, 'final_g': 3.195058e+01}


def _to_microbatches(a, axis):
    t = _jnp.moveaxis(a, axis, 0)
    t = t.reshape((N_MICROBATCH, t.shape[0] // N_MICROBATCH) + t.shape[1:])
    return _jnp.moveaxis(t, 1, axis + 1)


def setup_inputs(seed: int = 0) -> dict:
    inp = _fwd_setup_inputs(seed)
    key = _jax.random.fold_in(_jax.random.key(seed), 7919)
    shape, _ = _output_shape()
    out = dict(inp)
    out["loss_target"] = _jax.random.normal(_jax.random.fold_in(key, 0), shape, _jnp.float32)
    for i, name in enumerate(TWIN_WEIGHTS):
        w = inp[name].astype(_jnp.float32)
        if MOMENT_SCALE is None:
            s = _jnp.sqrt(_jnp.mean(_jnp.square(w)) + 1e-30)
        else:
            s = MOMENT_SCALE[name]
        km, kv = _jax.random.split(_jax.random.fold_in(key, i + 1))
        out[name] = w
        out["m_" + name] = s * _jax.random.normal(km, w.shape, _jnp.float32)
        out["v_" + name] = (s * s) * _jax.random.uniform(kv, w.shape, _jnp.float32, 0.5, 1.5)
    if N_MICROBATCH > 1:
        for name, axis in PER_EXAMPLE_BATCH_AXIS.items():
            out[name] = _to_microbatches(out[name], axis)
    return {'x': out['x'], 'c': out['c'], 'w_ada': out['w_ada'], 'b_ada': out['b_ada'], 'norm1_g': out['norm1_g'], 'w_in': out['w_in'], 'b_in': out['b_in'], 'conv_dw_w': out['conv_dw_w'], 'conv_dw_b': out['conv_dw_b'], 'conv_ln_g': out['conv_ln_g'], 'conv_ln_b': out['conv_ln_b'], 'w_conv_out': out['w_conv_out'], 'sgu_ln_g': out['sgu_ln_g'], 'sgu_ln_b': out['sgu_ln_b'], 'w_spatial': out['w_spatial'], 'b_spatial': out['b_spatial'], 'w_sgu_out': out['w_sgu_out'], 'w_out': out['w_out'], 'norm2_g': out['norm2_g'], 'w_up': out['w_up'], 'ffn_dw_w': out['ffn_dw_w'], 'ffn_dw_b': out['ffn_dw_b'], 'w_down': out['w_down'], 'final_g': out['final_g'], 'loss_target': out['loss_target'], 'm_w_ada': out['m_w_ada'], 'm_b_ada': out['m_b_ada'], 'm_norm1_g': out['m_norm1_g'], 'm_w_in': out['m_w_in'], 'm_b_in': out['m_b_in'], 'm_conv_dw_w': out['m_conv_dw_w'], 'm_conv_dw_b': out['m_conv_dw_b'], 'm_conv_ln_g': out['m_conv_ln_g'], 'm_conv_ln_b': out['m_conv_ln_b'], 'm_w_conv_out': out['m_w_conv_out'], 'm_sgu_ln_g': out['m_sgu_ln_g'], 'm_sgu_ln_b': out['m_sgu_ln_b'], 'm_w_spatial': out['m_w_spatial'], 'm_b_spatial': out['m_b_spatial'], 'm_w_sgu_out': out['m_w_sgu_out'], 'm_w_out': out['m_w_out'], 'm_norm2_g': out['m_norm2_g'], 'm_w_up': out['m_w_up'], 'm_ffn_dw_w': out['m_ffn_dw_w'], 'm_ffn_dw_b': out['m_ffn_dw_b'], 'm_w_down': out['m_w_down'], 'm_final_g': out['m_final_g'], 'v_w_ada': out['v_w_ada'], 'v_b_ada': out['v_b_ada'], 'v_norm1_g': out['v_norm1_g'], 'v_w_in': out['v_w_in'], 'v_b_in': out['v_b_in'], 'v_conv_dw_w': out['v_conv_dw_w'], 'v_conv_dw_b': out['v_conv_dw_b'], 'v_conv_ln_g': out['v_conv_ln_g'], 'v_conv_ln_b': out['v_conv_ln_b'], 'v_w_conv_out': out['v_w_conv_out'], 'v_sgu_ln_g': out['v_sgu_ln_g'], 'v_sgu_ln_b': out['v_sgu_ln_b'], 'v_w_spatial': out['v_w_spatial'], 'v_b_spatial': out['v_b_spatial'], 'v_w_sgu_out': out['v_w_sgu_out'], 'v_w_out': out['v_w_out'], 'v_norm2_g': out['v_norm2_g'], 'v_w_up': out['v_w_up'], 'v_ffn_dw_w': out['v_ffn_dw_w'], 'v_ffn_dw_b': out['v_ffn_dw_b'], 'v_w_down': out['v_w_down'], 'v_final_g': out['v_final_g']}


def _loss(weights, diff, rest, loss_target):
    with _jax.named_scope("forward"):
        args = {**rest, TWIN_DIFF_INPUT: diff, **{k: w.astype(_WEIGHT_DTYPES[k]) for k, w in weights.items()}}
        y = _forward(args)
    with _jax.named_scope("loss_head"):
        err = _jnp.square(y.astype(_jnp.float32) - loss_target)
        return 0.5 * _jnp.sum(_jnp.mean(err, axis=-1)) if err.ndim else 0.5 * err


def _adamw(w, g, m, v):
    m = ADAM_B1 * m + (1.0 - ADAM_B1) * g
    v = ADAM_B2 * v + (1.0 - ADAM_B2) * _jnp.square(g)
    m_hat = m / (1.0 - ADAM_B1 ** ADAM_STEP)
    v_hat = v / (1.0 - ADAM_B2 ** ADAM_STEP)
    delta = -ADAM_LR * (m_hat / (_jnp.sqrt(v_hat) + ADAM_EPS) + ADAM_WD * w)
    return delta, m, v


def reference(x, c, w_ada, b_ada, norm1_g, w_in, b_in, conv_dw_w, conv_dw_b, conv_ln_g, conv_ln_b, w_conv_out, sgu_ln_g, sgu_ln_b, w_spatial, b_spatial, w_sgu_out, w_out, norm2_g, w_up, ffn_dw_w, ffn_dw_b, w_down, final_g, loss_target, m_w_ada, m_b_ada, m_norm1_g, m_w_in, m_b_in, m_conv_dw_w, m_conv_dw_b, m_conv_ln_g, m_conv_ln_b, m_w_conv_out, m_sgu_ln_g, m_sgu_ln_b, m_w_spatial, m_b_spatial, m_w_sgu_out, m_w_out, m_norm2_g, m_w_up, m_ffn_dw_w, m_ffn_dw_b, m_w_down, m_final_g, v_w_ada, v_b_ada, v_norm1_g, v_w_in, v_b_in, v_conv_dw_w, v_conv_dw_b, v_conv_ln_g, v_conv_ln_b, v_w_conv_out, v_sgu_ln_g, v_sgu_ln_b, v_w_spatial, v_b_spatial, v_w_sgu_out, v_w_out, v_norm2_g, v_w_up, v_ffn_dw_w, v_ffn_dw_b, v_w_down, v_final_g):
    given = dict(x=x, c=c, w_ada=w_ada, b_ada=b_ada, norm1_g=norm1_g, w_in=w_in, b_in=b_in, conv_dw_w=conv_dw_w, conv_dw_b=conv_dw_b, conv_ln_g=conv_ln_g, conv_ln_b=conv_ln_b, w_conv_out=w_conv_out, sgu_ln_g=sgu_ln_g, sgu_ln_b=sgu_ln_b, w_spatial=w_spatial, b_spatial=b_spatial, w_sgu_out=w_sgu_out, w_out=w_out, norm2_g=norm2_g, w_up=w_up, ffn_dw_w=ffn_dw_w, ffn_dw_b=ffn_dw_b, w_down=w_down, final_g=final_g, loss_target=loss_target, m_w_ada=m_w_ada, m_b_ada=m_b_ada, m_norm1_g=m_norm1_g, m_w_in=m_w_in, m_b_in=m_b_in, m_conv_dw_w=m_conv_dw_w, m_conv_dw_b=m_conv_dw_b, m_conv_ln_g=m_conv_ln_g, m_conv_ln_b=m_conv_ln_b, m_w_conv_out=m_w_conv_out, m_sgu_ln_g=m_sgu_ln_g, m_sgu_ln_b=m_sgu_ln_b, m_w_spatial=m_w_spatial, m_b_spatial=m_b_spatial, m_w_sgu_out=m_w_sgu_out, m_w_out=m_w_out, m_norm2_g=m_norm2_g, m_w_up=m_w_up, m_ffn_dw_w=m_ffn_dw_w, m_ffn_dw_b=m_ffn_dw_b, m_w_down=m_w_down, m_final_g=m_final_g, v_w_ada=v_w_ada, v_b_ada=v_b_ada, v_norm1_g=v_norm1_g, v_w_in=v_w_in, v_b_in=v_b_in, v_conv_dw_w=v_conv_dw_w, v_conv_dw_b=v_conv_dw_b, v_conv_ln_g=v_conv_ln_g, v_conv_ln_b=v_conv_ln_b, v_w_conv_out=v_w_conv_out, v_sgu_ln_g=v_sgu_ln_g, v_sgu_ln_b=v_sgu_ln_b, v_w_spatial=v_w_spatial, v_b_spatial=v_b_spatial, v_w_sgu_out=v_w_sgu_out, v_w_out=v_w_out, v_norm2_g=v_norm2_g, v_w_up=v_w_up, v_ffn_dw_w=v_ffn_dw_w, v_ffn_dw_b=v_ffn_dw_b, v_w_down=v_w_down, v_final_g=v_final_g)
    weights = {n: given[n] for n in TWIN_WEIGHTS}
    shared = {n: given[n] for n in SHARED_INPUTS}
    per_example = {n: given[n] for n in ['x', 'c']}
    grad_fn = _jax.value_and_grad(_loss, argnums=(0, 1))

    def one_microbatch(ex, loss_target):
        ex = dict(ex)
        diff = ex.pop(TWIN_DIFF_INPUT)
        return grad_fn(weights, diff, {**shared, **ex}, loss_target)

    if N_MICROBATCH == 1:
        loss, (grad_w, grad_x) = one_microbatch(per_example, given["loss_target"])
    else:
        def body(carry, xs):
            loss_sum, grad_sum = carry
            l_k, (gw_k, gx_k) = one_microbatch(xs[0], xs[1])
            with _jax.named_scope("update"):
                return (loss_sum + l_k, _jax.tree.map(_jnp.add, grad_sum, gw_k)), gx_k

        init = (_jnp.zeros((), _jnp.float32), _jax.tree.map(_jnp.zeros_like, weights))
        (loss, grad_w), grad_x = _jax.lax.scan(body, init, (per_example, given["loss_target"]))
    with _jax.named_scope("update"):
        delta_w, new_m, new_v = {}, {}, {}
        for n in TWIN_WEIGHTS:
            delta_w[n], new_m[n], new_v[n] = _adamw(weights[n], grad_w[n], given["m_" + n], given["v_" + n])
    return (loss, grad_x, *[grad_w[n] for n in TWIN_WEIGHTS], *[delta_w[n] for n in TWIN_WEIGHTS],
            *[new_m[n] for n in TWIN_WEIGHTS], *[new_v[n] for n in TWIN_WEIGHTS])
```

```python
import functools

import jax
import jax.numpy as jnp
from jax import lax
from jax.experimental import pallas as pl
from jax.experimental.pallas import tpu as pltpu

F32, BF16 = jnp.float32, jnp.bfloat16
NDEV = 8
MESH_AXES = ("x", "y", "c")
MESH = pl.DeviceIdType.MESH
EPS = 1e-6
CHUNK = 128
CONV_HALO = 32
FFN_HALO = 8
LANE, SUBLANE = 128, 8
ROW_TILE = 256
MM_TILE = 1024
MM_WIDE = 1408
VMEM_LIMIT = 56 * 1024 * 1024
ADAM_LR, ADAM_B1, ADAM_B2, ADAM_EPS, ADAM_WD, ADAM_STEP = 0.001, 0.9, 0.999, 1e-08, 0.01, 10
SQRT_HALF = 0.7071067811865476
INV_SQRT_2PI = 0.3989422804014327


def _div(n, pref, mult=LANE):
    if n <= pref:
        return n
    for d in range(pref - pref % mult, 0, -mult):
        if n % d == 0:
            return d
    return n


def _cp(sem):
    return pltpu.CompilerParams(dimension_semantics=sem, vmem_limit_bytes=VMEM_LIMIT)


def _sig(v):
    return jax.nn.sigmoid(v)


def _gelu(v):
    return 0.5 * v * (1.0 + lax.erf(v * SQRT_HALF))


def _gelu_grad(v):
    return 0.5 * (1.0 + lax.erf(v * SQRT_HALF)) + v * (INV_SQRT_2PI * jnp.exp(-0.5 * v * v))


def _colsum(v):
    return jnp.sum(v, axis=0, keepdims=True)


def _rowmean(v):
    return jnp.mean(v, axis=-1, keepdims=True)


def _matmul(name, a, b, *, grid, a_spec, b_spec, o_spec, out_shape, dims, acc_shape, bias=None, bias_spec=None):
    nk = grid[2]

    def body(*refs):
        if bias is None:
            a_ref, b_ref, o_ref, *scr = refs
            bias_ref = None
        else:
            a_ref, b_ref, bias_ref, o_ref, *scr = refs
        part = lax.dot_general(a_ref[...], b_ref[...], (dims, ((), ())), preferred_element_type=F32)

        def finish(total):
            if bias_ref is not None:
                total = total + bias_ref[...]
            o_ref[...] = total.astype(o_ref.dtype)

        if nk == 1:
            finish(part)
        else:
            acc = scr[0]
            k = pl.program_id(2)

            @pl.when(k == 0)
            def _():
                acc[...] = part

            @pl.when(k > 0)
            def _():
                acc[...] += part

            @pl.when(k == nk - 1)
            def _():
                finish(acc[...])

    in_specs = [a_spec, b_spec] + ([bias_spec] if bias is not None else [])
    args = (a, b) + ((bias,) if bias is not None else ())
    return pl.pallas_call(
        body, name=name, grid=grid, in_specs=in_specs, out_specs=o_spec, out_shape=out_shape,
        scratch_shapes=[pltpu.VMEM(acc_shape, F32)] if nk > 1 else [],
        compiler_params=_cp(("parallel", "parallel", "arbitrary")))(*args)


def mm_nn(name, a, wb, out_dtype, *, bias=None, out_halves=False, tn_pref=MM_TILE):
    T, K = a.shape
    NB, _, Ns = wb.shape
    N = NB * Ns
    tm, tn, tk = _div(T, MM_TILE), _div(Ns, tn_pref), _div(K, 2048)
    npb, nj, nk = Ns // tn, N // tn, K // tk
    if out_halves:
        o_spec = pl.BlockSpec((None, tm, tn), lambda i, j, k: (j // (nj // 2), i, j % (nj // 2)))
        out_shape = jax.ShapeDtypeStruct((2, T, N // 2), out_dtype)
    else:
        o_spec = pl.BlockSpec((tm, tn), lambda i, j, k: (i, j))
        out_shape = jax.ShapeDtypeStruct((T, N), out_dtype)
    return _matmul(
        name, a, wb, grid=(T // tm, nj, nk),
        a_spec=pl.BlockSpec((tm, tk), lambda i, j, k: (i, k)),
        b_spec=pl.BlockSpec((None, tk, tn), lambda i, j, k: (j // npb, k, j % npb)),
        o_spec=o_spec, out_shape=out_shape, dims=((1,), (0,)), acc_shape=(tm, tn),
        bias=bias, bias_spec=pl.BlockSpec((1, tn), lambda i, j, k: (0, j)))


def mm_nt(name, a, wb, out_dtype, *, a_halves=False, tko_pref=MM_TILE, tc_pref=2048):
    NB, K, Ns = wb.shape
    T = a.shape[-2]
    tm, tko, tc = _div(T, MM_TILE), _div(K, tko_pref), _div(Ns, tc_pref)
    cpb = Ns // tc
    nkk = NB * cpb
    if a_halves:
        a_spec = pl.BlockSpec((None, tm, tc), lambda i, j, k: (k // (nkk // 2), i, k % (nkk // 2)))
    else:
        a_spec = pl.BlockSpec((tm, tc), lambda i, j, k: (i, k))
    return _matmul(
        name, a, wb, grid=(T // tm, K // tko, nkk), a_spec=a_spec,
        b_spec=pl.BlockSpec((None, tko, tc), lambda i, j, k: (k // cpb, j, k % cpb)),
        o_spec=pl.BlockSpec((tm, tko), lambda i, j, k: (i, j)),
        out_shape=jax.ShapeDtypeStruct((T, K), out_dtype), dims=((1,), (1,)), acc_shape=(tm, tko))


def mm_tn(name, a, g, nb, *, g_halves=False, tko_pref=MM_TILE, tn_pref=MM_TILE):
    T, K = a.shape
    N = g.shape[-1] * (2 if g_halves else 1)
    Ns = N // nb
    tt, tko, tn = _div(T, MM_TILE), _div(K, tko_pref), _div(Ns, tn_pref)
    npb, nj = Ns // tn, N // tn
    if g_halves:
        g_spec = pl.BlockSpec((None, tt, tn), lambda i, j, t: (j // (nj // 2), t, j % (nj // 2)))
    else:
        g_spec = pl.BlockSpec((tt, tn), lambda i, j, t: (t, j))
    return _matmul(
        name, a, g, grid=(K // tko, nj, T // tt),
        a_spec=pl.BlockSpec((tt, tko), lambda i, j, t: (t, i)), b_spec=g_spec,
        o_spec=pl.BlockSpec((None, tko, tn), lambda i, j, t: (j // npb, i, j % npb)),
        out_shape=jax.ShapeDtypeStruct((nb, K, Ns), F32), dims=((0,), (0,)), acc_shape=(tko, tn))


def _row_call(name, body, grid, in_specs, out_specs, out_shape, scratch=(), sem=None, aliases=None):
    return pl.pallas_call(
        body, name=name, grid=grid, in_specs=in_specs, out_specs=out_specs, out_shape=out_shape,
        scratch_shapes=list(scratch), input_output_aliases=aliases or {},
        compiler_params=_cp(sem or ("arbitrary",) * len(grid)))


def _full(shape):
    nd = len(shape)
    return pl.BlockSpec(shape, lambda *idx: (0,) * nd)


def pre_norm(name, x, mod, g, which, o1=None):
    T, D = x.shape
    tr = _div(T, ROW_TILE, SUBLANE)

    def body(*refs):
        if o1 is None:
            x_ref, mod_ref, g_ref, h_ref = refs
            xv = x_ref[...]
        else:
            x_ref, o1_ref, mod_ref, g_ref, h_ref = refs
            xv = x_ref[...] + mod_ref[:, 2 * D:3 * D] * o1_ref[...]
        shift = mod_ref[:, (3 * which) * D:(3 * which + 1) * D]
        scale = mod_ref[:, (3 * which + 1) * D:(3 * which + 2) * D]
        r = lax.rsqrt(_rowmean(xv * xv) + EPS)
        h_ref[...] = ((xv * r) * g_ref[...] * (1.0 + scale) + shift).astype(BF16)

    row = pl.BlockSpec((tr, D), lambda i: (i, 0))
    ins = [x] + ([o1] if o1 is not None else []) + [mod, g]
    specs = [row] * (1 if o1 is None else 2) + [_full(mod.shape), _full(g.shape)]
    return _row_call(name, body, (T // tr,), specs, row, jax.ShapeDtypeStruct((T, D), BF16), sem=("parallel",))(*ins)


def mix_a_fwd(proj, cw, cb, lg, lb):
    T = proj.shape[0]
    K, Dc = cw.shape
    tr = _div(T, ROW_TILE, CONV_HALO)
    hb = tr // CONV_HALO

    def body(val_ref, gate_ref, hval_ref, hgate_ref, cw_ref, cb_ref, lg_ref, lb_ref, ac_ref, asw_ref, buf):
        i = pl.program_id(0)
        hist = hval_ref[...] * _sig(hgate_ref[...])
        buf[0:CONV_HALO, :] = jnp.where(i > 0, hist, 0.0)
        buf[CONV_HALO:CONV_HALO + tr, :] = val_ref[...] * _sig(gate_ref[...])
        base = CONV_HALO - (K - 1)
        acc = jnp.zeros((tr, Dc), F32)
        for k in range(K):
            acc = acc + buf[base + k:base + k + tr, :] * cw_ref[k:k + 1, :]
        ac = acc + cb_ref[...]
        ac_ref[...] = ac
        mu = _rowmean(ac)
        cen = ac - mu
        y = cen * lax.rsqrt(_rowmean(cen * cen) + EPS)
        aln = y * lg_ref[...] + lb_ref[...]
        asw_ref[...] = (aln * _sig(aln)).astype(BF16)

    def halo(col):
        return pl.BlockSpec((CONV_HALO, Dc), lambda i: (jnp.maximum(i * hb - 1, 0), col))

    row = pl.BlockSpec((tr, Dc), lambda i: (i, 0))
    return _row_call(
        "mix_a_fwd", body, (T // tr,),
        [row, pl.BlockSpec((tr, Dc), lambda i: (i, 1)), halo(0), halo(1),
         _full(cw.shape), _full(cb.shape), _full(lg.shape), _full(lb.shape)],
        [row, row], [jax.ShapeDtypeStruct((T, Dc), F32), jax.ShapeDtypeStruct((T, Dc), BF16)],
        scratch=[pltpu.VMEM((CONV_HALO + tr, Dc), F32)], sem=("parallel",))(proj, proj, proj, proj, cw, cb, lg, lb)


def _spatial_mask():
    t = lax.broadcasted_iota(jnp.int32, (CHUNK, CHUNK), 0)
    s = lax.broadcasted_iota(jnp.int32, (CHUNK, CHUNK), 1)
    return s <= t


def mix_b_fwd(proj, lg, lb, wsp, bsp_t):
    T = proj.shape[0]
    Ds = lg.shape[-1]
    G = wsp.shape[0]
    hd = Ds // G
    tr = _div(T, ROW_TILE, CHUNK)

    def body(u_ref, v_ref, lg_ref, lb_ref, w_ref, b_ref, uv_ref, vs):
        v = _gelu(v_ref[...])
        mu = _rowmean(v)
        cen = v - mu
        vln = (cen * lax.rsqrt(_rowmean(cen * cen) + EPS) * lg_ref[...] + lb_ref[...]).astype(BF16)
        mask = _spatial_mask()
        for g in range(G):
            wg = jnp.where(mask, w_ref[g], 0.0).astype(BF16)
            for n in range(tr // CHUNK):
                rows, cols = slice(n * CHUNK, (n + 1) * CHUNK), slice(g * hd, (g + 1) * hd)
                vs[rows, cols] = jnp.dot(wg, vln[rows, cols], preferred_element_type=F32) + b_ref[:, g:g + 1]
        uv_ref[...] = (_gelu(u_ref[...]) * vs[...]).astype(BF16)

    return _row_call(
        "mix_b_fwd", body, (T // tr,),
        [pl.BlockSpec((tr, Ds), lambda i: (i, 2)), pl.BlockSpec((tr, Ds), lambda i: (i, 3)),
         _full(lg.shape), _full(lb.shape), _full(wsp.shape), _full(bsp_t.shape)],
        pl.BlockSpec((tr, Ds), lambda i: (i, 0)), jax.ShapeDtypeStruct((T, Ds), BF16),
        scratch=[pltpu.VMEM((tr, Ds), F32)], sem=("parallel",))(proj, proj, lg, lb, wsp, bsp_t)


def merge_fwd(proj, y_a, y_b):
    T, D = y_a.shape
    tr = _div(T, ROW_TILE, SUBLANE)

    def body(g_ref, ya_ref, yb_ref, o_ref):
        o_ref[...] = (_sig(g_ref[:, 0:D]) * ya_ref[...] + _sig(g_ref[:, D:2 * D]) * yb_ref[...]).astype(BF16)

    row = pl.BlockSpec((tr, D), lambda i: (i, 0))
    return _row_call("merge_fwd", body, (T // tr,), [pl.BlockSpec((tr, 2 * D), lambda i: (i, 1)), row, row], row,
                     jax.ShapeDtypeStruct((T, D), BF16), sem=("parallel",))(proj, y_a, y_b)


def _conv3(buf, w_ref, h, tr):
    return (buf[h, FFN_HALO - 2:FFN_HALO - 2 + tr, :] * w_ref[h, 0:1, :]
            + buf[h, FFN_HALO - 1:FFN_HALO - 1 + tr, :] * w_ref[h, 1:2, :]
            + buf[h, FFN_HALO:FFN_HALO + tr, :] * w_ref[h, 2:3, :])


def ffn_act_fwd(upre, fw, fb):
    _, T, F = upre.shape
    tr = _div(T, ROW_TILE, FFN_HALO)
    cb = _div(F, MM_WIDE)
    hb = tr // FFN_HALO

    def body(x_ref, h_ref, w_ref, b_ref, f_ref, buf):
        i = pl.program_id(0)
        buf[:, 0:FFN_HALO, :] = jnp.where(i > 0, h_ref[...], 0.0)
        buf[:, FFN_HALO:FFN_HALO + tr, :] = x_ref[...]
        val = _conv3(buf, w_ref, 0, tr) + b_ref[0]
        gt = _conv3(buf, w_ref, 1, tr) + b_ref[1]
        f_ref[...] = (gt * _sig(gt) * val).astype(BF16)

    return _row_call(
        "ffn_act_fwd", body, (T // tr, F // cb),
        [pl.BlockSpec((2, tr, cb), lambda i, j: (0, i, j)),
         pl.BlockSpec((2, FFN_HALO, cb), lambda i, j: (0, jnp.maximum(i * hb - 1, 0), j)),
         pl.BlockSpec((2, 3, cb), lambda i, j: (0, 0, j)), pl.BlockSpec((2, 1, cb), lambda i, j: (0, 0, j))],
        pl.BlockSpec((tr, cb), lambda i, j: (i, j)), jax.ShapeDtypeStruct((T, F), BF16),
        scratch=[pltpu.VMEM((2, FFN_HALO + tr, cb), F32)], sem=("parallel", "parallel"))(upre, upre, fw, fb)


def final_fwd_bwd(x, o1, o2, mod, gf, target):
    T, D = x.shape
    tr = _div(T, ROW_TILE, SUBLANE)
    nt = T // tr

    def body(x_ref, o1_ref, o2_ref, mod_ref, gf_ref, t_ref, dx3_ref, do2_ref, st_ref):
        i = pl.program_id(0)
        gate1, gate2 = mod_ref[:, 2 * D:3 * D], mod_ref[:, 5 * D:6 * D]
        o2v = o2_ref[...]
        x3 = x_ref[...] + gate1 * o1_ref[...] + gate2 * o2v
        r = lax.rsqrt(_rowmean(x3 * x3) + EPS)
        xn = x3 * r
        err = xn * gf_ref[...] - t_ref[...]
        dy = err * (1.0 / D)
        dxn = dy * gf_ref[...]
        dx3 = r * (dxn - xn * _rowmean(dxn * xn))
        dx3_ref[...] = dx3
        do2_ref[...] = (dx3 * gate2).astype(BF16)

        @pl.when(i == 0)
        def _():
            st_ref[...] = jnp.zeros_like(st_ref)

        st_ref[0:1, :] += _colsum(dy * xn)
        st_ref[1:2, :] += _colsum(dx3 * o2v)
        st_ref[2:3, :] += _colsum(err * err) * (0.5 / D)

        @pl.when(i == nt - 1)
        def _():
            st_ref[3:4, :] = jnp.zeros((1, D), F32) + jnp.sum(st_ref[2:3, :])

    row = pl.BlockSpec((tr, D), lambda i: (i, 0))
    return _row_call(
        "final_fwd_bwd", body, (nt,), [row, row, row, _full(mod.shape), _full(gf.shape), row],
        [row, row, _full((8, D))],
        [jax.ShapeDtypeStruct((T, D), F32), jax.ShapeDtypeStruct((T, D), BF16), jax.ShapeDtypeStruct((8, D), F32)],
    )(x, o1, o2, mod, gf, target)


def ffn_act_bwd(upre, df, fw, fb):
    _, T, F = upre.shape
    tr = _div(T, ROW_TILE, FFN_HALO)
    cb = _div(F, MM_WIDE)
    hb = tr // FFN_HALO
    nt = T // tr

    def body(x_ref, h_ref, df_ref, w_ref, b_ref, dpre_ref, dw_ref, db_ref, buf, dbuf, carry):
        i = pl.program_id(1)
        ri = nt - 1 - i
        buf[:, 0:FFN_HALO, :] = jnp.where(ri > 0, h_ref[...], 0.0)
        buf[:, FFN_HALO:FFN_HALO + tr, :] = x_ref[...]
        val = _conv3(buf, w_ref, 0, tr) + b_ref[0]
        gt = _conv3(buf, w_ref, 1, tr) + b_ref[1]
        sg = _sig(gt)
        dfv = df_ref[...]
        dup = (dfv * (gt * sg), dfv * val * (sg * (1.0 + gt * (1.0 - sg))))

        @pl.when(i == 0)
        def _():
            carry[...] = jnp.zeros_like(carry)
            dw_ref[...] = jnp.zeros_like(dw_ref)
            db_ref[...] = jnp.zeros_like(db_ref)

        for h in range(2):
            db_ref[h] += _colsum(dup[h])
            for k in range(3):
                dw_ref[h, k:k + 1, :] += _colsum(dup[h] * buf[h, FFN_HALO - 2 + k:FFN_HALO - 2 + k + tr, :])
            dbuf[h, 0:tr, :] = dup[h]
            dbuf[h, tr:tr + FFN_HALO, :] = carry[h]
            dpre_ref[h] = (dbuf[h, 0:tr, :] * w_ref[h, 2:3, :] + dbuf[h, 1:tr + 1, :] * w_ref[h, 1:2, :]
                           + dbuf[h, 2:tr + 2, :] * w_ref[h, 0:1, :]).astype(BF16)
            carry[h] = dup[h][0:FFN_HALO, :]

    return _row_call(
        "ffn_act_bwd", body, (F // cb, nt),
        [pl.BlockSpec((2, tr, cb), lambda j, i: (0, nt - 1 - i, j)),
         pl.BlockSpec((2, FFN_HALO, cb), lambda j, i: (0, jnp.maximum((nt - 1 - i) * hb - 1, 0), j)),
         pl.BlockSpec((tr, cb), lambda j, i: (nt - 1 - i, j)),
         pl.BlockSpec((2, 3, cb), lambda j, i: (0, 0, j)), pl.BlockSpec((2, 1, cb), lambda j, i: (0, 0, j))],
        [pl.BlockSpec((2, tr, cb), lambda j, i: (0, nt - 1 - i, j)),
         pl.BlockSpec((2, 3, cb), lambda j, i: (0, 0, j)), pl.BlockSpec((2, 1, cb), lambda j, i: (0, 0, j))],
        [jax.ShapeDtypeStruct((2, T, F), BF16), jax.ShapeDtypeStruct((2, 3, F), F32), jax.ShapeDtypeStruct((2, 1, F), F32)],
        scratch=[pltpu.VMEM((2, FFN_HALO + tr, cb), F32), pltpu.VMEM((2, tr + FFN_HALO, cb), F32),
                 pltpu.VMEM((2, FFN_HALO, cb), F32)],
        sem=("parallel", "arbitrary"))(upre, upre, df, fw, fb)


def norm2_bwd(dh2, x, o1, dx3, mod, g2):
    T, D = x.shape
    tr = _div(T, ROW_TILE, SUBLANE)

    def body(dh_ref, x_ref, o1_ref, dx3_ref, mod_ref, g_ref, dx2_ref, do1_ref, st_ref):
        i = pl.program_id(0)
        gate1, scale = mod_ref[:, 2 * D:3 * D], mod_ref[:, 4 * D:5 * D]
        o1v = o1_ref[...]
        x2 = x_ref[...] + gate1 * o1v
        r = lax.rsqrt(_rowmean(x2 * x2) + EPS)
        xn = x2 * r
        dh = dh_ref[...]
        dxn = dh * (g_ref[...] * (1.0 + scale))
        dx2 = r * (dxn - xn * _rowmean(dxn * xn)) + dx3_ref[...]
        dx2_ref[...] = dx2
        do1_ref[...] = (dx2 * gate1).astype(BF16)

        @pl.when(i == 0)
        def _():
            st_ref[...] = jnp.zeros_like(st_ref)

        st_ref[0:1, :] += _colsum(dh)
        st_ref[1:2, :] += _colsum(dh * xn) * g_ref[...]
        st_ref[2:3, :] += _colsum(dh * xn) * (1.0 + scale)
        st_ref[3:4, :] += _colsum(dx2 * o1v)

    row = pl.BlockSpec((tr, D), lambda i: (i, 0))
    return _row_call(
        "norm2_bwd", body, (T // tr,), [row, row, row, row, _full(mod.shape), _full(g2.shape)],
        [row, row, _full((8, D))],
        [jax.ShapeDtypeStruct((T, D), F32), jax.ShapeDtypeStruct((T, D), BF16), jax.ShapeDtypeStruct((8, D), F32)],
    )(dh2, x, o1, dx3, mod, g2)


def merge_bwd(dmerged, proj, y_a, y_b):
    T, D = y_a.shape
    tr = _div(T, ROW_TILE, SUBLANE)

    def body(dm_ref, g_ref, ya_ref, yb_ref, dya_ref, dyb_ref, dp_ref, db_ref):
        i = pl.program_id(0)
        dm = dm_ref[...]
        sa, sb = _sig(g_ref[:, 0:D]), _sig(g_ref[:, D:2 * D])
        dya_ref[...] = (dm * sa).astype(BF16)
        dyb_ref[...] = (dm * sb).astype(BF16)
        dga = dm * ya_ref[...] * (sa * (1.0 - sa))
        dgb = dm * yb_ref[...] * (sb * (1.0 - sb))
        dp_ref[:, 0:D] = dga.astype(BF16)
        dp_ref[:, D:2 * D] = dgb.astype(BF16)

        @pl.when(i == 0)
        def _():
            db_ref[...] = jnp.zeros_like(db_ref)

        db_ref[:, 0:D] += _colsum(dga)
        db_ref[:, D:2 * D] += _colsum(dgb)

    row = pl.BlockSpec((tr, D), lambda i: (i, 0))
    wide = pl.BlockSpec((tr, 2 * D), lambda i: (i, 1))
    return _row_call(
        "merge_bwd", body, (T // tr,), [row, wide, row, row], [row, row, wide, _full((1, 2 * D))],
        [jax.ShapeDtypeStruct((T, D), BF16), jax.ShapeDtypeStruct((T, D), BF16),
         jax.ShapeDtypeStruct((T, 4 * D), BF16), jax.ShapeDtypeStruct((1, 2 * D), F32)],
    )(dmerged, proj, y_a, y_b)


def mix_a_bwd(dasw, ac, proj, dproj, cw, lg, lb):
    T, Dc = ac.shape
    K = cw.shape[0]
    tr = _div(T, ROW_TILE, CONV_HALO)
    hb = tr // CONV_HALO
    nt = T // tr

    def body(dasw_ref, ac_ref, in_ref, hin_ref, dp_hbm, cw_ref, lg_ref, lb_ref,
             dp_ref, st_ref, dcw_ref, db_ref, abuf, dbuf, carry):
        del dp_hbm
        i = pl.program_id(0)
        ri = nt - 1 - i
        acv = ac_ref[...]
        mu = _rowmean(acv)
        cen = acv - mu
        rstd = lax.rsqrt(_rowmean(cen * cen) + EPS)
        y = cen * rstd
        aln = y * lg_ref[...] + lb_ref[...]
        sg = _sig(aln)
        daln = dasw_ref[...] * (sg * (1.0 + aln * (1.0 - sg)))
        dy = daln * lg_ref[...]
        dac = rstd * (dy - _rowmean(dy) - y * _rowmean(dy * y))

        @pl.when(i == 0)
        def _():
            carry[...] = jnp.zeros_like(carry)
            st_ref[...] = jnp.zeros_like(st_ref)
            dcw_ref[...] = jnp.zeros_like(dcw_ref)
            db_ref[...] = jnp.zeros_like(db_ref)

        st_ref[0:1, :] += _colsum(daln * y)
        st_ref[1:2, :] += _colsum(daln)
        st_ref[2:3, :] += _colsum(dac)
        val, gate = in_ref[:, 0:Dc], in_ref[:, Dc:2 * Dc]
        sgg = _sig(gate)
        hist = hin_ref[:, 0:Dc] * _sig(hin_ref[:, Dc:2 * Dc])
        abuf[0:CONV_HALO, :] = jnp.where(ri > 0, hist, 0.0)
        abuf[CONV_HALO:CONV_HALO + tr, :] = val * sgg
        dbuf[0:tr, :] = dac
        dbuf[tr:tr + CONV_HALO, :] = carry[...]
        base = CONV_HALO - (K - 1)
        da = jnp.zeros((tr, Dc), F32)
        for k in range(K):
            dcw_ref[k:k + 1, :] += _colsum(dac * abuf[base + k:base + k + tr, :])
            da = da + dbuf[K - 1 - k:K - 1 - k + tr, :] * cw_ref[k:k + 1, :]
        carry[...] = dac[0:CONV_HALO, :]
        dval = da * sgg
        dgate = da * val * (sgg * (1.0 - sgg))
        dp_ref[:, 0:Dc] = dval.astype(BF16)
        dp_ref[:, Dc:2 * Dc] = dgate.astype(BF16)
        db_ref[:, 0:Dc] += _colsum(dval)
        db_ref[:, Dc:2 * Dc] += _colsum(dgate)

    row = pl.BlockSpec((tr, Dc), lambda i: (nt - 1 - i, 0))
    wide = pl.BlockSpec((tr, 2 * Dc), lambda i: (nt - 1 - i, 0))
    return _row_call(
        "mix_a_bwd", body, (nt,),
        [row, row, wide, pl.BlockSpec((CONV_HALO, 2 * Dc), lambda i: (jnp.maximum((nt - 1 - i) * hb - 1, 0), 0)),
         pl.BlockSpec(memory_space=pl.ANY), _full(cw.shape), _full(lg.shape), _full(lb.shape)],
        [wide, _full((8, Dc)), _full((CONV_HALO, Dc)), _full((1, 2 * Dc))],
        [jax.ShapeDtypeStruct(dproj.shape, BF16), jax.ShapeDtypeStruct((8, Dc), F32),
         jax.ShapeDtypeStruct((CONV_HALO, Dc), F32), jax.ShapeDtypeStruct((1, 2 * Dc), F32)],
        scratch=[pltpu.VMEM((CONV_HALO + tr, Dc), F32), pltpu.VMEM((tr + CONV_HALO, Dc), F32),
                 pltpu.VMEM((CONV_HALO, Dc), F32)],
        aliases={4: 0})(dasw, ac, proj, proj, dproj, cw, lg, lb)


def mix_b_bwd(duv, proj, dproj, lg, lb, wsp, bsp_t):
    T, Ds = duv.shape
    G = wsp.shape[0]
    hd = Ds // G
    tr = _div(T, ROW_TILE, CHUNK)
    nt = T // tr

    def body(duv_ref, s_ref, dp_hbm, lg_ref, lb_ref, w_ref, b_ref,
             dp_ref, st_ref, dws_ref, dbs_ref, db_ref, vs, dvln):
        del dp_hbm
        i = pl.program_id(0)

        @pl.when(i == 0)
        def _():
            st_ref[...] = jnp.zeros_like(st_ref)
            dws_ref[...] = jnp.zeros_like(dws_ref)
            dbs_ref[...] = jnp.zeros_like(dbs_ref)
            db_ref[...] = jnp.zeros_like(db_ref)

        upre, vpre = s_ref[:, 0:Ds], s_ref[:, Ds:2 * Ds]
        u, v = _gelu(upre), _gelu(vpre)
        mu = _rowmean(v)
        cen = v - mu
        rstd = lax.rsqrt(_rowmean(cen * cen) + EPS)
        yv = cen * rstd
        vln = (yv * lg_ref[...] + lb_ref[...]).astype(BF16)
        duvv = duv_ref[...]
        dvs = duvv * u
        dvs_b = dvs.astype(BF16)
        mask = _spatial_mask()
        for g in range(G):
            wg = jnp.where(mask, w_ref[g], 0.0).astype(BF16)
            cols = slice(g * hd, (g + 1) * hd)
            dws = jnp.zeros((CHUNK, CHUNK), F32)
            dbs = jnp.zeros((CHUNK, 1), F32)
            for n in range(tr // CHUNK):
                rows = slice(n * CHUNK, (n + 1) * CHUNK)
                vs[rows, cols] = jnp.dot(wg, vln[rows, cols], preferred_element_type=F32) + b_ref[:, g:g + 1]
                dvln[rows, cols] = lax.dot_general(wg, dvs_b[rows, cols], (((0,), (0,)), ((), ())),
                                                   preferred_element_type=F32)
                dws = dws + lax.dot_general(dvs_b[rows, cols], vln[rows, cols], (((1,), (1,)), ((), ())),
                                            preferred_element_type=F32)
                dbs = dbs + jnp.sum(dvs[rows, cols], axis=1, keepdims=True)
            dws_ref[g] += jnp.where(mask, dws, 0.0)
            dbs_ref[:, g:g + 1] += dbs
        dvl = dvln[...]
        st_ref[0:1, :] += _colsum(dvl * yv)
        st_ref[1:2, :] += _colsum(dvl)
        dyv = dvl * lg_ref[...]
        dv = rstd * (dyv - _rowmean(dyv) - yv * _rowmean(dyv * yv))
        dupre = duvv * vs[...] * _gelu_grad(upre)
        dvpre = dv * _gelu_grad(vpre)
        dp_ref[:, 0:Ds] = dupre.astype(BF16)
        dp_ref[:, Ds:2 * Ds] = dvpre.astype(BF16)
        db_ref[:, 0:Ds] += _colsum(dupre)
        db_ref[:, Ds:2 * Ds] += _colsum(dvpre)

    wide = pl.BlockSpec((tr, 2 * Ds), lambda i: (i, 1))
    return _row_call(
        "mix_b_bwd", body, (nt,),
        [pl.BlockSpec((tr, Ds), lambda i: (i, 0)), wide, pl.BlockSpec(memory_space=pl.ANY),
         _full(lg.shape), _full(lb.shape), _full(wsp.shape), _full(bsp_t.shape)],
        [wide, _full((8, Ds)), _full(wsp.shape), _full(bsp_t.shape), _full((1, 2 * Ds))],
        [jax.ShapeDtypeStruct(dproj.shape, BF16), jax.ShapeDtypeStruct((8, Ds), F32),
         jax.ShapeDtypeStruct(wsp.shape, F32), jax.ShapeDtypeStruct(bsp_t.shape, F32),
         jax.ShapeDtypeStruct((1, 2 * Ds), F32)],
        scratch=[pltpu.VMEM((tr, Ds), F32), pltpu.VMEM((tr, Ds), F32)],
        aliases={2: 0})(duv, proj, dproj, lg, lb, wsp, bsp_t)


def norm1_bwd(dh1, x, dx2, mod, g1):
    T, D = x.shape
    tr = _div(T, ROW_TILE, SUBLANE)

    def body(dh_ref, x_ref, dx2_ref, mod_ref, g_ref, gx_ref, st_ref):
        i = pl.program_id(0)
        scale = mod_ref[:, D:2 * D]
        xv = x_ref[...]
        r = lax.rsqrt(_rowmean(xv * xv) + EPS)
        xn = xv * r
        dh = dh_ref[...]
        dxn = dh * (g_ref[...] * (1.0 + scale))
        gx_ref[...] = r * (dxn - xn * _rowmean(dxn * xn)) + dx2_ref[...]

        @pl.when(i == 0)
        def _():
            st_ref[...] = jnp.zeros_like(st_ref)

        st_ref[0:1, :] += _colsum(dh)
        st_ref[1:2, :] += _colsum(dh * xn) * g_ref[...]
        st_ref[2:3, :] += _colsum(dh * xn) * (1.0 + scale)

    row = pl.BlockSpec((tr, D), lambda i: (i, 0))
    return _row_call(
        "norm1_bwd", body, (T // tr,), [row, row, row, _full(mod.shape), _full(g1.shape)], [row, _full((8, D))],
        [jax.ShapeDtypeStruct((T, D), F32), jax.ShapeDtypeStruct((8, D), F32)])(dh1, x, dx2, mod, g1)


def ada_fwd_local(c_all, w_ada, b_cols):
    B, D = c_all.shape
    Na = w_ada.shape[1]
    tn = _div(Na, 512)

    def body(c_ref, w_ref, b_ref, o_ref):
        cv = c_ref[...]
        act = (cv * _sig(cv)).astype(BF16)
        o_ref[...] = jnp.dot(act, w_ref[...].astype(BF16), preferred_element_type=F32) + b_ref[...]

    return _row_call(
        "ada_fwd_local", body, (Na // tn,),
        [_full(c_all.shape), pl.BlockSpec((D, tn), lambda j: (0, j)), pl.BlockSpec((1, tn), lambda j: (0, j))],
        pl.BlockSpec((B, tn), lambda j: (0, j)), jax.ShapeDtypeStruct((B, Na), F32), sem=("parallel",))(c_all, w_ada, b_cols)


def ada_bwd_local(c_all_t, dmod_all):
    D, B = c_all_t.shape
    Na = dmod_all.shape[1]
    tr = _div(D, 512, SUBLANE)

    def body(c_ref, d_ref, o_ref):
        cv = c_ref[...]
        act = cv * _sig(cv)
        acc = act[:, 0:1] * d_ref[0:1, :]
        for b in range(1, B):
            acc = acc + act[:, b:b + 1] * d_ref[b:b + 1, :]
        o_ref[...] = acc

    return _row_call(
        "ada_bwd_local", body, (D // tr,), [pl.BlockSpec((tr, B), lambda i: (i, 0)), _full(dmod_all.shape)],
        pl.BlockSpec((tr, Na), lambda i: (i, 0)), jax.ShapeDtypeStruct((D, Na), F32), sem=("parallel",))(c_all_t, dmod_all)


def adamw(name, w, m, v, parts):
    R, C = w.shape
    tr = _div(R, max(SUBLANE, (1 << 18) // C // SUBLANE * SUBLANE), SUBLANE)
    bc1, bc2 = 1.0 - ADAM_B1 ** ADAM_STEP, 1.0 - ADAM_B2 ** ADAM_STEP
    n = len(parts)

    def body(*refs):
        w_ref, m_ref, v_ref = refs[:3]
        g_ref, d_ref, nm_ref, nv_ref = refs[3 + n:]
        g = refs[3][...]
        for p in refs[4:3 + n]:
            g = g + p[...]
        mn = ADAM_B1 * m_ref[...] + (1.0 - ADAM_B1) * g
        vn = ADAM_B2 * v_ref[...] + (1.0 - ADAM_B2) * (g * g)
        g_ref[...] = g
        nm_ref[...] = mn
        nv_ref[...] = vn
        d_ref[...] = -ADAM_LR * ((mn / bc1) / (jnp.sqrt(vn / bc2) + ADAM_EPS) + ADAM_WD * w_ref[...])

    row = pl.BlockSpec((tr, C), lambda i: (i, 0))
    pspecs = [row if lead is None else pl.BlockSpec((None, tr, C), lambda i, lead=lead: (lead, i, 0)) for _, lead in parts]
    out = jax.ShapeDtypeStruct((R, C), F32)
    return _row_call(name, body, (R // tr,), [row, row, row] + pspecs, [row] * 4, [out] * 4, sem=("parallel",))(
        w, m, v, *[a for a, _ in parts])


def pair_add(name, g, r, idx):
    _, R, C = g.shape
    tr = _div(R, max(SUBLANE, (1 << 18) // C // SUBLANE * SUBLANE), SUBLANE)

    def body(idx_ref, g_ref, r_ref, o_ref):
        del idx_ref
        o_ref[...] = g_ref[...] + r_ref[...]

    return pl.pallas_call(
        body, name=name,
        grid_spec=pltpu.PrefetchScalarGridSpec(
            num_scalar_prefetch=1, grid=(4, R // tr),
            in_specs=[pl.BlockSpec((None, tr, C), lambda m, i, idx_ref: (idx_ref[m], i, 0)),
                      pl.BlockSpec((None, tr, C), lambda m, i, idx_ref: (m, i, 0))],
            out_specs=pl.BlockSpec((None, tr, C), lambda m, i, idx_ref: (m, i, 0))),
        out_shape=jax.ShapeDtypeStruct((4, R, C), F32),
        compiler_params=_cp(("parallel", "parallel")))(idx, g, r)


def sum_rows(name, stacked):
    nb, _, N = stacked.shape
    tn = _div(N, 1 << 16)

    def body(s_ref, o_ref):
        acc = s_ref[0]
        for b in range(1, nb):
            acc = acc + s_ref[b]
        o_ref[...] = acc

    return _row_call(name, body, (N // tn,), [pl.BlockSpec((nb, 1, tn), lambda j: (0, 0, j))],
                     pl.BlockSpec((1, tn), lambda j: (0, j)), jax.ShapeDtypeStruct((1, N), F32), sem=("parallel",))(stacked)


def _coords():
    return lax.axis_index("x"), lax.axis_index("y"), lax.axis_index("c")


def _flip(v, bit):
    return 1 - v if bit else v


def _comm_call(name, body, ins, out_shapes, n_sems):
    any_spec = pl.BlockSpec(memory_space=pl.ANY)
    return pl.pallas_call(
        body, name=name, in_specs=[any_spec] * len(ins), out_specs=[any_spec] * len(out_shapes), out_shape=out_shapes,
        scratch_shapes=[pltpu.SemaphoreType.DMA((s,)) for s in n_sems],
        compiler_params=pltpu.CompilerParams(has_side_effects=True))(*ins)


def gather_rows(name, row):
    N = row.shape[1]

    def body(row_ref, out_ref, send_sems, recv_sems, local_sem):
        x, y, c = _coords()
        me = 4 * x + 2 * y + c
        mine = pltpu.make_async_copy(row_ref, out_ref.at[me], local_sem.at[0])
        mine.start()
        copies = []
        for k in range(1, NDEV):
            peer = (_flip(x, k & 4), _flip(y, k & 2), _flip(c, k & 1))
            copies.append(pltpu.make_async_remote_copy(
                src_ref=row_ref, dst_ref=out_ref.at[me], send_sem=send_sems.at[k - 1], recv_sem=recv_sems.at[k - 1],
                device_id=peer, device_id_type=MESH))
        for cp in copies:
            cp.start()
        for cp in copies:
            cp.wait_recv()
        for cp in copies:
            cp.wait_send()
        mine.wait()

    return _comm_call(name, body, [row], [jax.ShapeDtypeStruct((NDEV, 1, N), row.dtype)], (NDEV - 1, NDEV - 1, 1))[0]


def exchange_rows(name, slabs):
    def body(in_ref, out_ref, send_sems, recv_sems, local_sem):
        x, y, c = _coords()
        me = 4 * x + 2 * y + c
        mine = pltpu.make_async_copy(in_ref.at[me], out_ref.at[me], local_sem.at[0])
        mine.start()
        copies = []
        for k in range(1, NDEV):
            px, py, pc = _flip(x, k & 4), _flip(y, k & 2), _flip(c, k & 1)
            copies.append(pltpu.make_async_remote_copy(
                src_ref=in_ref.at[4 * px + 2 * py + pc], dst_ref=out_ref.at[me], send_sem=send_sems.at[k - 1],
                recv_sem=recv_sems.at[k - 1], device_id=(px, py, pc), device_id_type=MESH))
        for cp in copies:
            cp.start()
        for cp in copies:
            cp.wait_recv()
        for cp in copies:
            cp.wait_send()
        mine.wait()

    return _comm_call(name, body, [slabs], [jax.ShapeDtypeStruct(slabs.shape, slabs.dtype)], (NDEV - 1, NDEV - 1, 1))[0]


def gather_weights(shards):
    L = len(shards)

    def body(*refs):
        ins, outs = refs[:L], refs[L:2 * L]
        send_sems, recv_sems, local_sems = refs[2 * L:]
        x, y, c = _coords()
        sibling = (x, y, 1 - c)
        chips = [(_flip(x, m & 2), _flip(y, m & 1)) for m in (1, 2, 3)]

        def slab(px, py, pc):
            return 4 * px + 2 * py + pc

        def copy(l, k, block, to, src=None):
            dst = outs[l].at[slab(*block)]
            return pltpu.make_async_remote_copy(
                src_ref=dst if src is None else src, dst_ref=dst, send_sem=send_sems.at[7 * l + k],
                recv_sem=recv_sems.at[7 * l + k], device_id=to, device_id_type=MESH)

        local = [pltpu.make_async_copy(ins[l], outs[l].at[slab(x, y, c)], local_sems.at[l]) for l in range(L)]
        for cp in local:
            cp.start()
        first = []
        for l in range(L):
            first.append(copy(l, 0, (x, y, c), sibling, src=ins[l]))
            first += [copy(l, 1 + j, (x, y, c), (*chip, c), src=ins[l]) for j, chip in enumerate(chips)]
        for cp in first:
            cp.start()
        passed = []
        for j, chip in enumerate(chips):
            for l in range(L):
                copy(l, 1 + j, (*chip, c), (x, y, c)).wait_recv()
                fwd = copy(l, 4 + j, (*chip, c), sibling)
                fwd.start()
                passed.append(fwd)
        for l in range(L):
            copy(l, 0, sibling, (x, y, c)).wait_recv()
            for j, chip in enumerate(chips):
                copy(l, 4 + j, (*chip, 1 - c), (x, y, c)).wait_recv()
        for cp in first + passed:
            cp.wait_send()
        for cp in local:
            cp.wait()

    outs = [jax.ShapeDtypeStruct((NDEV,) + s.shape, s.dtype) for s in shards]
    return _comm_call("gather_weights", body, list(shards), outs, (7 * L, 7 * L, L))


def sibling_exchange(grads):
    L = len(grads)

    def body(*refs):
        ins, outs = refs[:L], refs[L:2 * L]
        send_sems, recv_sems = refs[2 * L:]
        x, y, c = _coords()
        q = 2 * x + y
        copies = []
        for l in range(L):
            for m in range(4):
                qm = 2 * _flip(x, m & 2) + _flip(y, m & 1)
                copies.append(pltpu.make_async_remote_copy(
                    src_ref=ins[l].at[2 * qm + (1 - c)], dst_ref=outs[l].at[m], send_sem=send_sems.at[4 * l + m],
                    recv_sem=recv_sems.at[4 * l + m], device_id=(x, y, 1 - c), device_id_type=MESH))
        del q
        for cp in copies:
            cp.start()
        for cp in copies:
            cp.wait_recv()
        for cp in copies:
            cp.wait_send()

    outs = [jax.ShapeDtypeStruct((4,) + g.shape[1:], g.dtype) for g in grads]
    return _comm_call("sibling_exchange", body, list(grads), outs, (4 * L, 4 * L))


def chip_exchange(partials):
    L = len(partials)

    def body(*refs):
        ins, outs = refs[:L], refs[L:2 * L]
        send_sems, recv_sems = refs[2 * L:]
        x, y, c = _coords()
        copies = []
        for l in range(L):
            for m in (1, 2, 3):
                copies.append(pltpu.make_async_remote_copy(
                    src_ref=ins[l].at[m], dst_ref=outs[l].at[m - 1], send_sem=send_sems.at[3 * l + m - 1],
                    recv_sem=recv_sems.at[3 * l + m - 1], device_id=(_flip(x, m & 2), _flip(y, m & 1), c),
                    device_id_type=MESH))
        for cp in copies:
            cp.start()
        for cp in copies:
            cp.wait_recv()
        for cp in copies:
            cp.wait_send()

    outs = [jax.ShapeDtypeStruct((3,) + p.shape[1:], p.dtype) for p in partials]
    return _comm_call("chip_exchange", body, list(partials), outs, (3 * L, 3 * L))


_SMALL = ("b_ada", "norm1_g", "b_in", "conv_dw_b", "conv_ln_g", "conv_ln_b", "sgu_ln_g", "sgu_ln_b", "w_spatial",
          "b_spatial", "norm2_g", "ffn_dw_b", "final_g")
_SMALL_SHARDED = ("conv_dw_w", "ffn_dw_w")
_BIG = ("w_in", "w_conv_out", "w_sgu_out", "w_out", "w_up", "w_down")
_ORDER = ("w_ada", "b_ada", "norm1_g", "w_in", "b_in", "conv_dw_w", "conv_dw_b", "conv_ln_g", "conv_ln_b", "w_conv_out",
          "sgu_ln_g", "sgu_ln_b", "w_spatial", "b_spatial", "w_sgu_out", "w_out", "norm2_g", "w_up", "ffn_dw_w", "ffn_dw_b",
          "w_down", "final_g")


def _pack(arrays, mult):
    flat = jnp.concatenate([a.reshape(-1) for a in arrays])
    pad = (-flat.shape[0]) % mult
    return jnp.pad(flat, (0, pad)) if pad else flat


def kernel(x, c, w_ada, b_ada, norm1_g, w_in, b_in, conv_dw_w, conv_dw_b, conv_ln_g, conv_ln_b, w_conv_out, sgu_ln_g, sgu_ln_b, w_spatial, b_spatial, w_sgu_out, w_out, norm2_g, w_up, ffn_dw_w, ffn_dw_b, w_down, final_g, loss_target, m_w_ada, m_b_ada, m_norm1_g, m_w_in, m_b_in, m_conv_dw_w, m_conv_dw_b, m_conv_ln_g, m_conv_ln_b, m_w_conv_out, m_sgu_ln_g, m_sgu_ln_b, m_w_spatial, m_b_spatial, m_w_sgu_out, m_w_out, m_norm2_g, m_w_up, m_ffn_dw_w, m_ffn_dw_b, m_w_down, m_final_g, v_w_ada, v_b_ada, v_norm1_g, v_w_in, v_b_in, v_conv_dw_w, v_conv_dw_b, v_conv_ln_g, v_conv_ln_b, v_w_conv_out, v_sgu_ln_g, v_sgu_ln_b, v_w_spatial, v_b_spatial, v_w_sgu_out, v_w_out, v_norm2_g, v_w_up, v_ffn_dw_w, v_ffn_dw_b, v_w_down, v_final_g):
    W = dict(w_ada=w_ada, b_ada=b_ada, norm1_g=norm1_g, w_in=w_in, b_in=b_in, conv_dw_w=conv_dw_w, conv_dw_b=conv_dw_b,
             conv_ln_g=conv_ln_g, conv_ln_b=conv_ln_b, w_conv_out=w_conv_out, sgu_ln_g=sgu_ln_g, sgu_ln_b=sgu_ln_b,
             w_spatial=w_spatial, b_spatial=b_spatial, w_sgu_out=w_sgu_out, w_out=w_out, norm2_g=norm2_g, w_up=w_up,
             ffn_dw_w=ffn_dw_w, ffn_dw_b=ffn_dw_b, w_down=w_down, final_g=final_g)
    M = dict(w_ada=m_w_ada, b_ada=m_b_ada, norm1_g=m_norm1_g, w_in=m_w_in, b_in=m_b_in, conv_dw_w=m_conv_dw_w,
             conv_dw_b=m_conv_dw_b, conv_ln_g=m_conv_ln_g, conv_ln_b=m_conv_ln_b, w_conv_out=m_w_conv_out,
             sgu_ln_g=m_sgu_ln_g, sgu_ln_b=m_sgu_ln_b, w_spatial=m_w_spatial, b_spatial=m_b_spatial,
             w_sgu_out=m_w_sgu_out, w_out=m_w_out, norm2_g=m_norm2_g, w_up=m_w_up, ffn_dw_w=m_ffn_dw_w,
             ffn_dw_b=m_ffn_dw_b, w_down=m_w_down, final_g=m_final_g)
    V = dict(w_ada=v_w_ada, b_ada=v_b_ada, norm1_g=v_norm1_g, w_in=v_w_in, b_in=v_b_in, conv_dw_w=v_conv_dw_w,
             conv_dw_b=v_conv_dw_b, conv_ln_g=v_conv_ln_g, conv_ln_b=v_conv_ln_b, w_conv_out=v_w_conv_out,
             sgu_ln_g=v_sgu_ln_g, sgu_ln_b=v_sgu_ln_b, w_spatial=v_w_spatial, b_spatial=v_b_spatial,
             w_sgu_out=v_w_sgu_out, w_out=v_w_out, norm2_g=v_norm2_g, w_up=v_w_up, ffn_dw_w=v_ffn_dw_w,
             ffn_dw_b=v_ffn_dw_b, w_down=v_w_down, final_g=v_final_g)

    xs, tgt = x[0], loss_target[0]
    T, D = xs.shape
    Dc = conv_dw_w.shape[-1] * NDEV
    F = w_down.shape[1] * NDEV
    K31 = conv_dw_w.shape[1]
    G = w_spatial.shape[1]
    assert D == 2 * Dc and sgu_ln_g.shape[-1] == Dc and T % CHUNK == 0
    me = 4 * lax.axis_index("x") + 2 * lax.axis_index("y") + lax.axis_index("c")

    na = w_ada.shape[-1]
    c_all = gather_rows("gather_c", c).reshape(NDEV, D)
    b_cols = lax.dynamic_slice(b_ada, (0, me * na), (1, na))
    mod_cols = ada_fwd_local(c_all, w_ada[0], b_cols)
    mod = exchange_rows("exchange_mod", mod_cols.reshape(NDEV, 1, na)).reshape(1, NDEV * na)

    gathered = gather_weights([W[k][0].astype(BF16) for k in _BIG] + [conv_dw_w[0], ffn_dw_w[0]])
    wb_in, wb_co, wb_so, wb_out, wb_up, wb_down, cw_g, fw_g = gathered
    wb_out = wb_out.reshape(1, D, D)
    wb_down = wb_down.reshape(1, F, D)
    cw = jnp.transpose(cw_g, (1, 0, 2)).reshape(K31, Dc)
    fw = jnp.transpose(jnp.transpose(fw_g, (1, 0, 2)).reshape(3, 2, F), (1, 0, 2))
    fb = ffn_dw_b.reshape(2, 1, F)
    bsp_t = jnp.transpose(b_spatial[0])
    wsp = w_spatial[0]

    h1 = pre_norm("pre_norm1", xs, mod, norm1_g, 0)
    proj = mm_nn("proj", h1, wb_in, F32, bias=b_in)
    ac, asw = mix_a_fwd(proj, cw, conv_dw_b, conv_ln_g, conv_ln_b)
    uv = mix_b_fwd(proj, sgu_ln_g, sgu_ln_b, wsp, bsp_t)
    y_a = mm_nn("y_a", asw, wb_co, F32)
    y_b = mm_nn("y_b", uv, wb_so, F32)
    merged = merge_fwd(proj, y_a, y_b)
    o1 = mm_nn("o1", merged, wb_out, F32)
    h2 = pre_norm("pre_norm2", xs, mod, norm2_g, 1, o1=o1)
    upre = mm_nn("upre", h2, wb_up, F32, out_halves=True, tn_pref=MM_WIDE)
    f = ffn_act_fwd(upre, fw, fb)
    o2 = mm_nn("o2", f, wb_down, F32)
    dx3, do2, st_f = final_fwd_bwd(xs, o1, o2, mod, final_g.reshape(1, D), tgt)
    loss = lax.psum(st_f[3, 0], MESH_AXES)

    df = mm_nt("df", do2, wb_down, F32, tko_pref=MM_WIDE)
    g_down = mm_tn("g_down", f, do2, 1, tko_pref=MM_WIDE)
    dupre, g_fw, g_fb = ffn_act_bwd(upre, df, fw, fb)
    dh2 = mm_nt("dh2", dupre, wb_up, F32, a_halves=True)
    g_up = mm_tn("g_up", h2, dupre, NDEV, g_halves=True, tn_pref=MM_WIDE)
    dx2, do1, st_2 = norm2_bwd(dh2, xs, o1, dx3, mod, norm2_g)
    dmerged = mm_nt("dmerged", do1, wb_out, F32)
    g_out = mm_tn("g_out", merged, do1, 1)
    dy_a, dy_b, dproj, db_g = merge_bwd(dmerged, proj, y_a, y_b)
    dasw = mm_nt("dasw", dy_a, wb_co, F32)
    g_co = mm_tn("g_co", asw, dy_a, NDEV)
    duv = mm_nt("duv", dy_b, wb_so, F32)
    g_so = mm_tn("g_so", uv, dy_b, NDEV)
    dproj, st_a, g_cw, db_a = mix_a_bwd(dasw, ac, proj, dproj, cw, conv_ln_g, conv_ln_b)
    dproj, st_b, g_wsp, g_bsp_t, db_s = mix_b_bwd(duv, proj, dproj, sgu_ln_g, sgu_ln_b, wsp, bsp_t)
    dh1 = mm_nt("dh1", dproj, wb_in, F32)
    g_in = mm_tn("g_in", h1, dproj, NDEV)
    grad_x, st_1 = norm1_bwd(dh1, xs, dx2, mod, norm1_g)

    dmod = jnp.concatenate([st_1[0], st_1[1], st_2[3], st_2[0], st_2[1], st_f[1]]).reshape(1, NDEV * na)
    dmod_all = exchange_rows("exchange_dmod", dmod.reshape(NDEV, 1, na)).reshape(NDEV, na)
    g_ada = ada_bwd_local(jnp.transpose(c_all), dmod_all)

    big_full = dict(w_in=g_in, w_conv_out=g_co, w_sgu_out=g_so, w_out=g_out.reshape(NDEV, D // NDEV, D), w_up=g_up,
                    w_down=g_down.reshape(NDEV, F // NDEV, D))
    from_sibling = sibling_exchange([big_full[k] for k in _BIG])
    xq, yq, cq = lax.axis_index("x"), lax.axis_index("y"), lax.axis_index("c")
    idx = jnp.stack([2 * (2 * _flip(xq, m & 2) + _flip(yq, m & 1)) + cq for m in range(4)]).astype(jnp.int32)
    partials = [pair_add("pair_add_" + k, big_full[k], r, idx) for k, r in zip(_BIG, from_sibling)]
    from_chips = chip_exchange(partials)
    res = {}
    for k, p, q in zip(_BIG, partials, from_chips):
        res[k] = adamw("adamw_" + k, W[k][0], M[k][0], V[k][0], [(p, 0), (q, 0), (q, 1), (q, 2)])
    res["w_ada"] = adamw("adamw_w_ada", w_ada[0], m_w_ada[0], v_w_ada[0], [(g_ada, None)])

    g_small = dict(
        b_ada=dmod, norm1_g=st_1[2], b_in=jnp.concatenate([db_a, db_s, db_g], axis=1), conv_dw_b=st_a[2],
        conv_ln_g=st_a[0], conv_ln_b=st_a[1], sgu_ln_g=st_b[0], sgu_ln_b=st_b[1], w_spatial=g_wsp,
        b_spatial=jnp.transpose(g_bsp_t), norm2_g=st_2[2], ffn_dw_b=g_fb, final_g=st_f[0],
        conv_dw_w=g_cw[:K31], ffn_dw_w=jnp.transpose(g_fw, (1, 0, 2)))
    names = _SMALL + _SMALL_SHARDED
    sizes = [g_small[k].size for k in names]
    packed = _pack([g_small[k] for k in names], 8 * LANE)
    summed = sum_rows("sum_small", gather_rows("gather_small", packed.reshape(1, -1)))[0]
    offs = [sum(sizes[:i]) for i in range(len(names))]
    seg = {k: summed[o:o + s] for k, o, s in zip(names, offs, sizes)}
    n_cw, n_fw = conv_dw_w.shape[-1], ffn_dw_w.shape[-1]
    seg["conv_dw_w"] = lax.dynamic_slice(seg["conv_dw_w"].reshape(K31, Dc), (0, me * n_cw), (K31, n_cw)).reshape(-1)
    seg["ffn_dw_w"] = lax.dynamic_slice(seg["ffn_dw_w"].reshape(3, 2 * F), (0, me * n_fw), (3, n_fw)).reshape(-1)
    gp = _pack([seg[k] for k in names], 8 * LANE).reshape(-1, LANE)
    wp, mp, vp = (_pack([S[k] for k in names], 8 * LANE).reshape(-1, LANE) for S in (W, M, V))
    small = adamw("adamw_small", wp, mp, vp, [(gp, None)])
    sizes2 = [W[k].size for k in names]
    offs2 = [sum(sizes2[:i]) for i in range(len(names))]
    for k, o, s in zip(names, offs2, sizes2):
        res[k] = tuple(a.reshape(-1)[o:o + s].reshape(W[k].shape) for a in small)

    outs = [[], [], [], []]
    for k in _ORDER:
        for slot in range(4):
            outs[slot].append(res[k][slot].reshape(W[k].shape))
    return (loss, grad_x[None], *outs[0], *outs[1], *outs[2], *outs[3])
```

```python
import functools

import jax
import jax.numpy as jnp
from jax import lax
from jax.experimental import pallas as pl
from jax.experimental.pallas import tpu as pltpu

F32, BF16 = jnp.float32, jnp.bfloat16
NDEV = 8
MESH_AXES = ("x", "y", "c")
MESH = pl.DeviceIdType.MESH
EPS = 1e-6
CHUNK = 128
CONV_HALO = 32
FFN_HALO = 8
LANE, SUBLANE = 128, 8
ROW_TILE = 256
MM_TILE = 1024
MM_WIDE = 1408
VMEM_LIMIT = 56 * 1024 * 1024
ADAM_LR, ADAM_B1, ADAM_B2, ADAM_EPS, ADAM_WD, ADAM_STEP = 0.001, 0.9, 0.999, 1e-08, 0.01, 10
SQRT_HALF = 0.7071067811865476
INV_SQRT_2PI = 0.3989422804014327


def _div(n, pref, mult=LANE):
    if n <= pref:
        return n
    for d in range(pref - pref % mult, 0, -mult):
        if n % d == 0:
            return d
    return n


def _cp(sem):
    return pltpu.CompilerParams(dimension_semantics=sem, vmem_limit_bytes=VMEM_LIMIT)


def _sig(v):
    return jax.nn.sigmoid(v)


def _gelu(v):
    return 0.5 * v * (1.0 + lax.erf(v * SQRT_HALF))


def _gelu_grad(v):
    return 0.5 * (1.0 + lax.erf(v * SQRT_HALF)) + v * (INV_SQRT_2PI * jnp.exp(-0.5 * v * v))


def _colsum(v):
    return jnp.sum(v, axis=0, keepdims=True)


def _rowmean(v):
    return jnp.mean(v, axis=-1, keepdims=True)


class Comm:
    def __init__(self, plan, ins, out_shapes, n_sems):
        self.plan, self.ins, self.out_shapes, self.n_sems = plan, list(ins), list(out_shapes), tuple(n_sems)


def _pcall(name, body, grid, in_specs, out_specs, out_shape, scratch=(), sem=None, aliases=None, comm=None):
    if comm is None:
        return pl.pallas_call(
            body, name=name, grid=grid, in_specs=in_specs, out_specs=out_specs, out_shape=out_shape,
            scratch_shapes=list(scratch), input_output_aliases=aliases or {},
            compiler_params=_cp(sem or ("arbitrary",) * len(grid)))
    single = not isinstance(out_shape, (list, tuple))
    own_specs, own_shapes = ([out_specs], [out_shape]) if single else (list(out_specs), list(out_shape))
    n_in, n_out, n_scr = len(in_specs), len(own_shapes), len(scratch)
    n_ci, n_co = len(comm.ins), len(comm.out_shapes)
    any_spec = pl.BlockSpec(memory_space=pl.ANY)

    def fused(*refs):
        ins, cins = refs[:n_in], refs[n_in:n_in + n_ci]
        outs = refs[n_in + n_ci:n_in + n_ci + n_out]
        couts = refs[n_in + n_ci + n_out:n_in + n_ci + n_out + n_co]
        scr = refs[n_in + n_ci + n_out + n_co:n_in + n_ci + n_out + n_co + n_scr]
        sems = refs[n_in + n_ci + n_out + n_co + n_scr:]
        first = functools.reduce(jnp.logical_and, [pl.program_id(d) == 0 for d in range(len(grid))])
        last = functools.reduce(jnp.logical_and, [pl.program_id(d) == grid[d] - 1 for d in range(len(grid))])

        @pl.when(first)
        def _():
            comm.plan(cins, couts, sems)[0]()

        body(*ins, *outs, *scr)

        @pl.when(last)
        def _():
            comm.plan(cins, couts, sems)[1]()

    call = pl.pallas_call(
        fused, name=name, grid=grid, in_specs=list(in_specs) + [any_spec] * n_ci,
        out_specs=own_specs + [any_spec] * n_co, out_shape=own_shapes + comm.out_shapes,
        scratch_shapes=list(scratch) + [pltpu.SemaphoreType.DMA((s,)) for s in comm.n_sems],
        input_output_aliases=aliases or {},
        compiler_params=pltpu.CompilerParams(dimension_semantics=("arbitrary",) * len(grid),
                                             vmem_limit_bytes=VMEM_LIMIT, has_side_effects=True))

    def run(*args):
        res = call(*args, *comm.ins)
        own = res[:n_out]
        return (own[0] if single else list(own)), list(res[n_out:])

    return run


def _matmul(name, a, b, *, grid, a_spec, b_spec, o_spec, out_shape, dims, acc_shape, bias=None, bias_spec=None, comm=None):
    nk = grid[2]

    def body(*refs):
        if bias is None:
            a_ref, b_ref, o_ref, *scr = refs
            bias_ref = None
        else:
            a_ref, b_ref, bias_ref, o_ref, *scr = refs
        part = lax.dot_general(a_ref[...], b_ref[...], (dims, ((), ())), preferred_element_type=F32)

        def finish(total):
            if bias_ref is not None:
                total = total + bias_ref[...]
            o_ref[...] = total.astype(o_ref.dtype)

        if nk == 1:
            finish(part)
        else:
            acc = scr[0]
            k = pl.program_id(2)

            @pl.when(k == 0)
            def _():
                acc[...] = part

            @pl.when(k > 0)
            def _():
                acc[...] += part

            @pl.when(k == nk - 1)
            def _():
                finish(acc[...])

    in_specs = [a_spec, b_spec] + ([bias_spec] if bias is not None else [])
    args = (a, b) + ((bias,) if bias is not None else ())
    return _pcall(name, body, grid, in_specs, o_spec, out_shape,
                  scratch=[pltpu.VMEM(acc_shape, F32)] if nk > 1 else [],
                  sem=("parallel", "parallel", "arbitrary"), comm=comm)(*args)


def mm_nn(name, a, wb, out_dtype, *, bias=None, out_halves=False, tn_pref=MM_TILE, comm=None):
    T, K = a.shape
    NB, _, Ns = wb.shape
    N = NB * Ns
    tm, tn, tk = _div(T, MM_TILE), _div(Ns, tn_pref), _div(K, 2048)
    npb, nj, nk = Ns // tn, N // tn, K // tk
    if out_halves:
        o_spec = pl.BlockSpec((None, tm, tn), lambda i, j, k: (j // (nj // 2), i, j % (nj // 2)))
        out_shape = jax.ShapeDtypeStruct((2, T, N // 2), out_dtype)
    else:
        o_spec = pl.BlockSpec((tm, tn), lambda i, j, k: (i, j))
        out_shape = jax.ShapeDtypeStruct((T, N), out_dtype)
    return _matmul(
        name, a, wb, grid=(T // tm, nj, nk),
        a_spec=pl.BlockSpec((tm, tk), lambda i, j, k: (i, k)),
        b_spec=pl.BlockSpec((None, tk, tn), lambda i, j, k: (j // npb, k, j % npb)),
        o_spec=o_spec, out_shape=out_shape, dims=((1,), (0,)), acc_shape=(tm, tn),
        bias=bias, bias_spec=pl.BlockSpec((1, tn), lambda i, j, k: (0, j)), comm=comm)


def mm_nt(name, a, wb, out_dtype, *, a_halves=False, tko_pref=MM_TILE, tc_pref=2048, comm=None):
    NB, K, Ns = wb.shape
    T = a.shape[-2]
    tm, tko, tc = _div(T, MM_TILE), _div(K, tko_pref), _div(Ns, tc_pref)
    cpb = Ns // tc
    nkk = NB * cpb
    if a_halves:
        a_spec = pl.BlockSpec((None, tm, tc), lambda i, j, k: (k // (nkk // 2), i, k % (nkk // 2)))
    else:
        a_spec = pl.BlockSpec((tm, tc), lambda i, j, k: (i, k))
    return _matmul(
        name, a, wb, grid=(T // tm, K // tko, nkk), a_spec=a_spec,
        b_spec=pl.BlockSpec((None, tko, tc), lambda i, j, k: (k // cpb, j, k % cpb)),
        o_spec=pl.BlockSpec((tm, tko), lambda i, j, k: (i, j)),
        out_shape=jax.ShapeDtypeStruct((T, K), out_dtype), dims=((1,), (1,)), acc_shape=(tm, tko), comm=comm)


def mm_tn(name, a, g, nb, *, g_halves=False, tko_pref=MM_TILE, tn_pref=MM_TILE):
    T, K = a.shape
    N = g.shape[-1] * (2 if g_halves else 1)
    Ns = N // nb
    tt, tko, tn = _div(T, MM_TILE), _div(K, tko_pref), _div(Ns, tn_pref)
    npb, nj = Ns // tn, N // tn
    if g_halves:
        g_spec = pl.BlockSpec((None, tt, tn), lambda i, j, t: (j // (nj // 2), t, j % (nj // 2)))
    else:
        g_spec = pl.BlockSpec((tt, tn), lambda i, j, t: (t, j))
    return _matmul(
        name, a, g, grid=(K // tko, nj, T // tt),
        a_spec=pl.BlockSpec((tt, tko), lambda i, j, t: (t, i)), b_spec=g_spec,
        o_spec=pl.BlockSpec((None, tko, tn), lambda i, j, t: (j // npb, i, j % npb)),
        out_shape=jax.ShapeDtypeStruct((nb, K, Ns), F32), dims=((0,), (0,)), acc_shape=(tko, tn))


def _row_call(name, body, grid, in_specs, out_specs, out_shape, scratch=(), sem=None, aliases=None, comm=None):
    return _pcall(name, body, grid, in_specs, out_specs, out_shape, scratch, sem, aliases, comm)


def _full(shape):
    nd = len(shape)
    return pl.BlockSpec(shape, lambda *idx: (0,) * nd)


def pre_norm(name, x, mod, g, which, o1=None):
    T, D = x.shape
    tr = _div(T, ROW_TILE, SUBLANE)

    def body(*refs):
        if o1 is None:
            x_ref, mod_ref, g_ref, h_ref = refs
            xv = x_ref[...]
        else:
            x_ref, o1_ref, mod_ref, g_ref, h_ref = refs
            xv = x_ref[...] + mod_ref[:, 2 * D:3 * D] * o1_ref[...]
        shift = mod_ref[:, (3 * which) * D:(3 * which + 1) * D]
        scale = mod_ref[:, (3 * which + 1) * D:(3 * which + 2) * D]
        r = lax.rsqrt(_rowmean(xv * xv) + EPS)
        h_ref[...] = ((xv * r) * g_ref[...] * (1.0 + scale) + shift).astype(BF16)

    row = pl.BlockSpec((tr, D), lambda i: (i, 0))
    ins = [x] + ([o1] if o1 is not None else []) + [mod, g]
    specs = [row] * (1 if o1 is None else 2) + [_full(mod.shape), _full(g.shape)]
    return _row_call(name, body, (T // tr,), specs, row, jax.ShapeDtypeStruct((T, D), BF16), sem=("parallel",))(*ins)


def mix_a_fwd(proj, cw, cb, lg, lb, comm=None):
    T = proj.shape[0]
    K, Dc = cw.shape
    tr = _div(T, ROW_TILE, CONV_HALO)
    hb = tr // CONV_HALO

    def body(val_ref, gate_ref, hval_ref, hgate_ref, cw_ref, cb_ref, lg_ref, lb_ref, ac_ref, asw_ref, buf):
        i = pl.program_id(0)
        hist = hval_ref[...] * _sig(hgate_ref[...])
        buf[0:CONV_HALO, :] = jnp.where(i > 0, hist, 0.0)
        buf[CONV_HALO:CONV_HALO + tr, :] = val_ref[...] * _sig(gate_ref[...])
        base = CONV_HALO - (K - 1)
        acc = jnp.zeros((tr, Dc), F32)
        for k in range(K):
            acc = acc + buf[base + k:base + k + tr, :] * cw_ref[k:k + 1, :]
        ac = acc + cb_ref[...]
        ac_ref[...] = ac
        mu = _rowmean(ac)
        cen = ac - mu
        y = cen * lax.rsqrt(_rowmean(cen * cen) + EPS)
        aln = y * lg_ref[...] + lb_ref[...]
        asw_ref[...] = (aln * _sig(aln)).astype(BF16)

    def halo(col):
        return pl.BlockSpec((CONV_HALO, Dc), lambda i: (jnp.maximum(i * hb - 1, 0), col))

    row = pl.BlockSpec((tr, Dc), lambda i: (i, 0))
    return _row_call(
        "mix_a_fwd", body, (T // tr,),
        [row, pl.BlockSpec((tr, Dc), lambda i: (i, 1)), halo(0), halo(1),
         _full(cw.shape), _full(cb.shape), _full(lg.shape), _full(lb.shape)],
        [row, row], [jax.ShapeDtypeStruct((T, Dc), F32), jax.ShapeDtypeStruct((T, Dc), BF16)],
        scratch=[pltpu.VMEM((CONV_HALO + tr, Dc), F32)], sem=("parallel",), comm=comm)(proj, proj, proj, proj, cw, cb, lg, lb)


def _spatial_mask():
    t = lax.broadcasted_iota(jnp.int32, (CHUNK, CHUNK), 0)
    s = lax.broadcasted_iota(jnp.int32, (CHUNK, CHUNK), 1)
    return s <= t


def mix_b_fwd(proj, lg, lb, wsp, bsp_t):
    T = proj.shape[0]
    Ds = lg.shape[-1]
    G = wsp.shape[0]
    hd = Ds // G
    tr = _div(T, ROW_TILE, CHUNK)

    def body(u_ref, v_ref, lg_ref, lb_ref, w_ref, b_ref, uv_ref, vs):
        v = _gelu(v_ref[...])
        mu = _rowmean(v)
        cen = v - mu
        vln = (cen * lax.rsqrt(_rowmean(cen * cen) + EPS) * lg_ref[...] + lb_ref[...]).astype(BF16)
        mask = _spatial_mask()
        for g in range(G):
            wg = jnp.where(mask, w_ref[g], 0.0).astype(BF16)
            for n in range(tr // CHUNK):
                rows, cols = slice(n * CHUNK, (n + 1) * CHUNK), slice(g * hd, (g + 1) * hd)
                vs[rows, cols] = jnp.dot(wg, vln[rows, cols], preferred_element_type=F32) + b_ref[:, g:g + 1]
        uv_ref[...] = (_gelu(u_ref[...]) * vs[...]).astype(BF16)

    return _row_call(
        "mix_b_fwd", body, (T // tr,),
        [pl.BlockSpec((tr, Ds), lambda i: (i, 2)), pl.BlockSpec((tr, Ds), lambda i: (i, 3)),
         _full(lg.shape), _full(lb.shape), _full(wsp.shape), _full(bsp_t.shape)],
        pl.BlockSpec((tr, Ds), lambda i: (i, 0)), jax.ShapeDtypeStruct((T, Ds), BF16),
        scratch=[pltpu.VMEM((tr, Ds), F32)], sem=("parallel",))(proj, proj, lg, lb, wsp, bsp_t)


def merge_fwd(proj, y_a, y_b):
    T, D = y_a.shape
    tr = _div(T, ROW_TILE, SUBLANE)

    def body(g_ref, ya_ref, yb_ref, o_ref):
        o_ref[...] = (_sig(g_ref[:, 0:D]) * ya_ref[...] + _sig(g_ref[:, D:2 * D]) * yb_ref[...]).astype(BF16)

    row = pl.BlockSpec((tr, D), lambda i: (i, 0))
    return _row_call("merge_fwd", body, (T // tr,), [pl.BlockSpec((tr, 2 * D), lambda i: (i, 1)), row, row], row,
                     jax.ShapeDtypeStruct((T, D), BF16), sem=("parallel",))(proj, y_a, y_b)


def _conv3(buf, w_ref, h, tr):
    return (buf[h, FFN_HALO - 2:FFN_HALO - 2 + tr, :] * w_ref[h, 0:1, :]
            + buf[h, FFN_HALO - 1:FFN_HALO - 1 + tr, :] * w_ref[h, 1:2, :]
            + buf[h, FFN_HALO:FFN_HALO + tr, :] * w_ref[h, 2:3, :])


def ffn_act_fwd(upre, fw, fb):
    _, T, F = upre.shape
    tr = _div(T, ROW_TILE, FFN_HALO)
    cb = _div(F, MM_WIDE)
    hb = tr // FFN_HALO

    def body(x_ref, h_ref, w_ref, b_ref, f_ref, buf):
        i = pl.program_id(0)
        buf[:, 0:FFN_HALO, :] = jnp.where(i > 0, h_ref[...], 0.0)
        buf[:, FFN_HALO:FFN_HALO + tr, :] = x_ref[...]
        val = _conv3(buf, w_ref, 0, tr) + b_ref[0]
        gt = _conv3(buf, w_ref, 1, tr) + b_ref[1]
        f_ref[...] = (gt * _sig(gt) * val).astype(BF16)

    return _row_call(
        "ffn_act_fwd", body, (T // tr, F // cb),
        [pl.BlockSpec((2, tr, cb), lambda i, j: (0, i, j)),
         pl.BlockSpec((2, FFN_HALO, cb), lambda i, j: (0, jnp.maximum(i * hb - 1, 0), j)),
         pl.BlockSpec((2, 3, cb), lambda i, j: (0, 0, j)), pl.BlockSpec((2, 1, cb), lambda i, j: (0, 0, j))],
        pl.BlockSpec((tr, cb), lambda i, j: (i, j)), jax.ShapeDtypeStruct((T, F), BF16),
        scratch=[pltpu.VMEM((2, FFN_HALO + tr, cb), F32)], sem=("parallel", "parallel"))(upre, upre, fw, fb)


def final_fwd_bwd(x, o1, o2, mod, gf, target):
    T, D = x.shape
    tr = _div(T, ROW_TILE, SUBLANE)
    nt = T // tr

    def body(x_ref, o1_ref, o2_ref, mod_ref, gf_ref, t_ref, dx3_ref, do2_ref, st_ref):
        i = pl.program_id(0)
        gate1, gate2 = mod_ref[:, 2 * D:3 * D], mod_ref[:, 5 * D:6 * D]
        o2v = o2_ref[...]
        x3 = x_ref[...] + gate1 * o1_ref[...] + gate2 * o2v
        r = lax.rsqrt(_rowmean(x3 * x3) + EPS)
        xn = x3 * r
        err = xn * gf_ref[...] - t_ref[...]
        dy = err * (1.0 / D)
        dxn = dy * gf_ref[...]
        dx3 = r * (dxn - xn * _rowmean(dxn * xn))
        dx3_ref[...] = dx3
        do2_ref[...] = (dx3 * gate2).astype(BF16)

        @pl.when(i == 0)
        def _():
            st_ref[...] = jnp.zeros_like(st_ref)

        st_ref[0:1, :] += _colsum(dy * xn)
        st_ref[1:2, :] += _colsum(dx3 * o2v)
        st_ref[2:3, :] += _colsum(err * err) * (0.5 / D)

        @pl.when(i == nt - 1)
        def _():
            st_ref[3:4, :] = jnp.zeros((1, D), F32) + jnp.sum(st_ref[2:3, :])

    row = pl.BlockSpec((tr, D), lambda i: (i, 0))
    return _row_call(
        "final_fwd_bwd", body, (nt,), [row, row, row, _full(mod.shape), _full(gf.shape), row],
        [row, row, _full((8, D))],
        [jax.ShapeDtypeStruct((T, D), F32), jax.ShapeDtypeStruct((T, D), BF16), jax.ShapeDtypeStruct((8, D), F32)],
    )(x, o1, o2, mod, gf, target)


def ffn_act_bwd(upre, df, fw, fb):
    _, T, F = upre.shape
    tr = _div(T, ROW_TILE, FFN_HALO)
    cb = _div(F, MM_WIDE)
    hb = tr // FFN_HALO
    nt = T // tr

    def body(x_ref, h_ref, df_ref, w_ref, b_ref, dpre_ref, dw_ref, db_ref, buf, dbuf, carry):
        i = pl.program_id(1)
        ri = nt - 1 - i
        buf[:, 0:FFN_HALO, :] = jnp.where(ri > 0, h_ref[...], 0.0)
        buf[:, FFN_HALO:FFN_HALO + tr, :] = x_ref[...]
        val = _conv3(buf, w_ref, 0, tr) + b_ref[0]
        gt = _conv3(buf, w_ref, 1, tr) + b_ref[1]
        sg = _sig(gt)
        dfv = df_ref[...]
        dup = (dfv * (gt * sg), dfv * val * (sg * (1.0 + gt * (1.0 - sg))))

        @pl.when(i == 0)
        def _():
            carry[...] = jnp.zeros_like(carry)
            dw_ref[...] = jnp.zeros_like(dw_ref)
            db_ref[...] = jnp.zeros_like(db_ref)

        for h in range(2):
            db_ref[h] += _colsum(dup[h])
            for k in range(3):
                dw_ref[h, k:k + 1, :] += _colsum(dup[h] * buf[h, FFN_HALO - 2 + k:FFN_HALO - 2 + k + tr, :])
            dbuf[h, 0:tr, :] = dup[h]
            dbuf[h, tr:tr + FFN_HALO, :] = carry[h]
            dpre_ref[h] = (dbuf[h, 0:tr, :] * w_ref[h, 2:3, :] + dbuf[h, 1:tr + 1, :] * w_ref[h, 1:2, :]
                           + dbuf[h, 2:tr + 2, :] * w_ref[h, 0:1, :]).astype(BF16)
            carry[h] = dup[h][0:FFN_HALO, :]

    return _row_call(
        "ffn_act_bwd", body, (F // cb, nt),
        [pl.BlockSpec((2, tr, cb), lambda j, i: (0, nt - 1 - i, j)),
         pl.BlockSpec((2, FFN_HALO, cb), lambda j, i: (0, jnp.maximum((nt - 1 - i) * hb - 1, 0), j)),
         pl.BlockSpec((tr, cb), lambda j, i: (nt - 1 - i, j)),
         pl.BlockSpec((2, 3, cb), lambda j, i: (0, 0, j)), pl.BlockSpec((2, 1, cb), lambda j, i: (0, 0, j))],
        [pl.BlockSpec((2, tr, cb), lambda j, i: (0, nt - 1 - i, j)),
         pl.BlockSpec((2, 3, cb), lambda j, i: (0, 0, j)), pl.BlockSpec((2, 1, cb), lambda j, i: (0, 0, j))],
        [jax.ShapeDtypeStruct((2, T, F), BF16), jax.ShapeDtypeStruct((2, 3, F), F32), jax.ShapeDtypeStruct((2, 1, F), F32)],
        scratch=[pltpu.VMEM((2, FFN_HALO + tr, cb), F32), pltpu.VMEM((2, tr + FFN_HALO, cb), F32),
                 pltpu.VMEM((2, FFN_HALO, cb), F32)],
        sem=("parallel", "arbitrary"))(upre, upre, df, fw, fb)


def norm2_bwd(dh2, x, o1, dx3, mod, g2, comm=None):
    T, D = x.shape
    tr = _div(T, ROW_TILE, SUBLANE)

    def body(dh_ref, x_ref, o1_ref, dx3_ref, mod_ref, g_ref, dx2_ref, do1_ref, st_ref):
        i = pl.program_id(0)
        gate1, scale = mod_ref[:, 2 * D:3 * D], mod_ref[:, 4 * D:5 * D]
        o1v = o1_ref[...]
        x2 = x_ref[...] + gate1 * o1v
        r = lax.rsqrt(_rowmean(x2 * x2) + EPS)
        xn = x2 * r
        dh = dh_ref[...]
        dxn = dh * (g_ref[...] * (1.0 + scale))
        dx2 = r * (dxn - xn * _rowmean(dxn * xn)) + dx3_ref[...]
        dx2_ref[...] = dx2
        do1_ref[...] = (dx2 * gate1).astype(BF16)

        @pl.when(i == 0)
        def _():
            st_ref[...] = jnp.zeros_like(st_ref)

        st_ref[0:1, :] += _colsum(dh)
        st_ref[1:2, :] += _colsum(dh * xn) * g_ref[...]
        st_ref[2:3, :] += _colsum(dh * xn) * (1.0 + scale)
        st_ref[3:4, :] += _colsum(dx2 * o1v)

    row = pl.BlockSpec((tr, D), lambda i: (i, 0))
    return _row_call(
        "norm2_bwd", body, (T // tr,), [row, row, row, row, _full(mod.shape), _full(g2.shape)],
        [row, row, _full((8, D))],
        [jax.ShapeDtypeStruct((T, D), F32), jax.ShapeDtypeStruct((T, D), BF16), jax.ShapeDtypeStruct((8, D), F32)],
        comm=comm)(dh2, x, o1, dx3, mod, g2)


def merge_bwd(dmerged, proj, y_a, y_b):
    T, D = y_a.shape
    tr = _div(T, ROW_TILE, SUBLANE)

    def body(dm_ref, g_ref, ya_ref, yb_ref, dya_ref, dyb_ref, dp_ref, db_ref):
        i = pl.program_id(0)
        dm = dm_ref[...]
        sa, sb = _sig(g_ref[:, 0:D]), _sig(g_ref[:, D:2 * D])
        dya_ref[...] = (dm * sa).astype(BF16)
        dyb_ref[...] = (dm * sb).astype(BF16)
        dga = dm * ya_ref[...] * (sa * (1.0 - sa))
        dgb = dm * yb_ref[...] * (sb * (1.0 - sb))
        dp_ref[:, 0:D] = dga.astype(BF16)
        dp_ref[:, D:2 * D] = dgb.astype(BF16)

        @pl.when(i == 0)
        def _():
            db_ref[...] = jnp.zeros_like(db_ref)

        db_ref[:, 0:D] += _colsum(dga)
        db_ref[:, D:2 * D] += _colsum(dgb)

    row = pl.BlockSpec((tr, D), lambda i: (i, 0))
    wide = pl.BlockSpec((tr, 2 * D), lambda i: (i, 1))
    return _row_call(
        "merge_bwd", body, (T // tr,), [row, wide, row, row], [row, row, wide, _full((1, 2 * D))],
        [jax.ShapeDtypeStruct((T, D), BF16), jax.ShapeDtypeStruct((T, D), BF16),
         jax.ShapeDtypeStruct((T, 4 * D), BF16), jax.ShapeDtypeStruct((1, 2 * D), F32)],
    )(dmerged, proj, y_a, y_b)


def mix_a_bwd(dasw, ac, proj, dproj, cw, lg, lb, comm=None):
    T, Dc = ac.shape
    K = cw.shape[0]
    tr = _div(T, ROW_TILE, CONV_HALO)
    hb = tr // CONV_HALO
    nt = T // tr

    def body(dasw_ref, ac_ref, in_ref, hin_ref, dp_hbm, cw_ref, lg_ref, lb_ref,
             dp_ref, st_ref, dcw_ref, db_ref, abuf, dbuf, carry):
        del dp_hbm
        i = pl.program_id(0)
        ri = nt - 1 - i
        acv = ac_ref[...]
        mu = _rowmean(acv)
        cen = acv - mu
        rstd = lax.rsqrt(_rowmean(cen * cen) + EPS)
        y = cen * rstd
        aln = y * lg_ref[...] + lb_ref[...]
        sg = _sig(aln)
        daln = dasw_ref[...] * (sg * (1.0 + aln * (1.0 - sg)))
        dy = daln * lg_ref[...]
        dac = rstd * (dy - _rowmean(dy) - y * _rowmean(dy * y))

        @pl.when(i == 0)
        def _():
            carry[...] = jnp.zeros_like(carry)
            st_ref[...] = jnp.zeros_like(st_ref)
            dcw_ref[...] = jnp.zeros_like(dcw_ref)
            db_ref[...] = jnp.zeros_like(db_ref)

        st_ref[0:1, :] += _colsum(daln * y)
        st_ref[1:2, :] += _colsum(daln)
        st_ref[2:3, :] += _colsum(dac)
        val, gate = in_ref[:, 0:Dc], in_ref[:, Dc:2 * Dc]
        sgg = _sig(gate)
        hist = hin_ref[:, 0:Dc] * _sig(hin_ref[:, Dc:2 * Dc])
        abuf[0:CONV_HALO, :] = jnp.where(ri > 0, hist, 0.0)
        abuf[CONV_HALO:CONV_HALO + tr, :] = val * sgg
        dbuf[0:tr, :] = dac
        dbuf[tr:tr + CONV_HALO, :] = carry[...]
        base = CONV_HALO - (K - 1)
        da = jnp.zeros((tr, Dc), F32)
        for k in range(K):
            dcw_ref[k:k + 1, :] += _colsum(dac * abuf[base + k:base + k + tr, :])
            da = da + dbuf[K - 1 - k:K - 1 - k + tr, :] * cw_ref[k:k + 1, :]
        carry[...] = dac[0:CONV_HALO, :]
        dval = da * sgg
        dgate = da * val * (sgg * (1.0 - sgg))
        dp_ref[:, 0:Dc] = dval.astype(BF16)
        dp_ref[:, Dc:2 * Dc] = dgate.astype(BF16)
        db_ref[:, 0:Dc] += _colsum(dval)
        db_ref[:, Dc:2 * Dc] += _colsum(dgate)

    row = pl.BlockSpec((tr, Dc), lambda i: (nt - 1 - i, 0))
    wide = pl.BlockSpec((tr, 2 * Dc), lambda i: (nt - 1 - i, 0))
    return _row_call(
        "mix_a_bwd", body, (nt,),
        [row, row, wide, pl.BlockSpec((CONV_HALO, 2 * Dc), lambda i: (jnp.maximum((nt - 1 - i) * hb - 1, 0), 0)),
         pl.BlockSpec(memory_space=pl.ANY), _full(cw.shape), _full(lg.shape), _full(lb.shape)],
        [wide, _full((8, Dc)), _full((CONV_HALO, Dc)), _full((1, 2 * Dc))],
        [jax.ShapeDtypeStruct(dproj.shape, BF16), jax.ShapeDtypeStruct((8, Dc), F32),
         jax.ShapeDtypeStruct((CONV_HALO, Dc), F32), jax.ShapeDtypeStruct((1, 2 * Dc), F32)],
        scratch=[pltpu.VMEM((CONV_HALO + tr, Dc), F32), pltpu.VMEM((tr + CONV_HALO, Dc), F32),
                 pltpu.VMEM((CONV_HALO, Dc), F32)],
        aliases={4: 0}, comm=comm)(dasw, ac, proj, proj, dproj, cw, lg, lb)


def mix_b_bwd(duv, proj, dproj, lg, lb, wsp, bsp_t, comm=None):
    T, Ds = duv.shape
    G = wsp.shape[0]
    hd = Ds // G
    tr = _div(T, ROW_TILE, CHUNK)
    nt = T // tr

    def body(duv_ref, s_ref, dp_hbm, lg_ref, lb_ref, w_ref, b_ref,
             dp_ref, st_ref, dws_ref, dbs_ref, db_ref, vs, dvln):
        del dp_hbm
        i = pl.program_id(0)

        @pl.when(i == 0)
        def _():
            st_ref[...] = jnp.zeros_like(st_ref)
            dws_ref[...] = jnp.zeros_like(dws_ref)
            dbs_ref[...] = jnp.zeros_like(dbs_ref)
            db_ref[...] = jnp.zeros_like(db_ref)

        upre, vpre = s_ref[:, 0:Ds], s_ref[:, Ds:2 * Ds]
        u, v = _gelu(upre), _gelu(vpre)
        mu = _rowmean(v)
        cen = v - mu
        rstd = lax.rsqrt(_rowmean(cen * cen) + EPS)
        yv = cen * rstd
        vln = (yv * lg_ref[...] + lb_ref[...]).astype(BF16)
        duvv = duv_ref[...]
        dvs = duvv * u
        dvs_b = dvs.astype(BF16)
        mask = _spatial_mask()
        for g in range(G):
            wg = jnp.where(mask, w_ref[g], 0.0).astype(BF16)
            cols = slice(g * hd, (g + 1) * hd)
            dws = jnp.zeros((CHUNK, CHUNK), F32)
            dbs = jnp.zeros((CHUNK, 1), F32)
            for n in range(tr // CHUNK):
                rows = slice(n * CHUNK, (n + 1) * CHUNK)
                vs[rows, cols] = jnp.dot(wg, vln[rows, cols], preferred_element_type=F32) + b_ref[:, g:g + 1]
                dvln[rows, cols] = lax.dot_general(wg, dvs_b[rows, cols], (((0,), (0,)), ((), ())),
                                                   preferred_element_type=F32)
                dws = dws + lax.dot_general(dvs_b[rows, cols], vln[rows, cols], (((1,), (1,)), ((), ())),
                                            preferred_element_type=F32)
                dbs = dbs + jnp.sum(dvs[rows, cols], axis=1, keepdims=True)
            dws_ref[g] += jnp.where(mask, dws, 0.0)
            dbs_ref[:, g:g + 1] += dbs
        dvl = dvln[...]
        st_ref[0:1, :] += _colsum(dvl * yv)
        st_ref[1:2, :] += _colsum(dvl)
        dyv = dvl * lg_ref[...]
        dv = rstd * (dyv - _rowmean(dyv) - yv * _rowmean(dyv * yv))
        dupre = duvv * vs[...] * _gelu_grad(upre)
        dvpre = dv * _gelu_grad(vpre)
        dp_ref[:, 0:Ds] = dupre.astype(BF16)
        dp_ref[:, Ds:2 * Ds] = dvpre.astype(BF16)
        db_ref[:, 0:Ds] += _colsum(dupre)
        db_ref[:, Ds:2 * Ds] += _colsum(dvpre)

    wide = pl.BlockSpec((tr, 2 * Ds), lambda i: (i, 1))
    return _row_call(
        "mix_b_bwd", body, (nt,),
        [pl.BlockSpec((tr, Ds), lambda i: (i, 0)), wide, pl.BlockSpec(memory_space=pl.ANY),
         _full(lg.shape), _full(lb.shape), _full(wsp.shape), _full(bsp_t.shape)],
        [wide, _full((8, Ds)), _full(wsp.shape), _full(bsp_t.shape), _full((1, 2 * Ds))],
        [jax.ShapeDtypeStruct(dproj.shape, BF16), jax.ShapeDtypeStruct((8, Ds), F32),
         jax.ShapeDtypeStruct(wsp.shape, F32), jax.ShapeDtypeStruct(bsp_t.shape, F32),
         jax.ShapeDtypeStruct((1, 2 * Ds), F32)],
        scratch=[pltpu.VMEM((tr, Ds), F32), pltpu.VMEM((tr, Ds), F32)],
        aliases={2: 0}, comm=comm)(duv, proj, dproj, lg, lb, wsp, bsp_t)


def norm1_bwd(dh1, x, dx2, mod, g1):
    T, D = x.shape
    tr = _div(T, ROW_TILE, SUBLANE)

    def body(dh_ref, x_ref, dx2_ref, mod_ref, g_ref, gx_ref, st_ref):
        i = pl.program_id(0)
        scale = mod_ref[:, D:2 * D]
        xv = x_ref[...]
        r = lax.rsqrt(_rowmean(xv * xv) + EPS)
        xn = xv * r
        dh = dh_ref[...]
        dxn = dh * (g_ref[...] * (1.0 + scale))
        gx_ref[...] = r * (dxn - xn * _rowmean(dxn * xn)) + dx2_ref[...]

        @pl.when(i == 0)
        def _():
            st_ref[...] = jnp.zeros_like(st_ref)

        st_ref[0:1, :] += _colsum(dh)
        st_ref[1:2, :] += _colsum(dh * xn) * g_ref[...]
        st_ref[2:3, :] += _colsum(dh * xn) * (1.0 + scale)

    row = pl.BlockSpec((tr, D), lambda i: (i, 0))
    return _row_call(
        "norm1_bwd", body, (T // tr,), [row, row, row, _full(mod.shape), _full(g1.shape)], [row, _full((8, D))],
        [jax.ShapeDtypeStruct((T, D), F32), jax.ShapeDtypeStruct((8, D), F32)])(dh1, x, dx2, mod, g1)


def ada_fwd_local(c_all, w_ada, b_cols):
    B, D = c_all.shape
    Na = w_ada.shape[1]
    tn = _div(Na, 512)

    def body(c_ref, w_ref, b_ref, o_ref):
        cv = c_ref[...]
        act = (cv * _sig(cv)).astype(BF16)
        o_ref[...] = jnp.dot(act, w_ref[...].astype(BF16), preferred_element_type=F32) + b_ref[...]

    return _row_call(
        "ada_fwd_local", body, (Na // tn,),
        [_full(c_all.shape), pl.BlockSpec((D, tn), lambda j: (0, j)), pl.BlockSpec((1, tn), lambda j: (0, j))],
        pl.BlockSpec((B, tn), lambda j: (0, j)), jax.ShapeDtypeStruct((B, Na), F32), sem=("parallel",))(c_all, w_ada, b_cols)


def ada_bwd_local(c_all_t, dmod_all):
    D, B = c_all_t.shape
    Na = dmod_all.shape[1]
    tr = _div(D, 512, SUBLANE)

    def body(c_ref, d_ref, o_ref):
        cv = c_ref[...]
        act = cv * _sig(cv)
        acc = act[:, 0:1] * d_ref[0:1, :]
        for b in range(1, B):
            acc = acc + act[:, b:b + 1] * d_ref[b:b + 1, :]
        o_ref[...] = acc

    return _row_call(
        "ada_bwd_local", body, (D // tr,), [pl.BlockSpec((tr, B), lambda i: (i, 0)), _full(dmod_all.shape)],
        pl.BlockSpec((tr, Na), lambda i: (i, 0)), jax.ShapeDtypeStruct((D, Na), F32), sem=("parallel",))(c_all_t, dmod_all)


def adamw(name, w, m, v, parts):
    R, C = w.shape
    tr = _div(R, max(SUBLANE, (1 << 18) // C // SUBLANE * SUBLANE), SUBLANE)
    bc1, bc2 = 1.0 - ADAM_B1 ** ADAM_STEP, 1.0 - ADAM_B2 ** ADAM_STEP
    n = len(parts)

    def body(*refs):
        w_ref, m_ref, v_ref = refs[:3]
        g_ref, d_ref, nm_ref, nv_ref = refs[3 + n:]
        g = refs[3][...].astype(F32)
        for p in refs[4:3 + n]:
            g = g + p[...].astype(F32)
        mn = ADAM_B1 * m_ref[...] + (1.0 - ADAM_B1) * g
        vn = ADAM_B2 * v_ref[...] + (1.0 - ADAM_B2) * (g * g)
        g_ref[...] = g
        nm_ref[...] = mn
        nv_ref[...] = vn
        d_ref[...] = -ADAM_LR * ((mn / bc1) / (jnp.sqrt(vn / bc2) + ADAM_EPS) + ADAM_WD * w_ref[...])

    row = pl.BlockSpec((tr, C), lambda i: (i, 0))
    pspecs = [row if lead is None else pl.BlockSpec((None, tr, C), lambda i, lead=lead: (lead, i, 0)) for _, lead in parts]
    out = jax.ShapeDtypeStruct((R, C), F32)
    return _row_call(name, body, (R // tr,), [row, row, row] + pspecs, [row] * 4, [out] * 4, sem=("parallel",))(
        w, m, v, *[a for a, _ in parts])


def pair_add(name, g, r, idx):
    _, R, C = g.shape
    tr = _div(R, max(SUBLANE, (1 << 17) // C // SUBLANE * SUBLANE), SUBLANE)

    def body(idx_ref, g0, g1, g2, g3, r_ref, own_ref, tr_ref):
        del idx_ref
        own_ref[...] = g0[...] + r_ref[0]
        for m, gm in ((1, g1), (2, g2), (3, g3)):
            tr_ref[m - 1] = (gm[...] + r_ref[m]).astype(BF16)

    def gspec(m):
        return pl.BlockSpec((None, tr, C), lambda i, idx_ref: (idx_ref[m], i, 0))

    return pl.pallas_call(
        body, name=name,
        grid_spec=pltpu.PrefetchScalarGridSpec(
            num_scalar_prefetch=1, grid=(R // tr,),
            in_specs=[gspec(0), gspec(1), gspec(2), gspec(3), pl.BlockSpec((4, tr, C), lambda i, idx_ref: (0, i, 0))],
            out_specs=[pl.BlockSpec((tr, C), lambda i, idx_ref: (i, 0)),
                       pl.BlockSpec((3, tr, C), lambda i, idx_ref: (0, i, 0))]),
        out_shape=[jax.ShapeDtypeStruct((R, C), F32), jax.ShapeDtypeStruct((3, R, C), BF16)],
        compiler_params=_cp(("parallel",)))(idx, g, g, g, g, r)


def sum_rows(name, stacked):
    nb, _, N = stacked.shape
    tn = _div(N, 1 << 16)

    def body(s_ref, o_ref):
        acc = s_ref[0]
        for b in range(1, nb):
            acc = acc + s_ref[b]
        o_ref[...] = acc

    return _row_call(name, body, (N // tn,), [pl.BlockSpec((nb, 1, tn), lambda j: (0, 0, j))],
                     pl.BlockSpec((1, tn), lambda j: (0, j)), jax.ShapeDtypeStruct((1, N), F32), sem=("parallel",))(stacked)


def _coords():
    return lax.axis_index("x"), lax.axis_index("y"), lax.axis_index("c")


def _flip(v, bit):
    return 1 - v if bit else v


def _comm_call(name, body, ins, out_shapes, n_sems):
    any_spec = pl.BlockSpec(memory_space=pl.ANY)
    return pl.pallas_call(
        body, name=name, in_specs=[any_spec] * len(ins), out_specs=[any_spec] * len(out_shapes), out_shape=out_shapes,
        scratch_shapes=[pltpu.SemaphoreType.DMA((s,)) for s in n_sems],
        compiler_params=pltpu.CompilerParams(has_side_effects=True))(*ins)


def gather_rows(name, row):
    N = row.shape[1]

    def body(row_ref, out_ref, send_sems, recv_sems, local_sem):
        x, y, c = _coords()
        me = 4 * x + 2 * y + c
        mine = pltpu.make_async_copy(row_ref, out_ref.at[me], local_sem.at[0])
        mine.start()
        copies = []
        for k in range(1, NDEV):
            peer = (_flip(x, k & 4), _flip(y, k & 2), _flip(c, k & 1))
            copies.append(pltpu.make_async_remote_copy(
                src_ref=row_ref, dst_ref=out_ref.at[me], send_sem=send_sems.at[k - 1], recv_sem=recv_sems.at[k - 1],
                device_id=peer, device_id_type=MESH))
        for cp in copies:
            cp.start()
        for cp in copies:
            cp.wait_recv()
        for cp in copies:
            cp.wait_send()
        mine.wait()

    return _comm_call(name, body, [row], [jax.ShapeDtypeStruct((NDEV, 1, N), row.dtype)], (NDEV - 1, NDEV - 1, 1))[0]


def exchange_rows(name, slabs):
    def body(in_ref, out_ref, send_sems, recv_sems, local_sem):
        x, y, c = _coords()
        me = 4 * x + 2 * y + c
        mine = pltpu.make_async_copy(in_ref.at[me], out_ref.at[me], local_sem.at[0])
        mine.start()
        copies = []
        for k in range(1, NDEV):
            px, py, pc = _flip(x, k & 4), _flip(y, k & 2), _flip(c, k & 1)
            copies.append(pltpu.make_async_remote_copy(
                src_ref=in_ref.at[4 * px + 2 * py + pc], dst_ref=out_ref.at[me], send_sem=send_sems.at[k - 1],
                recv_sem=recv_sems.at[k - 1], device_id=(px, py, pc), device_id_type=MESH))
        for cp in copies:
            cp.start()
        for cp in copies:
            cp.wait_recv()
        for cp in copies:
            cp.wait_send()
        mine.wait()

    return _comm_call(name, body, [slabs], [jax.ShapeDtypeStruct(slabs.shape, slabs.dtype)], (NDEV - 1, NDEV - 1, 1))[0]


def _run_comm(name, comm):
    n_i, n_o = len(comm.ins), len(comm.out_shapes)

    def body(*refs):
        start, finish = comm.plan(refs[:n_i], refs[n_i:n_i + n_o], refs[n_i + n_o:])
        start()
        finish()

    return _comm_call(name, body, comm.ins, comm.out_shapes, comm.n_sems)


def gather_comm(shards):
    L = len(shards)

    def plan(ins, outs, sems):
        send_sems, recv_sems, local_sems = sems
        x, y, c = _coords()
        sibling = (x, y, 1 - c)
        chips = [(_flip(x, m & 2), _flip(y, m & 1)) for m in (1, 2, 3)]

        def slab(px, py, pc):
            return 4 * px + 2 * py + pc

        def copy(l, k, block, to, src=None):
            dst = outs[l].at[slab(*block)]
            return pltpu.make_async_remote_copy(
                src_ref=dst if src is None else src, dst_ref=dst, send_sem=send_sems.at[7 * l + k],
                recv_sem=recv_sems.at[7 * l + k], device_id=to, device_id_type=MESH)

        local = [pltpu.make_async_copy(ins[l], outs[l].at[slab(x, y, c)], local_sems.at[l]) for l in range(L)]
        first = []
        for l in range(L):
            first.append(copy(l, 0, (x, y, c), sibling, src=ins[l]))
            first += [copy(l, 1 + j, (x, y, c), (*chip, c), src=ins[l]) for j, chip in enumerate(chips)]

        def start():
            for cp in local + first:
                cp.start()

        def finish():
            passed = []
            for j, chip in enumerate(chips):
                for l in range(L):
                    copy(l, 1 + j, (*chip, c), (x, y, c)).wait_recv()
                    fwd = copy(l, 4 + j, (*chip, c), sibling)
                    fwd.start()
                    passed.append(fwd)
            for l in range(L):
                copy(l, 0, sibling, (x, y, c)).wait_recv()
                for j, chip in enumerate(chips):
                    copy(l, 4 + j, (*chip, 1 - c), (x, y, c)).wait_recv()
            for cp in first + passed:
                cp.wait_send()
            for cp in local:
                cp.wait()

        return start, finish

    outs = [jax.ShapeDtypeStruct((NDEV,) + s.shape, s.dtype) for s in shards]
    return Comm(plan, shards, outs, (7 * L, 7 * L, L))


def _all_at_once(copies):
    def start():
        for cp in copies:
            cp.start()

    def finish():
        for cp in copies:
            cp.wait_recv()
        for cp in copies:
            cp.wait_send()

    return start, finish


def sibling_comm(grads):
    L = len(grads)

    def plan(ins, outs, sems):
        send_sems, recv_sems = sems
        x, y, c = _coords()
        copies = []
        for l in range(L):
            for m in range(4):
                qm = 2 * _flip(x, m & 2) + _flip(y, m & 1)
                copies.append(pltpu.make_async_remote_copy(
                    src_ref=ins[l].at[2 * qm + (1 - c)], dst_ref=outs[l].at[m], send_sem=send_sems.at[4 * l + m],
                    recv_sem=recv_sems.at[4 * l + m], device_id=(x, y, 1 - c), device_id_type=MESH))
        return _all_at_once(copies)

    outs = [jax.ShapeDtypeStruct((4,) + g.shape[1:], g.dtype) for g in grads]
    return Comm(plan, grads, outs, (4 * L, 4 * L))


def chip_comm(transits):
    L = len(transits)

    def plan(ins, outs, sems):
        send_sems, recv_sems = sems
        x, y, c = _coords()
        copies = []
        for l in range(L):
            for m in (1, 2, 3):
                copies.append(pltpu.make_async_remote_copy(
                    src_ref=ins[l].at[m - 1], dst_ref=outs[l].at[m - 1], send_sem=send_sems.at[3 * l + m - 1],
                    recv_sem=recv_sems.at[3 * l + m - 1], device_id=(_flip(x, m & 2), _flip(y, m & 1), c),
                    device_id_type=MESH))
        return _all_at_once(copies)

    outs = [jax.ShapeDtypeStruct(t.shape, t.dtype) for t in transits]
    return Comm(plan, transits, outs, (3 * L, 3 * L))


_SMALL = ("b_ada", "norm1_g", "b_in", "conv_dw_b", "conv_ln_g", "conv_ln_b", "sgu_ln_g", "sgu_ln_b", "w_spatial",
          "b_spatial", "norm2_g", "ffn_dw_b", "final_g")
_SMALL_SHARDED = ("conv_dw_w", "ffn_dw_w")
_BIG = ("w_in", "w_conv_out", "w_sgu_out", "w_out", "w_up", "w_down")
_ORDER = ("w_ada", "b_ada", "norm1_g", "w_in", "b_in", "conv_dw_w", "conv_dw_b", "conv_ln_g", "conv_ln_b", "w_conv_out",
          "sgu_ln_g", "sgu_ln_b", "w_spatial", "b_spatial", "w_sgu_out", "w_out", "norm2_g", "w_up", "ffn_dw_w", "ffn_dw_b",
          "w_down", "final_g")


def _pack(arrays, mult):
    flat = jnp.concatenate([a.reshape(-1) for a in arrays])
    pad = (-flat.shape[0]) % mult
    return jnp.pad(flat, (0, pad)) if pad else flat


def kernel(x, c, w_ada, b_ada, norm1_g, w_in, b_in, conv_dw_w, conv_dw_b, conv_ln_g, conv_ln_b, w_conv_out, sgu_ln_g, sgu_ln_b, w_spatial, b_spatial, w_sgu_out, w_out, norm2_g, w_up, ffn_dw_w, ffn_dw_b, w_down, final_g, loss_target, m_w_ada, m_b_ada, m_norm1_g, m_w_in, m_b_in, m_conv_dw_w, m_conv_dw_b, m_conv_ln_g, m_conv_ln_b, m_w_conv_out, m_sgu_ln_g, m_sgu_ln_b, m_w_spatial, m_b_spatial, m_w_sgu_out, m_w_out, m_norm2_g, m_w_up, m_ffn_dw_w, m_ffn_dw_b, m_w_down, m_final_g, v_w_ada, v_b_ada, v_norm1_g, v_w_in, v_b_in, v_conv_dw_w, v_conv_dw_b, v_conv_ln_g, v_conv_ln_b, v_w_conv_out, v_sgu_ln_g, v_sgu_ln_b, v_w_spatial, v_b_spatial, v_w_sgu_out, v_w_out, v_norm2_g, v_w_up, v_ffn_dw_w, v_ffn_dw_b, v_w_down, v_final_g):
    W = dict(w_ada=w_ada, b_ada=b_ada, norm1_g=norm1_g, w_in=w_in, b_in=b_in, conv_dw_w=conv_dw_w, conv_dw_b=conv_dw_b,
             conv_ln_g=conv_ln_g, conv_ln_b=conv_ln_b, w_conv_out=w_conv_out, sgu_ln_g=sgu_ln_g, sgu_ln_b=sgu_ln_b,
             w_spatial=w_spatial, b_spatial=b_spatial, w_sgu_out=w_sgu_out, w_out=w_out, norm2_g=norm2_g, w_up=w_up,
             ffn_dw_w=ffn_dw_w, ffn_dw_b=ffn_dw_b, w_down=w_down, final_g=final_g)
    M = dict(w_ada=m_w_ada, b_ada=m_b_ada, norm1_g=m_norm1_g, w_in=m_w_in, b_in=m_b_in, conv_dw_w=m_conv_dw_w,
             conv_dw_b=m_conv_dw_b, conv_ln_g=m_conv_ln_g, conv_ln_b=m_conv_ln_b, w_conv_out=m_w_conv_out,
             sgu_ln_g=m_sgu_ln_g, sgu_ln_b=m_sgu_ln_b, w_spatial=m_w_spatial, b_spatial=m_b_spatial,
             w_sgu_out=m_w_sgu_out, w_out=m_w_out, norm2_g=m_norm2_g, w_up=m_w_up, ffn_dw_w=m_ffn_dw_w,
             ffn_dw_b=m_ffn_dw_b, w_down=m_w_down, final_g=m_final_g)
    V = dict(w_ada=v_w_ada, b_ada=v_b_ada, norm1_g=v_norm1_g, w_in=v_w_in, b_in=v_b_in, conv_dw_w=v_conv_dw_w,
             conv_dw_b=v_conv_dw_b, conv_ln_g=v_conv_ln_g, conv_ln_b=v_conv_ln_b, w_conv_out=v_w_conv_out,
             sgu_ln_g=v_sgu_ln_g, sgu_ln_b=v_sgu_ln_b, w_spatial=v_w_spatial, b_spatial=v_b_spatial,
             w_sgu_out=v_w_sgu_out, w_out=v_w_out, norm2_g=v_norm2_g, w_up=v_w_up, ffn_dw_w=v_ffn_dw_w,
             ffn_dw_b=v_ffn_dw_b, w_down=v_w_down, final_g=v_final_g)

    xs, tgt = x[0], loss_target[0]
    T, D = xs.shape
    Dc = conv_dw_w.shape[-1] * NDEV
    F = w_down.shape[1] * NDEV
    K31 = conv_dw_w.shape[1]
    G = w_spatial.shape[1]
    assert D == 2 * Dc and sgu_ln_g.shape[-1] == Dc and T % CHUNK == 0
    me = 4 * lax.axis_index("x") + 2 * lax.axis_index("y") + lax.axis_index("c")

    na = w_ada.shape[-1]
    c_all = gather_rows("gather_c", c).reshape(NDEV, D)
    b_cols = lax.dynamic_slice(b_ada, (0, me * na), (1, na))
    mod_cols = ada_fwd_local(c_all, w_ada[0], b_cols)
    mod = exchange_rows("exchange_mod", mod_cols.reshape(NDEV, 1, na)).reshape(1, NDEV * na)

    wbf = {k: W[k][0].astype(BF16) for k in _BIG}
    wb_in, cw_g, fw_g = _run_comm("gather_w_in", gather_comm([wbf["w_in"], conv_dw_w[0], ffn_dw_w[0]]))
    cw = jnp.transpose(cw_g, (1, 0, 2)).reshape(K31, Dc)
    fw = jnp.transpose(jnp.transpose(fw_g, (1, 0, 2)).reshape(3, 2, F), (1, 0, 2))
    fb = ffn_dw_b.reshape(2, 1, F)
    bsp_t = jnp.transpose(b_spatial[0])
    wsp = w_spatial[0]

    h1 = pre_norm("pre_norm1", xs, mod, norm1_g, 0)
    proj, (wb_co, wb_so, wb_out, wb_up) = mm_nn(
        "proj", h1, wb_in, F32, bias=b_in,
        comm=gather_comm([wbf["w_conv_out"], wbf["w_sgu_out"], wbf["w_out"], wbf["w_up"]]))
    wb_out = wb_out.reshape(1, D, D)
    (ac, asw), (wb_down,) = mix_a_fwd(proj, cw, conv_dw_b, conv_ln_g, conv_ln_b, comm=gather_comm([wbf["w_down"]]))
    wb_down = wb_down.reshape(1, F, D)
    uv = mix_b_fwd(proj, sgu_ln_g, sgu_ln_b, wsp, bsp_t)
    y_a = mm_nn("y_a", asw, wb_co, F32)
    y_b = mm_nn("y_b", uv, wb_so, F32)
    merged = merge_fwd(proj, y_a, y_b)
    o1 = mm_nn("o1", merged, wb_out, F32)
    h2 = pre_norm("pre_norm2", xs, mod, norm2_g, 1, o1=o1)
    upre = mm_nn("upre", h2, wb_up, F32, out_halves=True, tn_pref=MM_WIDE)
    f = ffn_act_fwd(upre, fw, fb)
    o2 = mm_nn("o2", f, wb_down, F32)
    dx3, do2, st_f = final_fwd_bwd(xs, o1, o2, mod, final_g.reshape(1, D), tgt)
    loss = lax.psum(st_f[3, 0], MESH_AXES)

    xq, yq, cq = lax.axis_index("x"), lax.axis_index("y"), lax.axis_index("c")
    idx = jnp.stack([2 * (2 * _flip(xq, m & 2) + _flip(yq, m & 1)) + cq for m in range(4)]).astype(jnp.int32)
    own, transit, arrived = {}, {}, {}

    def add_pairs(keys, full, from_sibling):
        for k, g_full, r in zip(keys, full, from_sibling):
            own[k], transit[k] = pair_add("pair_add_" + k, g_full, r, idx)

    df = mm_nt("df", do2, wb_down, F32, tko_pref=MM_WIDE)
    g_down = mm_tn("g_down", f, do2, 1, tko_pref=MM_WIDE).reshape(NDEV, F // NDEV, D)
    dupre, g_fw, g_fb = ffn_act_bwd(upre, df, fw, fb)
    dh2 = mm_nt("dh2", dupre, wb_up, F32, a_halves=True)
    g_up = mm_tn("g_up", h2, dupre, NDEV, g_halves=True, tn_pref=MM_WIDE)
    (dx2, do1, st_2), sib = norm2_bwd(dh2, xs, o1, dx3, mod, norm2_g, comm=sibling_comm([g_down, g_up]))
    add_pairs(("w_down", "w_up"), (g_down, g_up), sib)
    dmerged = mm_nt("dmerged", do1, wb_out, F32)
    g_out = mm_tn("g_out", merged, do1, 1).reshape(NDEV, D // NDEV, D)
    dy_a, dy_b, dproj, db_g = merge_bwd(dmerged, proj, y_a, y_b)
    dasw = mm_nt("dasw", dy_a, wb_co, F32)
    g_co = mm_tn("g_co", asw, dy_a, NDEV)
    duv = mm_nt("duv", dy_b, wb_so, F32)
    g_so = mm_tn("g_so", uv, dy_b, NDEV)
    (dproj, st_a, g_cw, db_a), (arrived["w_down"], arrived["w_up"]) = mix_a_bwd(
        dasw, ac, proj, dproj, cw, conv_ln_g, conv_ln_b, comm=chip_comm([transit["w_down"], transit["w_up"]]))
    (dproj, st_b, g_wsp, g_bsp_t, db_s), sib = mix_b_bwd(
        duv, proj, dproj, sgu_ln_g, sgu_ln_b, wsp, bsp_t, comm=sibling_comm([g_out, g_co, g_so]))
    add_pairs(("w_out", "w_conv_out", "w_sgu_out"), (g_out, g_co, g_so), sib)
    g_in = mm_tn("g_in", h1, dproj, NDEV)
    add_pairs(("w_in",), (g_in,), _run_comm("sibling_w_in", sibling_comm([g_in])))
    late = ("w_out", "w_conv_out", "w_sgu_out", "w_in")
    dh1, got = mm_nt("dh1", dproj, wb_in, F32, comm=chip_comm([transit[k] for k in late]))
    arrived.update(zip(late, got))
    grad_x, st_1 = norm1_bwd(dh1, xs, dx2, mod, norm1_g)

    dmod = jnp.concatenate([st_1[0], st_1[1], st_2[3], st_2[0], st_2[1], st_f[1]]).reshape(1, NDEV * na)
    dmod_all = exchange_rows("exchange_dmod", dmod.reshape(NDEV, 1, na)).reshape(NDEV, na)
    g_ada = ada_bwd_local(jnp.transpose(c_all), dmod_all)

    res = {}
    for k in _BIG:
        q = arrived[k]
        res[k] = adamw("adamw_" + k, W[k][0], M[k][0], V[k][0], [(own[k], None), (q, 0), (q, 1), (q, 2)])
    res["w_ada"] = adamw("adamw_w_ada", w_ada[0], m_w_ada[0], v_w_ada[0], [(g_ada, None)])

    g_small = dict(
        b_ada=dmod, norm1_g=st_1[2], b_in=jnp.concatenate([db_a, db_s, db_g], axis=1), conv_dw_b=st_a[2],
        conv_ln_g=st_a[0], conv_ln_b=st_a[1], sgu_ln_g=st_b[0], sgu_ln_b=st_b[1], w_spatial=g_wsp,
        b_spatial=jnp.transpose(g_bsp_t), norm2_g=st_2[2], ffn_dw_b=g_fb, final_g=st_f[0],
        conv_dw_w=g_cw[:K31], ffn_dw_w=jnp.transpose(g_fw, (1, 0, 2)))
    names = _SMALL + _SMALL_SHARDED
    sizes = [g_small[k].size for k in names]
    packed = _pack([g_small[k] for k in names], 8 * LANE)
    summed = sum_rows("sum_small", gather_rows("gather_small", packed.reshape(1, -1)))[0]
    offs = [sum(sizes[:i]) for i in range(len(names))]
    seg = {k: summed[o:o + s] for k, o, s in zip(names, offs, sizes)}
    n_cw, n_fw = conv_dw_w.shape[-1], ffn_dw_w.shape[-1]
    seg["conv_dw_w"] = lax.dynamic_slice(seg["conv_dw_w"].reshape(K31, Dc), (0, me * n_cw), (K31, n_cw)).reshape(-1)
    seg["ffn_dw_w"] = lax.dynamic_slice(seg["ffn_dw_w"].reshape(3, 2 * F), (0, me * n_fw), (3, n_fw)).reshape(-1)
    gp = _pack([seg[k] for k in names], 8 * LANE).reshape(-1, LANE)
    wp, mp, vp = (_pack([S[k] for k in names], 8 * LANE).reshape(-1, LANE) for S in (W, M, V))
    small = adamw("adamw_small", wp, mp, vp, [(gp, None)])
    sizes2 = [W[k].size for k in names]
    offs2 = [sum(sizes2[:i]) for i in range(len(names))]
    for k, o, s in zip(names, offs2, sizes2):
        res[k] = tuple(a.reshape(-1)[o:o + s].reshape(W[k].shape) for a in small)

    outs = [[], [], [], []]
    for k in _ORDER:
        for slot in range(4):
            outs[slot].append(res[k][slot].reshape(W[k].shape))
    return (loss, grad_x[None], *outs[0], *outs[1], *outs[2], *outs[3])
```

```python
import functools

import jax
import jax.numpy as jnp
from jax import lax
from jax.experimental import pallas as pl
from jax.experimental.pallas import tpu as pltpu

F32, BF16 = jnp.float32, jnp.bfloat16
NDEV = 8
MESH_AXES = ("x", "y", "c")
MESH = pl.DeviceIdType.MESH
EPS = 1e-6
CHUNK = 128
CONV_HALO = 32
FFN_HALO = 8
LANE, SUBLANE = 128, 8
ROW_TILE = 256
FFN_ROW_TILE = 512
STRIP = 32
STRIP_UNROLL = 2
MM_TILE = 1024
MM_WIDE = 1408
VMEM_LIMIT = 56 * 1024 * 1024
ADAM_LR, ADAM_B1, ADAM_B2, ADAM_EPS, ADAM_WD, ADAM_STEP = 0.001, 0.9, 0.999, 1e-08, 0.01, 10
SQRT_HALF = 0.7071067811865476
INV_SQRT_2PI = 0.3989422804014327


def _div(n, pref, mult=LANE):
    if n <= pref:
        return n
    for d in range(pref - pref % mult, 0, -mult):
        if n % d == 0:
            return d
    return n


def _cp(sem):
    return pltpu.CompilerParams(dimension_semantics=sem, vmem_limit_bytes=VMEM_LIMIT)


def _sig(v):
    return jax.nn.sigmoid(v)


def _gelu(v):
    return 0.5 * v * (1.0 + lax.erf(v * SQRT_HALF))


def _gelu_grad(v):
    return 0.5 * (1.0 + lax.erf(v * SQRT_HALF)) + v * (INV_SQRT_2PI * jnp.exp(-0.5 * v * v))


def _colsum(v):
    return jnp.sum(v, axis=0, keepdims=True)


def _rowmean(v):
    return jnp.mean(v, axis=-1, keepdims=True)


class Comm:
    def __init__(self, plan, ins, out_shapes, n_sems):
        self.plan, self.ins, self.out_shapes, self.n_sems = plan, list(ins), list(out_shapes), tuple(n_sems)


def _pcall(name, body, grid, in_specs, out_specs, out_shape, scratch=(), sem=None, aliases=None, comm=None):
    if comm is None:
        return pl.pallas_call(
            body, name=name, grid=grid, in_specs=in_specs, out_specs=out_specs, out_shape=out_shape,
            scratch_shapes=list(scratch), input_output_aliases=aliases or {},
            compiler_params=_cp(sem or ("arbitrary",) * len(grid)))
    single = not isinstance(out_shape, (list, tuple))
    own_specs, own_shapes = ([out_specs], [out_shape]) if single else (list(out_specs), list(out_shape))
    n_in, n_out, n_scr = len(in_specs), len(own_shapes), len(scratch)
    n_ci, n_co = len(comm.ins), len(comm.out_shapes)
    any_spec = pl.BlockSpec(memory_space=pl.ANY)

    def fused(*refs):
        ins, cins = refs[:n_in], refs[n_in:n_in + n_ci]
        outs = refs[n_in + n_ci:n_in + n_ci + n_out]
        couts = refs[n_in + n_ci + n_out:n_in + n_ci + n_out + n_co]
        scr = refs[n_in + n_ci + n_out + n_co:n_in + n_ci + n_out + n_co + n_scr]
        sems = refs[n_in + n_ci + n_out + n_co + n_scr:]
        first = functools.reduce(jnp.logical_and, [pl.program_id(d) == 0 for d in range(len(grid))])
        last = functools.reduce(jnp.logical_and, [pl.program_id(d) == grid[d] - 1 for d in range(len(grid))])

        @pl.when(first)
        def _():
            comm.plan(cins, couts, sems)[0]()

        body(*ins, *outs, *scr)

        @pl.when(last)
        def _():
            comm.plan(cins, couts, sems)[1]()

    call = pl.pallas_call(
        fused, name=name, grid=grid, in_specs=list(in_specs) + [any_spec] * n_ci,
        out_specs=own_specs + [any_spec] * n_co, out_shape=own_shapes + comm.out_shapes,
        scratch_shapes=list(scratch) + [pltpu.SemaphoreType.DMA((s,)) for s in comm.n_sems],
        input_output_aliases=aliases or {},
        compiler_params=pltpu.CompilerParams(dimension_semantics=("arbitrary",) * len(grid),
                                             vmem_limit_bytes=VMEM_LIMIT, has_side_effects=True))

    def run(*args):
        res = call(*args, *comm.ins)
        own = res[:n_out]
        return (own[0] if single else list(own)), list(res[n_out:])

    return run


def _matmul(name, a, b, *, grid, a_spec, b_spec, o_spec, out_shape, dims, acc_shape, bias=None, bias_spec=None, comm=None):
    nk = grid[2]

    def body(*refs):
        if bias is None:
            a_ref, b_ref, o_ref, *scr = refs
            bias_ref = None
        else:
            a_ref, b_ref, bias_ref, o_ref, *scr = refs
        part = lax.dot_general(a_ref[...], b_ref[...], (dims, ((), ())), preferred_element_type=F32)

        def finish(total):
            if bias_ref is not None:
                total = total + bias_ref[...]
            o_ref[...] = total.astype(o_ref.dtype)

        if nk == 1:
            finish(part)
        else:
            acc = scr[0]
            k = pl.program_id(2)

            @pl.when(k == 0)
            def _():
                acc[...] = part

            @pl.when(k > 0)
            def _():
                acc[...] += part

            @pl.when(k == nk - 1)
            def _():
                finish(acc[...])

    in_specs = [a_spec, b_spec] + ([bias_spec] if bias is not None else [])
    args = (a, b) + ((bias,) if bias is not None else ())
    return _pcall(name, body, grid, in_specs, o_spec, out_shape,
                  scratch=[pltpu.VMEM(acc_shape, F32)] if nk > 1 else [],
                  sem=("parallel", "parallel", "arbitrary"), comm=comm)(*args)


def mm_nn(name, a, wb, out_dtype, *, bias=None, out_halves=False, tn_pref=MM_TILE, comm=None):
    T, K = a.shape
    NB, _, Ns = wb.shape
    N = NB * Ns
    tm, tn, tk = _div(T, MM_TILE), _div(Ns, tn_pref), _div(K, 2048)
    npb, nj, nk = Ns // tn, N // tn, K // tk
    if out_halves:
        o_spec = pl.BlockSpec((None, tm, tn), lambda i, j, k: (j // (nj // 2), i, j % (nj // 2)))
        out_shape = jax.ShapeDtypeStruct((2, T, N // 2), out_dtype)
    else:
        o_spec = pl.BlockSpec((tm, tn), lambda i, j, k: (i, j))
        out_shape = jax.ShapeDtypeStruct((T, N), out_dtype)
    return _matmul(
        name, a, wb, grid=(T // tm, nj, nk),
        a_spec=pl.BlockSpec((tm, tk), lambda i, j, k: (i, k)),
        b_spec=pl.BlockSpec((None, tk, tn), lambda i, j, k: (j // npb, k, j % npb)),
        o_spec=o_spec, out_shape=out_shape, dims=((1,), (0,)), acc_shape=(tm, tn),
        bias=bias, bias_spec=pl.BlockSpec((1, tn), lambda i, j, k: (0, j)), comm=comm)


def mm_nt(name, a, wb, out_dtype, *, a_halves=False, tko_pref=MM_TILE, tc_pref=2048, comm=None):
    NB, K, Ns = wb.shape
    T = a.shape[-2]
    tm, tko, tc = _div(T, MM_TILE), _div(K, tko_pref), _div(Ns, tc_pref)
    cpb = Ns // tc
    nkk = NB * cpb
    if a_halves:
        a_spec = pl.BlockSpec((None, tm, tc), lambda i, j, k: (k // (nkk // 2), i, k % (nkk // 2)))
    else:
        a_spec = pl.BlockSpec((tm, tc), lambda i, j, k: (i, k))
    return _matmul(
        name, a, wb, grid=(T // tm, K // tko, nkk), a_spec=a_spec,
        b_spec=pl.BlockSpec((None, tko, tc), lambda i, j, k: (k // cpb, j, k % cpb)),
        o_spec=pl.BlockSpec((tm, tko), lambda i, j, k: (i, j)),
        out_shape=jax.ShapeDtypeStruct((T, K), out_dtype), dims=((1,), (1,)), acc_shape=(tm, tko), comm=comm)


def mm_tn(name, a, g, nb, *, g_halves=False, tko_pref=MM_TILE, tn_pref=MM_TILE):
    T, K = a.shape
    N = g.shape[-1] * (2 if g_halves else 1)
    Ns = N // nb
    tt, tko, tn = _div(T, 2 * MM_TILE), _div(K, tko_pref), _div(Ns, tn_pref)
    npb, nj = Ns // tn, N // tn
    if g_halves:
        g_spec = pl.BlockSpec((None, tt, tn), lambda i, j, t: (j // (nj // 2), t, j % (nj // 2)))
    else:
        g_spec = pl.BlockSpec((tt, tn), lambda i, j, t: (t, j))
    return _matmul(
        name, a, g, grid=(K // tko, nj, T // tt),
        a_spec=pl.BlockSpec((tt, tko), lambda i, j, t: (t, i)), b_spec=g_spec,
        o_spec=pl.BlockSpec((None, tko, tn), lambda i, j, t: (j // npb, i, j % npb)),
        out_shape=jax.ShapeDtypeStruct((nb, K, Ns), F32), dims=((0,), (0,)), acc_shape=(tko, tn))


def _row_call(name, body, grid, in_specs, out_specs, out_shape, scratch=(), sem=None, aliases=None, comm=None):
    return _pcall(name, body, grid, in_specs, out_specs, out_shape, scratch, sem, aliases, comm)


def _full(shape):
    nd = len(shape)
    return pl.BlockSpec(shape, lambda *idx: (0,) * nd)


def pre_norm(name, x, mod, g, which, o1=None):
    T, D = x.shape
    tr = _div(T, ROW_TILE, SUBLANE)

    def body(*refs):
        if o1 is None:
            x_ref, mod_ref, g_ref, h_ref = refs
            xv = x_ref[...]
        else:
            x_ref, o1_ref, mod_ref, g_ref, h_ref = refs
            xv = x_ref[...] + mod_ref[:, 2 * D:3 * D] * o1_ref[...]
        shift = mod_ref[:, (3 * which) * D:(3 * which + 1) * D]
        scale = mod_ref[:, (3 * which + 1) * D:(3 * which + 2) * D]
        r = lax.rsqrt(_rowmean(xv * xv) + EPS)
        h_ref[...] = ((xv * r) * g_ref[...] * (1.0 + scale) + shift).astype(BF16)

    row = pl.BlockSpec((tr, D), lambda i: (i, 0))
    ins = [x] + ([o1] if o1 is not None else []) + [mod, g]
    specs = [row] * (1 if o1 is None else 2) + [_full(mod.shape), _full(g.shape)]
    return _row_call(name, body, (T // tr,), specs, row, jax.ShapeDtypeStruct((T, D), BF16), sem=("parallel",))(*ins)


def _window_taps(win, taps):
    n = win.shape[0]
    for r in range(SUBLANE):
        group = [(i, tap, off) for i, (tap, off) in enumerate(taps) if off % SUBLANE == r]
        if not group:
            continue
        shifted = win if r == 0 else pltpu.roll(win, n - r, 0)
        for i, tap, off in group:
            assert 0 <= off and off + CONV_HALO <= n
            yield i, tap, shifted[off - r:off - r + CONV_HALO]


def mix_a_fwd(proj, cw, cb, lg, lb, comm=None):
    T = proj.shape[0]
    K, Dc = cw.shape
    tr = _div(T, ROW_TILE, CONV_HALO)
    hb = tr // CONV_HALO

    def body(val_ref, gate_ref, hval_ref, hgate_ref, cw_ref, cb_ref, lg_ref, lb_ref, ac_ref, asw_ref, buf):
        i = pl.program_id(0)
        hist = hval_ref[...] * _sig(hgate_ref[...])
        buf[0:CONV_HALO, :] = jnp.where(i > 0, hist, 0.0)
        buf[CONV_HALO:CONV_HALO + tr, :] = val_ref[...] * _sig(gate_ref[...])
        base = CONV_HALO - (K - 1)
        for c0 in range(0, Dc, LANE):
            lanes = slice(c0, c0 + LANE)

            def step(s, carry):
                r0 = pl.multiple_of(s * CONV_HALO, CONV_HALO)
                win = buf[pl.ds(r0, 2 * CONV_HALO), lanes]
                acc = jnp.zeros((CONV_HALO, LANE), F32)
                for _, k, piece in _window_taps(win, [(k, base + k) for k in range(K)]):
                    acc = acc + piece * cw_ref[k:k + 1, lanes]
                ac_ref[pl.ds(r0, CONV_HALO), lanes] = acc + cb_ref[:, lanes]
                return carry

            lax.fori_loop(0, tr // CONV_HALO, step, 0)
        ac = ac_ref[...]
        mu = _rowmean(ac)
        cen = ac - mu
        y = cen * lax.rsqrt(_rowmean(cen * cen) + EPS)
        aln = y * lg_ref[...] + lb_ref[...]
        asw_ref[...] = (aln * _sig(aln)).astype(BF16)

    def halo(col):
        return pl.BlockSpec((CONV_HALO, Dc), lambda i: (jnp.maximum(i * hb - 1, 0), col))

    row = pl.BlockSpec((tr, Dc), lambda i: (i, 0))
    return _row_call(
        "mix_a_fwd", body, (T // tr,),
        [row, pl.BlockSpec((tr, Dc), lambda i: (i, 1)), halo(0), halo(1),
         _full(cw.shape), _full(cb.shape), _full(lg.shape), _full(lb.shape)],
        [row, row], [jax.ShapeDtypeStruct((T, Dc), F32), jax.ShapeDtypeStruct((T, Dc), BF16)],
        scratch=[pltpu.VMEM((CONV_HALO + tr, Dc), F32)], sem=("parallel",), comm=comm)(proj, proj, proj, proj, cw, cb, lg, lb)


def _spatial_mask():
    t = lax.broadcasted_iota(jnp.int32, (CHUNK, CHUNK), 0)
    s = lax.broadcasted_iota(jnp.int32, (CHUNK, CHUNK), 1)
    return s <= t


def mix_b_fwd(proj, lg, lb, wsp, bsp_t):
    T = proj.shape[0]
    Ds = lg.shape[-1]
    G = wsp.shape[0]
    hd = Ds // G
    tr = _div(T, ROW_TILE, CHUNK)

    def body(u_ref, v_ref, lg_ref, lb_ref, w_ref, b_ref, uv_ref, vs):
        v = _gelu(v_ref[...])
        mu = _rowmean(v)
        cen = v - mu
        vln = (cen * lax.rsqrt(_rowmean(cen * cen) + EPS) * lg_ref[...] + lb_ref[...]).astype(BF16)
        mask = _spatial_mask()
        for g in range(G):
            wg = jnp.where(mask, w_ref[g], 0.0).astype(BF16)
            for n in range(tr // CHUNK):
                rows, cols = slice(n * CHUNK, (n + 1) * CHUNK), slice(g * hd, (g + 1) * hd)
                vs[rows, cols] = jnp.dot(wg, vln[rows, cols], preferred_element_type=F32) + b_ref[:, g:g + 1]
        uv_ref[...] = (_gelu(u_ref[...]) * vs[...]).astype(BF16)

    return _row_call(
        "mix_b_fwd", body, (T // tr,),
        [pl.BlockSpec((tr, Ds), lambda i: (i, 2)), pl.BlockSpec((tr, Ds), lambda i: (i, 3)),
         _full(lg.shape), _full(lb.shape), _full(wsp.shape), _full(bsp_t.shape)],
        pl.BlockSpec((tr, Ds), lambda i: (i, 0)), jax.ShapeDtypeStruct((T, Ds), BF16),
        scratch=[pltpu.VMEM((tr, Ds), F32)], sem=("parallel",))(proj, proj, lg, lb, wsp, bsp_t)


def merge_fwd(proj, y_a, y_b):
    T, D = y_a.shape
    tr = _div(T, ROW_TILE, SUBLANE)

    def body(g_ref, ya_ref, yb_ref, o_ref):
        o_ref[...] = (_sig(g_ref[:, 0:D]) * ya_ref[...] + _sig(g_ref[:, D:2 * D]) * yb_ref[...]).astype(BF16)

    row = pl.BlockSpec((tr, D), lambda i: (i, 0))
    return _row_call("merge_fwd", body, (T // tr,), [pl.BlockSpec((tr, 2 * D), lambda i: (i, 1)), row, row], row,
                     jax.ShapeDtypeStruct((T, D), BF16), sem=("parallel",))(proj, y_a, y_b)


def _fold(v):
    acc = v[0:SUBLANE]
    for r in range(SUBLANE, v.shape[0], SUBLANE):
        acc = acc + v[r:r + SUBLANE]
    return acc


def _conv3(prev, cur, w):
    win = jnp.concatenate([prev, cur], axis=0)
    n = win.shape[0]
    x1 = pltpu.roll(win, 1, 0)[FFN_HALO:n]
    x2 = pltpu.roll(win, 2, 0)[FFN_HALO:n]
    return x2 * w[0] + x1 * w[1] + cur * w[2], (x2, x1, cur)


def _strip_taps(w_ref, b_ref, lanes):
    w = [[w_ref[h, k:k + 1, lanes] for k in range(3)] for h in range(2)]
    b = [b_ref[h, :, lanes] for h in range(2)]
    return w, b


def ffn_act_fwd(upre, fw, fb):
    _, T, F = upre.shape
    tr = _div(T, FFN_ROW_TILE, STRIP)
    cb = _div(F, MM_WIDE)
    hb = tr // FFN_HALO
    ns = tr // STRIP

    def body(x_ref, h_ref, w_ref, b_ref, f_ref):
        i = pl.program_id(0)
        for c0 in range(0, cb, LANE):
            lanes = slice(c0, c0 + LANE)
            w, b = _strip_taps(w_ref, b_ref, lanes)

            def strip(r0, prev):
                up = [_conv3(prev(h), x_ref[h, pl.ds(r0, STRIP), lanes], w[h])[0] + b[h] for h in range(2)]
                f_ref[pl.ds(r0, STRIP), lanes] = (up[1] * _sig(up[1]) * up[0]).astype(BF16)

            strip(0, lambda h: jnp.where(i > 0, h_ref[h, :, lanes], 0.0))

            def step(s, carry):
                r0 = pl.multiple_of(s * STRIP, STRIP)
                strip(r0, lambda h: x_ref[h, pl.ds(pl.multiple_of(r0 - FFN_HALO, FFN_HALO), FFN_HALO), lanes])
                return carry

            lax.fori_loop(1, ns, step, 0, unroll=STRIP_UNROLL)

    return _row_call(
        "ffn_act_fwd", body, (T // tr, F // cb),
        [pl.BlockSpec((2, tr, cb), lambda i, j: (0, i, j)),
         pl.BlockSpec((2, FFN_HALO, cb), lambda i, j: (0, jnp.maximum(i * hb - 1, 0), j)),
         pl.BlockSpec((2, 3, cb), lambda i, j: (0, 0, j)), pl.BlockSpec((2, 1, cb), lambda i, j: (0, 0, j))],
        pl.BlockSpec((tr, cb), lambda i, j: (i, j)), jax.ShapeDtypeStruct((T, F), BF16),
        sem=("parallel", "parallel"))(upre, upre, fw, fb)


def final_fwd_bwd(x, o1, o2, mod, gf, target):
    T, D = x.shape
    tr = _div(T, ROW_TILE, SUBLANE)
    nt = T // tr

    def body(x_ref, o1_ref, o2_ref, mod_ref, gf_ref, t_ref, dx3_ref, do2_ref, st_ref):
        i = pl.program_id(0)
        gate1, gate2 = mod_ref[:, 2 * D:3 * D], mod_ref[:, 5 * D:6 * D]
        o2v = o2_ref[...]
        x3 = x_ref[...] + gate1 * o1_ref[...] + gate2 * o2v
        r = lax.rsqrt(_rowmean(x3 * x3) + EPS)
        xn = x3 * r
        err = xn * gf_ref[...] - t_ref[...]
        dy = err * (1.0 / D)
        dxn = dy * gf_ref[...]
        dx3 = r * (dxn - xn * _rowmean(dxn * xn))
        dx3_ref[...] = dx3
        do2_ref[...] = (dx3 * gate2).astype(BF16)

        @pl.when(i == 0)
        def _():
            st_ref[...] = jnp.zeros_like(st_ref)

        st_ref[0:1, :] += _colsum(dy * xn)
        st_ref[1:2, :] += _colsum(dx3 * o2v)
        st_ref[2:3, :] += _colsum(err * err) * (0.5 / D)

        @pl.when(i == nt - 1)
        def _():
            st_ref[3:4, :] = jnp.zeros((1, D), F32) + jnp.sum(st_ref[2:3, :])

    row = pl.BlockSpec((tr, D), lambda i: (i, 0))
    return _row_call(
        "final_fwd_bwd", body, (nt,), [row, row, row, _full(mod.shape), _full(gf.shape), row],
        [row, row, _full((8, D))],
        [jax.ShapeDtypeStruct((T, D), F32), jax.ShapeDtypeStruct((T, D), BF16), jax.ShapeDtypeStruct((8, D), F32)],
    )(x, o1, o2, mod, gf, target)


def ffn_act_bwd(upre, df, fw, fb):
    _, T, F = upre.shape
    tr = _div(T, FFN_ROW_TILE, STRIP)
    cb = _div(F, MM_WIDE)
    hb = tr // FFN_HALO
    nt = T // tr
    ns = tr // STRIP

    def body(x_ref, h_ref, df_ref, w_ref, b_ref, dpre_ref, dw_ref, db_ref, carry):
        i = pl.program_id(1)
        ri = nt - 1 - i

        @pl.when(i == 0)
        def _():
            carry[...] = jnp.zeros_like(carry)
            dw_ref[...] = jnp.zeros_like(dw_ref)
            db_ref[...] = jnp.zeros_like(db_ref)

        for c0 in range(0, cb, LANE):
            lanes = slice(c0, c0 + LANE)
            w, b = _strip_taps(w_ref, b_ref, lanes)

            def strip(r0, prev, state):
                later, db, dw = state
                up, taps = [], []
                for h in range(2):
                    conv, xs3 = _conv3(prev(h), x_ref[h, pl.ds(r0, STRIP), lanes], w[h])
                    up.append(conv + b[h])
                    taps.append(xs3)
                val, gt = up
                sg = _sig(gt)
                dfv = df_ref[pl.ds(r0, STRIP), lanes]
                dup = (dfv * (gt * sg), dfv * val * (sg * (1.0 + gt * (1.0 - sg))))
                db = tuple(db[h] + _fold(dup[h]) for h in range(2))
                dw = tuple(tuple(dw[h][k] + _fold(dup[h] * taps[h][k]) for k in range(3)) for h in range(2))
                for h in range(2):
                    dwin = jnp.concatenate([dup[h], later[h]], axis=0)
                    n = dwin.shape[0]
                    d1 = pltpu.roll(dwin, n - 1, 0)[0:STRIP]
                    d2 = pltpu.roll(dwin, n - 2, 0)[0:STRIP]
                    dpre_ref[h, pl.ds(r0, STRIP), lanes] = (dup[h] * w[h][2] + d1 * w[h][1] + d2 * w[h][0]).astype(BF16)
                return tuple(dup[h][0:FFN_HALO] for h in range(2)), db, dw

            zero = jnp.zeros((SUBLANE, LANE), F32)
            state = ((carry[0, :, lanes], carry[1, :, lanes]), (zero, zero), ((zero,) * 3,) * 2)

            def step(s, state):
                r0 = pl.multiple_of((ns - 1 - s) * STRIP, STRIP)
                return strip(r0, lambda h: x_ref[h, pl.ds(pl.multiple_of(r0 - FFN_HALO, FFN_HALO), FFN_HALO), lanes], state)

            state = lax.fori_loop(0, ns - 1, step, state, unroll=STRIP_UNROLL)
            later, db, dw = strip(0, lambda h: jnp.where(ri > 0, h_ref[h, :, lanes], 0.0), state)
            for h in range(2):
                carry[h, :, lanes] = later[h]
                db_ref[h, :, lanes] += _colsum(db[h])
                for k in range(3):
                    dw_ref[h, k:k + 1, lanes] += _colsum(dw[h][k])

    return _row_call(
        "ffn_act_bwd", body, (F // cb, nt),
        [pl.BlockSpec((2, tr, cb), lambda j, i: (0, nt - 1 - i, j)),
         pl.BlockSpec((2, FFN_HALO, cb), lambda j, i: (0, jnp.maximum((nt - 1 - i) * hb - 1, 0), j)),
         pl.BlockSpec((tr, cb), lambda j, i: (nt - 1 - i, j)),
         pl.BlockSpec((2, 3, cb), lambda j, i: (0, 0, j)), pl.BlockSpec((2, 1, cb), lambda j, i: (0, 0, j))],
        [pl.BlockSpec((2, tr, cb), lambda j, i: (0, nt - 1 - i, j)),
         pl.BlockSpec((2, 3, cb), lambda j, i: (0, 0, j)), pl.BlockSpec((2, 1, cb), lambda j, i: (0, 0, j))],
        [jax.ShapeDtypeStruct((2, T, F), BF16), jax.ShapeDtypeStruct((2, 3, F), F32), jax.ShapeDtypeStruct((2, 1, F), F32)],
        scratch=[pltpu.VMEM((2, FFN_HALO, cb), F32)],
        sem=("parallel", "arbitrary"))(upre, upre, df, fw, fb)


def norm2_bwd(dh2, x, o1, dx3, mod, g2, comm=None):
    T, D = x.shape
    tr = _div(T, ROW_TILE, SUBLANE)

    def body(dh_ref, x_ref, o1_ref, dx3_ref, mod_ref, g_ref, dx2_ref, do1_ref, st_ref):
        i = pl.program_id(0)
        gate1, scale = mod_ref[:, 2 * D:3 * D], mod_ref[:, 4 * D:5 * D]
        o1v = o1_ref[...]
        x2 = x_ref[...] + gate1 * o1v
        r = lax.rsqrt(_rowmean(x2 * x2) + EPS)
        xn = x2 * r
        dh = dh_ref[...]
        dxn = dh * (g_ref[...] * (1.0 + scale))
        dx2 = r * (dxn - xn * _rowmean(dxn * xn)) + dx3_ref[...]
        dx2_ref[...] = dx2
        do1_ref[...] = (dx2 * gate1).astype(BF16)

        @pl.when(i == 0)
        def _():
            st_ref[...] = jnp.zeros_like(st_ref)

        st_ref[0:1, :] += _colsum(dh)
        st_ref[1:2, :] += _colsum(dh * xn) * g_ref[...]
        st_ref[2:3, :] += _colsum(dh * xn) * (1.0 + scale)
        st_ref[3:4, :] += _colsum(dx2 * o1v)

    row = pl.BlockSpec((tr, D), lambda i: (i, 0))
    return _row_call(
        "norm2_bwd", body, (T // tr,), [row, row, row, row, _full(mod.shape), _full(g2.shape)],
        [row, row, _full((8, D))],
        [jax.ShapeDtypeStruct((T, D), F32), jax.ShapeDtypeStruct((T, D), BF16), jax.ShapeDtypeStruct((8, D), F32)],
        comm=comm)(dh2, x, o1, dx3, mod, g2)


def merge_bwd(dmerged, proj, y_a, y_b):
    T, D = y_a.shape
    tr = _div(T, ROW_TILE, SUBLANE)

    def body(dm_ref, g_ref, ya_ref, yb_ref, dya_ref, dyb_ref, dp_ref, db_ref):
        i = pl.program_id(0)
        dm = dm_ref[...]
        sa, sb = _sig(g_ref[:, 0:D]), _sig(g_ref[:, D:2 * D])
        dya_ref[...] = (dm * sa).astype(BF16)
        dyb_ref[...] = (dm * sb).astype(BF16)
        dga = dm * ya_ref[...] * (sa * (1.0 - sa))
        dgb = dm * yb_ref[...] * (sb * (1.0 - sb))
        dp_ref[:, 0:D] = dga.astype(BF16)
        dp_ref[:, D:2 * D] = dgb.astype(BF16)

        @pl.when(i == 0)
        def _():
            db_ref[...] = jnp.zeros_like(db_ref)

        db_ref[:, 0:D] += _colsum(dga)
        db_ref[:, D:2 * D] += _colsum(dgb)

    row = pl.BlockSpec((tr, D), lambda i: (i, 0))
    wide = pl.BlockSpec((tr, 2 * D), lambda i: (i, 1))
    return _row_call(
        "merge_bwd", body, (T // tr,), [row, wide, row, row], [row, row, wide, _full((1, 2 * D))],
        [jax.ShapeDtypeStruct((T, D), BF16), jax.ShapeDtypeStruct((T, D), BF16),
         jax.ShapeDtypeStruct((T, 4 * D), BF16), jax.ShapeDtypeStruct((1, 2 * D), F32)],
    )(dmerged, proj, y_a, y_b)


def mix_a_bwd(dasw, ac, proj, dproj, cw, lg, lb, comm=None):
    T, Dc = ac.shape
    K = cw.shape[0]
    tr = _div(T, ROW_TILE, CONV_HALO)
    hb = tr // CONV_HALO
    nt = T // tr

    def body(dasw_ref, ac_ref, in_ref, hin_ref, dp_hbm, cw_ref, lg_ref, lb_ref,
             dp_ref, st_ref, dcw_ref, db_ref, abuf, dbuf, carry, da_buf):
        del dp_hbm
        i = pl.program_id(0)
        ri = nt - 1 - i
        acv = ac_ref[...]
        mu = _rowmean(acv)
        cen = acv - mu
        rstd = lax.rsqrt(_rowmean(cen * cen) + EPS)
        y = cen * rstd
        aln = y * lg_ref[...] + lb_ref[...]
        sg = _sig(aln)
        daln = dasw_ref[...] * (sg * (1.0 + aln * (1.0 - sg)))
        dy = daln * lg_ref[...]
        dac = rstd * (dy - _rowmean(dy) - y * _rowmean(dy * y))

        @pl.when(i == 0)
        def _():
            carry[...] = jnp.zeros_like(carry)
            st_ref[...] = jnp.zeros_like(st_ref)
            dcw_ref[...] = jnp.zeros_like(dcw_ref)
            db_ref[...] = jnp.zeros_like(db_ref)

        st_ref[0:1, :] += _colsum(daln * y)
        st_ref[1:2, :] += _colsum(daln)
        st_ref[2:3, :] += _colsum(dac)
        val, gate = in_ref[:, 0:Dc], in_ref[:, Dc:2 * Dc]
        sgg = _sig(gate)
        hist = hin_ref[:, 0:Dc] * _sig(hin_ref[:, Dc:2 * Dc])
        abuf[0:CONV_HALO, :] = jnp.where(ri > 0, hist, 0.0)
        abuf[CONV_HALO:CONV_HALO + tr, :] = val * sgg
        dbuf[0:tr, :] = dac
        dbuf[tr:tr + CONV_HALO, :] = carry[...]
        carry[...] = dac[0:CONV_HALO, :]
        base = CONV_HALO - (K - 1)
        zero = jnp.zeros((SUBLANE, LANE), F32)
        for c0 in range(0, Dc, LANE):
            lanes = slice(c0, c0 + LANE)

            def step(s, dws):
                r0 = pl.multiple_of(s * CONV_HALO, CONV_HALO)
                dwin = dbuf[pl.ds(r0, 2 * CONV_HALO), lanes]
                piece_dac = dwin[0:CONV_HALO]
                dws = list(dws)
                for _, k, piece in _window_taps(abuf[pl.ds(r0, 2 * CONV_HALO), lanes], [(k, base + k) for k in range(K)]):
                    dws[k] = dws[k] + _fold(piece_dac * piece)
                acc = jnp.zeros((CONV_HALO, LANE), F32)
                for _, k, piece in _window_taps(dwin, [(k, K - 1 - k) for k in range(K)]):
                    acc = acc + piece * cw_ref[k:k + 1, lanes]
                da_buf[pl.ds(r0, CONV_HALO), lanes] = acc
                return tuple(dws)

            dws = lax.fori_loop(0, tr // CONV_HALO, step, (zero,) * K)
            for k in range(K):
                dcw_ref[k:k + 1, lanes] += _colsum(dws[k])
        da = da_buf[...]
        dval = da * sgg
        dgate = da * val * (sgg * (1.0 - sgg))
        dp_ref[:, 0:Dc] = dval.astype(BF16)
        dp_ref[:, Dc:2 * Dc] = dgate.astype(BF16)
        db_ref[:, 0:Dc] += _colsum(dval)
        db_ref[:, Dc:2 * Dc] += _colsum(dgate)

    row = pl.BlockSpec((tr, Dc), lambda i: (nt - 1 - i, 0))
    wide = pl.BlockSpec((tr, 2 * Dc), lambda i: (nt - 1 - i, 0))
    return _row_call(
        "mix_a_bwd", body, (nt,),
        [row, row, wide, pl.BlockSpec((CONV_HALO, 2 * Dc), lambda i: (jnp.maximum((nt - 1 - i) * hb - 1, 0), 0)),
         pl.BlockSpec(memory_space=pl.ANY), _full(cw.shape), _full(lg.shape), _full(lb.shape)],
        [wide, _full((8, Dc)), _full((CONV_HALO, Dc)), _full((1, 2 * Dc))],
        [jax.ShapeDtypeStruct(dproj.shape, BF16), jax.ShapeDtypeStruct((8, Dc), F32),
         jax.ShapeDtypeStruct((CONV_HALO, Dc), F32), jax.ShapeDtypeStruct((1, 2 * Dc), F32)],
        scratch=[pltpu.VMEM((CONV_HALO + tr, Dc), F32), pltpu.VMEM((tr + CONV_HALO, Dc), F32),
                 pltpu.VMEM((CONV_HALO, Dc), F32), pltpu.VMEM((tr, Dc), F32)],
        aliases={4: 0}, comm=comm)(dasw, ac, proj, proj, dproj, cw, lg, lb)


def mix_b_bwd(duv, proj, dproj, lg, lb, wsp, bsp_t, comm=None):
    T, Ds = duv.shape
    G = wsp.shape[0]
    hd = Ds // G
    tr = _div(T, ROW_TILE, CHUNK)
    nt = T // tr

    def body(duv_ref, s_ref, dp_hbm, lg_ref, lb_ref, w_ref, b_ref,
             dp_ref, st_ref, dws_ref, dbs_ref, db_ref, vs, dvln):
        del dp_hbm
        i = pl.program_id(0)

        @pl.when(i == 0)
        def _():
            st_ref[...] = jnp.zeros_like(st_ref)
            dws_ref[...] = jnp.zeros_like(dws_ref)
            dbs_ref[...] = jnp.zeros_like(dbs_ref)
            db_ref[...] = jnp.zeros_like(db_ref)

        upre, vpre = s_ref[:, 0:Ds], s_ref[:, Ds:2 * Ds]
        u, v = _gelu(upre), _gelu(vpre)
        mu = _rowmean(v)
        cen = v - mu
        rstd = lax.rsqrt(_rowmean(cen * cen) + EPS)
        yv = cen * rstd
        vln = (yv * lg_ref[...] + lb_ref[...]).astype(BF16)
        duvv = duv_ref[...]
        dvs = duvv * u
        dvs_b = dvs.astype(BF16)
        mask = _spatial_mask()
        for g in range(G):
            wg = jnp.where(mask, w_ref[g], 0.0).astype(BF16)
            cols = slice(g * hd, (g + 1) * hd)
            dws = jnp.zeros((CHUNK, CHUNK), F32)
            dbs = jnp.zeros((CHUNK, 1), F32)
            for n in range(tr // CHUNK):
                rows = slice(n * CHUNK, (n + 1) * CHUNK)
                vs[rows, cols] = jnp.dot(wg, vln[rows, cols], preferred_element_type=F32) + b_ref[:, g:g + 1]
                dvln[rows, cols] = lax.dot_general(wg, dvs_b[rows, cols], (((0,), (0,)), ((), ())),
                                                   preferred_element_type=F32)
                dws = dws + lax.dot_general(dvs_b[rows, cols], vln[rows, cols], (((1,), (1,)), ((), ())),
                                            preferred_element_type=F32)
                dbs = dbs + jnp.sum(dvs[rows, cols], axis=1, keepdims=True)
            dws_ref[g] += jnp.where(mask, dws, 0.0)
            dbs_ref[:, g:g + 1] += dbs
        dvl = dvln[...]
        st_ref[0:1, :] += _colsum(dvl * yv)
        st_ref[1:2, :] += _colsum(dvl)
        dyv = dvl * lg_ref[...]
        dv = rstd * (dyv - _rowmean(dyv) - yv * _rowmean(dyv * yv))
        dupre = duvv * vs[...] * _gelu_grad(upre)
        dvpre = dv * _gelu_grad(vpre)
        dp_ref[:, 0:Ds] = dupre.astype(BF16)
        dp_ref[:, Ds:2 * Ds] = dvpre.astype(BF16)
        db_ref[:, 0:Ds] += _colsum(dupre)
        db_ref[:, Ds:2 * Ds] += _colsum(dvpre)

    wide = pl.BlockSpec((tr, 2 * Ds), lambda i: (i, 1))
    return _row_call(
        "mix_b_bwd", body, (nt,),
        [pl.BlockSpec((tr, Ds), lambda i: (i, 0)), wide, pl.BlockSpec(memory_space=pl.ANY),
         _full(lg.shape), _full(lb.shape), _full(wsp.shape), _full(bsp_t.shape)],
        [wide, _full((8, Ds)), _full(wsp.shape), _full(bsp_t.shape), _full((1, 2 * Ds))],
        [jax.ShapeDtypeStruct(dproj.shape, BF16), jax.ShapeDtypeStruct((8, Ds), F32),
         jax.ShapeDtypeStruct(wsp.shape, F32), jax.ShapeDtypeStruct(bsp_t.shape, F32),
         jax.ShapeDtypeStruct((1, 2 * Ds), F32)],
        scratch=[pltpu.VMEM((tr, Ds), F32), pltpu.VMEM((tr, Ds), F32)],
        aliases={2: 0}, comm=comm)(duv, proj, dproj, lg, lb, wsp, bsp_t)


def norm1_bwd(dh1, x, dx2, mod, g1):
    T, D = x.shape
    tr = _div(T, ROW_TILE, SUBLANE)

    def body(dh_ref, x_ref, dx2_ref, mod_ref, g_ref, gx_ref, st_ref):
        i = pl.program_id(0)
        scale = mod_ref[:, D:2 * D]
        xv = x_ref[...]
        r = lax.rsqrt(_rowmean(xv * xv) + EPS)
        xn = xv * r
        dh = dh_ref[...]
        dxn = dh * (g_ref[...] * (1.0 + scale))
        gx_ref[...] = r * (dxn - xn * _rowmean(dxn * xn)) + dx2_ref[...]

        @pl.when(i == 0)
        def _():
            st_ref[...] = jnp.zeros_like(st_ref)

        st_ref[0:1, :] += _colsum(dh)
        st_ref[1:2, :] += _colsum(dh * xn) * g_ref[...]
        st_ref[2:3, :] += _colsum(dh * xn) * (1.0 + scale)

    row = pl.BlockSpec((tr, D), lambda i: (i, 0))
    return _row_call(
        "norm1_bwd", body, (T // tr,), [row, row, row, _full(mod.shape), _full(g1.shape)], [row, _full((8, D))],
        [jax.ShapeDtypeStruct((T, D), F32), jax.ShapeDtypeStruct((8, D), F32)])(dh1, x, dx2, mod, g1)


def ada_fwd_local(c_all, w_ada, b_cols):
    B, D = c_all.shape
    Na = w_ada.shape[1]
    tn = _div(Na, 512)

    def body(c_ref, w_ref, b_ref, o_ref):
        cv = c_ref[...]
        act = (cv * _sig(cv)).astype(BF16)
        o_ref[...] = jnp.dot(act, w_ref[...].astype(BF16), preferred_element_type=F32) + b_ref[...]

    return _row_call(
        "ada_fwd_local", body, (Na // tn,),
        [_full(c_all.shape), pl.BlockSpec((D, tn), lambda j: (0, j)), pl.BlockSpec((1, tn), lambda j: (0, j))],
        pl.BlockSpec((B, tn), lambda j: (0, j)), jax.ShapeDtypeStruct((B, Na), F32), sem=("parallel",))(c_all, w_ada, b_cols)


def ada_bwd_local(c_all_t, dmod_all):
    D, B = c_all_t.shape
    Na = dmod_all.shape[1]
    tr = _div(D, 512, SUBLANE)

    def body(c_ref, d_ref, o_ref):
        cv = c_ref[...]
        act = cv * _sig(cv)
        acc = act[:, 0:1] * d_ref[0:1, :]
        for b in range(1, B):
            acc = acc + act[:, b:b + 1] * d_ref[b:b + 1, :]
        o_ref[...] = acc

    return _row_call(
        "ada_bwd_local", body, (D // tr,), [pl.BlockSpec((tr, B), lambda i: (i, 0)), _full(dmod_all.shape)],
        pl.BlockSpec((tr, Na), lambda i: (i, 0)), jax.ShapeDtypeStruct((D, Na), F32), sem=("parallel",))(c_all_t, dmod_all)


def adamw(name, w, m, v, parts):
    R, C = w.shape
    tr = _div(R, max(SUBLANE, (1 << 18) // C // SUBLANE * SUBLANE), SUBLANE)
    bc1, bc2 = 1.0 - ADAM_B1 ** ADAM_STEP, 1.0 - ADAM_B2 ** ADAM_STEP
    n = len(parts)

    def body(*refs):
        w_ref, m_ref, v_ref = refs[:3]
        g_ref, d_ref, nm_ref, nv_ref = refs[3 + n:]
        g = refs[3][...].astype(F32)
        for p in refs[4:3 + n]:
            g = g + p[...].astype(F32)
        mn = ADAM_B1 * m_ref[...] + (1.0 - ADAM_B1) * g
        vn = ADAM_B2 * v_ref[...] + (1.0 - ADAM_B2) * (g * g)
        g_ref[...] = g
        nm_ref[...] = mn
        nv_ref[...] = vn
        d_ref[...] = -ADAM_LR * ((mn / bc1) / (jnp.sqrt(vn / bc2) + ADAM_EPS) + ADAM_WD * w_ref[...])

    row = pl.BlockSpec((tr, C), lambda i: (i, 0))
    pspecs = [row if lead is None else pl.BlockSpec((None, tr, C), lambda i, lead=lead: (lead, i, 0)) for _, lead in parts]
    out = jax.ShapeDtypeStruct((R, C), F32)
    return _row_call(name, body, (R // tr,), [row, row, row] + pspecs, [row] * 4, [out] * 4, sem=("parallel",))(
        w, m, v, *[a for a, _ in parts])


def pair_add(name, g, r, idx):
    _, R, C = g.shape
    tr = _div(R, max(SUBLANE, (1 << 17) // C // SUBLANE * SUBLANE), SUBLANE)

    def body(idx_ref, g0, g1, g2, g3, r_ref, own_ref, tr_ref):
        del idx_ref
        own_ref[...] = g0[...] + r_ref[0]
        for m, gm in ((1, g1), (2, g2), (3, g3)):
            tr_ref[m - 1] = (gm[...] + r_ref[m]).astype(BF16)

    def gspec(m):
        return pl.BlockSpec((None, tr, C), lambda i, idx_ref: (idx_ref[m], i, 0))

    return pl.pallas_call(
        body, name=name,
        grid_spec=pltpu.PrefetchScalarGridSpec(
            num_scalar_prefetch=1, grid=(R // tr,),
            in_specs=[gspec(0), gspec(1), gspec(2), gspec(3), pl.BlockSpec((4, tr, C), lambda i, idx_ref: (0, i, 0))],
            out_specs=[pl.BlockSpec((tr, C), lambda i, idx_ref: (i, 0)),
                       pl.BlockSpec((3, tr, C), lambda i, idx_ref: (0, i, 0))]),
        out_shape=[jax.ShapeDtypeStruct((R, C), F32), jax.ShapeDtypeStruct((3, R, C), BF16)],
        compiler_params=_cp(("parallel",)))(idx, g, g, g, g, r)


def sum_rows(name, stacked):
    nb, _, N = stacked.shape
    tn = _div(N, 1 << 16)

    def body(s_ref, o_ref):
        acc = s_ref[0]
        for b in range(1, nb):
            acc = acc + s_ref[b]
        o_ref[...] = acc

    return _row_call(name, body, (N // tn,), [pl.BlockSpec((nb, 1, tn), lambda j: (0, 0, j))],
                     pl.BlockSpec((1, tn), lambda j: (0, j)), jax.ShapeDtypeStruct((1, N), F32), sem=("parallel",))(stacked)


def _coords():
    return lax.axis_index("x"), lax.axis_index("y"), lax.axis_index("c")


def _flip(v, bit):
    return 1 - v if bit else v


def _comm_call(name, body, ins, out_shapes, n_sems):
    any_spec = pl.BlockSpec(memory_space=pl.ANY)
    return pl.pallas_call(
        body, name=name, in_specs=[any_spec] * len(ins), out_specs=[any_spec] * len(out_shapes), out_shape=out_shapes,
        scratch_shapes=[pltpu.SemaphoreType.DMA((s,)) for s in n_sems],
        compiler_params=pltpu.CompilerParams(has_side_effects=True))(*ins)


def gather_rows(name, row):
    N = row.shape[1]

    def body(row_ref, out_ref, send_sems, recv_sems, local_sem):
        x, y, c = _coords()
        me = 4 * x + 2 * y + c
        mine = pltpu.make_async_copy(row_ref, out_ref.at[me], local_sem.at[0])
        mine.start()
        copies = []
        for k in range(1, NDEV):
            peer = (_flip(x, k & 4), _flip(y, k & 2), _flip(c, k & 1))
            copies.append(pltpu.make_async_remote_copy(
                src_ref=row_ref, dst_ref=out_ref.at[me], send_sem=send_sems.at[k - 1], recv_sem=recv_sems.at[k - 1],
                device_id=peer, device_id_type=MESH))
        for cp in copies:
            cp.start()
        for cp in copies:
            cp.wait_recv()
        for cp in copies:
            cp.wait_send()
        mine.wait()

    return _comm_call(name, body, [row], [jax.ShapeDtypeStruct((NDEV, 1, N), row.dtype)], (NDEV - 1, NDEV - 1, 1))[0]


def exchange_rows(name, slabs):
    def body(in_ref, out_ref, send_sems, recv_sems, local_sem):
        x, y, c = _coords()
        me = 4 * x + 2 * y + c
        mine = pltpu.make_async_copy(in_ref.at[me], out_ref.at[me], local_sem.at[0])
        mine.start()
        copies = []
        for k in range(1, NDEV):
            px, py, pc = _flip(x, k & 4), _flip(y, k & 2), _flip(c, k & 1)
            copies.append(pltpu.make_async_remote_copy(
                src_ref=in_ref.at[4 * px + 2 * py + pc], dst_ref=out_ref.at[me], send_sem=send_sems.at[k - 1],
                recv_sem=recv_sems.at[k - 1], device_id=(px, py, pc), device_id_type=MESH))
        for cp in copies:
            cp.start()
        for cp in copies:
            cp.wait_recv()
        for cp in copies:
            cp.wait_send()
        mine.wait()

    return _comm_call(name, body, [slabs], [jax.ShapeDtypeStruct(slabs.shape, slabs.dtype)], (NDEV - 1, NDEV - 1, 1))[0]


def _run_comm(name, comm):
    n_i, n_o = len(comm.ins), len(comm.out_shapes)

    def body(*refs):
        start, finish = comm.plan(refs[:n_i], refs[n_i:n_i + n_o], refs[n_i + n_o:])
        start()
        finish()

    return _comm_call(name, body, comm.ins, comm.out_shapes, comm.n_sems)


def gather_comm(shards):
    L = len(shards)

    def plan(ins, outs, sems):
        send_sems, recv_sems, local_sems = sems
        x, y, c = _coords()
        sibling = (x, y, 1 - c)
        chips = [(_flip(x, m & 2), _flip(y, m & 1)) for m in (1, 2, 3)]

        def slab(px, py, pc):
            return 4 * px + 2 * py + pc

        def copy(l, k, block, to, src=None):
            dst = outs[l].at[slab(*block)]
            return pltpu.make_async_remote_copy(
                src_ref=dst if src is None else src, dst_ref=dst, send_sem=send_sems.at[7 * l + k],
                recv_sem=recv_sems.at[7 * l + k], device_id=to, device_id_type=MESH)

        local = [pltpu.make_async_copy(ins[l], outs[l].at[slab(x, y, c)], local_sems.at[l]) for l in range(L)]
        first = []
        for l in range(L):
            first.append(copy(l, 0, (x, y, c), sibling, src=ins[l]))
            first += [copy(l, 1 + j, (x, y, c), (*chip, c), src=ins[l]) for j, chip in enumerate(chips)]

        def start():
            for cp in local + first:
                cp.start()

        def finish():
            passed = []
            for j, chip in enumerate(chips):
                for l in range(L):
                    copy(l, 1 + j, (*chip, c), (x, y, c)).wait_recv()
                    fwd = copy(l, 4 + j, (*chip, c), sibling)
                    fwd.start()
                    passed.append(fwd)
            for l in range(L):
                copy(l, 0, sibling, (x, y, c)).wait_recv()
                for j, chip in enumerate(chips):
                    copy(l, 4 + j, (*chip, 1 - c), (x, y, c)).wait_recv()
            for cp in first + passed:
                cp.wait_send()
            for cp in local:
                cp.wait()

        return start, finish

    outs = [jax.ShapeDtypeStruct((NDEV,) + s.shape, s.dtype) for s in shards]
    return Comm(plan, shards, outs, (7 * L, 7 * L, L))


def _all_at_once(copies):
    def start():
        for cp in copies:
            cp.start()

    def finish():
        for cp in copies:
            cp.wait_recv()
        for cp in copies:
            cp.wait_send()

    return start, finish


def sibling_comm(grads):
    L = len(grads)

    def plan(ins, outs, sems):
        send_sems, recv_sems = sems
        x, y, c = _coords()
        copies = []
        for l in range(L):
            for m in range(4):
                qm = 2 * _flip(x, m & 2) + _flip(y, m & 1)
                copies.append(pltpu.make_async_remote_copy(
                    src_ref=ins[l].at[2 * qm + (1 - c)], dst_ref=outs[l].at[m], send_sem=send_sems.at[4 * l + m],
                    recv_sem=recv_sems.at[4 * l + m], device_id=(x, y, 1 - c), device_id_type=MESH))
        return _all_at_once(copies)

    outs = [jax.ShapeDtypeStruct((4,) + g.shape[1:], g.dtype) for g in grads]
    return Comm(plan, grads, outs, (4 * L, 4 * L))


def chip_comm(transits):
    L = len(transits)

    def plan(ins, outs, sems):
        send_sems, recv_sems = sems
        x, y, c = _coords()
        copies = []
        for l in range(L):
            for m in (1, 2, 3):
                copies.append(pltpu.make_async_remote_copy(
                    src_ref=ins[l].at[m - 1], dst_ref=outs[l].at[m - 1], send_sem=send_sems.at[3 * l + m - 1],
                    recv_sem=recv_sems.at[3 * l + m - 1], device_id=(_flip(x, m & 2), _flip(y, m & 1), c),
                    device_id_type=MESH))
        return _all_at_once(copies)

    outs = [jax.ShapeDtypeStruct(t.shape, t.dtype) for t in transits]
    return Comm(plan, transits, outs, (3 * L, 3 * L))


_SMALL = ("b_ada", "norm1_g", "b_in", "conv_dw_b", "conv_ln_g", "conv_ln_b", "sgu_ln_g", "sgu_ln_b", "w_spatial",
          "b_spatial", "norm2_g", "ffn_dw_b", "final_g")
_SMALL_SHARDED = ("conv_dw_w", "ffn_dw_w")
_BIG = ("w_in", "w_conv_out", "w_sgu_out", "w_out", "w_up", "w_down")
_ORDER = ("w_ada", "b_ada", "norm1_g", "w_in", "b_in", "conv_dw_w", "conv_dw_b", "conv_ln_g", "conv_ln_b", "w_conv_out",
          "sgu_ln_g", "sgu_ln_b", "w_spatial", "b_spatial", "w_sgu_out", "w_out", "norm2_g", "w_up", "ffn_dw_w", "ffn_dw_b",
          "w_down", "final_g")


def _pack(arrays, mult):
    flat = jnp.concatenate([a.reshape(-1) for a in arrays])
    pad = (-flat.shape[0]) % mult
    return jnp.pad(flat, (0, pad)) if pad else flat


def kernel(x, c, w_ada, b_ada, norm1_g, w_in, b_in, conv_dw_w, conv_dw_b, conv_ln_g, conv_ln_b, w_conv_out, sgu_ln_g, sgu_ln_b, w_spatial, b_spatial, w_sgu_out, w_out, norm2_g, w_up, ffn_dw_w, ffn_dw_b, w_down, final_g, loss_target, m_w_ada, m_b_ada, m_norm1_g, m_w_in, m_b_in, m_conv_dw_w, m_conv_dw_b, m_conv_ln_g, m_conv_ln_b, m_w_conv_out, m_sgu_ln_g, m_sgu_ln_b, m_w_spatial, m_b_spatial, m_w_sgu_out, m_w_out, m_norm2_g, m_w_up, m_ffn_dw_w, m_ffn_dw_b, m_w_down, m_final_g, v_w_ada, v_b_ada, v_norm1_g, v_w_in, v_b_in, v_conv_dw_w, v_conv_dw_b, v_conv_ln_g, v_conv_ln_b, v_w_conv_out, v_sgu_ln_g, v_sgu_ln_b, v_w_spatial, v_b_spatial, v_w_sgu_out, v_w_out, v_norm2_g, v_w_up, v_ffn_dw_w, v_ffn_dw_b, v_w_down, v_final_g):
    W = dict(w_ada=w_ada, b_ada=b_ada, norm1_g=norm1_g, w_in=w_in, b_in=b_in, conv_dw_w=conv_dw_w, conv_dw_b=conv_dw_b,
             conv_ln_g=conv_ln_g, conv_ln_b=conv_ln_b, w_conv_out=w_conv_out, sgu_ln_g=sgu_ln_g, sgu_ln_b=sgu_ln_b,
             w_spatial=w_spatial, b_spatial=b_spatial, w_sgu_out=w_sgu_out, w_out=w_out, norm2_g=norm2_g, w_up=w_up,
             ffn_dw_w=ffn_dw_w, ffn_dw_b=ffn_dw_b, w_down=w_down, final_g=final_g)
    M = dict(w_ada=m_w_ada, b_ada=m_b_ada, norm1_g=m_norm1_g, w_in=m_w_in, b_in=m_b_in, conv_dw_w=m_conv_dw_w,
             conv_dw_b=m_conv_dw_b, conv_ln_g=m_conv_ln_g, conv_ln_b=m_conv_ln_b, w_conv_out=m_w_conv_out,
             sgu_ln_g=m_sgu_ln_g, sgu_ln_b=m_sgu_ln_b, w_spatial=m_w_spatial, b_spatial=m_b_spatial,
             w_sgu_out=m_w_sgu_out, w_out=m_w_out, norm2_g=m_norm2_g, w_up=m_w_up, ffn_dw_w=m_ffn_dw_w,
             ffn_dw_b=m_ffn_dw_b, w_down=m_w_down, final_g=m_final_g)
    V = dict(w_ada=v_w_ada, b_ada=v_b_ada, norm1_g=v_norm1_g, w_in=v_w_in, b_in=v_b_in, conv_dw_w=v_conv_dw_w,
             conv_dw_b=v_conv_dw_b, conv_ln_g=v_conv_ln_g, conv_ln_b=v_conv_ln_b, w_conv_out=v_w_conv_out,
             sgu_ln_g=v_sgu_ln_g, sgu_ln_b=v_sgu_ln_b, w_spatial=v_w_spatial, b_spatial=v_b_spatial,
             w_sgu_out=v_w_sgu_out, w_out=v_w_out, norm2_g=v_norm2_g, w_up=v_w_up, ffn_dw_w=v_ffn_dw_w,
             ffn_dw_b=v_ffn_dw_b, w_down=v_w_down, final_g=v_final_g)

    xs, tgt = x[0], loss_target[0]
    T, D = xs.shape
    Dc = conv_dw_w.shape[-1] * NDEV
    F = w_down.shape[1] * NDEV
    K31 = conv_dw_w.shape[1]
    G = w_spatial.shape[1]
    assert D == 2 * Dc and sgu_ln_g.shape[-1] == Dc and T % CHUNK == 0
    me = 4 * lax.axis_index("x") + 2 * lax.axis_index("y") + lax.axis_index("c")

    na = w_ada.shape[-1]
    c_all = gather_rows("gather_c", c).reshape(NDEV, D)
    b_cols = lax.dynamic_slice(b_ada, (0, me * na), (1, na))
    mod_cols = ada_fwd_local(c_all, w_ada[0], b_cols)
    mod = exchange_rows("exchange_mod", mod_cols.reshape(NDEV, 1, na)).reshape(1, NDEV * na)

    wbf = {k: W[k][0].astype(BF16) for k in _BIG}
    wb_in, cw_g, fw_g = _run_comm("gather_w_in", gather_comm([wbf["w_in"], conv_dw_w[0], ffn_dw_w[0]]))
    cw = jnp.transpose(cw_g, (1, 0, 2)).reshape(K31, Dc)
    fw = jnp.transpose(jnp.transpose(fw_g, (1, 0, 2)).reshape(3, 2, F), (1, 0, 2))
    fb = ffn_dw_b.reshape(2, 1, F)
    bsp_t = jnp.transpose(b_spatial[0])
    wsp = w_spatial[0]

    h1 = pre_norm("pre_norm1", xs, mod, norm1_g, 0)
    proj, (wb_co, wb_so, wb_out, wb_up) = mm_nn(
        "proj", h1, wb_in, F32, bias=b_in,
        comm=gather_comm([wbf["w_conv_out"], wbf["w_sgu_out"], wbf["w_out"], wbf["w_up"]]))
    wb_out = wb_out.reshape(1, D, D)
    (ac, asw), (wb_down,) = mix_a_fwd(proj, cw, conv_dw_b, conv_ln_g, conv_ln_b, comm=gather_comm([wbf["w_down"]]))
    wb_down = wb_down.reshape(1, F, D)
    uv = mix_b_fwd(proj, sgu_ln_g, sgu_ln_b, wsp, bsp_t)
    y_a = mm_nn("y_a", asw, wb_co, F32)
    y_b = mm_nn("y_b", uv, wb_so, F32)
    merged = merge_fwd(proj, y_a, y_b)
    o1 = mm_nn("o1", merged, wb_out, F32)
    h2 = pre_norm("pre_norm2", xs, mod, norm2_g, 1, o1=o1)
    upre = mm_nn("upre", h2, wb_up, F32, out_halves=True, tn_pref=MM_WIDE)
    f = ffn_act_fwd(upre, fw, fb)
    o2 = mm_nn("o2", f, wb_down, F32)
    dx3, do2, st_f = final_fwd_bwd(xs, o1, o2, mod, final_g.reshape(1, D), tgt)
    loss = lax.psum(st_f[3, 0], MESH_AXES)

    xq, yq, cq = lax.axis_index("x"), lax.axis_index("y"), lax.axis_index("c")
    idx = jnp.stack([2 * (2 * _flip(xq, m & 2) + _flip(yq, m & 1)) + cq for m in range(4)]).astype(jnp.int32)
    own, transit, arrived = {}, {}, {}

    def add_pairs(keys, full, from_sibling):
        for k, g_full, r in zip(keys, full, from_sibling):
            own[k], transit[k] = pair_add("pair_add_" + k, g_full, r, idx)

    df = mm_nt("df", do2, wb_down, F32, tko_pref=MM_WIDE)
    g_down = mm_tn("g_down", f, do2, 1, tko_pref=MM_WIDE).reshape(NDEV, F // NDEV, D)
    dupre, g_fw, g_fb = ffn_act_bwd(upre, df, fw, fb)
    dh2 = mm_nt("dh2", dupre, wb_up, F32, a_halves=True)
    g_up = mm_tn("g_up", h2, dupre, NDEV, g_halves=True, tn_pref=MM_WIDE)
    (dx2, do1, st_2), sib = norm2_bwd(dh2, xs, o1, dx3, mod, norm2_g, comm=sibling_comm([g_down, g_up]))
    add_pairs(("w_down", "w_up"), (g_down, g_up), sib)
    dmerged = mm_nt("dmerged", do1, wb_out, F32)
    g_out = mm_tn("g_out", merged, do1, 1).reshape(NDEV, D // NDEV, D)
    dy_a, dy_b, dproj, db_g = merge_bwd(dmerged, proj, y_a, y_b)
    dasw = mm_nt("dasw", dy_a, wb_co, F32)
    g_co = mm_tn("g_co", asw, dy_a, NDEV)
    duv = mm_nt("duv", dy_b, wb_so, F32)
    g_so = mm_tn("g_so", uv, dy_b, NDEV)
    (dproj, st_a, g_cw, db_a), (arrived["w_down"], arrived["w_up"]) = mix_a_bwd(
        dasw, ac, proj, dproj, cw, conv_ln_g, conv_ln_b, comm=chip_comm([transit["w_down"], transit["w_up"]]))
    (dproj, st_b, g_wsp, g_bsp_t, db_s), sib = mix_b_bwd(
        duv, proj, dproj, sgu_ln_g, sgu_ln_b, wsp, bsp_t, comm=sibling_comm([g_out, g_co, g_so]))
    add_pairs(("w_out", "w_conv_out", "w_sgu_out"), (g_out, g_co, g_so), sib)
    g_in = mm_tn("g_in", h1, dproj, NDEV)
    add_pairs(("w_in",), (g_in,), _run_comm("sibling_w_in", sibling_comm([g_in])))
    late = ("w_out", "w_conv_out", "w_sgu_out", "w_in")
    dh1, got = mm_nt("dh1", dproj, wb_in, F32, comm=chip_comm([transit[k] for k in late]))
    arrived.update(zip(late, got))
    grad_x, st_1 = norm1_bwd(dh1, xs, dx2, mod, norm1_g)

    dmod = jnp.concatenate([st_1[0], st_1[1], st_2[3], st_2[0], st_2[1], st_f[1]]).reshape(1, NDEV * na)
    dmod_all = exchange_rows("exchange_dmod", dmod.reshape(NDEV, 1, na)).reshape(NDEV, na)
    g_ada = ada_bwd_local(jnp.transpose(c_all), dmod_all)

    res = {}
    for k in _BIG:
        q = arrived[k]
        res[k] = adamw("adamw_" + k, W[k][0], M[k][0], V[k][0], [(own[k], None), (q, 0), (q, 1), (q, 2)])
    res["w_ada"] = adamw("adamw_w_ada", w_ada[0], m_w_ada[0], v_w_ada[0], [(g_ada, None)])

    g_small = dict(
        b_ada=dmod, norm1_g=st_1[2], b_in=jnp.concatenate([db_a, db_s, db_g], axis=1), conv_dw_b=st_a[2],
        conv_ln_g=st_a[0], conv_ln_b=st_a[1], sgu_ln_g=st_b[0], sgu_ln_b=st_b[1], w_spatial=g_wsp,
        b_spatial=jnp.transpose(g_bsp_t), norm2_g=st_2[2], ffn_dw_b=g_fb, final_g=st_f[0],
        conv_dw_w=g_cw[:K31], ffn_dw_w=jnp.transpose(g_fw, (1, 0, 2)))
    names = _SMALL + _SMALL_SHARDED
    sizes = [g_small[k].size for k in names]
    packed = _pack([g_small[k] for k in names], 8 * LANE)
    summed = sum_rows("sum_small", gather_rows("gather_small", packed.reshape(1, -1)))[0]
    offs = [sum(sizes[:i]) for i in range(len(names))]
    seg = {k: summed[o:o + s] for k, o, s in zip(names, offs, sizes)}
    n_cw, n_fw = conv_dw_w.shape[-1], ffn_dw_w.shape[-1]
    seg["conv_dw_w"] = lax.dynamic_slice(seg["conv_dw_w"].reshape(K31, Dc), (0, me * n_cw), (K31, n_cw)).reshape(-1)
    seg["ffn_dw_w"] = lax.dynamic_slice(seg["ffn_dw_w"].reshape(3, 2 * F), (0, me * n_fw), (3, n_fw)).reshape(-1)
    gp = _pack([seg[k] for k in names], 8 * LANE).reshape(-1, LANE)
    wp, mp, vp = (_pack([S[k] for k in names], 8 * LANE).reshape(-1, LANE) for S in (W, M, V))
    small = adamw("adamw_small", wp, mp, vp, [(gp, None)])
    sizes2 = [W[k].size for k in names]
    offs2 = [sum(sizes2[:i]) for i in range(len(names))]
    for k, o, s in zip(names, offs2, sizes2):
        res[k] = tuple(a.reshape(-1)[o:o + s].reshape(W[k].shape) for a in small)

    outs = [[], [], [], []]
    for k in _ORDER:
        for slot in range(4):
            outs[slot].append(res[k][slot].reshape(W[k].shape))
    return (loss, grad_x[None], *outs[0], *outs[1], *outs[2], *outs[3])
```

```python
import functools

import jax
import jax.numpy as jnp
from jax import lax
from jax.experimental import pallas as pl
from jax.experimental.pallas import tpu as pltpu

F32, BF16 = jnp.float32, jnp.bfloat16
NDEV = 8
MESH_AXES = ("x", "y", "c")
MESH = pl.DeviceIdType.MESH
EPS = 1e-6
CHUNK = 128
CONV_HALO = 32
FFN_HALO = 8
LANE, SUBLANE = 128, 8
ROW_TILE = 256
FFN_ROW_TILE = 512
STRIP = 32
STRIP_UNROLL = 2
MM_TILE = 1024
MM_WIDE = 1408
MM_DEEP = 2816
VMEM_LIMIT = 56 * 1024 * 1024
ADAM_LR, ADAM_B1, ADAM_B2, ADAM_EPS, ADAM_WD, ADAM_STEP = 0.001, 0.9, 0.999, 1e-08, 0.01, 10
SQRT_HALF = 0.7071067811865476
INV_SQRT_2PI = 0.3989422804014327


def _div(n, pref, mult=LANE):
    if n <= pref:
        return n
    for d in range(pref - pref % mult, 0, -mult):
        if n % d == 0:
            return d
    return n


def _cp(sem):
    return pltpu.CompilerParams(dimension_semantics=sem, vmem_limit_bytes=VMEM_LIMIT)


def _sig(v):
    return jax.nn.sigmoid(v)


def _gelu(v):
    return 0.5 * v * (1.0 + lax.erf(v * SQRT_HALF))


def _gelu_grad(v):
    return 0.5 * (1.0 + lax.erf(v * SQRT_HALF)) + v * (INV_SQRT_2PI * jnp.exp(-0.5 * v * v))


def _colsum(v):
    return jnp.sum(v, axis=0, keepdims=True)


def _rowmean(v):
    return jnp.mean(v, axis=-1, keepdims=True)


class Comm:
    def __init__(self, plan, ins, out_shapes, n_sems):
        self.plan, self.ins, self.out_shapes, self.n_sems = plan, list(ins), list(out_shapes), tuple(n_sems)


def _pcall(name, body, grid, in_specs, out_specs, out_shape, scratch=(), sem=None, aliases=None, comm=None):
    if comm is None:
        return pl.pallas_call(
            body, name=name, grid=grid, in_specs=in_specs, out_specs=out_specs, out_shape=out_shape,
            scratch_shapes=list(scratch), input_output_aliases=aliases or {},
            compiler_params=_cp(sem or ("arbitrary",) * len(grid)))
    single = not isinstance(out_shape, (list, tuple))
    own_specs, own_shapes = ([out_specs], [out_shape]) if single else (list(out_specs), list(out_shape))
    n_in, n_out, n_scr = len(in_specs), len(own_shapes), len(scratch)
    n_ci, n_co = len(comm.ins), len(comm.out_shapes)
    any_spec = pl.BlockSpec(memory_space=pl.ANY)

    def fused(*refs):
        ins, cins = refs[:n_in], refs[n_in:n_in + n_ci]
        outs = refs[n_in + n_ci:n_in + n_ci + n_out]
        couts = refs[n_in + n_ci + n_out:n_in + n_ci + n_out + n_co]
        scr = refs[n_in + n_ci + n_out + n_co:n_in + n_ci + n_out + n_co + n_scr]
        sems = refs[n_in + n_ci + n_out + n_co + n_scr:]
        first = functools.reduce(jnp.logical_and, [pl.program_id(d) == 0 for d in range(len(grid))])
        last = functools.reduce(jnp.logical_and, [pl.program_id(d) == grid[d] - 1 for d in range(len(grid))])

        @pl.when(first)
        def _():
            comm.plan(cins, couts, sems)[0]()

        body(*ins, *outs, *scr)

        @pl.when(last)
        def _():
            comm.plan(cins, couts, sems)[1]()

    call = pl.pallas_call(
        fused, name=name, grid=grid, in_specs=list(in_specs) + [any_spec] * n_ci,
        out_specs=own_specs + [any_spec] * n_co, out_shape=own_shapes + comm.out_shapes,
        scratch_shapes=list(scratch) + [pltpu.SemaphoreType.DMA((s,)) for s in comm.n_sems],
        input_output_aliases=aliases or {},
        compiler_params=pltpu.CompilerParams(dimension_semantics=("arbitrary",) * len(grid),
                                             vmem_limit_bytes=VMEM_LIMIT, has_side_effects=True))

    def run(*args):
        res = call(*args, *comm.ins)
        own = res[:n_out]
        return (own[0] if single else list(own)), list(res[n_out:])

    return run


def _matmul(name, a, b, *, grid, a_spec, b_spec, o_spec, out_shape, dims, acc_shape, bias=None, bias_spec=None, comm=None):
    nk = grid[2]

    def body(*refs):
        if bias is None:
            a_ref, b_ref, o_ref, *scr = refs
            bias_ref = None
        else:
            a_ref, b_ref, bias_ref, o_ref, *scr = refs
        part = lax.dot_general(a_ref[...], b_ref[...], (dims, ((), ())), preferred_element_type=F32)

        def finish(total):
            if bias_ref is not None:
                total = total + bias_ref[...]
            o_ref[...] = total.astype(o_ref.dtype)

        if nk == 1:
            finish(part)
        else:
            acc = scr[0]
            k = pl.program_id(2)

            @pl.when(k == 0)
            def _():
                acc[...] = part

            @pl.when(k > 0)
            def _():
                acc[...] += part

            @pl.when(k == nk - 1)
            def _():
                finish(acc[...])

    in_specs = [a_spec, b_spec] + ([bias_spec] if bias is not None else [])
    args = (a, b) + ((bias,) if bias is not None else ())
    return _pcall(name, body, grid, in_specs, o_spec, out_shape,
                  scratch=[pltpu.VMEM(acc_shape, F32)] if nk > 1 else [],
                  sem=("parallel", "parallel", "arbitrary"), comm=comm)(*args)


def mm_nn(name, a, wb, out_dtype, *, bias=None, out_halves=False, tn_pref=MM_TILE, comm=None):
    T, K = a.shape
    NB, _, Ns = wb.shape
    N = NB * Ns
    tm, tn, tk = _div(T, MM_TILE), _div(Ns, tn_pref), _div(K, MM_DEEP)
    npb, nj, nk = Ns // tn, N // tn, K // tk
    if out_halves:
        o_spec = pl.BlockSpec((None, tm, tn), lambda i, j, k: (j // (nj // 2), i, j % (nj // 2)))
        out_shape = jax.ShapeDtypeStruct((2, T, N // 2), out_dtype)
    else:
        o_spec = pl.BlockSpec((tm, tn), lambda i, j, k: (i, j))
        out_shape = jax.ShapeDtypeStruct((T, N), out_dtype)
    return _matmul(
        name, a, wb, grid=(T // tm, nj, nk),
        a_spec=pl.BlockSpec((tm, tk), lambda i, j, k: (i, k)),
        b_spec=pl.BlockSpec((None, tk, tn), lambda i, j, k: (j // npb, k, j % npb)),
        o_spec=o_spec, out_shape=out_shape, dims=((1,), (0,)), acc_shape=(tm, tn),
        bias=bias, bias_spec=pl.BlockSpec((1, tn), lambda i, j, k: (0, j)), comm=comm)


def mm_nt(name, a, wb, out_dtype, *, a_halves=False, tko_pref=MM_TILE, tc_pref=MM_DEEP, comm=None):
    NB, K, Ns = wb.shape
    T = a.shape[-2]
    span = Ns // 2 if a_halves and NB == 1 else Ns
    tm, tko, tc = _div(T, MM_TILE), _div(K, tko_pref), _div(span, tc_pref)
    cpb = Ns // tc
    nkk = NB * cpb
    if a_halves:
        a_spec = pl.BlockSpec((None, tm, tc), lambda i, j, k: (k // (nkk // 2), i, k % (nkk // 2)))
    else:
        a_spec = pl.BlockSpec((tm, tc), lambda i, j, k: (i, k))
    return _matmul(
        name, a, wb, grid=(T // tm, K // tko, nkk), a_spec=a_spec,
        b_spec=pl.BlockSpec((None, tko, tc), lambda i, j, k: (k // cpb, j, k % cpb)),
        o_spec=pl.BlockSpec((tm, tko), lambda i, j, k: (i, j)),
        out_shape=jax.ShapeDtypeStruct((T, K), out_dtype), dims=((1,), (1,)), acc_shape=(tm, tko), comm=comm)


def mm_tn(name, a, g, nb, *, g_halves=False, tko_pref=MM_TILE, tn_pref=MM_TILE, out_dtype=BF16):
    T, K = a.shape
    N = g.shape[-1] * (2 if g_halves else 1)
    Ns = N // nb
    tt, tko, tn = _div(T, 2 * MM_TILE), _div(K, tko_pref), _div(Ns, tn_pref)
    npb, nj = Ns // tn, N // tn
    if g_halves:
        g_spec = pl.BlockSpec((None, tt, tn), lambda i, j, t: (j // (nj // 2), t, j % (nj // 2)))
    else:
        g_spec = pl.BlockSpec((tt, tn), lambda i, j, t: (t, j))
    return _matmul(
        name, a, g, grid=(K // tko, nj, T // tt),
        a_spec=pl.BlockSpec((tt, tko), lambda i, j, t: (t, i)), b_spec=g_spec,
        o_spec=pl.BlockSpec((None, tko, tn), lambda i, j, t: (j // npb, i, j % npb)),
        out_shape=jax.ShapeDtypeStruct((nb, K, Ns), out_dtype), dims=((0,), (0,)), acc_shape=(tko, tn))


def _row_call(name, body, grid, in_specs, out_specs, out_shape, scratch=(), sem=None, aliases=None, comm=None):
    return _pcall(name, body, grid, in_specs, out_specs, out_shape, scratch, sem, aliases, comm)


def _full(shape):
    nd = len(shape)
    return pl.BlockSpec(shape, lambda *idx: (0,) * nd)


def pre_norm(name, x, mod, g, which, o1=None, comm=None):
    T, D = x.shape
    tr = _div(T, ROW_TILE, SUBLANE)

    def body(*refs):
        if o1 is None:
            x_ref, mod_ref, g_ref, h_ref = refs
            xv = x_ref[...]
        else:
            x_ref, o1_ref, mod_ref, g_ref, h_ref = refs
            xv = x_ref[...] + mod_ref[:, 2 * D:3 * D] * o1_ref[...]
        shift = mod_ref[:, (3 * which) * D:(3 * which + 1) * D]
        scale = mod_ref[:, (3 * which + 1) * D:(3 * which + 2) * D]
        r = lax.rsqrt(_rowmean(xv * xv) + EPS)
        h_ref[...] = ((xv * r) * g_ref[...] * (1.0 + scale) + shift).astype(BF16)

    row = pl.BlockSpec((tr, D), lambda i: (i, 0))
    ins = [x] + ([o1] if o1 is not None else []) + [mod, g]
    specs = [row] * (1 if o1 is None else 2) + [_full(mod.shape), _full(g.shape)]
    return _row_call(name, body, (T // tr,), specs, row, jax.ShapeDtypeStruct((T, D), BF16), sem=("parallel",),
                     comm=comm)(*ins)


def _window_taps(win, taps):
    n = win.shape[0]
    for r in range(SUBLANE):
        group = [(i, tap, off) for i, (tap, off) in enumerate(taps) if off % SUBLANE == r]
        if not group:
            continue
        shifted = win if r == 0 else pltpu.roll(win, n - r, 0)
        for i, tap, off in group:
            assert 0 <= off and off + CONV_HALO <= n
            yield i, tap, shifted[off - r:off - r + CONV_HALO]


def mix_a_fwd(proj, cw, cb, lg, lb, comm=None):
    T = proj.shape[0]
    K, Dc = cw.shape
    tr = _div(T, ROW_TILE, CONV_HALO)
    hb = tr // CONV_HALO

    def body(val_ref, gate_ref, hval_ref, hgate_ref, cw_ref, cb_ref, lg_ref, lb_ref, ac_ref, asw_ref, buf):
        i = pl.program_id(0)
        hist = hval_ref[...] * _sig(hgate_ref[...])
        buf[0:CONV_HALO, :] = jnp.where(i > 0, hist, 0.0)
        buf[CONV_HALO:CONV_HALO + tr, :] = val_ref[...] * _sig(gate_ref[...])
        base = CONV_HALO - (K - 1)
        for c0 in range(0, Dc, LANE):
            lanes = slice(c0, c0 + LANE)

            def step(s, carry):
                r0 = pl.multiple_of(s * CONV_HALO, CONV_HALO)
                win = buf[pl.ds(r0, 2 * CONV_HALO), lanes]
                acc = jnp.zeros((CONV_HALO, LANE), F32)
                for _, k, piece in _window_taps(win, [(k, base + k) for k in range(K)]):
                    acc = acc + piece * cw_ref[k:k + 1, lanes]
                ac_ref[pl.ds(r0, CONV_HALO), lanes] = acc + cb_ref[:, lanes]
                return carry

            lax.fori_loop(0, tr // CONV_HALO, step, 0)
        ac = ac_ref[...]
        mu = _rowmean(ac)
        cen = ac - mu
        y = cen * lax.rsqrt(_rowmean(cen * cen) + EPS)
        aln = y * lg_ref[...] + lb_ref[...]
        asw_ref[...] = (aln * _sig(aln)).astype(BF16)

    def halo(col):
        return pl.BlockSpec((CONV_HALO, Dc), lambda i: (jnp.maximum(i * hb - 1, 0), col))

    row = pl.BlockSpec((tr, Dc), lambda i: (i, 0))
    return _row_call(
        "mix_a_fwd", body, (T // tr,),
        [row, pl.BlockSpec((tr, Dc), lambda i: (i, 1)), halo(0), halo(1),
         _full(cw.shape), _full(cb.shape), _full(lg.shape), _full(lb.shape)],
        [row, row], [jax.ShapeDtypeStruct((T, Dc), F32), jax.ShapeDtypeStruct((T, Dc), BF16)],
        scratch=[pltpu.VMEM((CONV_HALO + tr, Dc), F32)], sem=("parallel",), comm=comm)(proj, proj, proj, proj, cw, cb, lg, lb)


def _spatial_mask():
    t = lax.broadcasted_iota(jnp.int32, (CHUNK, CHUNK), 0)
    s = lax.broadcasted_iota(jnp.int32, (CHUNK, CHUNK), 1)
    return s <= t


def mix_b_fwd(proj, lg, lb, wsp, bsp_t):
    T = proj.shape[0]
    Ds = lg.shape[-1]
    G = wsp.shape[0]
    hd = Ds // G
    tr = _div(T, ROW_TILE, CHUNK)

    def body(u_ref, v_ref, lg_ref, lb_ref, w_ref, b_ref, uv_ref, vs):
        v = _gelu(v_ref[...])
        mu = _rowmean(v)
        cen = v - mu
        vln = (cen * lax.rsqrt(_rowmean(cen * cen) + EPS) * lg_ref[...] + lb_ref[...]).astype(BF16)
        mask = _spatial_mask()
        for g in range(G):
            wg = jnp.where(mask, w_ref[g], 0.0).astype(BF16)
            for n in range(tr // CHUNK):
                rows, cols = slice(n * CHUNK, (n + 1) * CHUNK), slice(g * hd, (g + 1) * hd)
                vs[rows, cols] = jnp.dot(wg, vln[rows, cols], preferred_element_type=F32) + b_ref[:, g:g + 1]
        uv_ref[...] = (_gelu(u_ref[...]) * vs[...]).astype(BF16)

    return _row_call(
        "mix_b_fwd", body, (T // tr,),
        [pl.BlockSpec((tr, Ds), lambda i: (i, 2)), pl.BlockSpec((tr, Ds), lambda i: (i, 3)),
         _full(lg.shape), _full(lb.shape), _full(wsp.shape), _full(bsp_t.shape)],
        pl.BlockSpec((tr, Ds), lambda i: (i, 0)), jax.ShapeDtypeStruct((T, Ds), BF16),
        scratch=[pltpu.VMEM((tr, Ds), F32)], sem=("parallel",))(proj, proj, lg, lb, wsp, bsp_t)


def merge_fwd(proj, y_a, y_b):
    T, D = y_a.shape
    tr = _div(T, ROW_TILE, SUBLANE)

    def body(g_ref, ya_ref, yb_ref, o_ref):
        o_ref[...] = (_sig(g_ref[:, 0:D]) * ya_ref[...] + _sig(g_ref[:, D:2 * D]) * yb_ref[...]).astype(BF16)

    row = pl.BlockSpec((tr, D), lambda i: (i, 0))
    return _row_call("merge_fwd", body, (T // tr,), [pl.BlockSpec((tr, 2 * D), lambda i: (i, 1)), row, row], row,
                     jax.ShapeDtypeStruct((T, D), BF16), sem=("parallel",))(proj, y_a, y_b)


def _fold(v):
    acc = v[0:SUBLANE]
    for r in range(SUBLANE, v.shape[0], SUBLANE):
        acc = acc + v[r:r + SUBLANE]
    return acc


def _conv3(prev, cur, w):
    win = jnp.concatenate([prev, cur], axis=0)
    n = win.shape[0]
    x1 = pltpu.roll(win, 1, 0)[FFN_HALO:n]
    x2 = pltpu.roll(win, 2, 0)[FFN_HALO:n]
    return x2 * w[0] + x1 * w[1] + cur * w[2], (x2, x1, cur)


def _strip_taps(w_ref, b_ref, lanes):
    w = [[w_ref[h, k:k + 1, lanes] for k in range(3)] for h in range(2)]
    b = [b_ref[h, :, lanes] for h in range(2)]
    return w, b


def ffn_act_fwd(upre, fw, fb):
    _, T, F = upre.shape
    tr = _div(T, FFN_ROW_TILE, STRIP)
    cb = _div(F, MM_WIDE)
    hb = tr // FFN_HALO
    ns = tr // STRIP

    def body(x_ref, h_ref, w_ref, b_ref, f_ref):
        i = pl.program_id(0)
        for c0 in range(0, cb, LANE):
            lanes = slice(c0, c0 + LANE)
            w, b = _strip_taps(w_ref, b_ref, lanes)

            def strip(r0, prev):
                up = [_conv3(prev(h), x_ref[h, pl.ds(r0, STRIP), lanes], w[h])[0] + b[h] for h in range(2)]
                f_ref[pl.ds(r0, STRIP), lanes] = (up[1] * _sig(up[1]) * up[0]).astype(BF16)

            strip(0, lambda h: jnp.where(i > 0, h_ref[h, :, lanes], 0.0))

            def step(s, carry):
                r0 = pl.multiple_of(s * STRIP, STRIP)
                strip(r0, lambda h: x_ref[h, pl.ds(pl.multiple_of(r0 - FFN_HALO, FFN_HALO), FFN_HALO), lanes])
                return carry

            lax.fori_loop(1, ns, step, 0, unroll=STRIP_UNROLL)

    return _row_call(
        "ffn_act_fwd", body, (T // tr, F // cb),
        [pl.BlockSpec((2, tr, cb), lambda i, j: (0, i, j)),
         pl.BlockSpec((2, FFN_HALO, cb), lambda i, j: (0, jnp.maximum(i * hb - 1, 0), j)),
         pl.BlockSpec((2, 3, cb), lambda i, j: (0, 0, j)), pl.BlockSpec((2, 1, cb), lambda i, j: (0, 0, j))],
        pl.BlockSpec((tr, cb), lambda i, j: (i, j)), jax.ShapeDtypeStruct((T, F), BF16),
        sem=("parallel", "parallel"))(upre, upre, fw, fb)


def final_fwd_bwd(x, o1, o2, mod, gf, target):
    T, D = x.shape
    tr = _div(T, ROW_TILE, SUBLANE)
    nt = T // tr

    def body(x_ref, o1_ref, o2_ref, mod_ref, gf_ref, t_ref, dx3_ref, do2_ref, st_ref):
        i = pl.program_id(0)
        gate1, gate2 = mod_ref[:, 2 * D:3 * D], mod_ref[:, 5 * D:6 * D]
        o2v = o2_ref[...]
        x3 = x_ref[...] + gate1 * o1_ref[...] + gate2 * o2v
        r = lax.rsqrt(_rowmean(x3 * x3) + EPS)
        xn = x3 * r
        err = xn * gf_ref[...] - t_ref[...]
        dy = err * (1.0 / D)
        dxn = dy * gf_ref[...]
        dx3 = r * (dxn - xn * _rowmean(dxn * xn))
        dx3_ref[...] = dx3
        do2_ref[...] = (dx3 * gate2).astype(BF16)

        @pl.when(i == 0)
        def _():
            st_ref[...] = jnp.zeros_like(st_ref)

        st_ref[0:1, :] += _colsum(dy * xn)
        st_ref[1:2, :] += _colsum(dx3 * o2v)
        st_ref[2:3, :] += _colsum(err * err) * (0.5 / D)

        @pl.when(i == nt - 1)
        def _():
            st_ref[3:4, :] = jnp.zeros((1, D), F32) + jnp.sum(st_ref[2:3, :])

    row = pl.BlockSpec((tr, D), lambda i: (i, 0))
    return _row_call(
        "final_fwd_bwd", body, (nt,), [row, row, row, _full(mod.shape), _full(gf.shape), row],
        [row, row, _full((8, D))],
        [jax.ShapeDtypeStruct((T, D), F32), jax.ShapeDtypeStruct((T, D), BF16), jax.ShapeDtypeStruct((8, D), F32)],
    )(x, o1, o2, mod, gf, target)


def ffn_act_bwd(upre, df, fw, fb):
    _, T, F = upre.shape
    tr = _div(T, FFN_ROW_TILE, STRIP)
    cb = _div(F, MM_WIDE)
    hb = tr // FFN_HALO
    nt = T // tr
    ns = tr // STRIP

    def body(x_ref, h_ref, df_ref, w_ref, b_ref, dpre_ref, dw_ref, db_ref, carry):
        i = pl.program_id(1)
        ri = nt - 1 - i

        @pl.when(i == 0)
        def _():
            carry[...] = jnp.zeros_like(carry)
            dw_ref[...] = jnp.zeros_like(dw_ref)
            db_ref[...] = jnp.zeros_like(db_ref)

        for c0 in range(0, cb, LANE):
            lanes = slice(c0, c0 + LANE)
            w, b = _strip_taps(w_ref, b_ref, lanes)

            def strip(r0, prev, state):
                later, db, dw = state
                up, taps = [], []
                for h in range(2):
                    conv, xs3 = _conv3(prev(h), x_ref[h, pl.ds(r0, STRIP), lanes], w[h])
                    up.append(conv + b[h])
                    taps.append(xs3)
                val, gt = up
                sg = _sig(gt)
                dfv = df_ref[pl.ds(r0, STRIP), lanes]
                dup = (dfv * (gt * sg), dfv * val * (sg * (1.0 + gt * (1.0 - sg))))
                db = tuple(db[h] + _fold(dup[h]) for h in range(2))
                dw = tuple(tuple(dw[h][k] + _fold(dup[h] * taps[h][k]) for k in range(3)) for h in range(2))
                for h in range(2):
                    dwin = jnp.concatenate([dup[h], later[h]], axis=0)
                    n = dwin.shape[0]
                    d1 = pltpu.roll(dwin, n - 1, 0)[0:STRIP]
                    d2 = pltpu.roll(dwin, n - 2, 0)[0:STRIP]
                    dpre_ref[h, pl.ds(r0, STRIP), lanes] = (dup[h] * w[h][2] + d1 * w[h][1] + d2 * w[h][0]).astype(BF16)
                return tuple(dup[h][0:FFN_HALO] for h in range(2)), db, dw

            zero = jnp.zeros((SUBLANE, LANE), F32)
            state = ((carry[0, :, lanes], carry[1, :, lanes]), (zero, zero), ((zero,) * 3,) * 2)

            def step(s, state):
                r0 = pl.multiple_of((ns - 1 - s) * STRIP, STRIP)
                return strip(r0, lambda h: x_ref[h, pl.ds(pl.multiple_of(r0 - FFN_HALO, FFN_HALO), FFN_HALO), lanes], state)

            state = lax.fori_loop(0, ns - 1, step, state, unroll=STRIP_UNROLL)
            later, db, dw = strip(0, lambda h: jnp.where(ri > 0, h_ref[h, :, lanes], 0.0), state)
            for h in range(2):
                carry[h, :, lanes] = later[h]
                db_ref[h, :, lanes] += _colsum(db[h])
                for k in range(3):
                    dw_ref[h, k:k + 1, lanes] += _colsum(dw[h][k])

    return _row_call(
        "ffn_act_bwd", body, (F // cb, nt),
        [pl.BlockSpec((2, tr, cb), lambda j, i: (0, nt - 1 - i, j)),
         pl.BlockSpec((2, FFN_HALO, cb), lambda j, i: (0, jnp.maximum((nt - 1 - i) * hb - 1, 0), j)),
         pl.BlockSpec((tr, cb), lambda j, i: (nt - 1 - i, j)),
         pl.BlockSpec((2, 3, cb), lambda j, i: (0, 0, j)), pl.BlockSpec((2, 1, cb), lambda j, i: (0, 0, j))],
        [pl.BlockSpec((2, tr, cb), lambda j, i: (0, nt - 1 - i, j)),
         pl.BlockSpec((2, 3, cb), lambda j, i: (0, 0, j)), pl.BlockSpec((2, 1, cb), lambda j, i: (0, 0, j))],
        [jax.ShapeDtypeStruct((2, T, F), BF16), jax.ShapeDtypeStruct((2, 3, F), F32), jax.ShapeDtypeStruct((2, 1, F), F32)],
        scratch=[pltpu.VMEM((2, FFN_HALO, cb), F32)],
        sem=("parallel", "arbitrary"))(upre, upre, df, fw, fb)


def norm2_bwd(dh2, x, o1, dx3, mod, g2, comm=None):
    T, D = x.shape
    tr = _div(T, ROW_TILE, SUBLANE)

    def body(dh_ref, x_ref, o1_ref, dx3_ref, mod_ref, g_ref, dx2_ref, do1_ref, st_ref):
        i = pl.program_id(0)
        gate1, scale = mod_ref[:, 2 * D:3 * D], mod_ref[:, 4 * D:5 * D]
        o1v = o1_ref[...]
        x2 = x_ref[...] + gate1 * o1v
        r = lax.rsqrt(_rowmean(x2 * x2) + EPS)
        xn = x2 * r
        dh = dh_ref[...]
        dxn = dh * (g_ref[...] * (1.0 + scale))
        dx2 = r * (dxn - xn * _rowmean(dxn * xn)) + dx3_ref[...]
        dx2_ref[...] = dx2
        do1_ref[...] = (dx2 * gate1).astype(BF16)

        @pl.when(i == 0)
        def _():
            st_ref[...] = jnp.zeros_like(st_ref)

        st_ref[0:1, :] += _colsum(dh)
        st_ref[1:2, :] += _colsum(dh * xn) * g_ref[...]
        st_ref[2:3, :] += _colsum(dh * xn) * (1.0 + scale)
        st_ref[3:4, :] += _colsum(dx2 * o1v)

    row = pl.BlockSpec((tr, D), lambda i: (i, 0))
    return _row_call(
        "norm2_bwd", body, (T // tr,), [row, row, row, row, _full(mod.shape), _full(g2.shape)],
        [row, row, _full((8, D))],
        [jax.ShapeDtypeStruct((T, D), F32), jax.ShapeDtypeStruct((T, D), BF16), jax.ShapeDtypeStruct((8, D), F32)],
        comm=comm)(dh2, x, o1, dx3, mod, g2)


def merge_bwd(dmerged, proj, y_a, y_b):
    T, D = y_a.shape
    tr = _div(T, ROW_TILE, SUBLANE)

    def body(dm_ref, g_ref, ya_ref, yb_ref, dya_ref, dyb_ref, dp_ref, db_ref):
        i = pl.program_id(0)
        dm = dm_ref[...]
        sa, sb = _sig(g_ref[:, 0:D]), _sig(g_ref[:, D:2 * D])
        dya_ref[...] = (dm * sa).astype(BF16)
        dyb_ref[...] = (dm * sb).astype(BF16)
        dga = dm * ya_ref[...] * (sa * (1.0 - sa))
        dgb = dm * yb_ref[...] * (sb * (1.0 - sb))
        dp_ref[:, 0:D] = dga.astype(BF16)
        dp_ref[:, D:2 * D] = dgb.astype(BF16)

        @pl.when(i == 0)
        def _():
            db_ref[...] = jnp.zeros_like(db_ref)

        db_ref[:, 0:D] += _colsum(dga)
        db_ref[:, D:2 * D] += _colsum(dgb)

    row = pl.BlockSpec((tr, D), lambda i: (i, 0))
    wide = pl.BlockSpec((tr, 2 * D), lambda i: (i, 1))
    return _row_call(
        "merge_bwd", body, (T // tr,), [row, wide, row, row], [row, row, wide, _full((1, 2 * D))],
        [jax.ShapeDtypeStruct((T, D), BF16), jax.ShapeDtypeStruct((T, D), BF16),
         jax.ShapeDtypeStruct((T, 4 * D), BF16), jax.ShapeDtypeStruct((1, 2 * D), F32)],
    )(dmerged, proj, y_a, y_b)


def mix_a_bwd(dasw, ac, proj, dproj, cw, lg, lb, comm=None):
    T, Dc = ac.shape
    K = cw.shape[0]
    tr = _div(T, ROW_TILE, CONV_HALO)
    hb = tr // CONV_HALO
    nt = T // tr

    def body(dasw_ref, ac_ref, in_ref, hin_ref, dp_hbm, cw_ref, lg_ref, lb_ref,
             dp_ref, st_ref, dcw_ref, db_ref, abuf, dbuf, carry, da_buf):
        del dp_hbm
        i = pl.program_id(0)
        ri = nt - 1 - i
        acv = ac_ref[...]
        mu = _rowmean(acv)
        cen = acv - mu
        rstd = lax.rsqrt(_rowmean(cen * cen) + EPS)
        y = cen * rstd
        aln = y * lg_ref[...] + lb_ref[...]
        sg = _sig(aln)
        daln = dasw_ref[...] * (sg * (1.0 + aln * (1.0 - sg)))
        dy = daln * lg_ref[...]
        dac = rstd * (dy - _rowmean(dy) - y * _rowmean(dy * y))

        @pl.when(i == 0)
        def _():
            carry[...] = jnp.zeros_like(carry)
            st_ref[...] = jnp.zeros_like(st_ref)
            dcw_ref[...] = jnp.zeros_like(dcw_ref)
            db_ref[...] = jnp.zeros_like(db_ref)

        st_ref[0:1, :] += _colsum(daln * y)
        st_ref[1:2, :] += _colsum(daln)
        st_ref[2:3, :] += _colsum(dac)
        val, gate = in_ref[:, 0:Dc], in_ref[:, Dc:2 * Dc]
        sgg = _sig(gate)
        hist = hin_ref[:, 0:Dc] * _sig(hin_ref[:, Dc:2 * Dc])
        abuf[0:CONV_HALO, :] = jnp.where(ri > 0, hist, 0.0)
        abuf[CONV_HALO:CONV_HALO + tr, :] = val * sgg
        dbuf[0:tr, :] = dac
        dbuf[tr:tr + CONV_HALO, :] = carry[...]
        carry[...] = dac[0:CONV_HALO, :]
        base = CONV_HALO - (K - 1)
        zero = jnp.zeros((SUBLANE, LANE), F32)
        for c0 in range(0, Dc, LANE):
            lanes = slice(c0, c0 + LANE)

            def step(s, dws):
                r0 = pl.multiple_of(s * CONV_HALO, CONV_HALO)
                dwin = dbuf[pl.ds(r0, 2 * CONV_HALO), lanes]
                piece_dac = dwin[0:CONV_HALO]
                dws = list(dws)
                for _, k, piece in _window_taps(abuf[pl.ds(r0, 2 * CONV_HALO), lanes], [(k, base + k) for k in range(K)]):
                    dws[k] = dws[k] + _fold(piece_dac * piece)
                acc = jnp.zeros((CONV_HALO, LANE), F32)
                for _, k, piece in _window_taps(dwin, [(k, K - 1 - k) for k in range(K)]):
                    acc = acc + piece * cw_ref[k:k + 1, lanes]
                da_buf[pl.ds(r0, CONV_HALO), lanes] = acc
                return tuple(dws)

            dws = lax.fori_loop(0, tr // CONV_HALO, step, (zero,) * K)
            for k in range(K):
                dcw_ref[k:k + 1, lanes] += _colsum(dws[k])
        da = da_buf[...]
        dval = da * sgg
        dgate = da * val * (sgg * (1.0 - sgg))
        dp_ref[:, 0:Dc] = dval.astype(BF16)
        dp_ref[:, Dc:2 * Dc] = dgate.astype(BF16)
        db_ref[:, 0:Dc] += _colsum(dval)
        db_ref[:, Dc:2 * Dc] += _colsum(dgate)

    row = pl.BlockSpec((tr, Dc), lambda i: (nt - 1 - i, 0))
    wide = pl.BlockSpec((tr, 2 * Dc), lambda i: (nt - 1 - i, 0))
    return _row_call(
        "mix_a_bwd", body, (nt,),
        [row, row, wide, pl.BlockSpec((CONV_HALO, 2 * Dc), lambda i: (jnp.maximum((nt - 1 - i) * hb - 1, 0), 0)),
         pl.BlockSpec(memory_space=pl.ANY), _full(cw.shape), _full(lg.shape), _full(lb.shape)],
        [wide, _full((8, Dc)), _full((CONV_HALO, Dc)), _full((1, 2 * Dc))],
        [jax.ShapeDtypeStruct(dproj.shape, BF16), jax.ShapeDtypeStruct((8, Dc), F32),
         jax.ShapeDtypeStruct((CONV_HALO, Dc), F32), jax.ShapeDtypeStruct((1, 2 * Dc), F32)],
        scratch=[pltpu.VMEM((CONV_HALO + tr, Dc), F32), pltpu.VMEM((tr + CONV_HALO, Dc), F32),
                 pltpu.VMEM((CONV_HALO, Dc), F32), pltpu.VMEM((tr, Dc), F32)],
        aliases={4: 0}, comm=comm)(dasw, ac, proj, proj, dproj, cw, lg, lb)


def mix_b_bwd(duv, proj, dproj, lg, lb, wsp, bsp_t, comm=None):
    T, Ds = duv.shape
    G = wsp.shape[0]
    hd = Ds // G
    tr = _div(T, ROW_TILE, CHUNK)
    nt = T // tr

    def body(duv_ref, s_ref, dp_hbm, lg_ref, lb_ref, w_ref, b_ref,
             dp_ref, st_ref, dws_ref, dbs_ref, db_ref, vs, dvln):
        del dp_hbm
        i = pl.program_id(0)

        @pl.when(i == 0)
        def _():
            st_ref[...] = jnp.zeros_like(st_ref)
            dws_ref[...] = jnp.zeros_like(dws_ref)
            dbs_ref[...] = jnp.zeros_like(dbs_ref)
            db_ref[...] = jnp.zeros_like(db_ref)

        upre, vpre = s_ref[:, 0:Ds], s_ref[:, Ds:2 * Ds]
        u, v = _gelu(upre), _gelu(vpre)
        mu = _rowmean(v)
        cen = v - mu
        rstd = lax.rsqrt(_rowmean(cen * cen) + EPS)
        yv = cen * rstd
        vln = (yv * lg_ref[...] + lb_ref[...]).astype(BF16)
        duvv = duv_ref[...]
        dvs = duvv * u
        dvs_b = dvs.astype(BF16)
        mask = _spatial_mask()
        for g in range(G):
            wg = jnp.where(mask, w_ref[g], 0.0).astype(BF16)
            cols = slice(g * hd, (g + 1) * hd)
            dws = jnp.zeros((CHUNK, CHUNK), F32)
            dbs = jnp.zeros((CHUNK, 1), F32)
            for n in range(tr // CHUNK):
                rows = slice(n * CHUNK, (n + 1) * CHUNK)
                vs[rows, cols] = jnp.dot(wg, vln[rows, cols], preferred_element_type=F32) + b_ref[:, g:g + 1]
                dvln[rows, cols] = lax.dot_general(wg, dvs_b[rows, cols], (((0,), (0,)), ((), ())),
                                                   preferred_element_type=F32)
                dws = dws + lax.dot_general(dvs_b[rows, cols], vln[rows, cols], (((1,), (1,)), ((), ())),
                                            preferred_element_type=F32)
                dbs = dbs + jnp.sum(dvs[rows, cols], axis=1, keepdims=True)
            dws_ref[g] += jnp.where(mask, dws, 0.0)
            dbs_ref[:, g:g + 1] += dbs
        dvl = dvln[...]
        st_ref[0:1, :] += _colsum(dvl * yv)
        st_ref[1:2, :] += _colsum(dvl)
        dyv = dvl * lg_ref[...]
        dv = rstd * (dyv - _rowmean(dyv) - yv * _rowmean(dyv * yv))
        dupre = duvv * vs[...] * _gelu_grad(upre)
        dvpre = dv * _gelu_grad(vpre)
        dp_ref[:, 0:Ds] = dupre.astype(BF16)
        dp_ref[:, Ds:2 * Ds] = dvpre.astype(BF16)
        db_ref[:, 0:Ds] += _colsum(dupre)
        db_ref[:, Ds:2 * Ds] += _colsum(dvpre)

    wide = pl.BlockSpec((tr, 2 * Ds), lambda i: (i, 1))
    return _row_call(
        "mix_b_bwd", body, (nt,),
        [pl.BlockSpec((tr, Ds), lambda i: (i, 0)), wide, pl.BlockSpec(memory_space=pl.ANY),
         _full(lg.shape), _full(lb.shape), _full(wsp.shape), _full(bsp_t.shape)],
        [wide, _full((8, Ds)), _full(wsp.shape), _full(bsp_t.shape), _full((1, 2 * Ds))],
        [jax.ShapeDtypeStruct(dproj.shape, BF16), jax.ShapeDtypeStruct((8, Ds), F32),
         jax.ShapeDtypeStruct(wsp.shape, F32), jax.ShapeDtypeStruct(bsp_t.shape, F32),
         jax.ShapeDtypeStruct((1, 2 * Ds), F32)],
        scratch=[pltpu.VMEM((tr, Ds), F32), pltpu.VMEM((tr, Ds), F32)],
        aliases={2: 0}, comm=comm)(duv, proj, dproj, lg, lb, wsp, bsp_t)


def norm1_bwd(dh1, x, dx2, mod, g1):
    T, D = x.shape
    tr = _div(T, ROW_TILE, SUBLANE)

    def body(dh_ref, x_ref, dx2_ref, mod_ref, g_ref, gx_ref, st_ref):
        i = pl.program_id(0)
        scale = mod_ref[:, D:2 * D]
        xv = x_ref[...]
        r = lax.rsqrt(_rowmean(xv * xv) + EPS)
        xn = xv * r
        dh = dh_ref[...]
        dxn = dh * (g_ref[...] * (1.0 + scale))
        gx_ref[...] = r * (dxn - xn * _rowmean(dxn * xn)) + dx2_ref[...]

        @pl.when(i == 0)
        def _():
            st_ref[...] = jnp.zeros_like(st_ref)

        st_ref[0:1, :] += _colsum(dh)
        st_ref[1:2, :] += _colsum(dh * xn) * g_ref[...]
        st_ref[2:3, :] += _colsum(dh * xn) * (1.0 + scale)

    row = pl.BlockSpec((tr, D), lambda i: (i, 0))
    return _row_call(
        "norm1_bwd", body, (T // tr,), [row, row, row, _full(mod.shape), _full(g1.shape)], [row, _full((8, D))],
        [jax.ShapeDtypeStruct((T, D), F32), jax.ShapeDtypeStruct((8, D), F32)])(dh1, x, dx2, mod, g1)


def ada_fwd_local(c_all, w_ada, b_cols):
    B, D = c_all.shape
    Na = w_ada.shape[1]
    tn = _div(Na, 512)

    def body(c_ref, w_ref, b_ref, o_ref):
        cv = c_ref[...]
        act = (cv * _sig(cv)).astype(BF16)
        o_ref[...] = jnp.dot(act, w_ref[...].astype(BF16), preferred_element_type=F32) + b_ref[...]

    return _row_call(
        "ada_fwd_local", body, (Na // tn,),
        [_full(c_all.shape), pl.BlockSpec((D, tn), lambda j: (0, j)), pl.BlockSpec((1, tn), lambda j: (0, j))],
        pl.BlockSpec((B, tn), lambda j: (0, j)), jax.ShapeDtypeStruct((B, Na), F32), sem=("parallel",))(c_all, w_ada, b_cols)


def ada_bwd_local(c_all_t, dmod_all):
    D, B = c_all_t.shape
    Na = dmod_all.shape[1]
    tr = _div(D, 512, SUBLANE)

    def body(c_ref, d_ref, o_ref):
        cv = c_ref[...]
        act = cv * _sig(cv)
        acc = act[:, 0:1] * d_ref[0:1, :]
        for b in range(1, B):
            acc = acc + act[:, b:b + 1] * d_ref[b:b + 1, :]
        o_ref[...] = acc

    return _row_call(
        "ada_bwd_local", body, (D // tr,), [pl.BlockSpec((tr, B), lambda i: (i, 0)), _full(dmod_all.shape)],
        pl.BlockSpec((tr, Na), lambda i: (i, 0)), jax.ShapeDtypeStruct((D, Na), F32), sem=("parallel",))(c_all_t, dmod_all)


def adamw(name, w, m, v, parts):
    R, C = w.shape
    tr = _div(R, max(SUBLANE, (1 << 18) // C // SUBLANE * SUBLANE), SUBLANE)
    bc1, bc2 = 1.0 - ADAM_B1 ** ADAM_STEP, 1.0 - ADAM_B2 ** ADAM_STEP
    n = len(parts)

    def body(*refs):
        w_ref, m_ref, v_ref = refs[:3]
        g_ref, d_ref, nm_ref, nv_ref = refs[3 + n:]
        g = refs[3][...].astype(F32)
        for p in refs[4:3 + n]:
            g = g + p[...].astype(F32)
        mn = ADAM_B1 * m_ref[...] + (1.0 - ADAM_B1) * g
        vn = ADAM_B2 * v_ref[...] + (1.0 - ADAM_B2) * (g * g)
        g_ref[...] = g
        nm_ref[...] = mn
        nv_ref[...] = vn
        d_ref[...] = -ADAM_LR * ((mn / bc1) / (jnp.sqrt(vn / bc2) + ADAM_EPS) + ADAM_WD * w_ref[...])

    row = pl.BlockSpec((tr, C), lambda i: (i, 0))
    pspecs = [row if lead is None else pl.BlockSpec((None, tr, C), lambda i, lead=lead: (lead, i, 0)) for _, lead in parts]
    out = jax.ShapeDtypeStruct((R, C), F32)
    return _row_call(name, body, (R // tr,), [row, row, row] + pspecs, [row] * 4, [out] * 4, sem=("parallel",))(
        w, m, v, *[a for a, _ in parts])


def pair_add(name, g, r, idx):
    _, R, C = g.shape
    tr = _div(R, max(SUBLANE, (1 << 17) // C // SUBLANE * SUBLANE), SUBLANE)

    def body(idx_ref, g0, g1, g2, g3, r_ref, own_ref, tr_ref):
        del idx_ref
        own_ref[...] = g0[...].astype(F32) + r_ref[0].astype(F32)
        for m, gm in ((1, g1), (2, g2), (3, g3)):
            tr_ref[m - 1] = (gm[...].astype(F32) + r_ref[m].astype(F32)).astype(BF16)

    def gspec(m):
        return pl.BlockSpec((None, tr, C), lambda i, idx_ref: (idx_ref[m], i, 0))

    return pl.pallas_call(
        body, name=name,
        grid_spec=pltpu.PrefetchScalarGridSpec(
            num_scalar_prefetch=1, grid=(R // tr,),
            in_specs=[gspec(0), gspec(1), gspec(2), gspec(3), pl.BlockSpec((4, tr, C), lambda i, idx_ref: (0, i, 0))],
            out_specs=[pl.BlockSpec((tr, C), lambda i, idx_ref: (i, 0)),
                       pl.BlockSpec((3, tr, C), lambda i, idx_ref: (0, i, 0))]),
        out_shape=[jax.ShapeDtypeStruct((R, C), F32), jax.ShapeDtypeStruct((3, R, C), BF16)],
        compiler_params=_cp(("parallel",)))(idx, g, g, g, g, r)


def sum_rows(name, stacked):
    nb, _, N = stacked.shape
    tn = _div(N, 1 << 16)

    def body(s_ref, o_ref):
        acc = s_ref[0]
        for b in range(1, nb):
            acc = acc + s_ref[b]
        o_ref[...] = acc

    return _row_call(name, body, (N // tn,), [pl.BlockSpec((nb, 1, tn), lambda j: (0, 0, j))],
                     pl.BlockSpec((1, tn), lambda j: (0, j)), jax.ShapeDtypeStruct((1, N), F32), sem=("parallel",))(stacked)


def _coords():
    return lax.axis_index("x"), lax.axis_index("y"), lax.axis_index("c")


def _flip(v, bit):
    return 1 - v if bit else v


def _comm_call(name, body, ins, out_shapes, n_sems):
    any_spec = pl.BlockSpec(memory_space=pl.ANY)
    return pl.pallas_call(
        body, name=name, in_specs=[any_spec] * len(ins), out_specs=[any_spec] * len(out_shapes), out_shape=out_shapes,
        scratch_shapes=[pltpu.SemaphoreType.DMA((s,)) for s in n_sems],
        compiler_params=pltpu.CompilerParams(has_side_effects=True))(*ins)


def gather_rows(name, row):
    N = row.shape[1]

    def body(row_ref, out_ref, send_sems, recv_sems, local_sem):
        x, y, c = _coords()
        me = 4 * x + 2 * y + c
        mine = pltpu.make_async_copy(row_ref, out_ref.at[me], local_sem.at[0])
        mine.start()
        copies = []
        for k in range(1, NDEV):
            peer = (_flip(x, k & 4), _flip(y, k & 2), _flip(c, k & 1))
            copies.append(pltpu.make_async_remote_copy(
                src_ref=row_ref, dst_ref=out_ref.at[me], send_sem=send_sems.at[k - 1], recv_sem=recv_sems.at[k - 1],
                device_id=peer, device_id_type=MESH))
        for cp in copies:
            cp.start()
        for cp in copies:
            cp.wait_recv()
        for cp in copies:
            cp.wait_send()
        mine.wait()

    return _comm_call(name, body, [row], [jax.ShapeDtypeStruct((NDEV, 1, N), row.dtype)], (NDEV - 1, NDEV - 1, 1))[0]


def exchange_rows(name, slabs):
    def body(in_ref, out_ref, send_sems, recv_sems, local_sem):
        x, y, c = _coords()
        me = 4 * x + 2 * y + c
        mine = pltpu.make_async_copy(in_ref.at[me], out_ref.at[me], local_sem.at[0])
        mine.start()
        copies = []
        for k in range(1, NDEV):
            px, py, pc = _flip(x, k & 4), _flip(y, k & 2), _flip(c, k & 1)
            copies.append(pltpu.make_async_remote_copy(
                src_ref=in_ref.at[4 * px + 2 * py + pc], dst_ref=out_ref.at[me], send_sem=send_sems.at[k - 1],
                recv_sem=recv_sems.at[k - 1], device_id=(px, py, pc), device_id_type=MESH))
        for cp in copies:
            cp.start()
        for cp in copies:
            cp.wait_recv()
        for cp in copies:
            cp.wait_send()
        mine.wait()

    return _comm_call(name, body, [slabs], [jax.ShapeDtypeStruct(slabs.shape, slabs.dtype)], (NDEV - 1, NDEV - 1, 1))[0]


def _run_comm(name, comm):
    n_i, n_o = len(comm.ins), len(comm.out_shapes)

    def body(*refs):
        start, finish = comm.plan(refs[:n_i], refs[n_i:n_i + n_o], refs[n_i + n_o:])
        start()
        finish()

    return _comm_call(name, body, comm.ins, comm.out_shapes, comm.n_sems)


def gather_comm(shards):
    L = len(shards)

    def plan(ins, outs, sems):
        send_sems, recv_sems, local_sems = sems
        x, y, c = _coords()
        sibling = (x, y, 1 - c)
        chips = [(_flip(x, m & 2), _flip(y, m & 1)) for m in (1, 2, 3)]

        def slab(px, py, pc):
            return 4 * px + 2 * py + pc

        def copy(l, k, block, to, src=None):
            dst = outs[l].at[slab(*block)]
            return pltpu.make_async_remote_copy(
                src_ref=dst if src is None else src, dst_ref=dst, send_sem=send_sems.at[7 * l + k],
                recv_sem=recv_sems.at[7 * l + k], device_id=to, device_id_type=MESH)

        local = [pltpu.make_async_copy(ins[l], outs[l].at[slab(x, y, c)], local_sems.at[l]) for l in range(L)]
        first = []
        for l in range(L):
            first.append(copy(l, 0, (x, y, c), sibling, src=ins[l]))
            first += [copy(l, 1 + j, (x, y, c), (*chip, c), src=ins[l]) for j, chip in enumerate(chips)]

        def start():
            for cp in local + first:
                cp.start()

        def finish():
            passed = []
            for j, chip in enumerate(chips):
                for l in range(L):
                    copy(l, 1 + j, (*chip, c), (x, y, c)).wait_recv()
                    fwd = copy(l, 4 + j, (*chip, c), sibling)
                    fwd.start()
                    passed.append(fwd)
            for l in range(L):
                copy(l, 0, sibling, (x, y, c)).wait_recv()
                for j, chip in enumerate(chips):
                    copy(l, 4 + j, (*chip, 1 - c), (x, y, c)).wait_recv()
            for cp in first + passed:
                cp.wait_send()
            for cp in local:
                cp.wait()

        return start, finish

    outs = [jax.ShapeDtypeStruct((NDEV,) + s.shape, s.dtype) for s in shards]
    return Comm(plan, shards, outs, (7 * L, 7 * L, L))


def _all_at_once(copies):
    def start():
        for cp in copies:
            cp.start()

    def finish():
        for cp in copies:
            cp.wait_recv()
        for cp in copies:
            cp.wait_send()

    return start, finish


def sibling_comm(grads):
    L = len(grads)

    def plan(ins, outs, sems):
        send_sems, recv_sems = sems
        x, y, c = _coords()
        copies = []
        for l in range(L):
            for m in range(4):
                qm = 2 * _flip(x, m & 2) + _flip(y, m & 1)
                copies.append(pltpu.make_async_remote_copy(
                    src_ref=ins[l].at[2 * qm + (1 - c)], dst_ref=outs[l].at[m], send_sem=send_sems.at[4 * l + m],
                    recv_sem=recv_sems.at[4 * l + m], device_id=(x, y, 1 - c), device_id_type=MESH))
        return _all_at_once(copies)

    outs = [jax.ShapeDtypeStruct((4,) + g.shape[1:], g.dtype) for g in grads]
    return Comm(plan, grads, outs, (4 * L, 4 * L))


def chip_comm(transits):
    L = len(transits)

    def plan(ins, outs, sems):
        send_sems, recv_sems = sems
        x, y, c = _coords()
        copies = []
        for l in range(L):
            for m in (1, 2, 3):
                copies.append(pltpu.make_async_remote_copy(
                    src_ref=ins[l].at[m - 1], dst_ref=outs[l].at[m - 1], send_sem=send_sems.at[3 * l + m - 1],
                    recv_sem=recv_sems.at[3 * l + m - 1], device_id=(_flip(x, m & 2), _flip(y, m & 1), c),
                    device_id_type=MESH))
        return _all_at_once(copies)

    outs = [jax.ShapeDtypeStruct(t.shape, t.dtype) for t in transits]
    return Comm(plan, transits, outs, (3 * L, 3 * L))


_SMALL = ("b_ada", "norm1_g", "b_in", "conv_dw_b", "conv_ln_g", "conv_ln_b", "sgu_ln_g", "sgu_ln_b", "w_spatial",
          "b_spatial", "norm2_g", "ffn_dw_b", "final_g")
_SMALL_SHARDED = ("conv_dw_w", "ffn_dw_w")
_BIG = ("w_in", "w_conv_out", "w_sgu_out", "w_out", "w_up", "w_down")
_ORDER = ("w_ada", "b_ada", "norm1_g", "w_in", "b_in", "conv_dw_w", "conv_dw_b", "conv_ln_g", "conv_ln_b", "w_conv_out",
          "sgu_ln_g", "sgu_ln_b", "w_spatial", "b_spatial", "w_sgu_out", "w_out", "norm2_g", "w_up", "ffn_dw_w", "ffn_dw_b",
          "w_down", "final_g")


def _pack(arrays, mult):
    flat = jnp.concatenate([a.reshape(-1) for a in arrays])
    pad = (-flat.shape[0]) % mult
    return jnp.pad(flat, (0, pad)) if pad else flat


def kernel(x, c, w_ada, b_ada, norm1_g, w_in, b_in, conv_dw_w, conv_dw_b, conv_ln_g, conv_ln_b, w_conv_out, sgu_ln_g, sgu_ln_b, w_spatial, b_spatial, w_sgu_out, w_out, norm2_g, w_up, ffn_dw_w, ffn_dw_b, w_down, final_g, loss_target, m_w_ada, m_b_ada, m_norm1_g, m_w_in, m_b_in, m_conv_dw_w, m_conv_dw_b, m_conv_ln_g, m_conv_ln_b, m_w_conv_out, m_sgu_ln_g, m_sgu_ln_b, m_w_spatial, m_b_spatial, m_w_sgu_out, m_w_out, m_norm2_g, m_w_up, m_ffn_dw_w, m_ffn_dw_b, m_w_down, m_final_g, v_w_ada, v_b_ada, v_norm1_g, v_w_in, v_b_in, v_conv_dw_w, v_conv_dw_b, v_conv_ln_g, v_conv_ln_b, v_w_conv_out, v_sgu_ln_g, v_sgu_ln_b, v_w_spatial, v_b_spatial, v_w_sgu_out, v_w_out, v_norm2_g, v_w_up, v_ffn_dw_w, v_ffn_dw_b, v_w_down, v_final_g):
    W = dict(w_ada=w_ada, b_ada=b_ada, norm1_g=norm1_g, w_in=w_in, b_in=b_in, conv_dw_w=conv_dw_w, conv_dw_b=conv_dw_b,
             conv_ln_g=conv_ln_g, conv_ln_b=conv_ln_b, w_conv_out=w_conv_out, sgu_ln_g=sgu_ln_g, sgu_ln_b=sgu_ln_b,
             w_spatial=w_spatial, b_spatial=b_spatial, w_sgu_out=w_sgu_out, w_out=w_out, norm2_g=norm2_g, w_up=w_up,
             ffn_dw_w=ffn_dw_w, ffn_dw_b=ffn_dw_b, w_down=w_down, final_g=final_g)
    M = dict(w_ada=m_w_ada, b_ada=m_b_ada, norm1_g=m_norm1_g, w_in=m_w_in, b_in=m_b_in, conv_dw_w=m_conv_dw_w,
             conv_dw_b=m_conv_dw_b, conv_ln_g=m_conv_ln_g, conv_ln_b=m_conv_ln_b, w_conv_out=m_w_conv_out,
             sgu_ln_g=m_sgu_ln_g, sgu_ln_b=m_sgu_ln_b, w_spatial=m_w_spatial, b_spatial=m_b_spatial,
             w_sgu_out=m_w_sgu_out, w_out=m_w_out, norm2_g=m_norm2_g, w_up=m_w_up, ffn_dw_w=m_ffn_dw_w,
             ffn_dw_b=m_ffn_dw_b, w_down=m_w_down, final_g=m_final_g)
    V = dict(w_ada=v_w_ada, b_ada=v_b_ada, norm1_g=v_norm1_g, w_in=v_w_in, b_in=v_b_in, conv_dw_w=v_conv_dw_w,
             conv_dw_b=v_conv_dw_b, conv_ln_g=v_conv_ln_g, conv_ln_b=v_conv_ln_b, w_conv_out=v_w_conv_out,
             sgu_ln_g=v_sgu_ln_g, sgu_ln_b=v_sgu_ln_b, w_spatial=v_w_spatial, b_spatial=v_b_spatial,
             w_sgu_out=v_w_sgu_out, w_out=v_w_out, norm2_g=v_norm2_g, w_up=v_w_up, ffn_dw_w=v_ffn_dw_w,
             ffn_dw_b=v_ffn_dw_b, w_down=v_w_down, final_g=v_final_g)

    xs, tgt = x[0], loss_target[0]
    T, D = xs.shape
    Dc = conv_dw_w.shape[-1] * NDEV
    F = w_down.shape[1] * NDEV
    K31 = conv_dw_w.shape[1]
    G = w_spatial.shape[1]
    assert D == 2 * Dc and sgu_ln_g.shape[-1] == Dc and T % CHUNK == 0
    me = 4 * lax.axis_index("x") + 2 * lax.axis_index("y") + lax.axis_index("c")

    na = w_ada.shape[-1]
    c_all = gather_rows("gather_c", c).reshape(NDEV, D)
    b_cols = lax.dynamic_slice(b_ada, (0, me * na), (1, na))
    mod_cols = ada_fwd_local(c_all, w_ada[0], b_cols)
    mod = exchange_rows("exchange_mod", mod_cols.reshape(NDEV, 1, na)).reshape(1, NDEV * na)

    wbf = {k: W[k][0].astype(BF16) for k in _BIG}
    fb = ffn_dw_b.reshape(2, 1, F)
    bsp_t = jnp.transpose(b_spatial[0])
    wsp = w_spatial[0]

    def plain(wb):
        return jnp.transpose(wb, (1, 0, 2)).reshape(1, wb.shape[1], NDEV * wb.shape[2])

    h1, (wb_in,) = pre_norm("pre_norm1", xs, mod, norm1_g, 0, comm=gather_comm([wbf["w_in"]]))
    proj, (wb_co, wb_so, wb_out, wb_up, cw_g, fw_g) = mm_nn(
        "proj", h1, wb_in, F32, bias=b_in,
        comm=gather_comm([wbf["w_conv_out"], wbf["w_sgu_out"], wbf["w_out"], wbf["w_up"], conv_dw_w[0], ffn_dw_w[0]]))
    cw = jnp.transpose(cw_g, (1, 0, 2)).reshape(K31, Dc)
    fw = jnp.transpose(jnp.transpose(fw_g, (1, 0, 2)).reshape(3, 2, F), (1, 0, 2))
    wb_out = wb_out.reshape(1, D, D)
    wp_co, wp_so, wp_in, wp_up = plain(wb_co), plain(wb_so), plain(wb_in), plain(wb_up)
    (ac, asw), (wb_down,) = mix_a_fwd(proj, cw, conv_dw_b, conv_ln_g, conv_ln_b, comm=gather_comm([wbf["w_down"]]))
    wb_down = wb_down.reshape(1, F, D)
    uv = mix_b_fwd(proj, sgu_ln_g, sgu_ln_b, wsp, bsp_t)
    y_a = mm_nn("y_a", asw, wp_co, F32)
    y_b = mm_nn("y_b", uv, wp_so, F32)
    merged = merge_fwd(proj, y_a, y_b)
    o1 = mm_nn("o1", merged, wb_out, F32)
    h2 = pre_norm("pre_norm2", xs, mod, norm2_g, 1, o1=o1)
    upre = mm_nn("upre", h2, wb_up, F32, out_halves=True, tn_pref=MM_WIDE)
    f = ffn_act_fwd(upre, fw, fb)
    o2 = mm_nn("o2", f, wb_down, F32)
    dx3, do2, st_f = final_fwd_bwd(xs, o1, o2, mod, final_g.reshape(1, D), tgt)
    loss = lax.psum(st_f[3, 0], MESH_AXES)

    xq, yq, cq = lax.axis_index("x"), lax.axis_index("y"), lax.axis_index("c")
    idx = jnp.stack([2 * (2 * _flip(xq, m & 2) + _flip(yq, m & 1)) + cq for m in range(4)]).astype(jnp.int32)
    own, transit, arrived = {}, {}, {}

    def add_pairs(keys, full, from_sibling):
        for k, g_full, r in zip(keys, full, from_sibling):
            own[k], transit[k] = pair_add("pair_add_" + k, g_full, r, idx)

    df = mm_nt("df", do2, wb_down, F32, tko_pref=MM_WIDE)
    g_down = mm_tn("g_down", f, do2, 1, tko_pref=MM_WIDE).reshape(NDEV, F // NDEV, D)
    dupre, g_fw, g_fb = ffn_act_bwd(upre, df, fw, fb)
    dh2 = mm_nt("dh2", dupre, wp_up, F32, a_halves=True)
    g_up = mm_tn("g_up", h2, dupre, NDEV, g_halves=True, tn_pref=MM_WIDE)
    (dx2, do1, st_2), sib = norm2_bwd(dh2, xs, o1, dx3, mod, norm2_g, comm=sibling_comm([g_down, g_up]))
    add_pairs(("w_down", "w_up"), (g_down, g_up), sib)
    dmerged = mm_nt("dmerged", do1, wb_out, F32)
    g_out = mm_tn("g_out", merged, do1, 1).reshape(NDEV, D // NDEV, D)
    dy_a, dy_b, dproj, db_g = merge_bwd(dmerged, proj, y_a, y_b)
    def blocked(g):
        return jnp.transpose(g.reshape(g.shape[1], NDEV, g.shape[2] // NDEV), (1, 0, 2))

    dasw = mm_nt("dasw", dy_a, wp_co, F32)
    g_co = blocked(mm_tn("g_co", asw, dy_a, 1))
    duv = mm_nt("duv", dy_b, wp_so, F32)
    g_so = blocked(mm_tn("g_so", uv, dy_b, 1))
    (dproj, st_a, g_cw, db_a), (arrived["w_down"], arrived["w_up"]) = mix_a_bwd(
        dasw, ac, proj, dproj, cw, conv_ln_g, conv_ln_b, comm=chip_comm([transit["w_down"], transit["w_up"]]))
    (dproj, st_b, g_wsp, g_bsp_t, db_s), sib = mix_b_bwd(
        duv, proj, dproj, sgu_ln_g, sgu_ln_b, wsp, bsp_t, comm=sibling_comm([g_out, g_co, g_so]))
    add_pairs(("w_out", "w_conv_out", "w_sgu_out"), (g_out, g_co, g_so), sib)
    g_in = mm_tn("g_in", h1, dproj, NDEV)
    add_pairs(("w_in",), (g_in,), _run_comm("sibling_w_in", sibling_comm([g_in])))
    late = ("w_out", "w_conv_out", "w_sgu_out", "w_in")
    dh1, got = mm_nt("dh1", dproj, wp_in, F32, comm=chip_comm([transit[k] for k in late]))
    arrived.update(zip(late, got))
    grad_x, st_1 = norm1_bwd(dh1, xs, dx2, mod, norm1_g)

    dmod = jnp.concatenate([st_1[0], st_1[1], st_2[3], st_2[0], st_2[1], st_f[1]]).reshape(1, NDEV * na)
    dmod_all = exchange_rows("exchange_dmod", dmod.reshape(NDEV, 1, na)).reshape(NDEV, na)
    g_ada = ada_bwd_local(jnp.transpose(c_all), dmod_all)

    res = {}
    for k in _BIG:
        q = arrived[k]
        res[k] = adamw("adamw_" + k, W[k][0], M[k][0], V[k][0], [(own[k], None), (q, 0), (q, 1), (q, 2)])
    res["w_ada"] = adamw("adamw_w_ada", w_ada[0], m_w_ada[0], v_w_ada[0], [(g_ada, None)])

    g_small = dict(
        b_ada=dmod, norm1_g=st_1[2], b_in=jnp.concatenate([db_a, db_s, db_g], axis=1), conv_dw_b=st_a[2],
        conv_ln_g=st_a[0], conv_ln_b=st_a[1], sgu_ln_g=st_b[0], sgu_ln_b=st_b[1], w_spatial=g_wsp,
        b_spatial=jnp.transpose(g_bsp_t), norm2_g=st_2[2], ffn_dw_b=g_fb, final_g=st_f[0],
        conv_dw_w=g_cw[:K31], ffn_dw_w=jnp.transpose(g_fw, (1, 0, 2)))
    names = _SMALL + _SMALL_SHARDED
    sizes = [g_small[k].size for k in names]
    packed = _pack([g_small[k] for k in names], 8 * LANE)
    summed = sum_rows("sum_small", gather_rows("gather_small", packed.reshape(1, -1)))[0]
    offs = [sum(sizes[:i]) for i in range(len(names))]
    seg = {k: summed[o:o + s] for k, o, s in zip(names, offs, sizes)}
    n_cw, n_fw = conv_dw_w.shape[-1], ffn_dw_w.shape[-1]
    seg["conv_dw_w"] = lax.dynamic_slice(seg["conv_dw_w"].reshape(K31, Dc), (0, me * n_cw), (K31, n_cw)).reshape(-1)
    seg["ffn_dw_w"] = lax.dynamic_slice(seg["ffn_dw_w"].reshape(3, 2 * F), (0, me * n_fw), (3, n_fw)).reshape(-1)
    gp = _pack([seg[k] for k in names], 8 * LANE).reshape(-1, LANE)
    wp, mp, vp = (_pack([S[k] for k in names], 8 * LANE).reshape(-1, LANE) for S in (W, M, V))
    small = adamw("adamw_small", wp, mp, vp, [(gp, None)])
    sizes2 = [W[k].size for k in names]
    offs2 = [sum(sizes2[:i]) for i in range(len(names))]
    for k, o, s in zip(names, offs2, sizes2):
        res[k] = tuple(a.reshape(-1)[o:o + s].reshape(W[k].shape) for a in small)

    outs = [[], [], [], []]
    for k in _ORDER:
        for slot in range(4):
            outs[slot].append(res[k][slot].reshape(W[k].shape))
    return (loss, grad_x[None], *outs[0], *outs[1], *outs[2], *outs[3])
```

```python
import functools

import jax
import jax.numpy as jnp
from jax import lax
from jax.experimental import pallas as pl
from jax.experimental.pallas import tpu as pltpu

F32, BF16 = jnp.float32, jnp.bfloat16
ACT = BF16
NDEV = 8
MESH_AXES = ("x", "y", "c")
MESH = pl.DeviceIdType.MESH
EPS = 1e-6
CHUNK = 128
CONV_HALO = 32
FFN_HALO = 8
LANE, SUBLANE = 128, 8
ROW_TILE = 256
FFN_ROW_TILE = 512
STRIP = 32
STRIP_UNROLL = 2
MM_TILE = 1024
MM_WIDE = 1408
MM_DEEP = 2816
VMEM_LIMIT = 56 * 1024 * 1024
ADAM_LR, ADAM_B1, ADAM_B2, ADAM_EPS, ADAM_WD, ADAM_STEP = 0.001, 0.9, 0.999, 1e-08, 0.01, 10
SQRT_HALF = 0.7071067811865476
INV_SQRT_2PI = 0.3989422804014327


def _div(n, pref, mult=LANE):
    if n <= pref:
        return n
    for d in range(pref - pref % mult, 0, -mult):
        if n % d == 0:
            return d
    return n


def _cp(sem):
    return pltpu.CompilerParams(dimension_semantics=sem, vmem_limit_bytes=VMEM_LIMIT)


def _sig(v):
    return jax.nn.sigmoid(v)


def _gelu(v):
    return 0.5 * v * (1.0 + lax.erf(v * SQRT_HALF))


def _gelu_grad(v):
    return 0.5 * (1.0 + lax.erf(v * SQRT_HALF)) + v * (INV_SQRT_2PI * jnp.exp(-0.5 * v * v))


def _colsum(v):
    return jnp.sum(v, axis=0, keepdims=True)


def _rowmean(v):
    return jnp.mean(v, axis=-1, keepdims=True)


class Comm:
    def __init__(self, plan, ins, out_shapes, n_sems):
        self.plan, self.ins, self.out_shapes, self.n_sems = plan, list(ins), list(out_shapes), tuple(n_sems)


def _pcall(name, body, grid, in_specs, out_specs, out_shape, scratch=(), sem=None, aliases=None, comm=None):
    if comm is None:
        return pl.pallas_call(
            body, name=name, grid=grid, in_specs=in_specs, out_specs=out_specs, out_shape=out_shape,
            scratch_shapes=list(scratch), input_output_aliases=aliases or {},
            compiler_params=_cp(sem or ("arbitrary",) * len(grid)))
    single = not isinstance(out_shape, (list, tuple))
    own_specs, own_shapes = ([out_specs], [out_shape]) if single else (list(out_specs), list(out_shape))
    n_in, n_out, n_scr = len(in_specs), len(own_shapes), len(scratch)
    n_ci, n_co = len(comm.ins), len(comm.out_shapes)
    any_spec = pl.BlockSpec(memory_space=pl.ANY)

    def fused(*refs):
        ins, cins = refs[:n_in], refs[n_in:n_in + n_ci]
        outs = refs[n_in + n_ci:n_in + n_ci + n_out]
        couts = refs[n_in + n_ci + n_out:n_in + n_ci + n_out + n_co]
        scr = refs[n_in + n_ci + n_out + n_co:n_in + n_ci + n_out + n_co + n_scr]
        sems = refs[n_in + n_ci + n_out + n_co + n_scr:]
        first = functools.reduce(jnp.logical_and, [pl.program_id(d) == 0 for d in range(len(grid))])
        last = functools.reduce(jnp.logical_and, [pl.program_id(d) == grid[d] - 1 for d in range(len(grid))])

        @pl.when(first)
        def _():
            comm.plan(cins, couts, sems)[0]()

        body(*ins, *outs, *scr)

        @pl.when(last)
        def _():
            comm.plan(cins, couts, sems)[1]()

    call = pl.pallas_call(
        fused, name=name, grid=grid, in_specs=list(in_specs) + [any_spec] * n_ci,
        out_specs=own_specs + [any_spec] * n_co, out_shape=own_shapes + comm.out_shapes,
        scratch_shapes=list(scratch) + [pltpu.SemaphoreType.DMA((s,)) for s in comm.n_sems],
        input_output_aliases=aliases or {},
        compiler_params=pltpu.CompilerParams(dimension_semantics=("arbitrary",) * len(grid),
                                             vmem_limit_bytes=VMEM_LIMIT, has_side_effects=True))

    def run(*args):
        res = call(*args, *comm.ins)
        own = res[:n_out]
        return (own[0] if single else list(own)), list(res[n_out:])

    return run


def _matmul(name, a, b, *, grid, a_spec, b_spec, o_spec, out_shape, dims, acc_shape, bias=None, bias_spec=None, comm=None):
    nk = grid[2]

    def body(*refs):
        if bias is None:
            a_ref, b_ref, o_ref, *scr = refs
            bias_ref = None
        else:
            a_ref, b_ref, bias_ref, o_ref, *scr = refs
        part = lax.dot_general(a_ref[...], b_ref[...], (dims, ((), ())), preferred_element_type=F32)

        def finish(total):
            if bias_ref is not None:
                total = total + bias_ref[...]
            o_ref[...] = total.astype(o_ref.dtype)

        if nk == 1:
            finish(part)
        else:
            acc = scr[0]
            k = pl.program_id(2)

            @pl.when(k == 0)
            def _():
                acc[...] = part

            @pl.when(k > 0)
            def _():
                acc[...] += part

            @pl.when(k == nk - 1)
            def _():
                finish(acc[...])

    in_specs = [a_spec, b_spec] + ([bias_spec] if bias is not None else [])
    args = (a, b) + ((bias,) if bias is not None else ())
    return _pcall(name, body, grid, in_specs, o_spec, out_shape,
                  scratch=[pltpu.VMEM(acc_shape, F32)] if nk > 1 else [],
                  sem=("parallel", "parallel", "arbitrary"), comm=comm)(*args)


def mm_nn(name, a, wb, out_dtype, *, bias=None, out_halves=False, tn_pref=MM_TILE, comm=None):
    T, K = a.shape
    NB, _, Ns = wb.shape
    N = NB * Ns
    tm, tn, tk = _div(T, MM_TILE), _div(Ns, tn_pref), _div(K, MM_DEEP)
    npb, nj, nk = Ns // tn, N // tn, K // tk
    if out_halves:
        o_spec = pl.BlockSpec((None, tm, tn), lambda i, j, k: (j // (nj // 2), i, j % (nj // 2)))
        out_shape = jax.ShapeDtypeStruct((2, T, N // 2), out_dtype)
    else:
        o_spec = pl.BlockSpec((tm, tn), lambda i, j, k: (i, j))
        out_shape = jax.ShapeDtypeStruct((T, N), out_dtype)
    return _matmul(
        name, a, wb, grid=(T // tm, nj, nk),
        a_spec=pl.BlockSpec((tm, tk), lambda i, j, k: (i, k)),
        b_spec=pl.BlockSpec((None, tk, tn), lambda i, j, k: (j // npb, k, j % npb)),
        o_spec=o_spec, out_shape=out_shape, dims=((1,), (0,)), acc_shape=(tm, tn),
        bias=bias, bias_spec=pl.BlockSpec((1, tn), lambda i, j, k: (0, j)), comm=comm)


def to_plain(name, wb):
    NB, K, Ns = wb.shape
    tk = _div(K, MM_TILE, 2 * SUBLANE)

    def body(i_ref, o_ref):
        o_ref[...] = i_ref[...]

    return _pcall(name, body, (NB, K // tk), [pl.BlockSpec((None, tk, Ns), lambda b, k: (b, k, 0))],
                  pl.BlockSpec((None, tk, Ns), lambda b, k: (0, k, b)),
                  jax.ShapeDtypeStruct((1, K, NB * Ns), wb.dtype), sem=("parallel", "parallel"))(wb)


def mm_nt(name, a, wb, out_dtype, *, a_halves=False, tko_pref=MM_TILE, tc_pref=MM_DEEP, comm=None):
    NB, K, Ns = wb.shape
    T = a.shape[-2]
    span = Ns // 2 if a_halves and NB == 1 else Ns
    tm, tko, tc = _div(T, MM_TILE), _div(K, tko_pref), _div(span, tc_pref)
    cpb = Ns // tc
    nkk = NB * cpb
    if a_halves:
        a_spec = pl.BlockSpec((None, tm, tc), lambda i, j, k: (k // (nkk // 2), i, k % (nkk // 2)))
    else:
        a_spec = pl.BlockSpec((tm, tc), lambda i, j, k: (i, k))
    return _matmul(
        name, a, wb, grid=(T // tm, K // tko, nkk), a_spec=a_spec,
        b_spec=pl.BlockSpec((None, tko, tc), lambda i, j, k: (k // cpb, j, k % cpb)),
        o_spec=pl.BlockSpec((tm, tko), lambda i, j, k: (i, j)),
        out_shape=jax.ShapeDtypeStruct((T, K), out_dtype), dims=((1,), (1,)), acc_shape=(tm, tko), comm=comm)


def mm_tn(name, a, g, nb, *, g_halves=False, tko_pref=MM_TILE, tn_pref=MM_TILE, out_dtype=BF16):
    T, K = a.shape
    N = g.shape[-1] * (2 if g_halves else 1)
    Ns = N // nb
    tt, tko, tn = _div(T, 2 * MM_TILE), _div(K, tko_pref), _div(Ns, tn_pref)
    npb, nj = Ns // tn, N // tn
    if g_halves:
        g_spec = pl.BlockSpec((None, tt, tn), lambda i, j, t: (j // (nj // 2), t, j % (nj // 2)))
    else:
        g_spec = pl.BlockSpec((tt, tn), lambda i, j, t: (t, j))
    return _matmul(
        name, a, g, grid=(K // tko, nj, T // tt),
        a_spec=pl.BlockSpec((tt, tko), lambda i, j, t: (t, i)), b_spec=g_spec,
        o_spec=pl.BlockSpec((None, tko, tn), lambda i, j, t: (j // npb, i, j % npb)),
        out_shape=jax.ShapeDtypeStruct((nb, K, Ns), out_dtype), dims=((0,), (0,)), acc_shape=(tko, tn))


def _row_call(name, body, grid, in_specs, out_specs, out_shape, scratch=(), sem=None, aliases=None, comm=None):
    return _pcall(name, body, grid, in_specs, out_specs, out_shape, scratch, sem, aliases, comm)


def _full(shape):
    nd = len(shape)
    return pl.BlockSpec(shape, lambda *idx: (0,) * nd)


def pre_norm(name, x, mod, g, which, o1=None, comm=None):
    T, D = x.shape
    tr = _div(T, ROW_TILE, SUBLANE)

    def body(*refs):
        if o1 is None:
            x_ref, mod_ref, g_ref, h_ref = refs
            xv = x_ref[...]
        else:
            x_ref, o1_ref, mod_ref, g_ref, h_ref = refs
            xv = x_ref[...] + mod_ref[:, 2 * D:3 * D] * o1_ref[...]
        shift = mod_ref[:, (3 * which) * D:(3 * which + 1) * D]
        scale = mod_ref[:, (3 * which + 1) * D:(3 * which + 2) * D]
        r = lax.rsqrt(_rowmean(xv * xv) + EPS)
        h_ref[...] = ((xv * r) * g_ref[...] * (1.0 + scale) + shift).astype(BF16)

    row = pl.BlockSpec((tr, D), lambda i: (i, 0))
    ins = [x] + ([o1] if o1 is not None else []) + [mod, g]
    specs = [row] * (1 if o1 is None else 2) + [_full(mod.shape), _full(g.shape)]
    return _row_call(name, body, (T // tr,), specs, row, jax.ShapeDtypeStruct((T, D), BF16), sem=("parallel",),
                     comm=comm)(*ins)


def _window_taps(win, taps):
    n = win.shape[0]
    for r in range(SUBLANE):
        group = [(i, tap, off) for i, (tap, off) in enumerate(taps) if off % SUBLANE == r]
        if not group:
            continue
        shifted = win if r == 0 else pltpu.roll(win, n - r, 0)
        for i, tap, off in group:
            assert 0 <= off and off + CONV_HALO <= n
            yield i, tap, shifted[off - r:off - r + CONV_HALO]


def mix_a_fwd(proj, cw, cb, lg, lb, comm=None):
    T = proj.shape[0]
    K, Dc = cw.shape
    tr = _div(T, ROW_TILE, CONV_HALO)
    hb = tr // CONV_HALO

    def body(val_ref, gate_ref, hval_ref, hgate_ref, cw_ref, cb_ref, lg_ref, lb_ref, ac_ref, asw_ref, buf):
        i = pl.program_id(0)
        hist = hval_ref[...] * _sig(hgate_ref[...])
        buf[0:CONV_HALO, :] = jnp.where(i > 0, hist, 0.0)
        buf[CONV_HALO:CONV_HALO + tr, :] = val_ref[...] * _sig(gate_ref[...])
        base = CONV_HALO - (K - 1)
        for c0 in range(0, Dc, LANE):
            lanes = slice(c0, c0 + LANE)

            def step(s, carry):
                r0 = pl.multiple_of(s * CONV_HALO, CONV_HALO)
                win = buf[pl.ds(r0, 2 * CONV_HALO), lanes]
                acc = jnp.zeros((CONV_HALO, LANE), F32)
                for _, k, piece in _window_taps(win, [(k, base + k) for k in range(K)]):
                    acc = acc + piece * cw_ref[k:k + 1, lanes]
                ac_ref[pl.ds(r0, CONV_HALO), lanes] = acc + cb_ref[:, lanes]
                return carry

            lax.fori_loop(0, tr // CONV_HALO, step, 0)
        ac = ac_ref[...]
        mu = _rowmean(ac)
        cen = ac - mu
        y = cen * lax.rsqrt(_rowmean(cen * cen) + EPS)
        aln = y * lg_ref[...] + lb_ref[...]
        asw_ref[...] = (aln * _sig(aln)).astype(BF16)

    def halo(col):
        return pl.BlockSpec((CONV_HALO, Dc), lambda i: (jnp.maximum(i * hb - 1, 0), col))

    row = pl.BlockSpec((tr, Dc), lambda i: (i, 0))
    return _row_call(
        "mix_a_fwd", body, (T // tr,),
        [row, pl.BlockSpec((tr, Dc), lambda i: (i, 1)), halo(0), halo(1),
         _full(cw.shape), _full(cb.shape), _full(lg.shape), _full(lb.shape)],
        [row, row], [jax.ShapeDtypeStruct((T, Dc), F32), jax.ShapeDtypeStruct((T, Dc), BF16)],
        scratch=[pltpu.VMEM((CONV_HALO + tr, Dc), F32)], sem=("parallel",), comm=comm)(proj, proj, proj, proj, cw, cb, lg, lb)


def _spatial_mask():
    t = lax.broadcasted_iota(jnp.int32, (CHUNK, CHUNK), 0)
    s = lax.broadcasted_iota(jnp.int32, (CHUNK, CHUNK), 1)
    return s <= t


def mix_b_fwd(proj, lg, lb, wsp, bsp_t):
    T = proj.shape[0]
    Ds = lg.shape[-1]
    G = wsp.shape[0]
    hd = Ds // G
    tr = _div(T, ROW_TILE, CHUNK)

    def body(u_ref, v_ref, lg_ref, lb_ref, w_ref, b_ref, uv_ref, vs):
        v = _gelu(v_ref[...])
        mu = _rowmean(v)
        cen = v - mu
        vln = (cen * lax.rsqrt(_rowmean(cen * cen) + EPS) * lg_ref[...] + lb_ref[...]).astype(BF16)
        mask = _spatial_mask()
        for g in range(G):
            wg = jnp.where(mask, w_ref[g], 0.0).astype(BF16)
            for n in range(tr // CHUNK):
                rows, cols = slice(n * CHUNK, (n + 1) * CHUNK), slice(g * hd, (g + 1) * hd)
                vs[rows, cols] = jnp.dot(wg, vln[rows, cols], preferred_element_type=F32) + b_ref[:, g:g + 1]
        uv_ref[...] = (_gelu(u_ref[...]) * vs[...]).astype(BF16)

    return _row_call(
        "mix_b_fwd", body, (T // tr,),
        [pl.BlockSpec((tr, Ds), lambda i: (i, 2)), pl.BlockSpec((tr, Ds), lambda i: (i, 3)),
         _full(lg.shape), _full(lb.shape), _full(wsp.shape), _full(bsp_t.shape)],
        pl.BlockSpec((tr, Ds), lambda i: (i, 0)), jax.ShapeDtypeStruct((T, Ds), BF16),
        scratch=[pltpu.VMEM((tr, Ds), F32)], sem=("parallel",))(proj, proj, lg, lb, wsp, bsp_t)


def merge_fwd(proj, y_a, y_b):
    T, D = y_a.shape
    tr = _div(T, ROW_TILE, SUBLANE)

    def body(g_ref, ya_ref, yb_ref, o_ref):
        o_ref[...] = (_sig(g_ref[:, 0:D]) * ya_ref[...].astype(F32)
                      + _sig(g_ref[:, D:2 * D]) * yb_ref[...].astype(F32)).astype(BF16)

    row = pl.BlockSpec((tr, D), lambda i: (i, 0))
    return _row_call("merge_fwd", body, (T // tr,), [pl.BlockSpec((tr, 2 * D), lambda i: (i, 1)), row, row], row,
                     jax.ShapeDtypeStruct((T, D), BF16), sem=("parallel",))(proj, y_a, y_b)


def _fold(v):
    acc = v[0:SUBLANE]
    for r in range(SUBLANE, v.shape[0], SUBLANE):
        acc = acc + v[r:r + SUBLANE]
    return acc


def _conv3(prev, cur, w):
    win = jnp.concatenate([prev, cur], axis=0)
    n = win.shape[0]
    x1 = pltpu.roll(win, 1, 0)[FFN_HALO:n]
    x2 = pltpu.roll(win, 2, 0)[FFN_HALO:n]
    return x2 * w[0] + x1 * w[1] + cur * w[2], (x2, x1, cur)


def _strip_taps(w_ref, b_ref, lanes):
    w = [[w_ref[h, k:k + 1, lanes] for k in range(3)] for h in range(2)]
    b = [b_ref[h, :, lanes] for h in range(2)]
    return w, b


def ffn_act_fwd(upre, fw, fb):
    _, T, F = upre.shape
    tr = _div(T, FFN_ROW_TILE, STRIP)
    cb = _div(F, MM_WIDE)
    hb = tr // FFN_HALO
    ns = tr // STRIP

    def body(x_ref, h_ref, w_ref, b_ref, f_ref):
        i = pl.program_id(0)
        for c0 in range(0, cb, LANE):
            lanes = slice(c0, c0 + LANE)
            w, b = _strip_taps(w_ref, b_ref, lanes)

            def strip(r0, prev):
                up = [_conv3(prev(h), x_ref[h, pl.ds(r0, STRIP), lanes], w[h])[0] + b[h] for h in range(2)]
                f_ref[pl.ds(r0, STRIP), lanes] = (up[1] * _sig(up[1]) * up[0]).astype(BF16)

            strip(0, lambda h: jnp.where(i > 0, h_ref[h, :, lanes], 0.0))

            def step(s, carry):
                r0 = pl.multiple_of(s * STRIP, STRIP)
                strip(r0, lambda h: x_ref[h, pl.ds(pl.multiple_of(r0 - FFN_HALO, FFN_HALO), FFN_HALO), lanes])
                return carry

            lax.fori_loop(1, ns, step, 0, unroll=STRIP_UNROLL)

    return _row_call(
        "ffn_act_fwd", body, (T // tr, F // cb),
        [pl.BlockSpec((2, tr, cb), lambda i, j: (0, i, j)),
         pl.BlockSpec((2, FFN_HALO, cb), lambda i, j: (0, jnp.maximum(i * hb - 1, 0), j)),
         pl.BlockSpec((2, 3, cb), lambda i, j: (0, 0, j)), pl.BlockSpec((2, 1, cb), lambda i, j: (0, 0, j))],
        pl.BlockSpec((tr, cb), lambda i, j: (i, j)), jax.ShapeDtypeStruct((T, F), BF16),
        sem=("parallel", "parallel"))(upre, upre, fw, fb)


def final_fwd_bwd(x, o1, o2, mod, gf, target):
    T, D = x.shape
    tr = _div(T, ROW_TILE, SUBLANE)
    nt = T // tr

    def body(x_ref, o1_ref, o2_ref, mod_ref, gf_ref, t_ref, dx3_ref, do2_ref, st_ref):
        i = pl.program_id(0)
        gate1, gate2 = mod_ref[:, 2 * D:3 * D], mod_ref[:, 5 * D:6 * D]
        o2v = o2_ref[...]
        x3 = x_ref[...] + gate1 * o1_ref[...] + gate2 * o2v
        r = lax.rsqrt(_rowmean(x3 * x3) + EPS)
        xn = x3 * r
        err = xn * gf_ref[...] - t_ref[...]
        dy = err * (1.0 / D)
        dxn = dy * gf_ref[...]
        dx3 = r * (dxn - xn * _rowmean(dxn * xn))
        dx3_ref[...] = dx3
        do2_ref[...] = (dx3 * gate2).astype(BF16)

        @pl.when(i == 0)
        def _():
            st_ref[...] = jnp.zeros_like(st_ref)

        st_ref[0:1, :] += _colsum(dy * xn)
        st_ref[1:2, :] += _colsum(dx3 * o2v)
        st_ref[2:3, :] += _colsum(err * err) * (0.5 / D)

        @pl.when(i == nt - 1)
        def _():
            st_ref[3:4, :] = jnp.zeros((1, D), F32) + jnp.sum(st_ref[2:3, :])

    row = pl.BlockSpec((tr, D), lambda i: (i, 0))
    return _row_call(
        "final_fwd_bwd", body, (nt,), [row, row, row, _full(mod.shape), _full(gf.shape), row],
        [row, row, _full((8, D))],
        [jax.ShapeDtypeStruct((T, D), F32), jax.ShapeDtypeStruct((T, D), BF16), jax.ShapeDtypeStruct((8, D), F32)],
    )(x, o1, o2, mod, gf, target)


def ffn_act_bwd(upre, df, fw, fb):
    _, T, F = upre.shape
    tr = _div(T, FFN_ROW_TILE, STRIP)
    cb = _div(F, MM_WIDE)
    hb = tr // FFN_HALO
    nt = T // tr
    ns = tr // STRIP

    def body(x_ref, h_ref, df_ref, w_ref, b_ref, dpre_ref, dw_ref, db_ref, carry):
        i = pl.program_id(1)
        ri = nt - 1 - i

        @pl.when(i == 0)
        def _():
            carry[...] = jnp.zeros_like(carry)
            dw_ref[...] = jnp.zeros_like(dw_ref)
            db_ref[...] = jnp.zeros_like(db_ref)

        for c0 in range(0, cb, LANE):
            lanes = slice(c0, c0 + LANE)
            w, b = _strip_taps(w_ref, b_ref, lanes)

            def strip(r0, prev, state):
                later, db, dw = state
                up, taps = [], []
                for h in range(2):
                    conv, xs3 = _conv3(prev(h), x_ref[h, pl.ds(r0, STRIP), lanes], w[h])
                    up.append(conv + b[h])
                    taps.append(xs3)
                val, gt = up
                sg = _sig(gt)
                dfv = df_ref[pl.ds(r0, STRIP), lanes].astype(F32)
                dup = (dfv * (gt * sg), dfv * val * (sg * (1.0 + gt * (1.0 - sg))))
                db = tuple(db[h] + _fold(dup[h]) for h in range(2))
                dw = tuple(tuple(dw[h][k] + _fold(dup[h] * taps[h][k]) for k in range(3)) for h in range(2))
                for h in range(2):
                    dwin = jnp.concatenate([dup[h], later[h]], axis=0)
                    n = dwin.shape[0]
                    d1 = pltpu.roll(dwin, n - 1, 0)[0:STRIP]
                    d2 = pltpu.roll(dwin, n - 2, 0)[0:STRIP]
                    dpre_ref[h, pl.ds(r0, STRIP), lanes] = (dup[h] * w[h][2] + d1 * w[h][1] + d2 * w[h][0]).astype(BF16)
                return tuple(dup[h][0:FFN_HALO] for h in range(2)), db, dw

            zero = jnp.zeros((SUBLANE, LANE), F32)
            state = ((carry[0, :, lanes], carry[1, :, lanes]), (zero, zero), ((zero,) * 3,) * 2)

            def step(s, state):
                r0 = pl.multiple_of((ns - 1 - s) * STRIP, STRIP)
                return strip(r0, lambda h: x_ref[h, pl.ds(pl.multiple_of(r0 - FFN_HALO, FFN_HALO), FFN_HALO), lanes], state)

            state = lax.fori_loop(0, ns - 1, step, state, unroll=STRIP_UNROLL)
            later, db, dw = strip(0, lambda h: jnp.where(ri > 0, h_ref[h, :, lanes], 0.0), state)
            for h in range(2):
                carry[h, :, lanes] = later[h]
                db_ref[h, :, lanes] += _colsum(db[h])
                for k in range(3):
                    dw_ref[h, k:k + 1, lanes] += _colsum(dw[h][k])

    return _row_call(
        "ffn_act_bwd", body, (F // cb, nt),
        [pl.BlockSpec((2, tr, cb), lambda j, i: (0, nt - 1 - i, j)),
         pl.BlockSpec((2, FFN_HALO, cb), lambda j, i: (0, jnp.maximum((nt - 1 - i) * hb - 1, 0), j)),
         pl.BlockSpec((tr, cb), lambda j, i: (nt - 1 - i, j)),
         pl.BlockSpec((2, 3, cb), lambda j, i: (0, 0, j)), pl.BlockSpec((2, 1, cb), lambda j, i: (0, 0, j))],
        [pl.BlockSpec((2, tr, cb), lambda j, i: (0, nt - 1 - i, j)),
         pl.BlockSpec((2, 3, cb), lambda j, i: (0, 0, j)), pl.BlockSpec((2, 1, cb), lambda j, i: (0, 0, j))],
        [jax.ShapeDtypeStruct((2, T, F), BF16), jax.ShapeDtypeStruct((2, 3, F), F32), jax.ShapeDtypeStruct((2, 1, F), F32)],
        scratch=[pltpu.VMEM((2, FFN_HALO, cb), F32)],
        sem=("parallel", "arbitrary"))(upre, upre, df, fw, fb)


def norm2_bwd(dh2, x, o1, dx3, mod, g2, comm=None):
    T, D = x.shape
    tr = _div(T, ROW_TILE, SUBLANE)

    def body(dh_ref, x_ref, o1_ref, dx3_ref, mod_ref, g_ref, dx2_ref, do1_ref, st_ref):
        i = pl.program_id(0)
        gate1, scale = mod_ref[:, 2 * D:3 * D], mod_ref[:, 4 * D:5 * D]
        o1v = o1_ref[...]
        x2 = x_ref[...] + gate1 * o1v
        r = lax.rsqrt(_rowmean(x2 * x2) + EPS)
        xn = x2 * r
        dh = dh_ref[...].astype(F32)
        dxn = dh * (g_ref[...] * (1.0 + scale))
        dx2 = r * (dxn - xn * _rowmean(dxn * xn)) + dx3_ref[...]
        dx2_ref[...] = dx2
        do1_ref[...] = (dx2 * gate1).astype(BF16)

        @pl.when(i == 0)
        def _():
            st_ref[...] = jnp.zeros_like(st_ref)

        st_ref[0:1, :] += _colsum(dh)
        st_ref[1:2, :] += _colsum(dh * xn) * g_ref[...]
        st_ref[2:3, :] += _colsum(dh * xn) * (1.0 + scale)
        st_ref[3:4, :] += _colsum(dx2 * o1v)

    row = pl.BlockSpec((tr, D), lambda i: (i, 0))
    return _row_call(
        "norm2_bwd", body, (T // tr,), [row, row, row, row, _full(mod.shape), _full(g2.shape)],
        [row, row, _full((8, D))],
        [jax.ShapeDtypeStruct((T, D), F32), jax.ShapeDtypeStruct((T, D), BF16), jax.ShapeDtypeStruct((8, D), F32)],
        comm=comm)(dh2, x, o1, dx3, mod, g2)


def merge_bwd(dmerged, proj, y_a, y_b):
    T, D = y_a.shape
    tr = _div(T, ROW_TILE, SUBLANE)

    def body(dm_ref, g_ref, ya_ref, yb_ref, dya_ref, dyb_ref, dp_ref, db_ref):
        i = pl.program_id(0)
        dm = dm_ref[...].astype(F32)
        sa, sb = _sig(g_ref[:, 0:D]), _sig(g_ref[:, D:2 * D])
        dya_ref[...] = (dm * sa).astype(BF16)
        dyb_ref[...] = (dm * sb).astype(BF16)
        dga = dm * ya_ref[...].astype(F32) * (sa * (1.0 - sa))
        dgb = dm * yb_ref[...].astype(F32) * (sb * (1.0 - sb))
        dp_ref[:, 0:D] = dga.astype(BF16)
        dp_ref[:, D:2 * D] = dgb.astype(BF16)

        @pl.when(i == 0)
        def _():
            db_ref[...] = jnp.zeros_like(db_ref)

        db_ref[:, 0:D] += _colsum(dga)
        db_ref[:, D:2 * D] += _colsum(dgb)

    row = pl.BlockSpec((tr, D), lambda i: (i, 0))
    wide = pl.BlockSpec((tr, 2 * D), lambda i: (i, 1))
    return _row_call(
        "merge_bwd", body, (T // tr,), [row, wide, row, row], [row, row, wide, _full((1, 2 * D))],
        [jax.ShapeDtypeStruct((T, D), BF16), jax.ShapeDtypeStruct((T, D), BF16),
         jax.ShapeDtypeStruct((T, 4 * D), BF16), jax.ShapeDtypeStruct((1, 2 * D), F32)],
    )(dmerged, proj, y_a, y_b)


def mix_a_bwd(dasw, ac, proj, dproj, cw, lg, lb, comm=None):
    T, Dc = ac.shape
    K = cw.shape[0]
    tr = _div(T, ROW_TILE, CONV_HALO)
    hb = tr // CONV_HALO
    nt = T // tr

    def body(dasw_ref, ac_ref, in_ref, hin_ref, dp_hbm, cw_ref, lg_ref, lb_ref,
             dp_ref, st_ref, dcw_ref, db_ref, abuf, dbuf, carry, da_buf):
        del dp_hbm
        i = pl.program_id(0)
        ri = nt - 1 - i
        acv = ac_ref[...]
        mu = _rowmean(acv)
        cen = acv - mu
        rstd = lax.rsqrt(_rowmean(cen * cen) + EPS)
        y = cen * rstd
        aln = y * lg_ref[...] + lb_ref[...]
        sg = _sig(aln)
        daln = dasw_ref[...].astype(F32) * (sg * (1.0 + aln * (1.0 - sg)))
        dy = daln * lg_ref[...]
        dac = rstd * (dy - _rowmean(dy) - y * _rowmean(dy * y))

        @pl.when(i == 0)
        def _():
            carry[...] = jnp.zeros_like(carry)
            st_ref[...] = jnp.zeros_like(st_ref)
            dcw_ref[...] = jnp.zeros_like(dcw_ref)
            db_ref[...] = jnp.zeros_like(db_ref)

        st_ref[0:1, :] += _colsum(daln * y)
        st_ref[1:2, :] += _colsum(daln)
        st_ref[2:3, :] += _colsum(dac)
        val, gate = in_ref[:, 0:Dc], in_ref[:, Dc:2 * Dc]
        sgg = _sig(gate)
        hist = hin_ref[:, 0:Dc] * _sig(hin_ref[:, Dc:2 * Dc])
        abuf[0:CONV_HALO, :] = jnp.where(ri > 0, hist, 0.0)
        abuf[CONV_HALO:CONV_HALO + tr, :] = val * sgg
        dbuf[0:tr, :] = dac
        dbuf[tr:tr + CONV_HALO, :] = carry[...]
        carry[...] = dac[0:CONV_HALO, :]
        base = CONV_HALO - (K - 1)
        zero = jnp.zeros((SUBLANE, LANE), F32)
        for c0 in range(0, Dc, LANE):
            lanes = slice(c0, c0 + LANE)

            def step(s, dws):
                r0 = pl.multiple_of(s * CONV_HALO, CONV_HALO)
                dwin = dbuf[pl.ds(r0, 2 * CONV_HALO), lanes]
                piece_dac = dwin[0:CONV_HALO]
                dws = list(dws)
                for _, k, piece in _window_taps(abuf[pl.ds(r0, 2 * CONV_HALO), lanes], [(k, base + k) for k in range(K)]):
                    dws[k] = dws[k] + _fold(piece_dac * piece)
                acc = jnp.zeros((CONV_HALO, LANE), F32)
                for _, k, piece in _window_taps(dwin, [(k, K - 1 - k) for k in range(K)]):
                    acc = acc + piece * cw_ref[k:k + 1, lanes]
                da_buf[pl.ds(r0, CONV_HALO), lanes] = acc
                return tuple(dws)

            dws = lax.fori_loop(0, tr // CONV_HALO, step, (zero,) * K)
            for k in range(K):
                dcw_ref[k:k + 1, lanes] += _colsum(dws[k])
        da = da_buf[...]
        dval = da * sgg
        dgate = da * val * (sgg * (1.0 - sgg))
        dp_ref[:, 0:Dc] = dval.astype(BF16)
        dp_ref[:, Dc:2 * Dc] = dgate.astype(BF16)
        db_ref[:, 0:Dc] += _colsum(dval)
        db_ref[:, Dc:2 * Dc] += _colsum(dgate)

    row = pl.BlockSpec((tr, Dc), lambda i: (nt - 1 - i, 0))
    wide = pl.BlockSpec((tr, 2 * Dc), lambda i: (nt - 1 - i, 0))
    return _row_call(
        "mix_a_bwd", body, (nt,),
        [row, row, wide, pl.BlockSpec((CONV_HALO, 2 * Dc), lambda i: (jnp.maximum((nt - 1 - i) * hb - 1, 0), 0)),
         pl.BlockSpec(memory_space=pl.ANY), _full(cw.shape), _full(lg.shape), _full(lb.shape)],
        [wide, _full((8, Dc)), _full((CONV_HALO, Dc)), _full((1, 2 * Dc))],
        [jax.ShapeDtypeStruct(dproj.shape, BF16), jax.ShapeDtypeStruct((8, Dc), F32),
         jax.ShapeDtypeStruct((CONV_HALO, Dc), F32), jax.ShapeDtypeStruct((1, 2 * Dc), F32)],
        scratch=[pltpu.VMEM((CONV_HALO + tr, Dc), F32), pltpu.VMEM((tr + CONV_HALO, Dc), F32),
                 pltpu.VMEM((CONV_HALO, Dc), F32), pltpu.VMEM((tr, Dc), F32)],
        aliases={4: 0}, comm=comm)(dasw, ac, proj, proj, dproj, cw, lg, lb)


def mix_b_bwd(duv, proj, dproj, lg, lb, wsp, bsp_t, comm=None):
    T, Ds = duv.shape
    G = wsp.shape[0]
    hd = Ds // G
    tr = _div(T, ROW_TILE, CHUNK)
    nt = T // tr

    def body(duv_ref, s_ref, dp_hbm, lg_ref, lb_ref, w_ref, b_ref,
             dp_ref, st_ref, dws_ref, dbs_ref, db_ref, vs, dvln):
        del dp_hbm
        i = pl.program_id(0)

        @pl.when(i == 0)
        def _():
            st_ref[...] = jnp.zeros_like(st_ref)
            dws_ref[...] = jnp.zeros_like(dws_ref)
            dbs_ref[...] = jnp.zeros_like(dbs_ref)
            db_ref[...] = jnp.zeros_like(db_ref)

        upre, vpre = s_ref[:, 0:Ds], s_ref[:, Ds:2 * Ds]
        u, v = _gelu(upre), _gelu(vpre)
        mu = _rowmean(v)
        cen = v - mu
        rstd = lax.rsqrt(_rowmean(cen * cen) + EPS)
        yv = cen * rstd
        vln = (yv * lg_ref[...] + lb_ref[...]).astype(BF16)
        duvv = duv_ref[...].astype(F32)
        dvs = duvv * u
        dvs_b = dvs.astype(BF16)
        mask = _spatial_mask()
        for g in range(G):
            wg = jnp.where(mask, w_ref[g], 0.0).astype(BF16)
            cols = slice(g * hd, (g + 1) * hd)
            dws = jnp.zeros((CHUNK, CHUNK), F32)
            dbs = jnp.zeros((CHUNK, 1), F32)
            for n in range(tr // CHUNK):
                rows = slice(n * CHUNK, (n + 1) * CHUNK)
                vs[rows, cols] = jnp.dot(wg, vln[rows, cols], preferred_element_type=F32) + b_ref[:, g:g + 1]
                dvln[rows, cols] = lax.dot_general(wg, dvs_b[rows, cols], (((0,), (0,)), ((), ())),
                                                   preferred_element_type=F32)
                dws = dws + lax.dot_general(dvs_b[rows, cols], vln[rows, cols], (((1,), (1,)), ((), ())),
                                            preferred_element_type=F32)
                dbs = dbs + jnp.sum(dvs[rows, cols], axis=1, keepdims=True)
            dws_ref[g] += jnp.where(mask, dws, 0.0)
            dbs_ref[:, g:g + 1] += dbs
        dvl = dvln[...]
        st_ref[0:1, :] += _colsum(dvl * yv)
        st_ref[1:2, :] += _colsum(dvl)
        dyv = dvl * lg_ref[...]
        dv = rstd * (dyv - _rowmean(dyv) - yv * _rowmean(dyv * yv))
        dupre = duvv * vs[...] * _gelu_grad(upre)
        dvpre = dv * _gelu_grad(vpre)
        dp_ref[:, 0:Ds] = dupre.astype(BF16)
        dp_ref[:, Ds:2 * Ds] = dvpre.astype(BF16)
        db_ref[:, 0:Ds] += _colsum(dupre)
        db_ref[:, Ds:2 * Ds] += _colsum(dvpre)

    wide = pl.BlockSpec((tr, 2 * Ds), lambda i: (i, 1))
    return _row_call(
        "mix_b_bwd", body, (nt,),
        [pl.BlockSpec((tr, Ds), lambda i: (i, 0)), wide, pl.BlockSpec(memory_space=pl.ANY),
         _full(lg.shape), _full(lb.shape), _full(wsp.shape), _full(bsp_t.shape)],
        [wide, _full((8, Ds)), _full(wsp.shape), _full(bsp_t.shape), _full((1, 2 * Ds))],
        [jax.ShapeDtypeStruct(dproj.shape, BF16), jax.ShapeDtypeStruct((8, Ds), F32),
         jax.ShapeDtypeStruct(wsp.shape, F32), jax.ShapeDtypeStruct(bsp_t.shape, F32),
         jax.ShapeDtypeStruct((1, 2 * Ds), F32)],
        scratch=[pltpu.VMEM((tr, Ds), F32), pltpu.VMEM((tr, Ds), F32)],
        aliases={2: 0}, comm=comm)(duv, proj, dproj, lg, lb, wsp, bsp_t)


def norm1_bwd(dh1, x, dx2, mod, g1):
    T, D = x.shape
    tr = _div(T, ROW_TILE, SUBLANE)

    def body(dh_ref, x_ref, dx2_ref, mod_ref, g_ref, gx_ref, st_ref):
        i = pl.program_id(0)
        scale = mod_ref[:, D:2 * D]
        xv = x_ref[...]
        r = lax.rsqrt(_rowmean(xv * xv) + EPS)
        xn = xv * r
        dh = dh_ref[...].astype(F32)
        dxn = dh * (g_ref[...] * (1.0 + scale))
        gx_ref[...] = r * (dxn - xn * _rowmean(dxn * xn)) + dx2_ref[...]

        @pl.when(i == 0)
        def _():
            st_ref[...] = jnp.zeros_like(st_ref)

        st_ref[0:1, :] += _colsum(dh)
        st_ref[1:2, :] += _colsum(dh * xn) * g_ref[...]
        st_ref[2:3, :] += _colsum(dh * xn) * (1.0 + scale)

    row = pl.BlockSpec((tr, D), lambda i: (i, 0))
    return _row_call(
        "norm1_bwd", body, (T // tr,), [row, row, row, _full(mod.shape), _full(g1.shape)], [row, _full((8, D))],
        [jax.ShapeDtypeStruct((T, D), F32), jax.ShapeDtypeStruct((8, D), F32)])(dh1, x, dx2, mod, g1)


def ada_fwd_local(c_all, w_ada, b_cols):
    B, D = c_all.shape
    Na = w_ada.shape[1]
    tn = _div(Na, 512)

    def body(c_ref, w_ref, b_ref, o_ref):
        cv = c_ref[...]
        act = (cv * _sig(cv)).astype(BF16)
        o_ref[...] = jnp.dot(act, w_ref[...].astype(BF16), preferred_element_type=F32) + b_ref[...]

    return _row_call(
        "ada_fwd_local", body, (Na // tn,),
        [_full(c_all.shape), pl.BlockSpec((D, tn), lambda j: (0, j)), pl.BlockSpec((1, tn), lambda j: (0, j))],
        pl.BlockSpec((B, tn), lambda j: (0, j)), jax.ShapeDtypeStruct((B, Na), F32), sem=("parallel",))(c_all, w_ada, b_cols)


def ada_bwd_local(c_all_t, dmod_all):
    D, B = c_all_t.shape
    Na = dmod_all.shape[1]
    tr = _div(D, 512, SUBLANE)

    def body(c_ref, d_ref, o_ref):
        cv = c_ref[...]
        act = cv * _sig(cv)
        acc = act[:, 0:1] * d_ref[0:1, :]
        for b in range(1, B):
            acc = acc + act[:, b:b + 1] * d_ref[b:b + 1, :]
        o_ref[...] = acc

    return _row_call(
        "ada_bwd_local", body, (D // tr,), [pl.BlockSpec((tr, B), lambda i: (i, 0)), _full(dmod_all.shape)],
        pl.BlockSpec((tr, Na), lambda i: (i, 0)), jax.ShapeDtypeStruct((D, Na), F32), sem=("parallel",))(c_all_t, dmod_all)


def adamw(name, w, m, v, parts):
    R, C = w.shape
    tr = _div(R, max(SUBLANE, (1 << 18) // C // SUBLANE * SUBLANE), SUBLANE)
    bc1, bc2 = 1.0 - ADAM_B1 ** ADAM_STEP, 1.0 - ADAM_B2 ** ADAM_STEP
    n = len(parts)

    def body(*refs):
        w_ref, m_ref, v_ref = refs[:3]
        g_ref, d_ref, nm_ref, nv_ref = refs[3 + n:]
        g = refs[3][...].astype(F32)
        for p in refs[4:3 + n]:
            g = g + p[...].astype(F32)
        mn = ADAM_B1 * m_ref[...] + (1.0 - ADAM_B1) * g
        vn = ADAM_B2 * v_ref[...] + (1.0 - ADAM_B2) * (g * g)
        g_ref[...] = g
        nm_ref[...] = mn
        nv_ref[...] = vn
        d_ref[...] = -ADAM_LR * ((mn / bc1) / (jnp.sqrt(vn / bc2) + ADAM_EPS) + ADAM_WD * w_ref[...])

    row = pl.BlockSpec((tr, C), lambda i: (i, 0))
    pspecs = [row if lead is None else pl.BlockSpec((None, tr, C), lambda i, lead=lead: (lead, i, 0)) for _, lead in parts]
    out = jax.ShapeDtypeStruct((R, C), F32)
    return _row_call(name, body, (R // tr,), [row, row, row] + pspecs, [row] * 4, [out] * 4, sem=("parallel",))(
        w, m, v, *[a for a, _ in parts])


def pair_add(name, g, r, idx):
    _, R, C = g.shape
    tr = _div(R, max(SUBLANE, (1 << 17) // C // SUBLANE * SUBLANE), SUBLANE)

    def body(idx_ref, g0, g1, g2, g3, r_ref, own_ref, tr_ref):
        del idx_ref
        own_ref[...] = g0[...].astype(F32) + r_ref[0].astype(F32)
        for m, gm in ((1, g1), (2, g2), (3, g3)):
            tr_ref[m - 1] = (gm[...].astype(F32) + r_ref[m].astype(F32)).astype(BF16)

    def gspec(m):
        return pl.BlockSpec((None, tr, C), lambda i, idx_ref: (idx_ref[m], i, 0))

    return pl.pallas_call(
        body, name=name,
        grid_spec=pltpu.PrefetchScalarGridSpec(
            num_scalar_prefetch=1, grid=(R // tr,),
            in_specs=[gspec(0), gspec(1), gspec(2), gspec(3), pl.BlockSpec((4, tr, C), lambda i, idx_ref: (0, i, 0))],
            out_specs=[pl.BlockSpec((tr, C), lambda i, idx_ref: (i, 0)),
                       pl.BlockSpec((3, tr, C), lambda i, idx_ref: (0, i, 0))]),
        out_shape=[jax.ShapeDtypeStruct((R, C), F32), jax.ShapeDtypeStruct((3, R, C), BF16)],
        compiler_params=_cp(("parallel",)))(idx, g, g, g, g, r)


def sum_rows(name, stacked):
    nb, _, N = stacked.shape
    tn = _div(N, 1 << 16)

    def body(s_ref, o_ref):
        acc = s_ref[0]
        for b in range(1, nb):
            acc = acc + s_ref[b]
        o_ref[...] = acc

    return _row_call(name, body, (N // tn,), [pl.BlockSpec((nb, 1, tn), lambda j: (0, 0, j))],
                     pl.BlockSpec((1, tn), lambda j: (0, j)), jax.ShapeDtypeStruct((1, N), F32), sem=("parallel",))(stacked)


def _coords():
    return lax.axis_index("x"), lax.axis_index("y"), lax.axis_index("c")


def _flip(v, bit):
    return 1 - v if bit else v


def _comm_call(name, body, ins, out_shapes, n_sems):
    any_spec = pl.BlockSpec(memory_space=pl.ANY)
    return pl.pallas_call(
        body, name=name, in_specs=[any_spec] * len(ins), out_specs=[any_spec] * len(out_shapes), out_shape=out_shapes,
        scratch_shapes=[pltpu.SemaphoreType.DMA((s,)) for s in n_sems],
        compiler_params=pltpu.CompilerParams(has_side_effects=True))(*ins)


def gather_rows(name, row):
    N = row.shape[1]

    def body(row_ref, out_ref, send_sems, recv_sems, local_sem):
        x, y, c = _coords()
        me = 4 * x + 2 * y + c
        mine = pltpu.make_async_copy(row_ref, out_ref.at[me], local_sem.at[0])
        mine.start()
        copies = []
        for k in range(1, NDEV):
            peer = (_flip(x, k & 4), _flip(y, k & 2), _flip(c, k & 1))
            copies.append(pltpu.make_async_remote_copy(
                src_ref=row_ref, dst_ref=out_ref.at[me], send_sem=send_sems.at[k - 1], recv_sem=recv_sems.at[k - 1],
                device_id=peer, device_id_type=MESH))
        for cp in copies:
            cp.start()
        for cp in copies:
            cp.wait_recv()
        for cp in copies:
            cp.wait_send()
        mine.wait()

    return _comm_call(name, body, [row], [jax.ShapeDtypeStruct((NDEV, 1, N), row.dtype)], (NDEV - 1, NDEV - 1, 1))[0]


def exchange_rows(name, slabs):
    def body(in_ref, out_ref, send_sems, recv_sems, local_sem):
        x, y, c = _coords()
        me = 4 * x + 2 * y + c
        mine = pltpu.make_async_copy(in_ref.at[me], out_ref.at[me], local_sem.at[0])
        mine.start()
        copies = []
        for k in range(1, NDEV):
            px, py, pc = _flip(x, k & 4), _flip(y, k & 2), _flip(c, k & 1)
            copies.append(pltpu.make_async_remote_copy(
                src_ref=in_ref.at[4 * px + 2 * py + pc], dst_ref=out_ref.at[me], send_sem=send_sems.at[k - 1],
                recv_sem=recv_sems.at[k - 1], device_id=(px, py, pc), device_id_type=MESH))
        for cp in copies:
            cp.start()
        for cp in copies:
            cp.wait_recv()
        for cp in copies:
            cp.wait_send()
        mine.wait()

    return _comm_call(name, body, [slabs], [jax.ShapeDtypeStruct(slabs.shape, slabs.dtype)], (NDEV - 1, NDEV - 1, 1))[0]


def _run_comm(name, comm):
    n_i, n_o = len(comm.ins), len(comm.out_shapes)

    def body(*refs):
        start, finish = comm.plan(refs[:n_i], refs[n_i:n_i + n_o], refs[n_i + n_o:])
        start()
        finish()

    return _comm_call(name, body, comm.ins, comm.out_shapes, comm.n_sems)


def gather_comm(shards):
    L = len(shards)

    def plan(ins, outs, sems):
        send_sems, recv_sems, local_sems = sems
        x, y, c = _coords()
        sibling = (x, y, 1 - c)
        chips = [(_flip(x, m & 2), _flip(y, m & 1)) for m in (1, 2, 3)]

        def slab(px, py, pc):
            return 4 * px + 2 * py + pc

        def copy(l, k, block, to, src=None):
            dst = outs[l].at[slab(*block)]
            return pltpu.make_async_remote_copy(
                src_ref=dst if src is None else src, dst_ref=dst, send_sem=send_sems.at[7 * l + k],
                recv_sem=recv_sems.at[7 * l + k], device_id=to, device_id_type=MESH)

        local = [pltpu.make_async_copy(ins[l], outs[l].at[slab(x, y, c)], local_sems.at[l]) for l in range(L)]
        first = []
        for l in range(L):
            first.append(copy(l, 0, (x, y, c), sibling, src=ins[l]))
            first += [copy(l, 1 + j, (x, y, c), (*chip, c), src=ins[l]) for j, chip in enumerate(chips)]

        def start():
            for cp in local + first:
                cp.start()

        def finish():
            passed = []
            for j, chip in enumerate(chips):
                for l in range(L):
                    copy(l, 1 + j, (*chip, c), (x, y, c)).wait_recv()
                    fwd = copy(l, 4 + j, (*chip, c), sibling)
                    fwd.start()
                    passed.append(fwd)
            for l in range(L):
                copy(l, 0, sibling, (x, y, c)).wait_recv()
                for j, chip in enumerate(chips):
                    copy(l, 4 + j, (*chip, 1 - c), (x, y, c)).wait_recv()
            for cp in first + passed:
                cp.wait_send()
            for cp in local:
                cp.wait()

        return start, finish

    outs = [jax.ShapeDtypeStruct((NDEV,) + s.shape, s.dtype) for s in shards]
    return Comm(plan, shards, outs, (7 * L, 7 * L, L))


def _all_at_once(copies):
    def start():
        for cp in copies:
            cp.start()

    def finish():
        for cp in copies:
            cp.wait_recv()
        for cp in copies:
            cp.wait_send()

    return start, finish


def sibling_comm(grads):
    L = len(grads)

    def plan(ins, outs, sems):
        send_sems, recv_sems = sems
        x, y, c = _coords()
        copies = []
        for l in range(L):
            for m in range(4):
                qm = 2 * _flip(x, m & 2) + _flip(y, m & 1)
                copies.append(pltpu.make_async_remote_copy(
                    src_ref=ins[l].at[2 * qm + (1 - c)], dst_ref=outs[l].at[m], send_sem=send_sems.at[4 * l + m],
                    recv_sem=recv_sems.at[4 * l + m], device_id=(x, y, 1 - c), device_id_type=MESH))
        return _all_at_once(copies)

    outs = [jax.ShapeDtypeStruct((4,) + g.shape[1:], g.dtype) for g in grads]
    return Comm(plan, grads, outs, (4 * L, 4 * L))


def chip_comm(transits):
    L = len(transits)

    def plan(ins, outs, sems):
        send_sems, recv_sems = sems
        x, y, c = _coords()
        copies = []
        for l in range(L):
            for m in (1, 2, 3):
                copies.append(pltpu.make_async_remote_copy(
                    src_ref=ins[l].at[m - 1], dst_ref=outs[l].at[m - 1], send_sem=send_sems.at[3 * l + m - 1],
                    recv_sem=recv_sems.at[3 * l + m - 1], device_id=(_flip(x, m & 2), _flip(y, m & 1), c),
                    device_id_type=MESH))
        return _all_at_once(copies)

    outs = [jax.ShapeDtypeStruct(t.shape, t.dtype) for t in transits]
    return Comm(plan, transits, outs, (3 * L, 3 * L))


_SMALL = ("b_ada", "norm1_g", "b_in", "conv_dw_b", "conv_ln_g", "conv_ln_b", "sgu_ln_g", "sgu_ln_b", "w_spatial",
          "b_spatial", "norm2_g", "ffn_dw_b", "final_g")
_SMALL_SHARDED = ("conv_dw_w", "ffn_dw_w")
_BIG = ("w_in", "w_conv_out", "w_sgu_out", "w_out", "w_up", "w_down")
_ORDER = ("w_ada", "b_ada", "norm1_g", "w_in", "b_in", "conv_dw_w", "conv_dw_b", "conv_ln_g", "conv_ln_b", "w_conv_out",
          "sgu_ln_g", "sgu_ln_b", "w_spatial", "b_spatial", "w_sgu_out", "w_out", "norm2_g", "w_up", "ffn_dw_w", "ffn_dw_b",
          "w_down", "final_g")


def _pack(arrays, mult):
    flat = jnp.concatenate([a.reshape(-1) for a in arrays])
    pad = (-flat.shape[0]) % mult
    return jnp.pad(flat, (0, pad)) if pad else flat


def kernel(x, c, w_ada, b_ada, norm1_g, w_in, b_in, conv_dw_w, conv_dw_b, conv_ln_g, conv_ln_b, w_conv_out, sgu_ln_g, sgu_ln_b, w_spatial, b_spatial, w_sgu_out, w_out, norm2_g, w_up, ffn_dw_w, ffn_dw_b, w_down, final_g, loss_target, m_w_ada, m_b_ada, m_norm1_g, m_w_in, m_b_in, m_conv_dw_w, m_conv_dw_b, m_conv_ln_g, m_conv_ln_b, m_w_conv_out, m_sgu_ln_g, m_sgu_ln_b, m_w_spatial, m_b_spatial, m_w_sgu_out, m_w_out, m_norm2_g, m_w_up, m_ffn_dw_w, m_ffn_dw_b, m_w_down, m_final_g, v_w_ada, v_b_ada, v_norm1_g, v_w_in, v_b_in, v_conv_dw_w, v_conv_dw_b, v_conv_ln_g, v_conv_ln_b, v_w_conv_out, v_sgu_ln_g, v_sgu_ln_b, v_w_spatial, v_b_spatial, v_w_sgu_out, v_w_out, v_norm2_g, v_w_up, v_ffn_dw_w, v_ffn_dw_b, v_w_down, v_final_g):
    W = dict(w_ada=w_ada, b_ada=b_ada, norm1_g=norm1_g, w_in=w_in, b_in=b_in, conv_dw_w=conv_dw_w, conv_dw_b=conv_dw_b,
             conv_ln_g=conv_ln_g, conv_ln_b=conv_ln_b, w_conv_out=w_conv_out, sgu_ln_g=sgu_ln_g, sgu_ln_b=sgu_ln_b,
             w_spatial=w_spatial, b_spatial=b_spatial, w_sgu_out=w_sgu_out, w_out=w_out, norm2_g=norm2_g, w_up=w_up,
             ffn_dw_w=ffn_dw_w, ffn_dw_b=ffn_dw_b, w_down=w_down, final_g=final_g)
    M = dict(w_ada=m_w_ada, b_ada=m_b_ada, norm1_g=m_norm1_g, w_in=m_w_in, b_in=m_b_in, conv_dw_w=m_conv_dw_w,
             conv_dw_b=m_conv_dw_b, conv_ln_g=m_conv_ln_g, conv_ln_b=m_conv_ln_b, w_conv_out=m_w_conv_out,
             sgu_ln_g=m_sgu_ln_g, sgu_ln_b=m_sgu_ln_b, w_spatial=m_w_spatial, b_spatial=m_b_spatial,
             w_sgu_out=m_w_sgu_out, w_out=m_w_out, norm2_g=m_norm2_g, w_up=m_w_up, ffn_dw_w=m_ffn_dw_w,
             ffn_dw_b=m_ffn_dw_b, w_down=m_w_down, final_g=m_final_g)
    V = dict(w_ada=v_w_ada, b_ada=v_b_ada, norm1_g=v_norm1_g, w_in=v_w_in, b_in=v_b_in, conv_dw_w=v_conv_dw_w,
             conv_dw_b=v_conv_dw_b, conv_ln_g=v_conv_ln_g, conv_ln_b=v_conv_ln_b, w_conv_out=v_w_conv_out,
             sgu_ln_g=v_sgu_ln_g, sgu_ln_b=v_sgu_ln_b, w_spatial=v_w_spatial, b_spatial=v_b_spatial,
             w_sgu_out=v_w_sgu_out, w_out=v_w_out, norm2_g=v_norm2_g, w_up=v_w_up, ffn_dw_w=v_ffn_dw_w,
             ffn_dw_b=v_ffn_dw_b, w_down=v_w_down, final_g=v_final_g)

    xs, tgt = x[0], loss_target[0]
    T, D = xs.shape
    Dc = conv_dw_w.shape[-1] * NDEV
    F = w_down.shape[1] * NDEV
    K31 = conv_dw_w.shape[1]
    G = w_spatial.shape[1]
    assert D == 2 * Dc and sgu_ln_g.shape[-1] == Dc and T % CHUNK == 0
    me = 4 * lax.axis_index("x") + 2 * lax.axis_index("y") + lax.axis_index("c")

    na = w_ada.shape[-1]
    c_all = gather_rows("gather_c", c).reshape(NDEV, D)
    b_cols = lax.dynamic_slice(b_ada, (0, me * na), (1, na))
    mod_cols = ada_fwd_local(c_all, w_ada[0], b_cols)
    mod = exchange_rows("exchange_mod", mod_cols.reshape(NDEV, 1, na)).reshape(1, NDEV * na)

    wbf = {k: W[k][0].astype(BF16) for k in _BIG}
    fb = ffn_dw_b.reshape(2, 1, F)
    bsp_t = jnp.transpose(b_spatial[0])
    wsp = w_spatial[0]

    def plain(wb):
        return jnp.transpose(wb, (1, 0, 2)).reshape(1, wb.shape[1], NDEV * wb.shape[2])

    h1, (wb_in,) = pre_norm("pre_norm1", xs, mod, norm1_g, 0, comm=gather_comm([wbf["w_in"]]))
    proj, (wb_co, wb_so, wb_out, wb_up, cw_g, fw_g) = mm_nn(
        "proj", h1, wb_in, F32, bias=b_in,
        comm=gather_comm([wbf["w_conv_out"], wbf["w_sgu_out"], wbf["w_out"], wbf["w_up"], conv_dw_w[0], ffn_dw_w[0]]))
    cw = jnp.transpose(cw_g, (1, 0, 2)).reshape(K31, Dc)
    fw = jnp.transpose(jnp.transpose(fw_g, (1, 0, 2)).reshape(3, 2, F), (1, 0, 2))
    wb_out = wb_out.reshape(1, D, D)
    wp_co, wp_so = plain(wb_co), plain(wb_so)
    wp_in, wp_up = to_plain("plain_w_in", wb_in), to_plain("plain_w_up", wb_up)
    (ac, asw), (wb_down,) = mix_a_fwd(proj, cw, conv_dw_b, conv_ln_g, conv_ln_b, comm=gather_comm([wbf["w_down"]]))
    wb_down = wb_down.reshape(1, F, D)
    uv = mix_b_fwd(proj, sgu_ln_g, sgu_ln_b, wsp, bsp_t)
    y_a = mm_nn("y_a", asw, wp_co, ACT)
    y_b = mm_nn("y_b", uv, wp_so, ACT)
    merged = merge_fwd(proj, y_a, y_b)
    o1 = mm_nn("o1", merged, wb_out, F32)
    h2 = pre_norm("pre_norm2", xs, mod, norm2_g, 1, o1=o1)
    upre = mm_nn("upre", h2, wb_up, F32, out_halves=True, tn_pref=MM_WIDE)
    f = ffn_act_fwd(upre, fw, fb)
    o2 = mm_nn("o2", f, wb_down, F32)
    dx3, do2, st_f = final_fwd_bwd(xs, o1, o2, mod, final_g.reshape(1, D), tgt)
    loss = lax.psum(st_f[3, 0], MESH_AXES)

    xq, yq, cq = lax.axis_index("x"), lax.axis_index("y"), lax.axis_index("c")
    idx = jnp.stack([2 * (2 * _flip(xq, m & 2) + _flip(yq, m & 1)) + cq for m in range(4)]).astype(jnp.int32)
    own, transit, arrived = {}, {}, {}

    def add_pairs(keys, full, from_sibling):
        for k, g_full, r in zip(keys, full, from_sibling):
            own[k], transit[k] = pair_add("pair_add_" + k, g_full, r, idx)

    df = mm_nt("df", do2, wb_down, ACT, tko_pref=MM_WIDE)
    g_down = mm_tn("g_down", f, do2, 1, tko_pref=MM_WIDE).reshape(NDEV, F // NDEV, D)
    dupre, g_fw, g_fb = ffn_act_bwd(upre, df, fw, fb)
    dh2 = mm_nt("dh2", dupre, wp_up, ACT, a_halves=True)
    g_up = mm_tn("g_up", h2, dupre, NDEV, g_halves=True, tn_pref=MM_WIDE)
    (dx2, do1, st_2), sib = norm2_bwd(dh2, xs, o1, dx3, mod, norm2_g, comm=sibling_comm([g_down, g_up]))
    add_pairs(("w_down", "w_up"), (g_down, g_up), sib)
    dmerged = mm_nt("dmerged", do1, wb_out, ACT)
    g_out = mm_tn("g_out", merged, do1, 1).reshape(NDEV, D // NDEV, D)
    dy_a, dy_b, dproj, db_g = merge_bwd(dmerged, proj, y_a, y_b)
    def blocked(g):
        return jnp.transpose(g.reshape(g.shape[1], NDEV, g.shape[2] // NDEV), (1, 0, 2))

    dasw = mm_nt("dasw", dy_a, wp_co, ACT)
    g_co = blocked(mm_tn("g_co", asw, dy_a, 1))
    duv = mm_nt("duv", dy_b, wp_so, ACT)
    g_so = blocked(mm_tn("g_so", uv, dy_b, 1))
    (dproj, st_a, g_cw, db_a), (arrived["w_down"], arrived["w_up"]) = mix_a_bwd(
        dasw, ac, proj, dproj, cw, conv_ln_g, conv_ln_b, comm=chip_comm([transit["w_down"], transit["w_up"]]))
    (dproj, st_b, g_wsp, g_bsp_t, db_s), sib = mix_b_bwd(
        duv, proj, dproj, sgu_ln_g, sgu_ln_b, wsp, bsp_t, comm=sibling_comm([g_out, g_co, g_so]))
    add_pairs(("w_out", "w_conv_out", "w_sgu_out"), (g_out, g_co, g_so), sib)
    g_in = mm_tn("g_in", h1, dproj, NDEV)
    add_pairs(("w_in",), (g_in,), _run_comm("sibling_w_in", sibling_comm([g_in])))
    late = ("w_out", "w_conv_out", "w_sgu_out", "w_in")
    dh1, got = mm_nt("dh1", dproj, wp_in, ACT, comm=chip_comm([transit[k] for k in late]))
    arrived.update(zip(late, got))
    grad_x, st_1 = norm1_bwd(dh1, xs, dx2, mod, norm1_g)

    dmod = jnp.concatenate([st_1[0], st_1[1], st_2[3], st_2[0], st_2[1], st_f[1]]).reshape(1, NDEV * na)
    dmod_all = exchange_rows("exchange_dmod", dmod.reshape(NDEV, 1, na)).reshape(NDEV, na)
    g_ada = ada_bwd_local(jnp.transpose(c_all), dmod_all)

    res = {}
    for k in _BIG:
        q = arrived[k]
        res[k] = adamw("adamw_" + k, W[k][0], M[k][0], V[k][0], [(own[k], None), (q, 0), (q, 1), (q, 2)])
    res["w_ada"] = adamw("adamw_w_ada", w_ada[0], m_w_ada[0], v_w_ada[0], [(g_ada, None)])

    g_small = dict(
        b_ada=dmod, norm1_g=st_1[2], b_in=jnp.concatenate([db_a, db_s, db_g], axis=1), conv_dw_b=st_a[2],
        conv_ln_g=st_a[0], conv_ln_b=st_a[1], sgu_ln_g=st_b[0], sgu_ln_b=st_b[1], w_spatial=g_wsp,
        b_spatial=jnp.transpose(g_bsp_t), norm2_g=st_2[2], ffn_dw_b=g_fb, final_g=st_f[0],
        conv_dw_w=g_cw[:K31], ffn_dw_w=jnp.transpose(g_fw, (1, 0, 2)))
    names = _SMALL + _SMALL_SHARDED
    sizes = [g_small[k].size for k in names]
    packed = _pack([g_small[k] for k in names], 8 * LANE)
    summed = sum_rows("sum_small", gather_rows("gather_small", packed.reshape(1, -1)))[0]
    offs = [sum(sizes[:i]) for i in range(len(names))]
    seg = {k: summed[o:o + s] for k, o, s in zip(names, offs, sizes)}
    n_cw, n_fw = conv_dw_w.shape[-1], ffn_dw_w.shape[-1]
    seg["conv_dw_w"] = lax.dynamic_slice(seg["conv_dw_w"].reshape(K31, Dc), (0, me * n_cw), (K31, n_cw)).reshape(-1)
    seg["ffn_dw_w"] = lax.dynamic_slice(seg["ffn_dw_w"].reshape(3, 2 * F), (0, me * n_fw), (3, n_fw)).reshape(-1)
    gp = _pack([seg[k] for k in names], 8 * LANE).reshape(-1, LANE)
    wp, mp, vp = (_pack([S[k] for k in names], 8 * LANE).reshape(-1, LANE) for S in (W, M, V))
    small = adamw("adamw_small", wp, mp, vp, [(gp, None)])
    sizes2 = [W[k].size for k in names]
    offs2 = [sum(sizes2[:i]) for i in range(len(names))]
    for k, o, s in zip(names, offs2, sizes2):
        res[k] = tuple(a.reshape(-1)[o:o + s].reshape(W[k].shape) for a in small)

    outs = [[], [], [], []]
    for k in _ORDER:
        for slot in range(4):
            outs[slot].append(res[k][slot].reshape(W[k].shape))
    return (loss, grad_x[None], *outs[0], *outs[1], *outs[2], *outs[3])
```

```python
import functools

import jax
import jax.numpy as jnp
from jax import lax
from jax.experimental import pallas as pl
from jax.experimental.pallas import tpu as pltpu

F32, BF16 = jnp.float32, jnp.bfloat16
ACT = BF16
NDEV = 8
MESH_AXES = ("x", "y", "c")
MESH = pl.DeviceIdType.MESH
EPS = 1e-6
CHUNK = 128
CONV_HALO = 32
FFN_HALO = 8
LANE, SUBLANE = 128, 8
ROW_TILE = 256
FFN_ROW_TILE = 512
STRIP = 32
STRIP_UNROLL = 2
MM_TILE = 1024
MM_WIDE = 1408
MM_DEEP = 2816
VMEM_LIMIT = 56 * 1024 * 1024
ADAM_LR, ADAM_B1, ADAM_B2, ADAM_EPS, ADAM_WD, ADAM_STEP = 0.001, 0.9, 0.999, 1e-08, 0.01, 10
SQRT_HALF = 0.7071067811865476
INV_SQRT_2PI = 0.3989422804014327


def _div(n, pref, mult=LANE):
    if n <= pref:
        return n
    for d in range(pref - pref % mult, 0, -mult):
        if n % d == 0:
            return d
    return n


def _cp(sem):
    return pltpu.CompilerParams(dimension_semantics=sem, vmem_limit_bytes=VMEM_LIMIT)


def _sig(v):
    return jax.nn.sigmoid(v)


def _gelu(v):
    return 0.5 * v * (1.0 + lax.erf(v * SQRT_HALF))


def _gelu_grad(v):
    return 0.5 * (1.0 + lax.erf(v * SQRT_HALF)) + v * (INV_SQRT_2PI * jnp.exp(-0.5 * v * v))


def _colsum(v):
    return jnp.sum(v, axis=0, keepdims=True)


def _rowmean(v):
    return jnp.mean(v, axis=-1, keepdims=True)


class Comm:
    def __init__(self, plan, ins, out_shapes, n_sems):
        self.plan, self.ins, self.out_shapes, self.n_sems = plan, list(ins), list(out_shapes), tuple(n_sems)


def _pcall(name, body, grid, in_specs, out_specs, out_shape, scratch=(), sem=None, aliases=None, comm=None):
    if comm is None:
        return pl.pallas_call(
            body, name=name, grid=grid, in_specs=in_specs, out_specs=out_specs, out_shape=out_shape,
            scratch_shapes=list(scratch), input_output_aliases=aliases or {},
            compiler_params=_cp(sem or ("arbitrary",) * len(grid)))
    single = not isinstance(out_shape, (list, tuple))
    own_specs, own_shapes = ([out_specs], [out_shape]) if single else (list(out_specs), list(out_shape))
    n_in, n_out, n_scr = len(in_specs), len(own_shapes), len(scratch)
    n_ci, n_co = len(comm.ins), len(comm.out_shapes)
    any_spec = pl.BlockSpec(memory_space=pl.ANY)

    def fused(*refs):
        ins, cins = refs[:n_in], refs[n_in:n_in + n_ci]
        outs = refs[n_in + n_ci:n_in + n_ci + n_out]
        couts = refs[n_in + n_ci + n_out:n_in + n_ci + n_out + n_co]
        scr = refs[n_in + n_ci + n_out + n_co:n_in + n_ci + n_out + n_co + n_scr]
        sems = refs[n_in + n_ci + n_out + n_co + n_scr:]
        first = functools.reduce(jnp.logical_and, [pl.program_id(d) == 0 for d in range(len(grid))])
        last = functools.reduce(jnp.logical_and, [pl.program_id(d) == grid[d] - 1 for d in range(len(grid))])

        @pl.when(first)
        def _():
            comm.plan(cins, couts, sems)[0]()

        body(*ins, *outs, *scr)

        @pl.when(last)
        def _():
            comm.plan(cins, couts, sems)[1]()

    call = pl.pallas_call(
        fused, name=name, grid=grid, in_specs=list(in_specs) + [any_spec] * n_ci,
        out_specs=own_specs + [any_spec] * n_co, out_shape=own_shapes + comm.out_shapes,
        scratch_shapes=list(scratch) + [pltpu.SemaphoreType.DMA((s,)) for s in comm.n_sems],
        input_output_aliases=aliases or {},
        compiler_params=pltpu.CompilerParams(dimension_semantics=("arbitrary",) * len(grid),
                                             vmem_limit_bytes=VMEM_LIMIT, has_side_effects=True))

    def run(*args):
        res = call(*args, *comm.ins)
        own = res[:n_out]
        return (own[0] if single else list(own)), list(res[n_out:])

    return run


def _matmul(name, a, b, *, grid, a_spec, b_spec, o_spec, out_shape, dims, acc_shape, bias=None, bias_spec=None, comm=None):
    nk = grid[2]

    def body(*refs):
        if bias is None:
            a_ref, b_ref, o_ref, *scr = refs
            bias_ref = None
        else:
            a_ref, b_ref, bias_ref, o_ref, *scr = refs
        part = lax.dot_general(a_ref[...], b_ref[...], (dims, ((), ())), preferred_element_type=F32)

        def finish(total):
            if bias_ref is not None:
                total = total + bias_ref[...]
            o_ref[...] = total.astype(o_ref.dtype)

        if nk == 1:
            finish(part)
        else:
            acc = scr[0]
            k = pl.program_id(2)

            @pl.when(k == 0)
            def _():
                acc[...] = part

            @pl.when(k > 0)
            def _():
                acc[...] += part

            @pl.when(k == nk - 1)
            def _():
                finish(acc[...])

    in_specs = [a_spec, b_spec] + ([bias_spec] if bias is not None else [])
    args = (a, b) + ((bias,) if bias is not None else ())
    return _pcall(name, body, grid, in_specs, o_spec, out_shape,
                  scratch=[pltpu.VMEM(acc_shape, F32)] if nk > 1 else [],
                  sem=("parallel", "parallel", "arbitrary"), comm=comm)(*args)


def mm_nn(name, a, wb, out_dtype, *, bias=None, out_halves=False, tn_pref=MM_TILE, comm=None):
    T, K = a.shape
    NB, _, Ns = wb.shape
    N = NB * Ns
    tm, tn, tk = _div(T, MM_TILE), _div(Ns, tn_pref), _div(K, MM_DEEP)
    npb, nj, nk = Ns // tn, N // tn, K // tk
    if out_halves:
        o_spec = pl.BlockSpec((None, tm, tn), lambda i, j, k: (j // (nj // 2), i, j % (nj // 2)))
        out_shape = jax.ShapeDtypeStruct((2, T, N // 2), out_dtype)
    else:
        o_spec = pl.BlockSpec((tm, tn), lambda i, j, k: (i, j))
        out_shape = jax.ShapeDtypeStruct((T, N), out_dtype)
    return _matmul(
        name, a, wb, grid=(T // tm, nj, nk),
        a_spec=pl.BlockSpec((tm, tk), lambda i, j, k: (i, k)),
        b_spec=pl.BlockSpec((None, tk, tn), lambda i, j, k: (j // npb, k, j % npb)),
        o_spec=o_spec, out_shape=out_shape, dims=((1,), (0,)), acc_shape=(tm, tn),
        bias=bias, bias_spec=pl.BlockSpec((1, tn), lambda i, j, k: (0, j)), comm=comm)


def to_plain(name, wb):
    NB, K, Ns = wb.shape
    tk = _div(K, MM_TILE, 2 * SUBLANE)

    def body(i_ref, o_ref):
        o_ref[...] = i_ref[...]

    return _pcall(name, body, (NB, K // tk), [pl.BlockSpec((None, tk, Ns), lambda b, k: (b, k, 0))],
                  pl.BlockSpec((None, tk, Ns), lambda b, k: (0, k, b)),
                  jax.ShapeDtypeStruct((1, K, NB * Ns), wb.dtype), sem=("parallel", "parallel"))(wb)


def mm_nt(name, a, wb, out_dtype, *, a_halves=False, tko_pref=MM_TILE, tc_pref=MM_DEEP, comm=None):
    NB, K, Ns = wb.shape
    T = a.shape[-2]
    span = Ns // 2 if a_halves and NB == 1 else Ns
    tm, tko, tc = _div(T, MM_TILE), _div(K, tko_pref), _div(span, tc_pref)
    cpb = Ns // tc
    nkk = NB * cpb
    if a_halves:
        a_spec = pl.BlockSpec((None, tm, tc), lambda i, j, k: (k // (nkk // 2), i, k % (nkk // 2)))
    else:
        a_spec = pl.BlockSpec((tm, tc), lambda i, j, k: (i, k))
    return _matmul(
        name, a, wb, grid=(T // tm, K // tko, nkk), a_spec=a_spec,
        b_spec=pl.BlockSpec((None, tko, tc), lambda i, j, k: (k // cpb, j, k % cpb)),
        o_spec=pl.BlockSpec((tm, tko), lambda i, j, k: (i, j)),
        out_shape=jax.ShapeDtypeStruct((T, K), out_dtype), dims=((1,), (1,)), acc_shape=(tm, tko), comm=comm)


def mm_tn(name, a, g, nb, *, g_halves=False, tko_pref=MM_TILE, tn_pref=MM_TILE, out_dtype=BF16):
    T, K = a.shape
    N = g.shape[-1] * (2 if g_halves else 1)
    Ns = N // nb
    tt, tko, tn = _div(T, 2 * MM_TILE), _div(K, tko_pref), _div(Ns, tn_pref)
    npb, nj = Ns // tn, N // tn
    if g_halves:
        g_spec = pl.BlockSpec((None, tt, tn), lambda i, j, t: (j // (nj // 2), t, j % (nj // 2)))
    else:
        g_spec = pl.BlockSpec((tt, tn), lambda i, j, t: (t, j))
    return _matmul(
        name, a, g, grid=(K // tko, nj, T // tt),
        a_spec=pl.BlockSpec((tt, tko), lambda i, j, t: (t, i)), b_spec=g_spec,
        o_spec=pl.BlockSpec((None, tko, tn), lambda i, j, t: (j // npb, i, j % npb)),
        out_shape=jax.ShapeDtypeStruct((nb, K, Ns), out_dtype), dims=((0,), (0,)), acc_shape=(tko, tn))


def _row_call(name, body, grid, in_specs, out_specs, out_shape, scratch=(), sem=None, aliases=None, comm=None):
    return _pcall(name, body, grid, in_specs, out_specs, out_shape, scratch, sem, aliases, comm)


def _full(shape):
    nd = len(shape)
    return pl.BlockSpec(shape, lambda *idx: (0,) * nd)


def pre_norm(name, x, mod, g, which, o1=None, comm=None):
    T, D = x.shape
    tr = _div(T, ROW_TILE, SUBLANE)

    def body(*refs):
        if o1 is None:
            x_ref, mod_ref, g_ref, h_ref = refs
            xv = x_ref[...]
        else:
            x_ref, o1_ref, mod_ref, g_ref, h_ref = refs
            xv = x_ref[...] + mod_ref[:, 2 * D:3 * D] * o1_ref[...]
        shift = mod_ref[:, (3 * which) * D:(3 * which + 1) * D]
        scale = mod_ref[:, (3 * which + 1) * D:(3 * which + 2) * D]
        r = lax.rsqrt(_rowmean(xv * xv) + EPS)
        h_ref[...] = ((xv * r) * g_ref[...] * (1.0 + scale) + shift).astype(BF16)

    row = pl.BlockSpec((tr, D), lambda i: (i, 0))
    ins = [x] + ([o1] if o1 is not None else []) + [mod, g]
    specs = [row] * (1 if o1 is None else 2) + [_full(mod.shape), _full(g.shape)]
    return _row_call(name, body, (T // tr,), specs, row, jax.ShapeDtypeStruct((T, D), BF16), sem=("parallel",),
                     comm=comm)(*ins)


def _window_taps(win, taps):
    n = win.shape[0]
    for r in range(SUBLANE):
        group = [(i, tap, off) for i, (tap, off) in enumerate(taps) if off % SUBLANE == r]
        if not group:
            continue
        shifted = win if r == 0 else pltpu.roll(win, n - r, 0)
        for i, tap, off in group:
            assert 0 <= off and off + CONV_HALO <= n
            yield i, tap, shifted[off - r:off - r + CONV_HALO]


def mix_a_fwd(proj, cw, cb, lg, lb, comm=None):
    T = proj.shape[0]
    K, Dc = cw.shape
    tr = _div(T, ROW_TILE, CONV_HALO)
    hb = tr // CONV_HALO

    def body(val_ref, gate_ref, hval_ref, hgate_ref, cw_ref, cb_ref, lg_ref, lb_ref, ac_ref, asw_ref, buf):
        i = pl.program_id(0)
        hist = hval_ref[...] * _sig(hgate_ref[...])
        buf[0:CONV_HALO, :] = jnp.where(i > 0, hist, 0.0)
        buf[CONV_HALO:CONV_HALO + tr, :] = val_ref[...] * _sig(gate_ref[...])
        base = CONV_HALO - (K - 1)
        for c0 in range(0, Dc, LANE):
            lanes = slice(c0, c0 + LANE)

            def step(s, carry):
                r0 = pl.multiple_of(s * CONV_HALO, CONV_HALO)
                win = buf[pl.ds(r0, 2 * CONV_HALO), lanes]
                acc = jnp.zeros((CONV_HALO, LANE), F32)
                for _, k, piece in _window_taps(win, [(k, base + k) for k in range(K)]):
                    acc = acc + piece * cw_ref[k:k + 1, lanes]
                ac_ref[pl.ds(r0, CONV_HALO), lanes] = acc + cb_ref[:, lanes]
                return carry

            lax.fori_loop(0, tr // CONV_HALO, step, 0)
        ac = ac_ref[...]
        mu = _rowmean(ac)
        cen = ac - mu
        y = cen * lax.rsqrt(_rowmean(cen * cen) + EPS)
        aln = y * lg_ref[...] + lb_ref[...]
        asw_ref[...] = (aln * _sig(aln)).astype(BF16)

    def halo(col):
        return pl.BlockSpec((CONV_HALO, Dc), lambda i: (jnp.maximum(i * hb - 1, 0), col))

    row = pl.BlockSpec((tr, Dc), lambda i: (i, 0))
    return _row_call(
        "mix_a_fwd", body, (T // tr,),
        [row, pl.BlockSpec((tr, Dc), lambda i: (i, 1)), halo(0), halo(1),
         _full(cw.shape), _full(cb.shape), _full(lg.shape), _full(lb.shape)],
        [row, row], [jax.ShapeDtypeStruct((T, Dc), F32), jax.ShapeDtypeStruct((T, Dc), BF16)],
        scratch=[pltpu.VMEM((CONV_HALO + tr, Dc), F32)], sem=("parallel",), comm=comm)(proj, proj, proj, proj, cw, cb, lg, lb)


def _spatial_mask():
    t = lax.broadcasted_iota(jnp.int32, (CHUNK, CHUNK), 0)
    s = lax.broadcasted_iota(jnp.int32, (CHUNK, CHUNK), 1)
    return s <= t


def mix_b_fwd(proj, lg, lb, wsp, bsp_t):
    T = proj.shape[0]
    Ds = lg.shape[-1]
    G = wsp.shape[0]
    hd = Ds // G
    tr = _div(T, ROW_TILE, CHUNK)

    def body(u_ref, v_ref, lg_ref, lb_ref, w_ref, b_ref, uv_ref, vs):
        v = _gelu(v_ref[...])
        mu = _rowmean(v)
        cen = v - mu
        vln = (cen * lax.rsqrt(_rowmean(cen * cen) + EPS) * lg_ref[...] + lb_ref[...]).astype(BF16)
        mask = _spatial_mask()
        for g in range(G):
            wg = jnp.where(mask, w_ref[g], 0.0).astype(BF16)
            for n in range(tr // CHUNK):
                rows, cols = slice(n * CHUNK, (n + 1) * CHUNK), slice(g * hd, (g + 1) * hd)
                vs[rows, cols] = jnp.dot(wg, vln[rows, cols], preferred_element_type=F32) + b_ref[:, g:g + 1]
        uv_ref[...] = (_gelu(u_ref[...]) * vs[...]).astype(BF16)

    return _row_call(
        "mix_b_fwd", body, (T // tr,),
        [pl.BlockSpec((tr, Ds), lambda i: (i, 2)), pl.BlockSpec((tr, Ds), lambda i: (i, 3)),
         _full(lg.shape), _full(lb.shape), _full(wsp.shape), _full(bsp_t.shape)],
        pl.BlockSpec((tr, Ds), lambda i: (i, 0)), jax.ShapeDtypeStruct((T, Ds), BF16),
        scratch=[pltpu.VMEM((tr, Ds), F32)], sem=("parallel",))(proj, proj, lg, lb, wsp, bsp_t)


def merge_fwd(proj, y_a, y_b):
    T, D = y_a.shape
    tr = _div(T, ROW_TILE, SUBLANE)

    def body(g_ref, ya_ref, yb_ref, o_ref):
        o_ref[...] = (_sig(g_ref[:, 0:D]) * ya_ref[...].astype(F32)
                      + _sig(g_ref[:, D:2 * D]) * yb_ref[...].astype(F32)).astype(BF16)

    row = pl.BlockSpec((tr, D), lambda i: (i, 0))
    return _row_call("merge_fwd", body, (T // tr,), [pl.BlockSpec((tr, 2 * D), lambda i: (i, 1)), row, row], row,
                     jax.ShapeDtypeStruct((T, D), BF16), sem=("parallel",))(proj, y_a, y_b)


def _fold(v):
    acc = v[0:SUBLANE]
    for r in range(SUBLANE, v.shape[0], SUBLANE):
        acc = acc + v[r:r + SUBLANE]
    return acc


def _conv3(prev, cur, w):
    win = jnp.concatenate([prev, cur], axis=0)
    n = win.shape[0]
    x1 = pltpu.roll(win, 1, 0)[FFN_HALO:n]
    x2 = pltpu.roll(win, 2, 0)[FFN_HALO:n]
    return x2 * w[0] + x1 * w[1] + cur * w[2], (x2, x1, cur)


def _strip_taps(w_ref, b_ref, lanes):
    w = [[w_ref[h, k:k + 1, lanes] for k in range(3)] for h in range(2)]
    b = [b_ref[h, :, lanes] for h in range(2)]
    return w, b


def ffn_act_fwd(upre, fw, fb):
    _, T, F = upre.shape
    tr = _div(T, FFN_ROW_TILE, STRIP)
    cb = _div(F, MM_WIDE)
    hb = tr // FFN_HALO
    ns = tr // STRIP

    def body(x_ref, h_ref, w_ref, b_ref, f_ref):
        i = pl.program_id(0)
        for c0 in range(0, cb, LANE):
            lanes = slice(c0, c0 + LANE)
            w, b = _strip_taps(w_ref, b_ref, lanes)

            def strip(r0, prev):
                up = [_conv3(prev(h), x_ref[h, pl.ds(r0, STRIP), lanes], w[h])[0] + b[h] for h in range(2)]
                f_ref[pl.ds(r0, STRIP), lanes] = (up[1] * _sig(up[1]) * up[0]).astype(BF16)

            strip(0, lambda h: jnp.where(i > 0, h_ref[h, :, lanes], 0.0))

            def step(s, carry):
                r0 = pl.multiple_of(s * STRIP, STRIP)
                strip(r0, lambda h: x_ref[h, pl.ds(pl.multiple_of(r0 - FFN_HALO, FFN_HALO), FFN_HALO), lanes])
                return carry

            lax.fori_loop(1, ns, step, 0, unroll=STRIP_UNROLL)

    return _row_call(
        "ffn_act_fwd", body, (T // tr, F // cb),
        [pl.BlockSpec((2, tr, cb), lambda i, j: (0, i, j)),
         pl.BlockSpec((2, FFN_HALO, cb), lambda i, j: (0, jnp.maximum(i * hb - 1, 0), j)),
         pl.BlockSpec((2, 3, cb), lambda i, j: (0, 0, j)), pl.BlockSpec((2, 1, cb), lambda i, j: (0, 0, j))],
        pl.BlockSpec((tr, cb), lambda i, j: (i, j)), jax.ShapeDtypeStruct((T, F), BF16),
        sem=("parallel", "parallel"))(upre, upre, fw, fb)


def final_fwd_bwd(x, o1, o2, mod, gf, target):
    T, D = x.shape
    tr = _div(T, ROW_TILE, SUBLANE)
    nt = T // tr

    def body(x_ref, o1_ref, o2_ref, mod_ref, gf_ref, t_ref, dx3_ref, do2_ref, st_ref):
        i = pl.program_id(0)
        gate1, gate2 = mod_ref[:, 2 * D:3 * D], mod_ref[:, 5 * D:6 * D]
        o2v = o2_ref[...]
        x3 = x_ref[...] + gate1 * o1_ref[...] + gate2 * o2v
        r = lax.rsqrt(_rowmean(x3 * x3) + EPS)
        xn = x3 * r
        err = xn * gf_ref[...] - t_ref[...]
        dy = err * (1.0 / D)
        dxn = dy * gf_ref[...]
        dx3 = r * (dxn - xn * _rowmean(dxn * xn))
        dx3_ref[...] = dx3
        do2_ref[...] = (dx3 * gate2).astype(BF16)

        @pl.when(i == 0)
        def _():
            st_ref[...] = jnp.zeros_like(st_ref)

        st_ref[0:1, :] += _colsum(dy * xn)
        st_ref[1:2, :] += _colsum(dx3 * o2v)
        st_ref[2:3, :] += _colsum(err * err) * (0.5 / D)

        @pl.when(i == nt - 1)
        def _():
            st_ref[3:4, :] = jnp.zeros((1, D), F32) + jnp.sum(st_ref[2:3, :])

    row = pl.BlockSpec((tr, D), lambda i: (i, 0))
    return _row_call(
        "final_fwd_bwd", body, (nt,), [row, row, row, _full(mod.shape), _full(gf.shape), row],
        [row, row, _full((8, D))],
        [jax.ShapeDtypeStruct((T, D), F32), jax.ShapeDtypeStruct((T, D), BF16), jax.ShapeDtypeStruct((8, D), F32)],
    )(x, o1, o2, mod, gf, target)


def ffn_act_bwd(upre, df, fw, fb):
    _, T, F = upre.shape
    tr = _div(T, FFN_ROW_TILE, STRIP)
    cb = _div(F, MM_WIDE)
    hb = tr // FFN_HALO
    nt = T // tr
    ns = tr // STRIP

    def body(x_ref, h_ref, df_ref, w_ref, b_ref, dpre_ref, dw_ref, db_ref, carry):
        i = pl.program_id(1)
        ri = nt - 1 - i

        @pl.when(i == 0)
        def _():
            carry[...] = jnp.zeros_like(carry)
            dw_ref[...] = jnp.zeros_like(dw_ref)
            db_ref[...] = jnp.zeros_like(db_ref)

        for c0 in range(0, cb, LANE):
            lanes = slice(c0, c0 + LANE)
            w, b = _strip_taps(w_ref, b_ref, lanes)

            def strip(r0, prev, state):
                later, db, dw = state
                up, taps = [], []
                for h in range(2):
                    conv, xs3 = _conv3(prev(h), x_ref[h, pl.ds(r0, STRIP), lanes], w[h])
                    up.append(conv + b[h])
                    taps.append(xs3)
                val, gt = up
                sg = _sig(gt)
                dfv = df_ref[pl.ds(r0, STRIP), lanes].astype(F32)
                dup = (dfv * (gt * sg), dfv * val * (sg * (1.0 + gt * (1.0 - sg))))
                db = tuple(db[h] + _fold(dup[h]) for h in range(2))
                dw = tuple(tuple(dw[h][k] + _fold(dup[h] * taps[h][k]) for k in range(3)) for h in range(2))
                for h in range(2):
                    dwin = jnp.concatenate([dup[h], later[h]], axis=0)
                    n = dwin.shape[0]
                    d1 = pltpu.roll(dwin, n - 1, 0)[0:STRIP]
                    d2 = pltpu.roll(dwin, n - 2, 0)[0:STRIP]
                    dpre_ref[h, pl.ds(r0, STRIP), lanes] = (dup[h] * w[h][2] + d1 * w[h][1] + d2 * w[h][0]).astype(BF16)
                return tuple(dup[h][0:FFN_HALO] for h in range(2)), db, dw

            zero = jnp.zeros((SUBLANE, LANE), F32)
            state = ((carry[0, :, lanes], carry[1, :, lanes]), (zero, zero), ((zero,) * 3,) * 2)

            def step(s, state):
                r0 = pl.multiple_of((ns - 1 - s) * STRIP, STRIP)
                return strip(r0, lambda h: x_ref[h, pl.ds(pl.multiple_of(r0 - FFN_HALO, FFN_HALO), FFN_HALO), lanes], state)

            state = lax.fori_loop(0, ns - 1, step, state, unroll=STRIP_UNROLL)
            later, db, dw = strip(0, lambda h: jnp.where(ri > 0, h_ref[h, :, lanes], 0.0), state)
            for h in range(2):
                carry[h, :, lanes] = later[h]
                db_ref[h, :, lanes] += _colsum(db[h])
                for k in range(3):
                    dw_ref[h, k:k + 1, lanes] += _colsum(dw[h][k])

    return _row_call(
        "ffn_act_bwd", body, (F // cb, nt),
        [pl.BlockSpec((2, tr, cb), lambda j, i: (0, nt - 1 - i, j)),
         pl.BlockSpec((2, FFN_HALO, cb), lambda j, i: (0, jnp.maximum((nt - 1 - i) * hb - 1, 0), j)),
         pl.BlockSpec((tr, cb), lambda j, i: (nt - 1 - i, j)),
         pl.BlockSpec((2, 3, cb), lambda j, i: (0, 0, j)), pl.BlockSpec((2, 1, cb), lambda j, i: (0, 0, j))],
        [pl.BlockSpec((2, tr, cb), lambda j, i: (0, nt - 1 - i, j)),
         pl.BlockSpec((2, 3, cb), lambda j, i: (0, 0, j)), pl.BlockSpec((2, 1, cb), lambda j, i: (0, 0, j))],
        [jax.ShapeDtypeStruct((2, T, F), BF16), jax.ShapeDtypeStruct((2, 3, F), F32), jax.ShapeDtypeStruct((2, 1, F), F32)],
        scratch=[pltpu.VMEM((2, FFN_HALO, cb), F32)],
        sem=("parallel", "arbitrary"))(upre, upre, df, fw, fb)


def norm2_bwd(dh2, x, o1, dx3, mod, g2, comm=None):
    T, D = x.shape
    tr = _div(T, ROW_TILE, SUBLANE)

    def body(dh_ref, x_ref, o1_ref, dx3_ref, mod_ref, g_ref, dx2_ref, do1_ref, st_ref):
        i = pl.program_id(0)
        gate1, scale = mod_ref[:, 2 * D:3 * D], mod_ref[:, 4 * D:5 * D]
        o1v = o1_ref[...]
        x2 = x_ref[...] + gate1 * o1v
        r = lax.rsqrt(_rowmean(x2 * x2) + EPS)
        xn = x2 * r
        dh = dh_ref[...].astype(F32)
        dxn = dh * (g_ref[...] * (1.0 + scale))
        dx2 = r * (dxn - xn * _rowmean(dxn * xn)) + dx3_ref[...]
        dx2_ref[...] = dx2
        do1_ref[...] = (dx2 * gate1).astype(BF16)

        @pl.when(i == 0)
        def _():
            st_ref[...] = jnp.zeros_like(st_ref)

        st_ref[0:1, :] += _colsum(dh)
        st_ref[1:2, :] += _colsum(dh * xn) * g_ref[...]
        st_ref[2:3, :] += _colsum(dh * xn) * (1.0 + scale)
        st_ref[3:4, :] += _colsum(dx2 * o1v)

    row = pl.BlockSpec((tr, D), lambda i: (i, 0))
    return _row_call(
        "norm2_bwd", body, (T // tr,), [row, row, row, row, _full(mod.shape), _full(g2.shape)],
        [row, row, _full((8, D))],
        [jax.ShapeDtypeStruct((T, D), F32), jax.ShapeDtypeStruct((T, D), BF16), jax.ShapeDtypeStruct((8, D), F32)],
        comm=comm)(dh2, x, o1, dx3, mod, g2)


def merge_bwd(dmerged, proj, y_a, y_b):
    T, D = y_a.shape
    tr = _div(T, ROW_TILE, SUBLANE)

    def body(dm_ref, g_ref, ya_ref, yb_ref, dya_ref, dyb_ref, dp_ref, db_ref):
        i = pl.program_id(0)
        dm = dm_ref[...].astype(F32)
        sa, sb = _sig(g_ref[:, 0:D]), _sig(g_ref[:, D:2 * D])
        dya_ref[...] = (dm * sa).astype(BF16)
        dyb_ref[...] = (dm * sb).astype(BF16)
        dga = dm * ya_ref[...].astype(F32) * (sa * (1.0 - sa))
        dgb = dm * yb_ref[...].astype(F32) * (sb * (1.0 - sb))
        dp_ref[:, 0:D] = dga.astype(BF16)
        dp_ref[:, D:2 * D] = dgb.astype(BF16)

        @pl.when(i == 0)
        def _():
            db_ref[...] = jnp.zeros_like(db_ref)

        db_ref[:, 0:D] += _colsum(dga)
        db_ref[:, D:2 * D] += _colsum(dgb)

    row = pl.BlockSpec((tr, D), lambda i: (i, 0))
    wide = pl.BlockSpec((tr, 2 * D), lambda i: (i, 1))
    return _row_call(
        "merge_bwd", body, (T // tr,), [row, wide, row, row], [row, row, wide, _full((1, 2 * D))],
        [jax.ShapeDtypeStruct((T, D), BF16), jax.ShapeDtypeStruct((T, D), BF16),
         jax.ShapeDtypeStruct((T, 4 * D), BF16), jax.ShapeDtypeStruct((1, 2 * D), F32)],
    )(dmerged, proj, y_a, y_b)


def mix_a_bwd(dasw, ac, proj, dproj, cw, lg, lb, comm=None):
    T, Dc = ac.shape
    K = cw.shape[0]
    tr = _div(T, ROW_TILE, CONV_HALO)
    hb = tr // CONV_HALO
    nt = T // tr

    def body(dasw_ref, ac_ref, in_ref, hin_ref, dp_hbm, cw_ref, lg_ref, lb_ref,
             dp_ref, st_ref, dcw_ref, db_ref, abuf, dbuf, carry, da_buf):
        del dp_hbm
        i = pl.program_id(0)
        ri = nt - 1 - i
        acv = ac_ref[...]
        mu = _rowmean(acv)
        cen = acv - mu
        rstd = lax.rsqrt(_rowmean(cen * cen) + EPS)
        y = cen * rstd
        aln = y * lg_ref[...] + lb_ref[...]
        sg = _sig(aln)
        daln = dasw_ref[...].astype(F32) * (sg * (1.0 + aln * (1.0 - sg)))
        dy = daln * lg_ref[...]
        dac = rstd * (dy - _rowmean(dy) - y * _rowmean(dy * y))

        @pl.when(i == 0)
        def _():
            carry[...] = jnp.zeros_like(carry)
            st_ref[...] = jnp.zeros_like(st_ref)
            dcw_ref[...] = jnp.zeros_like(dcw_ref)
            db_ref[...] = jnp.zeros_like(db_ref)

        st_ref[0:1, :] += _colsum(daln * y)
        st_ref[1:2, :] += _colsum(daln)
        st_ref[2:3, :] += _colsum(dac)
        val, gate = in_ref[:, 0:Dc], in_ref[:, Dc:2 * Dc]
        sgg = _sig(gate)
        hist = hin_ref[:, 0:Dc] * _sig(hin_ref[:, Dc:2 * Dc])
        abuf[0:CONV_HALO, :] = jnp.where(ri > 0, hist, 0.0)
        abuf[CONV_HALO:CONV_HALO + tr, :] = val * sgg
        dbuf[0:tr, :] = dac
        dbuf[tr:tr + CONV_HALO, :] = carry[...]
        carry[...] = dac[0:CONV_HALO, :]
        base = CONV_HALO - (K - 1)
        zero = jnp.zeros((SUBLANE, LANE), F32)
        for c0 in range(0, Dc, LANE):
            lanes = slice(c0, c0 + LANE)

            def step(s, dws):
                r0 = pl.multiple_of(s * CONV_HALO, CONV_HALO)
                dwin = dbuf[pl.ds(r0, 2 * CONV_HALO), lanes]
                piece_dac = dwin[0:CONV_HALO]
                dws = list(dws)
                for _, k, piece in _window_taps(abuf[pl.ds(r0, 2 * CONV_HALO), lanes], [(k, base + k) for k in range(K)]):
                    dws[k] = dws[k] + _fold(piece_dac * piece)
                acc = jnp.zeros((CONV_HALO, LANE), F32)
                for _, k, piece in _window_taps(dwin, [(k, K - 1 - k) for k in range(K)]):
                    acc = acc + piece * cw_ref[k:k + 1, lanes]
                da_buf[pl.ds(r0, CONV_HALO), lanes] = acc
                return tuple(dws)

            dws = lax.fori_loop(0, tr // CONV_HALO, step, (zero,) * K)
            for k in range(K):
                dcw_ref[k:k + 1, lanes] += _colsum(dws[k])
        da = da_buf[...]
        dval = da * sgg
        dgate = da * val * (sgg * (1.0 - sgg))
        dp_ref[:, 0:Dc] = dval.astype(BF16)
        dp_ref[:, Dc:2 * Dc] = dgate.astype(BF16)
        db_ref[:, 0:Dc] += _colsum(dval)
        db_ref[:, Dc:2 * Dc] += _colsum(dgate)

    row = pl.BlockSpec((tr, Dc), lambda i: (nt - 1 - i, 0))
    wide = pl.BlockSpec((tr, 2 * Dc), lambda i: (nt - 1 - i, 0))
    return _row_call(
        "mix_a_bwd", body, (nt,),
        [row, row, wide, pl.BlockSpec((CONV_HALO, 2 * Dc), lambda i: (jnp.maximum((nt - 1 - i) * hb - 1, 0), 0)),
         pl.BlockSpec(memory_space=pl.ANY), _full(cw.shape), _full(lg.shape), _full(lb.shape)],
        [wide, _full((8, Dc)), _full((CONV_HALO, Dc)), _full((1, 2 * Dc))],
        [jax.ShapeDtypeStruct(dproj.shape, BF16), jax.ShapeDtypeStruct((8, Dc), F32),
         jax.ShapeDtypeStruct((CONV_HALO, Dc), F32), jax.ShapeDtypeStruct((1, 2 * Dc), F32)],
        scratch=[pltpu.VMEM((CONV_HALO + tr, Dc), F32), pltpu.VMEM((tr + CONV_HALO, Dc), F32),
                 pltpu.VMEM((CONV_HALO, Dc), F32), pltpu.VMEM((tr, Dc), F32)],
        aliases={4: 0}, comm=comm)(dasw, ac, proj, proj, dproj, cw, lg, lb)


def mix_b_bwd(duv, proj, dproj, lg, lb, wsp, bsp_t, comm=None):
    T, Ds = duv.shape
    G = wsp.shape[0]
    hd = Ds // G
    tr = _div(T, ROW_TILE, CHUNK)
    nt = T // tr

    def body(duv_ref, s_ref, dp_hbm, lg_ref, lb_ref, w_ref, b_ref,
             dp_ref, st_ref, dws_ref, dbs_ref, db_ref, vs, dvln):
        del dp_hbm
        i = pl.program_id(0)

        @pl.when(i == 0)
        def _():
            st_ref[...] = jnp.zeros_like(st_ref)
            dws_ref[...] = jnp.zeros_like(dws_ref)
            dbs_ref[...] = jnp.zeros_like(dbs_ref)
            db_ref[...] = jnp.zeros_like(db_ref)

        upre, vpre = s_ref[:, 0:Ds], s_ref[:, Ds:2 * Ds]
        u, v = _gelu(upre), _gelu(vpre)
        mu = _rowmean(v)
        cen = v - mu
        rstd = lax.rsqrt(_rowmean(cen * cen) + EPS)
        yv = cen * rstd
        vln = (yv * lg_ref[...] + lb_ref[...]).astype(BF16)
        duvv = duv_ref[...].astype(F32)
        dvs = duvv * u
        dvs_b = dvs.astype(BF16)
        mask = _spatial_mask()
        for g in range(G):
            wg = jnp.where(mask, w_ref[g], 0.0).astype(BF16)
            cols = slice(g * hd, (g + 1) * hd)
            dws = jnp.zeros((CHUNK, CHUNK), F32)
            dbs = jnp.zeros((CHUNK, 1), F32)
            for n in range(tr // CHUNK):
                rows = slice(n * CHUNK, (n + 1) * CHUNK)
                vs[rows, cols] = jnp.dot(wg, vln[rows, cols], preferred_element_type=F32) + b_ref[:, g:g + 1]
                dvln[rows, cols] = lax.dot_general(wg, dvs_b[rows, cols], (((0,), (0,)), ((), ())),
                                                   preferred_element_type=F32)
                dws = dws + lax.dot_general(dvs_b[rows, cols], vln[rows, cols], (((1,), (1,)), ((), ())),
                                            preferred_element_type=F32)
                dbs = dbs + jnp.sum(dvs[rows, cols], axis=1, keepdims=True)
            dws_ref[g] += jnp.where(mask, dws, 0.0)
            dbs_ref[:, g:g + 1] += dbs
        dvl = dvln[...]
        st_ref[0:1, :] += _colsum(dvl * yv)
        st_ref[1:2, :] += _colsum(dvl)
        dyv = dvl * lg_ref[...]
        dv = rstd * (dyv - _rowmean(dyv) - yv * _rowmean(dyv * yv))
        dupre = duvv * vs[...] * _gelu_grad(upre)
        dvpre = dv * _gelu_grad(vpre)
        dp_ref[:, 0:Ds] = dupre.astype(BF16)
        dp_ref[:, Ds:2 * Ds] = dvpre.astype(BF16)
        db_ref[:, 0:Ds] += _colsum(dupre)
        db_ref[:, Ds:2 * Ds] += _colsum(dvpre)

    wide = pl.BlockSpec((tr, 2 * Ds), lambda i: (i, 1))
    return _row_call(
        "mix_b_bwd", body, (nt,),
        [pl.BlockSpec((tr, Ds), lambda i: (i, 0)), wide, pl.BlockSpec(memory_space=pl.ANY),
         _full(lg.shape), _full(lb.shape), _full(wsp.shape), _full(bsp_t.shape)],
        [wide, _full((8, Ds)), _full(wsp.shape), _full(bsp_t.shape), _full((1, 2 * Ds))],
        [jax.ShapeDtypeStruct(dproj.shape, BF16), jax.ShapeDtypeStruct((8, Ds), F32),
         jax.ShapeDtypeStruct(wsp.shape, F32), jax.ShapeDtypeStruct(bsp_t.shape, F32),
         jax.ShapeDtypeStruct((1, 2 * Ds), F32)],
        scratch=[pltpu.VMEM((tr, Ds), F32), pltpu.VMEM((tr, Ds), F32)],
        aliases={2: 0}, comm=comm)(duv, proj, dproj, lg, lb, wsp, bsp_t)


def norm1_bwd(dh1, x, dx2, mod, g1):
    T, D = x.shape
    tr = _div(T, ROW_TILE, SUBLANE)

    def body(dh_ref, x_ref, dx2_ref, mod_ref, g_ref, gx_ref, st_ref):
        i = pl.program_id(0)
        scale = mod_ref[:, D:2 * D]
        xv = x_ref[...]
        r = lax.rsqrt(_rowmean(xv * xv) + EPS)
        xn = xv * r
        dh = dh_ref[...].astype(F32)
        dxn = dh * (g_ref[...] * (1.0 + scale))
        gx_ref[...] = r * (dxn - xn * _rowmean(dxn * xn)) + dx2_ref[...]

        @pl.when(i == 0)
        def _():
            st_ref[...] = jnp.zeros_like(st_ref)

        st_ref[0:1, :] += _colsum(dh)
        st_ref[1:2, :] += _colsum(dh * xn) * g_ref[...]
        st_ref[2:3, :] += _colsum(dh * xn) * (1.0 + scale)

    row = pl.BlockSpec((tr, D), lambda i: (i, 0))
    return _row_call(
        "norm1_bwd", body, (T // tr,), [row, row, row, _full(mod.shape), _full(g1.shape)], [row, _full((8, D))],
        [jax.ShapeDtypeStruct((T, D), F32), jax.ShapeDtypeStruct((8, D), F32)])(dh1, x, dx2, mod, g1)


def ada_fwd_local(c_all, w_ada, b_cols):
    B, D = c_all.shape
    Na = w_ada.shape[1]
    tn = _div(Na, 512)

    def body(c_ref, w_ref, b_ref, o_ref):
        cv = c_ref[...]
        act = (cv * _sig(cv)).astype(BF16)
        o_ref[...] = jnp.dot(act, w_ref[...].astype(BF16), preferred_element_type=F32) + b_ref[...]

    return _row_call(
        "ada_fwd_local", body, (Na // tn,),
        [_full(c_all.shape), pl.BlockSpec((D, tn), lambda j: (0, j)), pl.BlockSpec((1, tn), lambda j: (0, j))],
        pl.BlockSpec((B, tn), lambda j: (0, j)), jax.ShapeDtypeStruct((B, Na), F32), sem=("parallel",))(c_all, w_ada, b_cols)


def ada_bwd_local(c_all_t, dmod_all):
    D, B = c_all_t.shape
    Na = dmod_all.shape[1]
    tr = _div(D, 512, SUBLANE)

    def body(c_ref, d_ref, o_ref):
        cv = c_ref[...]
        act = cv * _sig(cv)
        acc = act[:, 0:1] * d_ref[0:1, :]
        for b in range(1, B):
            acc = acc + act[:, b:b + 1] * d_ref[b:b + 1, :]
        o_ref[...] = acc

    return _row_call(
        "ada_bwd_local", body, (D // tr,), [pl.BlockSpec((tr, B), lambda i: (i, 0)), _full(dmod_all.shape)],
        pl.BlockSpec((tr, Na), lambda i: (i, 0)), jax.ShapeDtypeStruct((D, Na), F32), sem=("parallel",))(c_all_t, dmod_all)


def _adam_update(w, m, v, g):
    mn = ADAM_B1 * m + (1.0 - ADAM_B1) * g
    vn = ADAM_B2 * v + (1.0 - ADAM_B2) * (g * g)
    bc1, bc2 = 1.0 - ADAM_B1 ** ADAM_STEP, 1.0 - ADAM_B2 ** ADAM_STEP
    return g, -ADAM_LR * ((mn / bc1) / (jnp.sqrt(vn / bc2) + ADAM_EPS) + ADAM_WD * w), mn, vn


def adamw(name, w, m, v, parts):
    R, C = w.shape
    tr = _div(R, max(SUBLANE, (1 << 18) // C // SUBLANE * SUBLANE), SUBLANE)
    n = len(parts)

    def body(*refs):
        w_ref, m_ref, v_ref = refs[:3]
        g_ref, d_ref, nm_ref, nv_ref = refs[3 + n:]
        g = refs[3][...].astype(F32)
        for p in refs[4:3 + n]:
            g = g + p[...].astype(F32)
        g_ref[...], d_ref[...], nm_ref[...], nv_ref[...] = _adam_update(w_ref[...], m_ref[...], v_ref[...], g)

    row = pl.BlockSpec((tr, C), lambda i: (i, 0))
    pspecs = [row if lead is None else pl.BlockSpec((None, tr, C), lambda i, lead=lead: (lead, i, 0)) for _, lead in parts]
    out = jax.ShapeDtypeStruct((R, C), F32)
    return _row_call(name, body, (R // tr,), [row, row, row] + pspecs, [row] * 4, [out] * 4, sem=("parallel",))(
        w, m, v, *[a for a, _ in parts])


def pair_add(name, g, r, idx):
    _, R, C = g.shape
    tr = _div(R, max(SUBLANE, (1 << 17) // C // SUBLANE * SUBLANE), SUBLANE)

    def body(idx_ref, g0, g1, g2, g3, r_ref, own_ref, tr_ref):
        del idx_ref
        own_ref[...] = g0[...].astype(F32) + r_ref[0].astype(F32)
        for m, gm in ((1, g1), (2, g2), (3, g3)):
            tr_ref[m - 1] = (gm[...].astype(F32) + r_ref[m].astype(F32)).astype(BF16)

    def gspec(m):
        return pl.BlockSpec((None, tr, C), lambda i, idx_ref: (idx_ref[m], i, 0))

    return pl.pallas_call(
        body, name=name,
        grid_spec=pltpu.PrefetchScalarGridSpec(
            num_scalar_prefetch=1, grid=(R // tr,),
            in_specs=[gspec(0), gspec(1), gspec(2), gspec(3), pl.BlockSpec((4, tr, C), lambda i, idx_ref: (0, i, 0))],
            out_specs=[pl.BlockSpec((tr, C), lambda i, idx_ref: (i, 0)),
                       pl.BlockSpec((3, tr, C), lambda i, idx_ref: (0, i, 0))]),
        out_shape=[jax.ShapeDtypeStruct((R, C), F32), jax.ShapeDtypeStruct((3, R, C), BF16)],
        compiler_params=_cp(("parallel",)))(idx, g, g, g, g, r)


def adamw_small(me, rows, g_rows, wsp, g_wsp, cwp, g_cw, fwp, g_fw):
    sizes = [w.shape[1] for w, _, _ in rows]
    offs = [sum(sizes[:i]) for i in range(len(rows))]
    k_cw, n_cw = cwp[0].shape
    n_fw = fwp[0].shape[1]
    per_half = g_fw.shape[-1] // n_fw
    groups = list(rows) + [wsp, cwp, fwp]
    n_p, n_r = len(groups), len(rows)

    def body(me_ref, g_rows_ref, g_wsp_ref, g_cw_ref, g_fw_ref, *refs):
        del me_ref
        wmv, outs = refs[:3 * n_p], refs[3 * n_p:]
        for p in range(n_p):
            if p < n_r:
                pick = lambda d, p=p: g_rows_ref[d, :, offs[p]:offs[p] + sizes[p]]
            elif p == n_r:
                pick = lambda d: g_wsp_ref[d]
            elif p == n_r + 1:
                pick = lambda d: g_cw_ref[d, 0:k_cw, :]
            else:
                pick = lambda d: g_fw_ref[d]
            g = pick(0)
            for d in range(1, NDEV):
                g = g + pick(d)
            res = _adam_update(wmv[3 * p][...], wmv[3 * p + 1][...], wmv[3 * p + 2][...], g)
            for slot in range(4):
                outs[4 * p + slot][...] = res[slot]

    def full(a):
        nd = len(a.shape)
        return pl.BlockSpec(a.shape, lambda i, me_ref: (0,) * nd)

    in_specs = [full(g_rows), full(g_wsp),
                pl.BlockSpec((NDEV, g_cw.shape[1], n_cw), lambda i, me_ref: (0, 0, me_ref[0])),
                pl.BlockSpec((NDEV, None, 3, n_fw), lambda i, me_ref: (0, me_ref[0] // per_half, 0, me_ref[0] % per_half))]
    in_specs += [full(a) for grp in groups for a in grp]
    out_shape = [jax.ShapeDtypeStruct(grp[0].shape, F32) for grp in groups for _ in range(4)]
    flat = pl.pallas_call(
        body, name="adamw_small",
        grid_spec=pltpu.PrefetchScalarGridSpec(num_scalar_prefetch=1, grid=(1,), in_specs=in_specs,
                                               out_specs=[full(s) for s in out_shape]),
        out_shape=out_shape, compiler_params=_cp(("arbitrary",)))(
            me, g_rows, g_wsp, g_cw, g_fw, *[a for grp in groups for a in grp])
    return [tuple(flat[4 * p:4 * p + 4]) for p in range(n_p)]


def _coords():
    return lax.axis_index("x"), lax.axis_index("y"), lax.axis_index("c")


def _flip(v, bit):
    return 1 - v if bit else v


def _comm_call(name, body, ins, out_shapes, n_sems):
    any_spec = pl.BlockSpec(memory_space=pl.ANY)
    return pl.pallas_call(
        body, name=name, in_specs=[any_spec] * len(ins), out_specs=[any_spec] * len(out_shapes), out_shape=out_shapes,
        scratch_shapes=[pltpu.SemaphoreType.DMA((s,)) for s in n_sems],
        compiler_params=pltpu.CompilerParams(has_side_effects=True))(*ins)


def gather_all(name, tensors):
    L = len(tensors)

    def body(*refs):
        ins, outs = refs[:L], refs[L:2 * L]
        send_sems, recv_sems, local_sems = refs[2 * L:]
        x, y, c = _coords()
        me = 4 * x + 2 * y + c
        local = [pltpu.make_async_copy(ins[l], outs[l].at[me], local_sems.at[l]) for l in range(L)]
        copies = []
        for l in range(L):
            for k in range(1, NDEV):
                peer = (_flip(x, k & 4), _flip(y, k & 2), _flip(c, k & 1))
                copies.append(pltpu.make_async_remote_copy(
                    src_ref=ins[l], dst_ref=outs[l].at[me], send_sem=send_sems.at[7 * l + k - 1],
                    recv_sem=recv_sems.at[7 * l + k - 1], device_id=peer, device_id_type=MESH))
        for cp in local + copies:
            cp.start()
        for cp in copies:
            cp.wait_recv()
        for cp in copies:
            cp.wait_send()
        for cp in local:
            cp.wait()

    outs = [jax.ShapeDtypeStruct((NDEV,) + t.shape, t.dtype) for t in tensors]
    return _comm_call(name, body, list(tensors), outs, (7 * L, 7 * L, L))


def exchange_rows(name, slabs):
    def body(in_ref, out_ref, send_sems, recv_sems, local_sem):
        x, y, c = _coords()
        me = 4 * x + 2 * y + c
        mine = pltpu.make_async_copy(in_ref.at[me], out_ref.at[me], local_sem.at[0])
        mine.start()
        copies = []
        for k in range(1, NDEV):
            px, py, pc = _flip(x, k & 4), _flip(y, k & 2), _flip(c, k & 1)
            copies.append(pltpu.make_async_remote_copy(
                src_ref=in_ref.at[4 * px + 2 * py + pc], dst_ref=out_ref.at[me], send_sem=send_sems.at[k - 1],
                recv_sem=recv_sems.at[k - 1], device_id=(px, py, pc), device_id_type=MESH))
        for cp in copies:
            cp.start()
        for cp in copies:
            cp.wait_recv()
        for cp in copies:
            cp.wait_send()
        mine.wait()

    return _comm_call(name, body, [slabs], [jax.ShapeDtypeStruct(slabs.shape, slabs.dtype)], (NDEV - 1, NDEV - 1, 1))[0]


def _run_comm(name, comm):
    n_i, n_o = len(comm.ins), len(comm.out_shapes)

    def body(*refs):
        start, finish = comm.plan(refs[:n_i], refs[n_i:n_i + n_o], refs[n_i + n_o:])
        start()
        finish()

    return _comm_call(name, body, comm.ins, comm.out_shapes, comm.n_sems)


def gather_comm(shards):
    L = len(shards)

    def plan(ins, outs, sems):
        send_sems, recv_sems, local_sems = sems
        x, y, c = _coords()
        sibling = (x, y, 1 - c)
        chips = [(_flip(x, m & 2), _flip(y, m & 1)) for m in (1, 2, 3)]

        def slab(px, py, pc):
            return 4 * px + 2 * py + pc

        def copy(l, k, block, to, src=None):
            dst = outs[l].at[slab(*block)]
            return pltpu.make_async_remote_copy(
                src_ref=dst if src is None else src, dst_ref=dst, send_sem=send_sems.at[7 * l + k],
                recv_sem=recv_sems.at[7 * l + k], device_id=to, device_id_type=MESH)

        local = [pltpu.make_async_copy(ins[l], outs[l].at[slab(x, y, c)], local_sems.at[l]) for l in range(L)]
        first = []
        for l in range(L):
            first.append(copy(l, 0, (x, y, c), sibling, src=ins[l]))
            first += [copy(l, 1 + j, (x, y, c), (*chip, c), src=ins[l]) for j, chip in enumerate(chips)]

        def start():
            for cp in local + first:
                cp.start()

        def finish():
            passed = []
            for j, chip in enumerate(chips):
                for l in range(L):
                    copy(l, 1 + j, (*chip, c), (x, y, c)).wait_recv()
                    fwd = copy(l, 4 + j, (*chip, c), sibling)
                    fwd.start()
                    passed.append(fwd)
            for l in range(L):
                copy(l, 0, sibling, (x, y, c)).wait_recv()
                for j, chip in enumerate(chips):
                    copy(l, 4 + j, (*chip, 1 - c), (x, y, c)).wait_recv()
            for cp in first + passed:
                cp.wait_send()
            for cp in local:
                cp.wait()

        return start, finish

    outs = [jax.ShapeDtypeStruct((NDEV,) + s.shape, s.dtype) for s in shards]
    return Comm(plan, shards, outs, (7 * L, 7 * L, L))


def _all_at_once(copies):
    def start():
        for cp in copies:
            cp.start()

    def finish():
        for cp in copies:
            cp.wait_recv()
        for cp in copies:
            cp.wait_send()

    return start, finish


def sibling_comm(grads):
    L = len(grads)

    def plan(ins, outs, sems):
        send_sems, recv_sems = sems
        x, y, c = _coords()
        copies = []
        for l in range(L):
            for m in range(4):
                qm = 2 * _flip(x, m & 2) + _flip(y, m & 1)
                copies.append(pltpu.make_async_remote_copy(
                    src_ref=ins[l].at[2 * qm + (1 - c)], dst_ref=outs[l].at[m], send_sem=send_sems.at[4 * l + m],
                    recv_sem=recv_sems.at[4 * l + m], device_id=(x, y, 1 - c), device_id_type=MESH))
        return _all_at_once(copies)

    outs = [jax.ShapeDtypeStruct((4,) + g.shape[1:], g.dtype) for g in grads]
    return Comm(plan, grads, outs, (4 * L, 4 * L))


def chip_comm(transits):
    L = len(transits)

    def plan(ins, outs, sems):
        send_sems, recv_sems = sems
        x, y, c = _coords()
        copies = []
        for l in range(L):
            for m in (1, 2, 3):
                copies.append(pltpu.make_async_remote_copy(
                    src_ref=ins[l].at[m - 1], dst_ref=outs[l].at[m - 1], send_sem=send_sems.at[3 * l + m - 1],
                    recv_sem=recv_sems.at[3 * l + m - 1], device_id=(_flip(x, m & 2), _flip(y, m & 1), c),
                    device_id_type=MESH))
        return _all_at_once(copies)

    outs = [jax.ShapeDtypeStruct(t.shape, t.dtype) for t in transits]
    return Comm(plan, transits, outs, (3 * L, 3 * L))


_SMALL_ROWS = ("b_ada", "norm1_g", "b_in", "conv_dw_b", "conv_ln_g", "conv_ln_b", "sgu_ln_g", "sgu_ln_b", "b_spatial",
               "norm2_g", "ffn_dw_b", "final_g")
_BIG = ("w_in", "w_conv_out", "w_sgu_out", "w_out", "w_up", "w_down")
_ORDER = ("w_ada", "b_ada", "norm1_g", "w_in", "b_in", "conv_dw_w", "conv_dw_b", "conv_ln_g", "conv_ln_b", "w_conv_out",
          "sgu_ln_g", "sgu_ln_b", "w_spatial", "b_spatial", "w_sgu_out", "w_out", "norm2_g", "w_up", "ffn_dw_w", "ffn_dw_b",
          "w_down", "final_g")


def kernel(x, c, w_ada, b_ada, norm1_g, w_in, b_in, conv_dw_w, conv_dw_b, conv_ln_g, conv_ln_b, w_conv_out, sgu_ln_g, sgu_ln_b, w_spatial, b_spatial, w_sgu_out, w_out, norm2_g, w_up, ffn_dw_w, ffn_dw_b, w_down, final_g, loss_target, m_w_ada, m_b_ada, m_norm1_g, m_w_in, m_b_in, m_conv_dw_w, m_conv_dw_b, m_conv_ln_g, m_conv_ln_b, m_w_conv_out, m_sgu_ln_g, m_sgu_ln_b, m_w_spatial, m_b_spatial, m_w_sgu_out, m_w_out, m_norm2_g, m_w_up, m_ffn_dw_w, m_ffn_dw_b, m_w_down, m_final_g, v_w_ada, v_b_ada, v_norm1_g, v_w_in, v_b_in, v_conv_dw_w, v_conv_dw_b, v_conv_ln_g, v_conv_ln_b, v_w_conv_out, v_sgu_ln_g, v_sgu_ln_b, v_w_spatial, v_b_spatial, v_w_sgu_out, v_w_out, v_norm2_g, v_w_up, v_ffn_dw_w, v_ffn_dw_b, v_w_down, v_final_g):
    W = dict(w_ada=w_ada, b_ada=b_ada, norm1_g=norm1_g, w_in=w_in, b_in=b_in, conv_dw_w=conv_dw_w, conv_dw_b=conv_dw_b,
             conv_ln_g=conv_ln_g, conv_ln_b=conv_ln_b, w_conv_out=w_conv_out, sgu_ln_g=sgu_ln_g, sgu_ln_b=sgu_ln_b,
             w_spatial=w_spatial, b_spatial=b_spatial, w_sgu_out=w_sgu_out, w_out=w_out, norm2_g=norm2_g, w_up=w_up,
             ffn_dw_w=ffn_dw_w, ffn_dw_b=ffn_dw_b, w_down=w_down, final_g=final_g)
    M = dict(w_ada=m_w_ada, b_ada=m_b_ada, norm1_g=m_norm1_g, w_in=m_w_in, b_in=m_b_in, conv_dw_w=m_conv_dw_w,
             conv_dw_b=m_conv_dw_b, conv_ln_g=m_conv_ln_g, conv_ln_b=m_conv_ln_b, w_conv_out=m_w_conv_out,
             sgu_ln_g=m_sgu_ln_g, sgu_ln_b=m_sgu_ln_b, w_spatial=m_w_spatial, b_spatial=m_b_spatial,
             w_sgu_out=m_w_sgu_out, w_out=m_w_out, norm2_g=m_norm2_g, w_up=m_w_up, ffn_dw_w=m_ffn_dw_w,
             ffn_dw_b=m_ffn_dw_b, w_down=m_w_down, final_g=m_final_g)
    V = dict(w_ada=v_w_ada, b_ada=v_b_ada, norm1_g=v_norm1_g, w_in=v_w_in, b_in=v_b_in, conv_dw_w=v_conv_dw_w,
             conv_dw_b=v_conv_dw_b, conv_ln_g=v_conv_ln_g, conv_ln_b=v_conv_ln_b, w_conv_out=v_w_conv_out,
             sgu_ln_g=v_sgu_ln_g, sgu_ln_b=v_sgu_ln_b, w_spatial=v_w_spatial, b_spatial=v_b_spatial,
             w_sgu_out=v_w_sgu_out, w_out=v_w_out, norm2_g=v_norm2_g, w_up=v_w_up, ffn_dw_w=v_ffn_dw_w,
             ffn_dw_b=v_ffn_dw_b, w_down=v_w_down, final_g=v_final_g)

    xs, tgt = x[0], loss_target[0]
    T, D = xs.shape
    Dc = conv_dw_w.shape[-1] * NDEV
    F = w_down.shape[1] * NDEV
    K31 = conv_dw_w.shape[1]
    G = w_spatial.shape[1]
    assert D == 2 * Dc and sgu_ln_g.shape[-1] == Dc and T % CHUNK == 0
    me = 4 * lax.axis_index("x") + 2 * lax.axis_index("y") + lax.axis_index("c")

    na = w_ada.shape[-1]
    c_all = gather_all("gather_c", [c])[0].reshape(NDEV, D)
    b_cols = lax.dynamic_slice(b_ada, (0, me * na), (1, na))
    mod_cols = ada_fwd_local(c_all, w_ada[0], b_cols)
    mod = exchange_rows("exchange_mod", mod_cols.reshape(NDEV, 1, na)).reshape(1, NDEV * na)

    wbf = {k: W[k][0].astype(BF16) for k in _BIG}
    fb = ffn_dw_b.reshape(2, 1, F)
    bsp_t = jnp.transpose(b_spatial[0])
    wsp = w_spatial[0]

    def plain(wb):
        return jnp.transpose(wb, (1, 0, 2)).reshape(1, wb.shape[1], NDEV * wb.shape[2])

    h1, (wb_in,) = pre_norm("pre_norm1", xs, mod, norm1_g, 0, comm=gather_comm([wbf["w_in"]]))
    proj, (wb_co, wb_so, wb_out, wb_up, cw_g, fw_g) = mm_nn(
        "proj", h1, wb_in, F32, bias=b_in,
        comm=gather_comm([wbf["w_conv_out"], wbf["w_sgu_out"], wbf["w_out"], wbf["w_up"], conv_dw_w[0], ffn_dw_w[0]]))
    cw = jnp.transpose(cw_g, (1, 0, 2)).reshape(K31, Dc)
    fw = jnp.transpose(jnp.transpose(fw_g, (1, 0, 2)).reshape(3, 2, F), (1, 0, 2))
    wb_out = wb_out.reshape(1, D, D)
    wp_co, wp_so = plain(wb_co), plain(wb_so)
    wp_in, wp_up = to_plain("plain_w_in", wb_in), to_plain("plain_w_up", wb_up)
    (ac, asw), (wb_down,) = mix_a_fwd(proj, cw, conv_dw_b, conv_ln_g, conv_ln_b, comm=gather_comm([wbf["w_down"]]))
    wb_down = wb_down.reshape(1, F, D)
    uv = mix_b_fwd(proj, sgu_ln_g, sgu_ln_b, wsp, bsp_t)
    y_a = mm_nn("y_a", asw, wp_co, ACT)
    y_b = mm_nn("y_b", uv, wp_so, ACT)
    merged = merge_fwd(proj, y_a, y_b)
    o1 = mm_nn("o1", merged, wb_out, F32)
    h2 = pre_norm("pre_norm2", xs, mod, norm2_g, 1, o1=o1)
    upre = mm_nn("upre", h2, wb_up, F32, out_halves=True, tn_pref=MM_WIDE)
    f = ffn_act_fwd(upre, fw, fb)
    o2 = mm_nn("o2", f, wb_down, F32)
    dx3, do2, st_f = final_fwd_bwd(xs, o1, o2, mod, final_g.reshape(1, D), tgt)
    loss = lax.psum(st_f[3, 0], MESH_AXES)

    xq, yq, cq = lax.axis_index("x"), lax.axis_index("y"), lax.axis_index("c")
    idx = jnp.stack([2 * (2 * _flip(xq, m & 2) + _flip(yq, m & 1)) + cq for m in range(4)]).astype(jnp.int32)
    own, transit, arrived = {}, {}, {}

    def add_pairs(keys, full, from_sibling):
        for k, g_full, r in zip(keys, full, from_sibling):
            own[k], transit[k] = pair_add("pair_add_" + k, g_full, r, idx)

    df = mm_nt("df", do2, wb_down, ACT, tko_pref=MM_WIDE)
    g_down = mm_tn("g_down", f, do2, 1, tko_pref=MM_WIDE).reshape(NDEV, F // NDEV, D)
    dupre, g_fw, g_fb = ffn_act_bwd(upre, df, fw, fb)
    dh2 = mm_nt("dh2", dupre, wp_up, ACT, a_halves=True)
    g_up = mm_tn("g_up", h2, dupre, NDEV, g_halves=True, tn_pref=MM_WIDE)
    (dx2, do1, st_2), sib = norm2_bwd(dh2, xs, o1, dx3, mod, norm2_g, comm=sibling_comm([g_down, g_up]))
    add_pairs(("w_down", "w_up"), (g_down, g_up), sib)
    dmerged = mm_nt("dmerged", do1, wb_out, ACT)
    g_out = mm_tn("g_out", merged, do1, 1).reshape(NDEV, D // NDEV, D)
    dy_a, dy_b, dproj, db_g = merge_bwd(dmerged, proj, y_a, y_b)
    def blocked(g):
        return jnp.transpose(g.reshape(g.shape[1], NDEV, g.shape[2] // NDEV), (1, 0, 2))

    dasw = mm_nt("dasw", dy_a, wp_co, ACT)
    g_co = blocked(mm_tn("g_co", asw, dy_a, 1))
    duv = mm_nt("duv", dy_b, wp_so, ACT)
    g_so = blocked(mm_tn("g_so", uv, dy_b, 1))
    (dproj, st_a, g_cw, db_a), (arrived["w_down"], arrived["w_up"]) = mix_a_bwd(
        dasw, ac, proj, dproj, cw, conv_ln_g, conv_ln_b, comm=chip_comm([transit["w_down"], transit["w_up"]]))
    (dproj, st_b, g_wsp, g_bsp_t, db_s), sib = mix_b_bwd(
        duv, proj, dproj, sgu_ln_g, sgu_ln_b, wsp, bsp_t, comm=sibling_comm([g_out, g_co, g_so]))
    add_pairs(("w_out", "w_conv_out", "w_sgu_out"), (g_out, g_co, g_so), sib)
    g_in = mm_tn("g_in", h1, dproj, NDEV)
    add_pairs(("w_in",), (g_in,), _run_comm("sibling_w_in", sibling_comm([g_in])))
    late = ("w_out", "w_conv_out", "w_sgu_out", "w_in")
    dh1, got = mm_nt("dh1", dproj, wp_in, ACT, comm=chip_comm([transit[k] for k in late]))
    arrived.update(zip(late, got))
    grad_x, st_1 = norm1_bwd(dh1, xs, dx2, mod, norm1_g)

    dmod = jnp.concatenate([st_1[0], st_1[1], st_2[3], st_2[0], st_2[1], st_f[1]]).reshape(1, NDEV * na)
    dmod_all = exchange_rows("exchange_dmod", dmod.reshape(NDEV, 1, na)).reshape(NDEV, na)
    g_ada = ada_bwd_local(jnp.transpose(c_all), dmod_all)

    res = {}
    for k in _BIG:
        q = arrived[k]
        res[k] = adamw("adamw_" + k, W[k][0], M[k][0], V[k][0], [(own[k], None), (q, 0), (q, 1), (q, 2)])
    res["w_ada"] = adamw("adamw_w_ada", w_ada[0], m_w_ada[0], v_w_ada[0], [(g_ada, None)])

    g_row = dict(
        b_ada=dmod, norm1_g=st_1[2], b_in=jnp.concatenate([db_a, db_s, db_g], axis=1), conv_dw_b=st_a[2],
        conv_ln_g=st_a[0], conv_ln_b=st_a[1], sgu_ln_g=st_b[0], sgu_ln_b=st_b[1], b_spatial=jnp.transpose(g_bsp_t),
        norm2_g=st_2[2], ffn_dw_b=g_fb, final_g=st_f[0])
    packed = jnp.concatenate([g_row[k].reshape(1, -1) for k in _SMALL_ROWS], axis=1)
    g_rows, g_wsp_all, g_cw_all, g_fw_all = gather_all("gather_small", [packed, g_wsp.reshape(-1, CHUNK), g_cw, g_fw])
    rows = [tuple(S[k].reshape(1, -1) for S in (W, M, V)) for k in _SMALL_ROWS]
    wsp3 = tuple(S["w_spatial"].reshape(-1, CHUNK) for S in (W, M, V))
    cw3, fw3 = (tuple(S[k][0] for S in (W, M, V)) for k in ("conv_dw_w", "ffn_dw_w"))
    small = adamw_small(jnp.reshape(me, (1,)).astype(jnp.int32), rows, g_rows, wsp3, g_wsp_all, cw3, g_cw_all, fw3, g_fw_all)
    res.update(zip(_SMALL_ROWS + ("w_spatial", "conv_dw_w", "ffn_dw_w"), small))

    outs = [[], [], [], []]
    for k in _ORDER:
        for slot in range(4):
            outs[slot].append(res[k][slot].reshape(W[k].shape))
    return (loss, grad_x[None], *outs[0], *outs[1], *outs[2], *outs[3])
```

```python
import functools

import jax
import jax.numpy as jnp
from jax import lax
from jax.experimental import pallas as pl
from jax.experimental.pallas import tpu as pltpu

F32, BF16 = jnp.float32, jnp.bfloat16
ACT = BF16
NDEV = 8
MESH_AXES = ("x", "y", "c")
MESH = pl.DeviceIdType.MESH
EPS = 1e-6
CHUNK = 128
CONV_HALO = 32
FFN_HALO = 8
LANE, SUBLANE = 128, 8
ROW_TILE = 256
FFN_ROW_TILE = 512
STRIP = 32
STRIP_UNROLL = 2
MM_TILE = 1024
MM_WIDE = 1408
MM_DEEP = 2816
VMEM_LIMIT = 56 * 1024 * 1024
ADAM_LR, ADAM_B1, ADAM_B2, ADAM_EPS, ADAM_WD, ADAM_STEP = 0.001, 0.9, 0.999, 1e-08, 0.01, 10
SQRT_HALF = 0.7071067811865476
INV_SQRT_2PI = 0.3989422804014327


def _div(n, pref, mult=LANE):
    if n <= pref:
        return n
    for d in range(pref - pref % mult, 0, -mult):
        if n % d == 0:
            return d
    return n


def _cp(sem):
    return pltpu.CompilerParams(dimension_semantics=sem, vmem_limit_bytes=VMEM_LIMIT)


def _sig(v):
    return jax.nn.sigmoid(v)


def _gelu(v):
    return 0.5 * v * (1.0 + lax.erf(v * SQRT_HALF))


def _gelu_grad(v):
    return 0.5 * (1.0 + lax.erf(v * SQRT_HALF)) + v * (INV_SQRT_2PI * jnp.exp(-0.5 * v * v))


def _colsum(v):
    return jnp.sum(v, axis=0, keepdims=True)


def _rowmean(v):
    return jnp.mean(v, axis=-1, keepdims=True)


class Comm:
    def __init__(self, plan, ins, out_shapes, n_sems):
        self.plan, self.ins, self.out_shapes, self.n_sems = plan, list(ins), list(out_shapes), tuple(n_sems)


def _pcall(name, body, grid, in_specs, out_specs, out_shape, scratch=(), sem=None, aliases=None, comm=None):
    if comm is None:
        return pl.pallas_call(
            body, name=name, grid=grid, in_specs=in_specs, out_specs=out_specs, out_shape=out_shape,
            scratch_shapes=list(scratch), input_output_aliases=aliases or {},
            compiler_params=_cp(sem or ("arbitrary",) * len(grid)))
    single = not isinstance(out_shape, (list, tuple))
    own_specs, own_shapes = ([out_specs], [out_shape]) if single else (list(out_specs), list(out_shape))
    n_in, n_out, n_scr = len(in_specs), len(own_shapes), len(scratch)
    n_ci, n_co = len(comm.ins), len(comm.out_shapes)
    any_spec = pl.BlockSpec(memory_space=pl.ANY)

    def fused(*refs):
        ins, cins = refs[:n_in], refs[n_in:n_in + n_ci]
        outs = refs[n_in + n_ci:n_in + n_ci + n_out]
        couts = refs[n_in + n_ci + n_out:n_in + n_ci + n_out + n_co]
        scr = refs[n_in + n_ci + n_out + n_co:n_in + n_ci + n_out + n_co + n_scr]
        sems = refs[n_in + n_ci + n_out + n_co + n_scr:]
        first = functools.reduce(jnp.logical_and, [pl.program_id(d) == 0 for d in range(len(grid))])
        last = functools.reduce(jnp.logical_and, [pl.program_id(d) == grid[d] - 1 for d in range(len(grid))])

        @pl.when(first)
        def _():
            comm.plan(cins, couts, sems)[0]()

        body(*ins, *outs, *scr)

        @pl.when(last)
        def _():
            comm.plan(cins, couts, sems)[1]()

    call = pl.pallas_call(
        fused, name=name, grid=grid, in_specs=list(in_specs) + [any_spec] * n_ci,
        out_specs=own_specs + [any_spec] * n_co, out_shape=own_shapes + comm.out_shapes,
        scratch_shapes=list(scratch) + [pltpu.SemaphoreType.DMA((s,)) for s in comm.n_sems],
        input_output_aliases=aliases or {},
        compiler_params=pltpu.CompilerParams(dimension_semantics=("arbitrary",) * len(grid),
                                             vmem_limit_bytes=VMEM_LIMIT, has_side_effects=True))

    def run(*args):
        res = call(*args, *comm.ins)
        own = res[:n_out]
        return (own[0] if single else list(own)), list(res[n_out:])

    return run


def _matmul(name, a, b, *, grid, a_spec, b_spec, o_spec, out_shape, dims, acc_shape, bias=None, bias_spec=None, comm=None):
    nk = grid[2]

    def body(*refs):
        if bias is None:
            a_ref, b_ref, o_ref, *scr = refs
            bias_ref = None
        else:
            a_ref, b_ref, bias_ref, o_ref, *scr = refs
        part = lax.dot_general(a_ref[...], b_ref[...], (dims, ((), ())), preferred_element_type=F32)

        def finish(total):
            if bias_ref is not None:
                total = total + bias_ref[...]
            o_ref[...] = total.astype(o_ref.dtype)

        if nk == 1:
            finish(part)
        else:
            acc = scr[0]
            k = pl.program_id(2)

            @pl.when(k == 0)
            def _():
                acc[...] = part

            @pl.when(k > 0)
            def _():
                acc[...] += part

            @pl.when(k == nk - 1)
            def _():
                finish(acc[...])

    in_specs = [a_spec, b_spec] + ([bias_spec] if bias is not None else [])
    args = (a, b) + ((bias,) if bias is not None else ())
    return _pcall(name, body, grid, in_specs, o_spec, out_shape,
                  scratch=[pltpu.VMEM(acc_shape, F32)] if nk > 1 else [],
                  sem=("parallel", "parallel", "arbitrary"), comm=comm)(*args)


def mm_nn(name, a, wb, out_dtype, *, bias=None, out_halves=False, tn_pref=MM_TILE, comm=None):
    T, K = a.shape
    NB, _, Ns = wb.shape
    N = NB * Ns
    tm, tn, tk = _div(T, MM_TILE), _div(Ns, tn_pref), _div(K, MM_DEEP)
    npb, nj, nk = Ns // tn, N // tn, K // tk
    if out_halves:
        o_spec = pl.BlockSpec((None, tm, tn), lambda i, j, k: (j // (nj // 2), i, j % (nj // 2)))
        out_shape = jax.ShapeDtypeStruct((2, T, N // 2), out_dtype)
    else:
        o_spec = pl.BlockSpec((tm, tn), lambda i, j, k: (i, j))
        out_shape = jax.ShapeDtypeStruct((T, N), out_dtype)
    return _matmul(
        name, a, wb, grid=(T // tm, nj, nk),
        a_spec=pl.BlockSpec((tm, tk), lambda i, j, k: (i, k)),
        b_spec=pl.BlockSpec((None, tk, tn), lambda i, j, k: (j // npb, k, j % npb)),
        o_spec=o_spec, out_shape=out_shape, dims=((1,), (0,)), acc_shape=(tm, tn),
        bias=bias, bias_spec=pl.BlockSpec((1, tn), lambda i, j, k: (0, j)), comm=comm)


def to_plain(name, wb):
    NB, K, Ns = wb.shape
    tk = _div(K, MM_TILE, 2 * SUBLANE)

    def body(i_ref, o_ref):
        o_ref[...] = i_ref[...]

    return _pcall(name, body, (NB, K // tk), [pl.BlockSpec((None, tk, Ns), lambda b, k: (b, k, 0))],
                  pl.BlockSpec((None, tk, Ns), lambda b, k: (0, k, b)),
                  jax.ShapeDtypeStruct((1, K, NB * Ns), wb.dtype), sem=("parallel", "parallel"))(wb)


def mm_nt(name, a, wb, out_dtype, *, a_halves=False, tko_pref=MM_TILE, tc_pref=MM_DEEP, comm=None):
    NB, K, Ns = wb.shape
    T = a.shape[-2]
    span = Ns // 2 if a_halves and NB == 1 else Ns
    tm, tko, tc = _div(T, MM_TILE), _div(K, tko_pref), _div(span, tc_pref)
    cpb = Ns // tc
    nkk = NB * cpb
    if a_halves:
        a_spec = pl.BlockSpec((None, tm, tc), lambda i, j, k: (k // (nkk // 2), i, k % (nkk // 2)))
    else:
        a_spec = pl.BlockSpec((tm, tc), lambda i, j, k: (i, k))
    return _matmul(
        name, a, wb, grid=(T // tm, K // tko, nkk), a_spec=a_spec,
        b_spec=pl.BlockSpec((None, tko, tc), lambda i, j, k: (k // cpb, j, k % cpb)),
        o_spec=pl.BlockSpec((tm, tko), lambda i, j, k: (i, j)),
        out_shape=jax.ShapeDtypeStruct((T, K), out_dtype), dims=((1,), (1,)), acc_shape=(tm, tko), comm=comm)


def mm_tn(name, a, g, nb, *, g_halves=False, tko_pref=MM_TILE, tn_pref=MM_TILE, out_dtype=BF16):
    T, K = a.shape
    N = g.shape[-1] * (2 if g_halves else 1)
    Ns = N // nb
    tt, tko, tn = _div(T, 2 * MM_TILE), _div(K, tko_pref), _div(Ns, tn_pref)
    npb, nj = Ns // tn, N // tn
    if g_halves:
        g_spec = pl.BlockSpec((None, tt, tn), lambda i, j, t: (j // (nj // 2), t, j % (nj // 2)))
    else:
        g_spec = pl.BlockSpec((tt, tn), lambda i, j, t: (t, j))
    return _matmul(
        name, a, g, grid=(K // tko, nj, T // tt),
        a_spec=pl.BlockSpec((tt, tko), lambda i, j, t: (t, i)), b_spec=g_spec,
        o_spec=pl.BlockSpec((None, tko, tn), lambda i, j, t: (j // npb, i, j % npb)),
        out_shape=jax.ShapeDtypeStruct((nb, K, Ns), out_dtype), dims=((0,), (0,)), acc_shape=(tko, tn))


def _row_call(name, body, grid, in_specs, out_specs, out_shape, scratch=(), sem=None, aliases=None, comm=None):
    return _pcall(name, body, grid, in_specs, out_specs, out_shape, scratch, sem, aliases, comm)


def _full(shape):
    nd = len(shape)
    return pl.BlockSpec(shape, lambda *idx: (0,) * nd)


def pre_norm(name, x, mod, g, which, o1=None, comm=None):
    T, D = x.shape
    tr = _div(T, ROW_TILE, SUBLANE)

    def body(*refs):
        if o1 is None:
            x_ref, mod_ref, g_ref, h_ref = refs
            xv = x_ref[...]
        else:
            x_ref, o1_ref, mod_ref, g_ref, h_ref = refs
            xv = x_ref[...] + mod_ref[:, 2 * D:3 * D] * o1_ref[...]
        shift = mod_ref[:, (3 * which) * D:(3 * which + 1) * D]
        scale = mod_ref[:, (3 * which + 1) * D:(3 * which + 2) * D]
        r = lax.rsqrt(_rowmean(xv * xv) + EPS)
        h_ref[...] = ((xv * r) * g_ref[...] * (1.0 + scale) + shift).astype(BF16)

    row = pl.BlockSpec((tr, D), lambda i: (i, 0))
    ins = [x] + ([o1] if o1 is not None else []) + [mod, g]
    specs = [row] * (1 if o1 is None else 2) + [_full(mod.shape), _full(g.shape)]
    return _row_call(name, body, (T // tr,), specs, row, jax.ShapeDtypeStruct((T, D), BF16), sem=("parallel",),
                     comm=comm)(*ins)


def _window_taps(win, taps):
    n = win.shape[0]
    for r in range(SUBLANE):
        group = [(i, tap, off) for i, (tap, off) in enumerate(taps) if off % SUBLANE == r]
        if not group:
            continue
        shifted = win if r == 0 else pltpu.roll(win, n - r, 0)
        for i, tap, off in group:
            assert 0 <= off and off + CONV_HALO <= n
            yield i, tap, shifted[off - r:off - r + CONV_HALO]


def mix_a_fwd(proj, cw, cb, lg, lb, comm=None):
    T = proj.shape[0]
    K, Dc = cw.shape
    tr = _div(T, ROW_TILE, CONV_HALO)
    hb = tr // CONV_HALO

    def body(val_ref, gate_ref, hval_ref, hgate_ref, cw_ref, cb_ref, lg_ref, lb_ref, ac_ref, asw_ref, buf):
        i = pl.program_id(0)
        hist = hval_ref[...] * _sig(hgate_ref[...])
        buf[0:CONV_HALO, :] = jnp.where(i > 0, hist, 0.0)
        buf[CONV_HALO:CONV_HALO + tr, :] = val_ref[...] * _sig(gate_ref[...])
        base = CONV_HALO - (K - 1)
        for c0 in range(0, Dc, LANE):
            lanes = slice(c0, c0 + LANE)

            def step(s, carry):
                r0 = pl.multiple_of(s * CONV_HALO, CONV_HALO)
                win = buf[pl.ds(r0, 2 * CONV_HALO), lanes]
                acc = jnp.zeros((CONV_HALO, LANE), F32)
                for _, k, piece in _window_taps(win, [(k, base + k) for k in range(K)]):
                    acc = acc + piece * cw_ref[k:k + 1, lanes]
                ac_ref[pl.ds(r0, CONV_HALO), lanes] = acc + cb_ref[:, lanes]
                return carry

            lax.fori_loop(0, tr // CONV_HALO, step, 0)
        ac = ac_ref[...]
        mu = _rowmean(ac)
        cen = ac - mu
        y = cen * lax.rsqrt(_rowmean(cen * cen) + EPS)
        aln = y * lg_ref[...] + lb_ref[...]
        asw_ref[...] = (aln * _sig(aln)).astype(BF16)

    def halo(col):
        return pl.BlockSpec((CONV_HALO, Dc), lambda i: (jnp.maximum(i * hb - 1, 0), col))

    row = pl.BlockSpec((tr, Dc), lambda i: (i, 0))
    return _row_call(
        "mix_a_fwd", body, (T // tr,),
        [row, pl.BlockSpec((tr, Dc), lambda i: (i, 1)), halo(0), halo(1),
         _full(cw.shape), _full(cb.shape), _full(lg.shape), _full(lb.shape)],
        [row, row], [jax.ShapeDtypeStruct((T, Dc), F32), jax.ShapeDtypeStruct((T, Dc), BF16)],
        scratch=[pltpu.VMEM((CONV_HALO + tr, Dc), F32)], sem=("parallel",), comm=comm)(proj, proj, proj, proj, cw, cb, lg, lb)


def _spatial_mask():
    t = lax.broadcasted_iota(jnp.int32, (CHUNK, CHUNK), 0)
    s = lax.broadcasted_iota(jnp.int32, (CHUNK, CHUNK), 1)
    return s <= t


def mix_b_fwd(proj, lg, lb, wsp, bsp_t):
    T = proj.shape[0]
    Ds = lg.shape[-1]
    G = wsp.shape[0]
    hd = Ds // G
    tr = _div(T, ROW_TILE, CHUNK)

    def body(u_ref, v_ref, lg_ref, lb_ref, w_ref, b_ref, uv_ref, vs):
        v = _gelu(v_ref[...])
        mu = _rowmean(v)
        cen = v - mu
        vln = (cen * lax.rsqrt(_rowmean(cen * cen) + EPS) * lg_ref[...] + lb_ref[...]).astype(BF16)
        mask = _spatial_mask()
        for g in range(G):
            wg = jnp.where(mask, w_ref[g], 0.0).astype(BF16)
            for n in range(tr // CHUNK):
                rows, cols = slice(n * CHUNK, (n + 1) * CHUNK), slice(g * hd, (g + 1) * hd)
                vs[rows, cols] = jnp.dot(wg, vln[rows, cols], preferred_element_type=F32) + b_ref[:, g:g + 1]
        uv_ref[...] = (_gelu(u_ref[...]) * vs[...]).astype(BF16)

    return _row_call(
        "mix_b_fwd", body, (T // tr,),
        [pl.BlockSpec((tr, Ds), lambda i: (i, 2)), pl.BlockSpec((tr, Ds), lambda i: (i, 3)),
         _full(lg.shape), _full(lb.shape), _full(wsp.shape), _full(bsp_t.shape)],
        pl.BlockSpec((tr, Ds), lambda i: (i, 0)), jax.ShapeDtypeStruct((T, Ds), BF16),
        scratch=[pltpu.VMEM((tr, Ds), F32)], sem=("parallel",))(proj, proj, lg, lb, wsp, bsp_t)


def merge_fwd(proj, y_a, y_b):
    T, D = y_a.shape
    tr = _div(T, ROW_TILE, SUBLANE)

    def body(g_ref, ya_ref, yb_ref, o_ref):
        o_ref[...] = (_sig(g_ref[:, 0:D]) * ya_ref[...].astype(F32)
                      + _sig(g_ref[:, D:2 * D]) * yb_ref[...].astype(F32)).astype(BF16)

    row = pl.BlockSpec((tr, D), lambda i: (i, 0))
    return _row_call("merge_fwd", body, (T // tr,), [pl.BlockSpec((tr, 2 * D), lambda i: (i, 1)), row, row], row,
                     jax.ShapeDtypeStruct((T, D), BF16), sem=("parallel",))(proj, y_a, y_b)


def _fold(v):
    acc = v[0:SUBLANE]
    for r in range(SUBLANE, v.shape[0], SUBLANE):
        acc = acc + v[r:r + SUBLANE]
    return acc


def _conv3(prev, cur, w):
    win = jnp.concatenate([prev, cur], axis=0)
    n = win.shape[0]
    x1 = pltpu.roll(win, 1, 0)[FFN_HALO:n]
    x2 = pltpu.roll(win, 2, 0)[FFN_HALO:n]
    return x2 * w[0] + x1 * w[1] + cur * w[2], (x2, x1, cur)


def _strip_taps(w_ref, b_ref, lanes):
    w = [[w_ref[h, k:k + 1, lanes] for k in range(3)] for h in range(2)]
    b = [b_ref[h, :, lanes] for h in range(2)]
    return w, b


def ffn_act_fwd(upre, fw, fb):
    _, T, F = upre.shape
    tr = _div(T, FFN_ROW_TILE, STRIP)
    cb = _div(F, MM_WIDE)
    hb = tr // FFN_HALO
    ns = tr // STRIP

    def body(x_ref, h_ref, w_ref, b_ref, f_ref, u_ref):
        i = pl.program_id(0)
        for c0 in range(0, cb, LANE):
            lanes = slice(c0, c0 + LANE)
            w, b = _strip_taps(w_ref, b_ref, lanes)

            def strip(r0, prev):
                up = [_conv3(prev(h), x_ref[h, pl.ds(r0, STRIP), lanes], w[h])[0] + b[h] for h in range(2)]
                f_ref[pl.ds(r0, STRIP), lanes] = (up[1] * _sig(up[1]) * up[0]).astype(BF16)
                for h in range(2):
                    u_ref[h, pl.ds(r0, STRIP), lanes] = up[h]

            strip(0, lambda h: jnp.where(i > 0, h_ref[h, :, lanes], 0.0))

            def step(s, carry):
                r0 = pl.multiple_of(s * STRIP, STRIP)
                strip(r0, lambda h: x_ref[h, pl.ds(pl.multiple_of(r0 - FFN_HALO, FFN_HALO), FFN_HALO), lanes])
                return carry

            lax.fori_loop(1, ns, step, 0, unroll=STRIP_UNROLL)

    return _row_call(
        "ffn_act_fwd", body, (T // tr, F // cb),
        [pl.BlockSpec((2, tr, cb), lambda i, j: (0, i, j)),
         pl.BlockSpec((2, FFN_HALO, cb), lambda i, j: (0, jnp.maximum(i * hb - 1, 0), j)),
         pl.BlockSpec((2, 3, cb), lambda i, j: (0, 0, j)), pl.BlockSpec((2, 1, cb), lambda i, j: (0, 0, j))],
        [pl.BlockSpec((tr, cb), lambda i, j: (i, j)), pl.BlockSpec((2, tr, cb), lambda i, j: (0, i, j))],
        [jax.ShapeDtypeStruct((T, F), BF16), jax.ShapeDtypeStruct((2, T, F), F32)],
        sem=("parallel", "parallel"))(upre, upre, fw, fb)


def final_fwd_bwd(x, o1, o2, mod, gf, target):
    T, D = x.shape
    tr = _div(T, ROW_TILE, SUBLANE)
    nt = T // tr

    def body(x_ref, o1_ref, o2_ref, mod_ref, gf_ref, t_ref, dx3_ref, do2_ref, st_ref):
        i = pl.program_id(0)
        gate1, gate2 = mod_ref[:, 2 * D:3 * D], mod_ref[:, 5 * D:6 * D]
        o2v = o2_ref[...]
        x3 = x_ref[...] + gate1 * o1_ref[...] + gate2 * o2v
        r = lax.rsqrt(_rowmean(x3 * x3) + EPS)
        xn = x3 * r
        err = xn * gf_ref[...] - t_ref[...]
        dy = err * (1.0 / D)
        dxn = dy * gf_ref[...]
        dx3 = r * (dxn - xn * _rowmean(dxn * xn))
        dx3_ref[...] = dx3
        do2_ref[...] = (dx3 * gate2).astype(BF16)

        @pl.when(i == 0)
        def _():
            st_ref[...] = jnp.zeros_like(st_ref)

        st_ref[0:1, :] += _colsum(dy * xn)
        st_ref[1:2, :] += _colsum(dx3 * o2v)
        st_ref[2:3, :] += _colsum(err * err) * (0.5 / D)

        @pl.when(i == nt - 1)
        def _():
            st_ref[3:4, :] = jnp.zeros((1, D), F32) + jnp.sum(st_ref[2:3, :])

    row = pl.BlockSpec((tr, D), lambda i: (i, 0))
    return _row_call(
        "final_fwd_bwd", body, (nt,), [row, row, row, _full(mod.shape), _full(gf.shape), row],
        [row, row, _full((8, D))],
        [jax.ShapeDtypeStruct((T, D), F32), jax.ShapeDtypeStruct((T, D), BF16), jax.ShapeDtypeStruct((8, D), F32)],
    )(x, o1, o2, mod, gf, target)


def ffn_act_bwd(upre, up, df, fw):
    _, T, F = upre.shape
    tr = _div(T, FFN_ROW_TILE, STRIP)
    cb = _div(F, MM_WIDE)
    nt = T // tr
    ns = tr // STRIP

    def body(x_ref, u_ref, df_ref, w_ref, dpre_ref, dw_ref, db_ref, carry):
        i = pl.program_id(1)

        @pl.when(i == 0)
        def _():
            carry[...] = jnp.zeros_like(carry)
            dw_ref[...] = jnp.zeros_like(dw_ref)
            db_ref[...] = jnp.zeros_like(db_ref)

        for c0 in range(0, cb, LANE):
            lanes = slice(c0, c0 + LANE)
            w = [[w_ref[h, k:k + 1, lanes] for k in range(3)] for h in range(2)]

            def step(s, state):
                later, db, dw = state
                r0 = pl.multiple_of((ns - 1 - s) * STRIP, STRIP)
                rows = pl.ds(r0, STRIP)
                val, gt = u_ref[0, rows, lanes], u_ref[1, rows, lanes]
                sg = _sig(gt)
                dfv = df_ref[rows, lanes].astype(F32)
                dup = (dfv * (gt * sg), dfv * val * (sg * (1.0 + gt * (1.0 - sg))))
                new_db, new_dw = [], []
                for h in range(2):
                    dwin = jnp.concatenate([dup[h], later[h]], axis=0)
                    n = dwin.shape[0]
                    d1 = pltpu.roll(dwin, n - 1, 0)[0:STRIP]
                    d2 = pltpu.roll(dwin, n - 2, 0)[0:STRIP]
                    xs = x_ref[h, rows, lanes]
                    new_db.append(db[h] + _fold(dup[h]))
                    new_dw.append((dw[h][0] + _fold(d2 * xs), dw[h][1] + _fold(d1 * xs), dw[h][2] + _fold(dup[h] * xs)))
                    dpre_ref[h, rows, lanes] = (dup[h] * w[h][2] + d1 * w[h][1] + d2 * w[h][0]).astype(BF16)
                return tuple(dup[h][0:FFN_HALO] for h in range(2)), tuple(new_db), tuple(new_dw)

            zero = jnp.zeros((SUBLANE, LANE), F32)
            state = ((carry[0, :, lanes], carry[1, :, lanes]), (zero, zero), ((zero,) * 3,) * 2)
            later, db, dw = lax.fori_loop(0, ns, step, state, unroll=STRIP_UNROLL)
            for h in range(2):
                carry[h, :, lanes] = later[h]
                db_ref[h, :, lanes] += _colsum(db[h])
                for k in range(3):
                    dw_ref[h, k:k + 1, lanes] += _colsum(dw[h][k])

    tile = pl.BlockSpec((2, tr, cb), lambda j, i: (0, nt - 1 - i, j))
    return _row_call(
        "ffn_act_bwd", body, (F // cb, nt),
        [tile, tile, pl.BlockSpec((tr, cb), lambda j, i: (nt - 1 - i, j)), pl.BlockSpec((2, 3, cb), lambda j, i: (0, 0, j))],
        [tile, pl.BlockSpec((2, 3, cb), lambda j, i: (0, 0, j)), pl.BlockSpec((2, 1, cb), lambda j, i: (0, 0, j))],
        [jax.ShapeDtypeStruct((2, T, F), BF16), jax.ShapeDtypeStruct((2, 3, F), F32), jax.ShapeDtypeStruct((2, 1, F), F32)],
        scratch=[pltpu.VMEM((2, FFN_HALO, cb), F32)],
        sem=("parallel", "arbitrary"))(upre, up, df, fw)


def norm2_bwd(dh2, x, o1, dx3, mod, g2, comm=None):
    T, D = x.shape
    tr = _div(T, ROW_TILE, SUBLANE)

    def body(dh_ref, x_ref, o1_ref, dx3_ref, mod_ref, g_ref, dx2_ref, do1_ref, st_ref):
        i = pl.program_id(0)
        gate1, scale = mod_ref[:, 2 * D:3 * D], mod_ref[:, 4 * D:5 * D]
        o1v = o1_ref[...]
        x2 = x_ref[...] + gate1 * o1v
        r = lax.rsqrt(_rowmean(x2 * x2) + EPS)
        xn = x2 * r
        dh = dh_ref[...].astype(F32)
        dxn = dh * (g_ref[...] * (1.0 + scale))
        dx2 = r * (dxn - xn * _rowmean(dxn * xn)) + dx3_ref[...]
        dx2_ref[...] = dx2
        do1_ref[...] = (dx2 * gate1).astype(BF16)

        @pl.when(i == 0)
        def _():
            st_ref[...] = jnp.zeros_like(st_ref)

        st_ref[0:1, :] += _colsum(dh)
        st_ref[1:2, :] += _colsum(dh * xn) * g_ref[...]
        st_ref[2:3, :] += _colsum(dh * xn) * (1.0 + scale)
        st_ref[3:4, :] += _colsum(dx2 * o1v)

    row = pl.BlockSpec((tr, D), lambda i: (i, 0))
    return _row_call(
        "norm2_bwd", body, (T // tr,), [row, row, row, row, _full(mod.shape), _full(g2.shape)],
        [row, row, _full((8, D))],
        [jax.ShapeDtypeStruct((T, D), F32), jax.ShapeDtypeStruct((T, D), BF16), jax.ShapeDtypeStruct((8, D), F32)],
        comm=comm)(dh2, x, o1, dx3, mod, g2)


def merge_bwd(dmerged, proj, y_a, y_b):
    T, D = y_a.shape
    tr = _div(T, ROW_TILE, SUBLANE)

    def body(dm_ref, g_ref, ya_ref, yb_ref, dya_ref, dyb_ref, dp_ref, db_ref):
        i = pl.program_id(0)
        dm = dm_ref[...].astype(F32)
        sa, sb = _sig(g_ref[:, 0:D]), _sig(g_ref[:, D:2 * D])
        dya_ref[...] = (dm * sa).astype(BF16)
        dyb_ref[...] = (dm * sb).astype(BF16)
        dga = dm * ya_ref[...].astype(F32) * (sa * (1.0 - sa))
        dgb = dm * yb_ref[...].astype(F32) * (sb * (1.0 - sb))
        dp_ref[:, 0:D] = dga.astype(BF16)
        dp_ref[:, D:2 * D] = dgb.astype(BF16)

        @pl.when(i == 0)
        def _():
            db_ref[...] = jnp.zeros_like(db_ref)

        db_ref[:, 0:D] += _colsum(dga)
        db_ref[:, D:2 * D] += _colsum(dgb)

    row = pl.BlockSpec((tr, D), lambda i: (i, 0))
    wide = pl.BlockSpec((tr, 2 * D), lambda i: (i, 1))
    return _row_call(
        "merge_bwd", body, (T // tr,), [row, wide, row, row], [row, row, wide, _full((1, 2 * D))],
        [jax.ShapeDtypeStruct((T, D), BF16), jax.ShapeDtypeStruct((T, D), BF16),
         jax.ShapeDtypeStruct((T, 4 * D), BF16), jax.ShapeDtypeStruct((1, 2 * D), F32)],
    )(dmerged, proj, y_a, y_b)


def mix_a_bwd(dasw, ac, proj, dproj, cw, lg, lb, comm=None):
    T, Dc = ac.shape
    K = cw.shape[0]
    tr = _div(T, ROW_TILE, CONV_HALO)
    nt = T // tr

    def body(dasw_ref, ac_ref, in_ref, dp_hbm, cw_ref, lg_ref, lb_ref,
             dp_ref, st_ref, dcw_ref, db_ref, abuf, dbuf, carry, da_buf):
        del dp_hbm
        i = pl.program_id(0)
        acv = ac_ref[...]
        mu = _rowmean(acv)
        cen = acv - mu
        rstd = lax.rsqrt(_rowmean(cen * cen) + EPS)
        y = cen * rstd
        aln = y * lg_ref[...] + lb_ref[...]
        sg = _sig(aln)
        daln = dasw_ref[...].astype(F32) * (sg * (1.0 + aln * (1.0 - sg)))
        dy = daln * lg_ref[...]
        dac = rstd * (dy - _rowmean(dy) - y * _rowmean(dy * y))

        @pl.when(i == 0)
        def _():
            carry[...] = jnp.zeros_like(carry)
            st_ref[...] = jnp.zeros_like(st_ref)
            dcw_ref[...] = jnp.zeros_like(dcw_ref)
            db_ref[...] = jnp.zeros_like(db_ref)

        st_ref[0:1, :] += _colsum(daln * y)
        st_ref[1:2, :] += _colsum(daln)
        st_ref[2:3, :] += _colsum(dac)
        val, gate = in_ref[:, 0:Dc], in_ref[:, Dc:2 * Dc]
        sgg = _sig(gate)
        abuf[...] = val * sgg
        dbuf[0:tr, :] = dac
        dbuf[tr:tr + CONV_HALO, :] = carry[...]
        carry[...] = dac[0:CONV_HALO, :]
        zero = jnp.zeros((SUBLANE, LANE), F32)
        for c0 in range(0, Dc, LANE):
            lanes = slice(c0, c0 + LANE)

            def step(s, dws):
                r0 = pl.multiple_of(s * CONV_HALO, CONV_HALO)
                dwin = dbuf[pl.ds(r0, 2 * CONV_HALO), lanes]
                dws = list(dws)
                a_piece = abuf[pl.ds(r0, CONV_HALO), lanes]
                acc = jnp.zeros((CONV_HALO, LANE), F32)
                for _, k, piece in _window_taps(dwin, [(k, K - 1 - k) for k in range(K)]):
                    acc = acc + piece * cw_ref[k:k + 1, lanes]
                    dws[k] = dws[k] + _fold(piece * a_piece)
                da_buf[pl.ds(r0, CONV_HALO), lanes] = acc
                return tuple(dws)

            dws = lax.fori_loop(0, tr // CONV_HALO, step, (zero,) * K)
            for k in range(K):
                dcw_ref[k:k + 1, lanes] += _colsum(dws[k])
        da = da_buf[...]
        dval = da * sgg
        dgate = da * val * (sgg * (1.0 - sgg))
        dp_ref[:, 0:Dc] = dval.astype(BF16)
        dp_ref[:, Dc:2 * Dc] = dgate.astype(BF16)
        db_ref[:, 0:Dc] += _colsum(dval)
        db_ref[:, Dc:2 * Dc] += _colsum(dgate)

    row = pl.BlockSpec((tr, Dc), lambda i: (nt - 1 - i, 0))
    wide = pl.BlockSpec((tr, 2 * Dc), lambda i: (nt - 1 - i, 0))
    return _row_call(
        "mix_a_bwd", body, (nt,),
        [row, row, wide, pl.BlockSpec(memory_space=pl.ANY), _full(cw.shape), _full(lg.shape), _full(lb.shape)],
        [wide, _full((8, Dc)), _full((CONV_HALO, Dc)), _full((1, 2 * Dc))],
        [jax.ShapeDtypeStruct(dproj.shape, BF16), jax.ShapeDtypeStruct((8, Dc), F32),
         jax.ShapeDtypeStruct((CONV_HALO, Dc), F32), jax.ShapeDtypeStruct((1, 2 * Dc), F32)],
        scratch=[pltpu.VMEM((tr, Dc), F32), pltpu.VMEM((tr + CONV_HALO, Dc), F32),
                 pltpu.VMEM((CONV_HALO, Dc), F32), pltpu.VMEM((tr, Dc), F32)],
        aliases={3: 0}, comm=comm)(dasw, ac, proj, dproj, cw, lg, lb)


def mix_b_bwd(duv, proj, dproj, lg, lb, wsp, bsp_t, comm=None):
    T, Ds = duv.shape
    G = wsp.shape[0]
    hd = Ds // G
    tr = _div(T, ROW_TILE, CHUNK)
    nt = T // tr

    def body(duv_ref, s_ref, dp_hbm, lg_ref, lb_ref, w_ref, b_ref,
             dp_ref, st_ref, dws_ref, dbs_ref, db_ref, vs, dvln):
        del dp_hbm
        i = pl.program_id(0)

        @pl.when(i == 0)
        def _():
            st_ref[...] = jnp.zeros_like(st_ref)
            dws_ref[...] = jnp.zeros_like(dws_ref)
            dbs_ref[...] = jnp.zeros_like(dbs_ref)
            db_ref[...] = jnp.zeros_like(db_ref)

        upre, vpre = s_ref[:, 0:Ds], s_ref[:, Ds:2 * Ds]
        u, v = _gelu(upre), _gelu(vpre)
        mu = _rowmean(v)
        cen = v - mu
        rstd = lax.rsqrt(_rowmean(cen * cen) + EPS)
        yv = cen * rstd
        vln = (yv * lg_ref[...] + lb_ref[...]).astype(BF16)
        duvv = duv_ref[...].astype(F32)
        dvs = duvv * u
        dvs_b = dvs.astype(BF16)
        mask = _spatial_mask()
        for g in range(G):
            wg = jnp.where(mask, w_ref[g], 0.0).astype(BF16)
            cols = slice(g * hd, (g + 1) * hd)
            dws = jnp.zeros((CHUNK, CHUNK), F32)
            dbs = jnp.zeros((CHUNK, 1), F32)
            for n in range(tr // CHUNK):
                rows = slice(n * CHUNK, (n + 1) * CHUNK)
                vs[rows, cols] = jnp.dot(wg, vln[rows, cols], preferred_element_type=F32) + b_ref[:, g:g + 1]
                dvln[rows, cols] = lax.dot_general(wg, dvs_b[rows, cols], (((0,), (0,)), ((), ())),
                                                   preferred_element_type=F32)
                dws = dws + lax.dot_general(dvs_b[rows, cols], vln[rows, cols], (((1,), (1,)), ((), ())),
                                            preferred_element_type=F32)
                dbs = dbs + jnp.sum(dvs[rows, cols], axis=1, keepdims=True)
            dws_ref[g] += jnp.where(mask, dws, 0.0)
            dbs_ref[:, g:g + 1] += dbs
        dvl = dvln[...]
        st_ref[0:1, :] += _colsum(dvl * yv)
        st_ref[1:2, :] += _colsum(dvl)
        dyv = dvl * lg_ref[...]
        dv = rstd * (dyv - _rowmean(dyv) - yv * _rowmean(dyv * yv))
        dupre = duvv * vs[...] * _gelu_grad(upre)
        dvpre = dv * _gelu_grad(vpre)
        dp_ref[:, 0:Ds] = dupre.astype(BF16)
        dp_ref[:, Ds:2 * Ds] = dvpre.astype(BF16)
        db_ref[:, 0:Ds] += _colsum(dupre)
        db_ref[:, Ds:2 * Ds] += _colsum(dvpre)

    wide = pl.BlockSpec((tr, 2 * Ds), lambda i: (i, 1))
    return _row_call(
        "mix_b_bwd", body, (nt,),
        [pl.BlockSpec((tr, Ds), lambda i: (i, 0)), wide, pl.BlockSpec(memory_space=pl.ANY),
         _full(lg.shape), _full(lb.shape), _full(wsp.shape), _full(bsp_t.shape)],
        [wide, _full((8, Ds)), _full(wsp.shape), _full(bsp_t.shape), _full((1, 2 * Ds))],
        [jax.ShapeDtypeStruct(dproj.shape, BF16), jax.ShapeDtypeStruct((8, Ds), F32),
         jax.ShapeDtypeStruct(wsp.shape, F32), jax.ShapeDtypeStruct(bsp_t.shape, F32),
         jax.ShapeDtypeStruct((1, 2 * Ds), F32)],
        scratch=[pltpu.VMEM((tr, Ds), F32), pltpu.VMEM((tr, Ds), F32)],
        aliases={2: 0}, comm=comm)(duv, proj, dproj, lg, lb, wsp, bsp_t)


def norm1_bwd(dh1, x, dx2, mod, g1):
    T, D = x.shape
    tr = _div(T, ROW_TILE, SUBLANE)

    def body(dh_ref, x_ref, dx2_ref, mod_ref, g_ref, gx_ref, st_ref):
        i = pl.program_id(0)
        scale = mod_ref[:, D:2 * D]
        xv = x_ref[...]
        r = lax.rsqrt(_rowmean(xv * xv) + EPS)
        xn = xv * r
        dh = dh_ref[...].astype(F32)
        dxn = dh * (g_ref[...] * (1.0 + scale))
        gx_ref[...] = r * (dxn - xn * _rowmean(dxn * xn)) + dx2_ref[...]

        @pl.when(i == 0)
        def _():
            st_ref[...] = jnp.zeros_like(st_ref)

        st_ref[0:1, :] += _colsum(dh)
        st_ref[1:2, :] += _colsum(dh * xn) * g_ref[...]
        st_ref[2:3, :] += _colsum(dh * xn) * (1.0 + scale)

    row = pl.BlockSpec((tr, D), lambda i: (i, 0))
    return _row_call(
        "norm1_bwd", body, (T // tr,), [row, row, row, _full(mod.shape), _full(g1.shape)], [row, _full((8, D))],
        [jax.ShapeDtypeStruct((T, D), F32), jax.ShapeDtypeStruct((8, D), F32)])(dh1, x, dx2, mod, g1)


def ada_fwd_local(c_all, w_ada, b_cols):
    B, D = c_all.shape
    Na = w_ada.shape[1]
    tn = _div(Na, 512)

    def body(c_ref, w_ref, b_ref, o_ref):
        cv = c_ref[...]
        act = (cv * _sig(cv)).astype(BF16)
        o_ref[...] = jnp.dot(act, w_ref[...].astype(BF16), preferred_element_type=F32) + b_ref[...]

    return _row_call(
        "ada_fwd_local", body, (Na // tn,),
        [_full(c_all.shape), pl.BlockSpec((D, tn), lambda j: (0, j)), pl.BlockSpec((1, tn), lambda j: (0, j))],
        pl.BlockSpec((B, tn), lambda j: (0, j)), jax.ShapeDtypeStruct((B, Na), F32), sem=("parallel",))(c_all, w_ada, b_cols)


def ada_bwd_local(c_all_t, dmod_all):
    D, B = c_all_t.shape
    Na = dmod_all.shape[1]
    tr = _div(D, 512, SUBLANE)

    def body(c_ref, d_ref, o_ref):
        cv = c_ref[...]
        act = cv * _sig(cv)
        acc = act[:, 0:1] * d_ref[0:1, :]
        for b in range(1, B):
            acc = acc + act[:, b:b + 1] * d_ref[b:b + 1, :]
        o_ref[...] = acc

    return _row_call(
        "ada_bwd_local", body, (D // tr,), [pl.BlockSpec((tr, B), lambda i: (i, 0)), _full(dmod_all.shape)],
        pl.BlockSpec((tr, Na), lambda i: (i, 0)), jax.ShapeDtypeStruct((D, Na), F32), sem=("parallel",))(c_all_t, dmod_all)


def _adam_update(w, m, v, g):
    mn = ADAM_B1 * m + (1.0 - ADAM_B1) * g
    vn = ADAM_B2 * v + (1.0 - ADAM_B2) * (g * g)
    bc1, bc2 = 1.0 - ADAM_B1 ** ADAM_STEP, 1.0 - ADAM_B2 ** ADAM_STEP
    return g, -ADAM_LR * ((mn / bc1) / (jnp.sqrt(vn / bc2) + ADAM_EPS) + ADAM_WD * w), mn, vn


def adamw(name, w, m, v, parts):
    R, C = w.shape
    tr = _div(R, max(SUBLANE, (1 << 18) // C // SUBLANE * SUBLANE), SUBLANE)
    n = len(parts)

    def body(*refs):
        w_ref, m_ref, v_ref = refs[:3]
        g_ref, d_ref, nm_ref, nv_ref = refs[3 + n:]
        g = refs[3][...].astype(F32)
        for p in refs[4:3 + n]:
            g = g + p[...].astype(F32)
        g_ref[...], d_ref[...], nm_ref[...], nv_ref[...] = _adam_update(w_ref[...], m_ref[...], v_ref[...], g)

    row = pl.BlockSpec((tr, C), lambda i: (i, 0))
    pspecs = [row if lead is None else pl.BlockSpec((None, tr, C), lambda i, lead=lead: (lead, i, 0)) for _, lead in parts]
    out = jax.ShapeDtypeStruct((R, C), F32)
    return _row_call(name, body, (R // tr,), [row, row, row] + pspecs, [row] * 4, [out] * 4, sem=("parallel",))(
        w, m, v, *[a for a, _ in parts])


def pair_add(name, g, r, idx):
    _, R, C = g.shape
    tr = _div(R, max(SUBLANE, (1 << 17) // C // SUBLANE * SUBLANE), SUBLANE)

    def body(idx_ref, g0, g1, g2, g3, r_ref, own_ref, tr_ref):
        del idx_ref
        own_ref[...] = g0[...].astype(F32) + r_ref[0].astype(F32)
        for m, gm in ((1, g1), (2, g2), (3, g3)):
            tr_ref[m - 1] = (gm[...].astype(F32) + r_ref[m].astype(F32)).astype(BF16)

    def gspec(m):
        return pl.BlockSpec((None, tr, C), lambda i, idx_ref: (idx_ref[m], i, 0))

    return pl.pallas_call(
        body, name=name,
        grid_spec=pltpu.PrefetchScalarGridSpec(
            num_scalar_prefetch=1, grid=(R // tr,),
            in_specs=[gspec(0), gspec(1), gspec(2), gspec(3), pl.BlockSpec((4, tr, C), lambda i, idx_ref: (0, i, 0))],
            out_specs=[pl.BlockSpec((tr, C), lambda i, idx_ref: (i, 0)),
                       pl.BlockSpec((3, tr, C), lambda i, idx_ref: (0, i, 0))]),
        out_shape=[jax.ShapeDtypeStruct((R, C), F32), jax.ShapeDtypeStruct((3, R, C), BF16)],
        compiler_params=_cp(("parallel",)))(idx, g, g, g, g, r)


def adamw_small(me, rows, g_rows, wsp, g_wsp, cwp, g_cw, fwp, g_fw):
    sizes = [w.shape[1] for w, _, _ in rows]
    offs = [sum(sizes[:i]) for i in range(len(rows))]
    k_cw, n_cw = cwp[0].shape
    n_fw = fwp[0].shape[1]
    per_half = g_fw.shape[-1] // n_fw
    groups = list(rows) + [wsp, cwp, fwp]
    n_p, n_r = len(groups), len(rows)

    def body(me_ref, g_rows_ref, g_wsp_ref, g_cw_ref, g_fw_ref, *refs):
        del me_ref
        wmv, outs = refs[:3 * n_p], refs[3 * n_p:]
        for p in range(n_p):
            if p < n_r:
                pick = lambda d, p=p: g_rows_ref[d, :, offs[p]:offs[p] + sizes[p]]
            elif p == n_r:
                pick = lambda d: g_wsp_ref[d]
            elif p == n_r + 1:
                pick = lambda d: g_cw_ref[d, 0:k_cw, :]
            else:
                pick = lambda d: g_fw_ref[d]
            g = pick(0)
            for d in range(1, NDEV):
                g = g + pick(d)
            res = _adam_update(wmv[3 * p][...], wmv[3 * p + 1][...], wmv[3 * p + 2][...], g)
            for slot in range(4):
                outs[4 * p + slot][...] = res[slot]

    def full(a):
        nd = len(a.shape)
        return pl.BlockSpec(a.shape, lambda i, me_ref: (0,) * nd)

    in_specs = [full(g_rows), full(g_wsp),
                pl.BlockSpec((NDEV, g_cw.shape[1], n_cw), lambda i, me_ref: (0, 0, me_ref[0])),
                pl.BlockSpec((NDEV, None, 3, n_fw), lambda i, me_ref: (0, me_ref[0] // per_half, 0, me_ref[0] % per_half))]
    in_specs += [full(a) for grp in groups for a in grp]
    out_shape = [jax.ShapeDtypeStruct(grp[0].shape, F32) for grp in groups for _ in range(4)]
    flat = pl.pallas_call(
        body, name="adamw_small",
        grid_spec=pltpu.PrefetchScalarGridSpec(num_scalar_prefetch=1, grid=(1,), in_specs=in_specs,
                                               out_specs=[full(s) for s in out_shape]),
        out_shape=out_shape, compiler_params=_cp(("arbitrary",)))(
            me, g_rows, g_wsp, g_cw, g_fw, *[a for grp in groups for a in grp])
    return [tuple(flat[4 * p:4 * p + 4]) for p in range(n_p)]


def _coords():
    return lax.axis_index("x"), lax.axis_index("y"), lax.axis_index("c")


def _flip(v, bit):
    return 1 - v if bit else v


def _comm_call(name, body, ins, out_shapes, n_sems):
    any_spec = pl.BlockSpec(memory_space=pl.ANY)
    return pl.pallas_call(
        body, name=name, in_specs=[any_spec] * len(ins), out_specs=[any_spec] * len(out_shapes), out_shape=out_shapes,
        scratch_shapes=[pltpu.SemaphoreType.DMA((s,)) for s in n_sems],
        compiler_params=pltpu.CompilerParams(has_side_effects=True))(*ins)


def gather_all(name, tensors):
    L = len(tensors)

    def body(*refs):
        ins, outs = refs[:L], refs[L:2 * L]
        send_sems, recv_sems, local_sems = refs[2 * L:]
        x, y, c = _coords()
        me = 4 * x + 2 * y + c
        local = [pltpu.make_async_copy(ins[l], outs[l].at[me], local_sems.at[l]) for l in range(L)]
        copies = []
        for l in range(L):
            for k in range(1, NDEV):
                peer = (_flip(x, k & 4), _flip(y, k & 2), _flip(c, k & 1))
                copies.append(pltpu.make_async_remote_copy(
                    src_ref=ins[l], dst_ref=outs[l].at[me], send_sem=send_sems.at[7 * l + k - 1],
                    recv_sem=recv_sems.at[7 * l + k - 1], device_id=peer, device_id_type=MESH))
        for cp in local + copies:
            cp.start()
        for cp in copies:
            cp.wait_recv()
        for cp in copies:
            cp.wait_send()
        for cp in local:
            cp.wait()

    outs = [jax.ShapeDtypeStruct((NDEV,) + t.shape, t.dtype) for t in tensors]
    return _comm_call(name, body, list(tensors), outs, (7 * L, 7 * L, L))


def exchange_rows(name, slabs):
    def body(in_ref, out_ref, send_sems, recv_sems, local_sem):
        x, y, c = _coords()
        me = 4 * x + 2 * y + c
        mine = pltpu.make_async_copy(in_ref.at[me], out_ref.at[me], local_sem.at[0])
        mine.start()
        copies = []
        for k in range(1, NDEV):
            px, py, pc = _flip(x, k & 4), _flip(y, k & 2), _flip(c, k & 1)
            copies.append(pltpu.make_async_remote_copy(
                src_ref=in_ref.at[4 * px + 2 * py + pc], dst_ref=out_ref.at[me], send_sem=send_sems.at[k - 1],
                recv_sem=recv_sems.at[k - 1], device_id=(px, py, pc), device_id_type=MESH))
        for cp in copies:
            cp.start()
        for cp in copies:
            cp.wait_recv()
        for cp in copies:
            cp.wait_send()
        mine.wait()

    return _comm_call(name, body, [slabs], [jax.ShapeDtypeStruct(slabs.shape, slabs.dtype)], (NDEV - 1, NDEV - 1, 1))[0]


def _run_comm(name, comm):
    n_i, n_o = len(comm.ins), len(comm.out_shapes)

    def body(*refs):
        start, finish = comm.plan(refs[:n_i], refs[n_i:n_i + n_o], refs[n_i + n_o:])
        start()
        finish()

    return _comm_call(name, body, comm.ins, comm.out_shapes, comm.n_sems)


def gather_comm(shards):
    L = len(shards)

    def plan(ins, outs, sems):
        send_sems, recv_sems, local_sems = sems
        x, y, c = _coords()
        sibling = (x, y, 1 - c)
        chips = [(_flip(x, m & 2), _flip(y, m & 1)) for m in (1, 2, 3)]

        def slab(px, py, pc):
            return 4 * px + 2 * py + pc

        def copy(l, k, block, to, src=None):
            dst = outs[l].at[slab(*block)]
            return pltpu.make_async_remote_copy(
                src_ref=dst if src is None else src, dst_ref=dst, send_sem=send_sems.at[7 * l + k],
                recv_sem=recv_sems.at[7 * l + k], device_id=to, device_id_type=MESH)

        local = [pltpu.make_async_copy(ins[l], outs[l].at[slab(x, y, c)], local_sems.at[l]) for l in range(L)]
        first = []
        for l in range(L):
            first.append(copy(l, 0, (x, y, c), sibling, src=ins[l]))
            first += [copy(l, 1 + j, (x, y, c), (*chip, c), src=ins[l]) for j, chip in enumerate(chips)]

        def start():
            for cp in local + first:
                cp.start()

        def finish():
            passed = []
            for j, chip in enumerate(chips):
                for l in range(L):
                    copy(l, 1 + j, (*chip, c), (x, y, c)).wait_recv()
                    fwd = copy(l, 4 + j, (*chip, c), sibling)
                    fwd.start()
                    passed.append(fwd)
            for l in range(L):
                copy(l, 0, sibling, (x, y, c)).wait_recv()
                for j, chip in enumerate(chips):
                    copy(l, 4 + j, (*chip, 1 - c), (x, y, c)).wait_recv()
            for cp in first + passed:
                cp.wait_send()
            for cp in local:
                cp.wait()

        return start, finish

    outs = [jax.ShapeDtypeStruct((NDEV,) + s.shape, s.dtype) for s in shards]
    return Comm(plan, shards, outs, (7 * L, 7 * L, L))


def _all_at_once(copies):
    def start():
        for cp in copies:
            cp.start()

    def finish():
        for cp in copies:
            cp.wait_recv()
        for cp in copies:
            cp.wait_send()

    return start, finish


def sibling_comm(grads):
    L = len(grads)

    def plan(ins, outs, sems):
        send_sems, recv_sems = sems
        x, y, c = _coords()
        copies = []
        for l in range(L):
            for m in range(4):
                qm = 2 * _flip(x, m & 2) + _flip(y, m & 1)
                copies.append(pltpu.make_async_remote_copy(
                    src_ref=ins[l].at[2 * qm + (1 - c)], dst_ref=outs[l].at[m], send_sem=send_sems.at[4 * l + m],
                    recv_sem=recv_sems.at[4 * l + m], device_id=(x, y, 1 - c), device_id_type=MESH))
        return _all_at_once(copies)

    outs = [jax.ShapeDtypeStruct((4,) + g.shape[1:], g.dtype) for g in grads]
    return Comm(plan, grads, outs, (4 * L, 4 * L))


def chip_comm(transits):
    L = len(transits)

    def plan(ins, outs, sems):
        send_sems, recv_sems = sems
        x, y, c = _coords()
        copies = []
        for l in range(L):
            for m in (1, 2, 3):
                copies.append(pltpu.make_async_remote_copy(
                    src_ref=ins[l].at[m - 1], dst_ref=outs[l].at[m - 1], send_sem=send_sems.at[3 * l + m - 1],
                    recv_sem=recv_sems.at[3 * l + m - 1], device_id=(_flip(x, m & 2), _flip(y, m & 1), c),
                    device_id_type=MESH))
        return _all_at_once(copies)

    outs = [jax.ShapeDtypeStruct(t.shape, t.dtype) for t in transits]
    return Comm(plan, transits, outs, (3 * L, 3 * L))


_SMALL_ROWS = ("b_ada", "norm1_g", "b_in", "conv_dw_b", "conv_ln_g", "conv_ln_b", "sgu_ln_g", "sgu_ln_b", "b_spatial",
               "norm2_g", "ffn_dw_b", "final_g")
_BIG = ("w_in", "w_conv_out", "w_sgu_out", "w_out", "w_up", "w_down")
_ORDER = ("w_ada", "b_ada", "norm1_g", "w_in", "b_in", "conv_dw_w", "conv_dw_b", "conv_ln_g", "conv_ln_b", "w_conv_out",
          "sgu_ln_g", "sgu_ln_b", "w_spatial", "b_spatial", "w_sgu_out", "w_out", "norm2_g", "w_up", "ffn_dw_w", "ffn_dw_b",
          "w_down", "final_g")


def kernel(x, c, w_ada, b_ada, norm1_g, w_in, b_in, conv_dw_w, conv_dw_b, conv_ln_g, conv_ln_b, w_conv_out, sgu_ln_g, sgu_ln_b, w_spatial, b_spatial, w_sgu_out, w_out, norm2_g, w_up, ffn_dw_w, ffn_dw_b, w_down, final_g, loss_target, m_w_ada, m_b_ada, m_norm1_g, m_w_in, m_b_in, m_conv_dw_w, m_conv_dw_b, m_conv_ln_g, m_conv_ln_b, m_w_conv_out, m_sgu_ln_g, m_sgu_ln_b, m_w_spatial, m_b_spatial, m_w_sgu_out, m_w_out, m_norm2_g, m_w_up, m_ffn_dw_w, m_ffn_dw_b, m_w_down, m_final_g, v_w_ada, v_b_ada, v_norm1_g, v_w_in, v_b_in, v_conv_dw_w, v_conv_dw_b, v_conv_ln_g, v_conv_ln_b, v_w_conv_out, v_sgu_ln_g, v_sgu_ln_b, v_w_spatial, v_b_spatial, v_w_sgu_out, v_w_out, v_norm2_g, v_w_up, v_ffn_dw_w, v_ffn_dw_b, v_w_down, v_final_g):
    W = dict(w_ada=w_ada, b_ada=b_ada, norm1_g=norm1_g, w_in=w_in, b_in=b_in, conv_dw_w=conv_dw_w, conv_dw_b=conv_dw_b,
             conv_ln_g=conv_ln_g, conv_ln_b=conv_ln_b, w_conv_out=w_conv_out, sgu_ln_g=sgu_ln_g, sgu_ln_b=sgu_ln_b,
             w_spatial=w_spatial, b_spatial=b_spatial, w_sgu_out=w_sgu_out, w_out=w_out, norm2_g=norm2_g, w_up=w_up,
             ffn_dw_w=ffn_dw_w, ffn_dw_b=ffn_dw_b, w_down=w_down, final_g=final_g)
    M = dict(w_ada=m_w_ada, b_ada=m_b_ada, norm1_g=m_norm1_g, w_in=m_w_in, b_in=m_b_in, conv_dw_w=m_conv_dw_w,
             conv_dw_b=m_conv_dw_b, conv_ln_g=m_conv_ln_g, conv_ln_b=m_conv_ln_b, w_conv_out=m_w_conv_out,
             sgu_ln_g=m_sgu_ln_g, sgu_ln_b=m_sgu_ln_b, w_spatial=m_w_spatial, b_spatial=m_b_spatial,
             w_sgu_out=m_w_sgu_out, w_out=m_w_out, norm2_g=m_norm2_g, w_up=m_w_up, ffn_dw_w=m_ffn_dw_w,
             ffn_dw_b=m_ffn_dw_b, w_down=m_w_down, final_g=m_final_g)
    V = dict(w_ada=v_w_ada, b_ada=v_b_ada, norm1_g=v_norm1_g, w_in=v_w_in, b_in=v_b_in, conv_dw_w=v_conv_dw_w,
             conv_dw_b=v_conv_dw_b, conv_ln_g=v_conv_ln_g, conv_ln_b=v_conv_ln_b, w_conv_out=v_w_conv_out,
             sgu_ln_g=v_sgu_ln_g, sgu_ln_b=v_sgu_ln_b, w_spatial=v_w_spatial, b_spatial=v_b_spatial,
             w_sgu_out=v_w_sgu_out, w_out=v_w_out, norm2_g=v_norm2_g, w_up=v_w_up, ffn_dw_w=v_ffn_dw_w,
             ffn_dw_b=v_ffn_dw_b, w_down=v_w_down, final_g=v_final_g)

    xs, tgt = x[0], loss_target[0]
    T, D = xs.shape
    Dc = conv_dw_w.shape[-1] * NDEV
    F = w_down.shape[1] * NDEV
    K31 = conv_dw_w.shape[1]
    G = w_spatial.shape[1]
    assert D == 2 * Dc and sgu_ln_g.shape[-1] == Dc and T % CHUNK == 0
    me = 4 * lax.axis_index("x") + 2 * lax.axis_index("y") + lax.axis_index("c")

    na = w_ada.shape[-1]
    c_all = gather_all("gather_c", [c])[0].reshape(NDEV, D)
    b_cols = lax.dynamic_slice(b_ada, (0, me * na), (1, na))
    mod_cols = ada_fwd_local(c_all, w_ada[0], b_cols)
    mod = exchange_rows("exchange_mod", mod_cols.reshape(NDEV, 1, na)).reshape(1, NDEV * na)

    wbf = {k: W[k][0].astype(BF16) for k in _BIG}
    fb = ffn_dw_b.reshape(2, 1, F)
    bsp_t = jnp.transpose(b_spatial[0])
    wsp = w_spatial[0]

    def plain(wb):
        return jnp.transpose(wb, (1, 0, 2)).reshape(1, wb.shape[1], NDEV * wb.shape[2])

    h1, (wb_in,) = pre_norm("pre_norm1", xs, mod, norm1_g, 0, comm=gather_comm([wbf["w_in"]]))
    proj, (wb_co, wb_so, wb_out, wb_up, cw_g, fw_g) = mm_nn(
        "proj", h1, wb_in, F32, bias=b_in,
        comm=gather_comm([wbf["w_conv_out"], wbf["w_sgu_out"], wbf["w_out"], wbf["w_up"], conv_dw_w[0], ffn_dw_w[0]]))
    cw = jnp.transpose(cw_g, (1, 0, 2)).reshape(K31, Dc)
    fw = jnp.transpose(jnp.transpose(fw_g, (1, 0, 2)).reshape(3, 2, F), (1, 0, 2))
    wb_out = wb_out.reshape(1, D, D)
    wp_co, wp_so = plain(wb_co), plain(wb_so)
    wp_in, wp_up = to_plain("plain_w_in", wb_in), to_plain("plain_w_up", wb_up)
    (ac, asw), (wb_down,) = mix_a_fwd(proj, cw, conv_dw_b, conv_ln_g, conv_ln_b, comm=gather_comm([wbf["w_down"]]))
    wb_down = wb_down.reshape(1, F, D)
    uv = mix_b_fwd(proj, sgu_ln_g, sgu_ln_b, wsp, bsp_t)
    y_a = mm_nn("y_a", asw, wp_co, ACT)
    y_b = mm_nn("y_b", uv, wp_so, ACT)
    merged = merge_fwd(proj, y_a, y_b)
    o1 = mm_nn("o1", merged, wb_out, F32)
    h2 = pre_norm("pre_norm2", xs, mod, norm2_g, 1, o1=o1)
    upre = mm_nn("upre", h2, wb_up, F32, out_halves=True, tn_pref=MM_WIDE)
    f, up = ffn_act_fwd(upre, fw, fb)
    o2 = mm_nn("o2", f, wb_down, F32)
    dx3, do2, st_f = final_fwd_bwd(xs, o1, o2, mod, final_g.reshape(1, D), tgt)
    loss = lax.psum(st_f[3, 0], MESH_AXES)

    xq, yq, cq = lax.axis_index("x"), lax.axis_index("y"), lax.axis_index("c")
    idx = jnp.stack([2 * (2 * _flip(xq, m & 2) + _flip(yq, m & 1)) + cq for m in range(4)]).astype(jnp.int32)
    own, transit, arrived = {}, {}, {}

    def add_pairs(keys, full, from_sibling):
        for k, g_full, r in zip(keys, full, from_sibling):
            own[k], transit[k] = pair_add("pair_add_" + k, g_full, r, idx)

    df = mm_nt("df", do2, wb_down, ACT, tko_pref=MM_WIDE)
    g_down = mm_tn("g_down", f, do2, 1, tko_pref=MM_WIDE).reshape(NDEV, F // NDEV, D)
    dupre, g_fw, g_fb = ffn_act_bwd(upre, up, df, fw)
    dh2 = mm_nt("dh2", dupre, wp_up, ACT, a_halves=True)
    g_up = mm_tn("g_up", h2, dupre, NDEV, g_halves=True, tn_pref=MM_WIDE)
    (dx2, do1, st_2), sib = norm2_bwd(dh2, xs, o1, dx3, mod, norm2_g, comm=sibling_comm([g_down, g_up]))
    add_pairs(("w_down", "w_up"), (g_down, g_up), sib)
    dmerged = mm_nt("dmerged", do1, wb_out, ACT)
    g_out = mm_tn("g_out", merged, do1, 1).reshape(NDEV, D // NDEV, D)
    dy_a, dy_b, dproj, db_g = merge_bwd(dmerged, proj, y_a, y_b)
    def blocked(g):
        return jnp.transpose(g.reshape(g.shape[1], NDEV, g.shape[2] // NDEV), (1, 0, 2))

    dasw = mm_nt("dasw", dy_a, wp_co, ACT)
    g_co = blocked(mm_tn("g_co", asw, dy_a, 1))
    duv = mm_nt("duv", dy_b, wp_so, ACT)
    g_so = blocked(mm_tn("g_so", uv, dy_b, 1))
    (dproj, st_a, g_cw, db_a), (arrived["w_down"], arrived["w_up"]) = mix_a_bwd(
        dasw, ac, proj, dproj, cw, conv_ln_g, conv_ln_b, comm=chip_comm([transit["w_down"], transit["w_up"]]))
    (dproj, st_b, g_wsp, g_bsp_t, db_s), sib = mix_b_bwd(
        duv, proj, dproj, sgu_ln_g, sgu_ln_b, wsp, bsp_t, comm=sibling_comm([g_out, g_co, g_so]))
    add_pairs(("w_out", "w_conv_out", "w_sgu_out"), (g_out, g_co, g_so), sib)
    g_in = mm_tn("g_in", h1, dproj, NDEV)
    add_pairs(("w_in",), (g_in,), _run_comm("sibling_w_in", sibling_comm([g_in])))
    late = ("w_out", "w_conv_out", "w_sgu_out", "w_in")
    dh1, got = mm_nt("dh1", dproj, wp_in, ACT, comm=chip_comm([transit[k] for k in late]))
    arrived.update(zip(late, got))
    grad_x, st_1 = norm1_bwd(dh1, xs, dx2, mod, norm1_g)

    dmod = jnp.concatenate([st_1[0], st_1[1], st_2[3], st_2[0], st_2[1], st_f[1]]).reshape(1, NDEV * na)
    dmod_all = exchange_rows("exchange_dmod", dmod.reshape(NDEV, 1, na)).reshape(NDEV, na)
    g_ada = ada_bwd_local(jnp.transpose(c_all), dmod_all)

    res = {}
    for k in _BIG:
        q = arrived[k]
        res[k] = adamw("adamw_" + k, W[k][0], M[k][0], V[k][0], [(own[k], None), (q, 0), (q, 1), (q, 2)])
    res["w_ada"] = adamw("adamw_w_ada", w_ada[0], m_w_ada[0], v_w_ada[0], [(g_ada, None)])

    g_row = dict(
        b_ada=dmod, norm1_g=st_1[2], b_in=jnp.concatenate([db_a, db_s, db_g], axis=1), conv_dw_b=st_a[2],
        conv_ln_g=st_a[0], conv_ln_b=st_a[1], sgu_ln_g=st_b[0], sgu_ln_b=st_b[1], b_spatial=jnp.transpose(g_bsp_t),
        norm2_g=st_2[2], ffn_dw_b=g_fb, final_g=st_f[0])
    packed = jnp.concatenate([g_row[k].reshape(1, -1) for k in _SMALL_ROWS], axis=1)
    g_rows, g_wsp_all, g_cw_all, g_fw_all = gather_all("gather_small", [packed, g_wsp.reshape(-1, CHUNK), g_cw, g_fw])
    rows = [tuple(S[k].reshape(1, -1) for S in (W, M, V)) for k in _SMALL_ROWS]
    wsp3 = tuple(S["w_spatial"].reshape(-1, CHUNK) for S in (W, M, V))
    cw3, fw3 = (tuple(S[k][0] for S in (W, M, V)) for k in ("conv_dw_w", "ffn_dw_w"))
    small = adamw_small(jnp.reshape(me, (1,)).astype(jnp.int32), rows, g_rows, wsp3, g_wsp_all, cw3, g_cw_all, fw3, g_fw_all)
    res.update(zip(_SMALL_ROWS + ("w_spatial", "conv_dw_w", "ffn_dw_w"), small))

    outs = [[], [], [], []]
    for k in _ORDER:
        for slot in range(4):
            outs[slot].append(res[k][slot].reshape(W[k].shape))
    return (loss, grad_x[None], *outs[0], *outs[1], *outs[2], *outs[3])
```

```python
import functools

import jax
import jax.numpy as jnp
from jax import lax
from jax.experimental import pallas as pl
from jax.experimental.pallas import tpu as pltpu

F32, BF16 = jnp.float32, jnp.bfloat16
ACT = BF16
NDEV = 8
MESH_AXES = ("x", "y", "c")
MESH = pl.DeviceIdType.MESH
EPS = 1e-6
CHUNK = 128
CONV_HALO = 32
FFN_HALO = 8
LANE, SUBLANE = 128, 8
ROW_TILE = 256
FFN_ROW_TILE = 512
STRIP = 32
STRIP_UNROLL = 2
MM_TILE = 1024
MM_WIDE = 1408
MM_DEEP = 2816
VMEM_LIMIT = 56 * 1024 * 1024
ADAM_LR, ADAM_B1, ADAM_B2, ADAM_EPS, ADAM_WD, ADAM_STEP = 0.001, 0.9, 0.999, 1e-08, 0.01, 10
SQRT_HALF = 0.7071067811865476
INV_SQRT_2PI = 0.3989422804014327


def _div(n, pref, mult=LANE):
    if n <= pref:
        return n
    for d in range(pref - pref % mult, 0, -mult):
        if n % d == 0:
            return d
    return n


def _cp(sem):
    return pltpu.CompilerParams(dimension_semantics=sem, vmem_limit_bytes=VMEM_LIMIT)


def _sig(v):
    return jax.nn.sigmoid(v)


def _gelu(v):
    return 0.5 * v * (1.0 + lax.erf(v * SQRT_HALF))


def _gelu_grad(v):
    return 0.5 * (1.0 + lax.erf(v * SQRT_HALF)) + v * (INV_SQRT_2PI * jnp.exp(-0.5 * v * v))


def _colsum(v):
    return jnp.sum(v, axis=0, keepdims=True)


def _rowmean(v):
    return jnp.mean(v, axis=-1, keepdims=True)


class Comm:
    def __init__(self, plan, ins, out_shapes, n_sems):
        self.plan, self.ins, self.out_shapes, self.n_sems = plan, list(ins), list(out_shapes), tuple(n_sems)


def _pcall(name, body, grid, in_specs, out_specs, out_shape, scratch=(), sem=None, aliases=None, comm=None):
    if comm is None:
        return pl.pallas_call(
            body, name=name, grid=grid, in_specs=in_specs, out_specs=out_specs, out_shape=out_shape,
            scratch_shapes=list(scratch), input_output_aliases=aliases or {},
            compiler_params=_cp(sem or ("arbitrary",) * len(grid)))
    single = not isinstance(out_shape, (list, tuple))
    own_specs, own_shapes = ([out_specs], [out_shape]) if single else (list(out_specs), list(out_shape))
    n_in, n_out, n_scr = len(in_specs), len(own_shapes), len(scratch)
    n_ci, n_co = len(comm.ins), len(comm.out_shapes)
    any_spec = pl.BlockSpec(memory_space=pl.ANY)

    def fused(*refs):
        ins, cins = refs[:n_in], refs[n_in:n_in + n_ci]
        outs = refs[n_in + n_ci:n_in + n_ci + n_out]
        couts = refs[n_in + n_ci + n_out:n_in + n_ci + n_out + n_co]
        scr = refs[n_in + n_ci + n_out + n_co:n_in + n_ci + n_out + n_co + n_scr]
        sems = refs[n_in + n_ci + n_out + n_co + n_scr:]
        first = functools.reduce(jnp.logical_and, [pl.program_id(d) == 0 for d in range(len(grid))])
        last = functools.reduce(jnp.logical_and, [pl.program_id(d) == grid[d] - 1 for d in range(len(grid))])

        @pl.when(first)
        def _():
            comm.plan(cins, couts, sems)[0]()

        body(*ins, *outs, *scr)

        @pl.when(last)
        def _():
            comm.plan(cins, couts, sems)[1]()

    call = pl.pallas_call(
        fused, name=name, grid=grid, in_specs=list(in_specs) + [any_spec] * n_ci,
        out_specs=own_specs + [any_spec] * n_co, out_shape=own_shapes + comm.out_shapes,
        scratch_shapes=list(scratch) + [pltpu.SemaphoreType.DMA((s,)) for s in comm.n_sems],
        input_output_aliases=aliases or {},
        compiler_params=pltpu.CompilerParams(dimension_semantics=("arbitrary",) * len(grid),
                                             vmem_limit_bytes=VMEM_LIMIT, has_side_effects=True))

    def run(*args):
        res = call(*args, *comm.ins)
        own = res[:n_out]
        return (own[0] if single else list(own)), list(res[n_out:])

    return run


def _matmul(name, a, b, *, grid, a_spec, b_spec, o_spec, out_shape, dims, acc_shape, bias=None, bias_spec=None, comm=None):
    nk = grid[2]

    def body(*refs):
        if bias is None:
            a_ref, b_ref, o_ref, *scr = refs
            bias_ref = None
        else:
            a_ref, b_ref, bias_ref, o_ref, *scr = refs
        part = lax.dot_general(a_ref[...], b_ref[...], (dims, ((), ())), preferred_element_type=F32)

        def finish(total):
            if bias_ref is not None:
                total = total + bias_ref[...]
            o_ref[...] = total.astype(o_ref.dtype)

        if nk == 1:
            finish(part)
        else:
            acc = scr[0]
            k = pl.program_id(2)

            @pl.when(k == 0)
            def _():
                acc[...] = part

            @pl.when(k > 0)
            def _():
                acc[...] += part

            @pl.when(k == nk - 1)
            def _():
                finish(acc[...])

    in_specs = [a_spec, b_spec] + ([bias_spec] if bias is not None else [])
    args = (a, b) + ((bias,) if bias is not None else ())
    return _pcall(name, body, grid, in_specs, o_spec, out_shape,
                  scratch=[pltpu.VMEM(acc_shape, F32)] if nk > 1 else [],
                  sem=("parallel", "parallel", "arbitrary"), comm=comm)(*args)


def mm_nn(name, a, wb, out_dtype, *, bias=None, out_halves=False, tn_pref=MM_TILE, comm=None):
    T, K = a.shape
    NB, _, Ns = wb.shape
    N = NB * Ns
    tm, tn, tk = _div(T, MM_TILE), _div(Ns, tn_pref), _div(K, MM_DEEP)
    npb, nj, nk = Ns // tn, N // tn, K // tk
    if out_halves:
        o_spec = pl.BlockSpec((None, tm, tn), lambda i, j, k: (j // (nj // 2), i, j % (nj // 2)))
        out_shape = jax.ShapeDtypeStruct((2, T, N // 2), out_dtype)
    else:
        o_spec = pl.BlockSpec((tm, tn), lambda i, j, k: (i, j))
        out_shape = jax.ShapeDtypeStruct((T, N), out_dtype)
    return _matmul(
        name, a, wb, grid=(T // tm, nj, nk),
        a_spec=pl.BlockSpec((tm, tk), lambda i, j, k: (i, k)),
        b_spec=pl.BlockSpec((None, tk, tn), lambda i, j, k: (j // npb, k, j % npb)),
        o_spec=o_spec, out_shape=out_shape, dims=((1,), (0,)), acc_shape=(tm, tn),
        bias=bias, bias_spec=pl.BlockSpec((1, tn), lambda i, j, k: (0, j)), comm=comm)


def to_plain(name, wb):
    NB, K, Ns = wb.shape
    tk = _div(K, MM_TILE, 2 * SUBLANE)

    def body(i_ref, o_ref):
        o_ref[...] = i_ref[...]

    return _pcall(name, body, (NB, K // tk), [pl.BlockSpec((None, tk, Ns), lambda b, k: (b, k, 0))],
                  pl.BlockSpec((None, tk, Ns), lambda b, k: (0, k, b)),
                  jax.ShapeDtypeStruct((1, K, NB * Ns), wb.dtype), sem=("parallel", "parallel"))(wb)


def mm_nt(name, a, wb, out_dtype, *, a_halves=False, tko_pref=MM_TILE, tc_pref=MM_DEEP, comm=None):
    NB, K, Ns = wb.shape
    T = a.shape[-2]
    span = Ns // 2 if a_halves and NB == 1 else Ns
    tm, tko, tc = _div(T, MM_TILE), _div(K, tko_pref), _div(span, tc_pref)
    cpb = Ns // tc
    nkk = NB * cpb
    if a_halves:
        a_spec = pl.BlockSpec((None, tm, tc), lambda i, j, k: (k // (nkk // 2), i, k % (nkk // 2)))
    else:
        a_spec = pl.BlockSpec((tm, tc), lambda i, j, k: (i, k))
    return _matmul(
        name, a, wb, grid=(T // tm, K // tko, nkk), a_spec=a_spec,
        b_spec=pl.BlockSpec((None, tko, tc), lambda i, j, k: (k // cpb, j, k % cpb)),
        o_spec=pl.BlockSpec((tm, tko), lambda i, j, k: (i, j)),
        out_shape=jax.ShapeDtypeStruct((T, K), out_dtype), dims=((1,), (1,)), acc_shape=(tm, tko), comm=comm)


def mm_tn(name, a, g, nb, *, g_halves=False, tko_pref=MM_TILE, tn_pref=MM_TILE, out_dtype=BF16, comm=None):
    T, K = a.shape
    N = g.shape[-1] * (2 if g_halves else 1)
    Ns = N // nb
    tt, tko, tn = _div(T, 2 * MM_TILE), _div(K, tko_pref), _div(Ns, tn_pref)
    npb, nj = Ns // tn, N // tn
    if g_halves:
        g_spec = pl.BlockSpec((None, tt, tn), lambda i, j, t: (j // (nj // 2), t, j % (nj // 2)))
    else:
        g_spec = pl.BlockSpec((tt, tn), lambda i, j, t: (t, j))
    return _matmul(
        name, a, g, grid=(K // tko, nj, T // tt),
        a_spec=pl.BlockSpec((tt, tko), lambda i, j, t: (t, i)), b_spec=g_spec,
        o_spec=pl.BlockSpec((None, tko, tn), lambda i, j, t: (j // npb, i, j % npb)),
        out_shape=jax.ShapeDtypeStruct((nb, K, Ns), out_dtype), dims=((0,), (0,)), acc_shape=(tko, tn), comm=comm)


def _row_call(name, body, grid, in_specs, out_specs, out_shape, scratch=(), sem=None, aliases=None, comm=None):
    return _pcall(name, body, grid, in_specs, out_specs, out_shape, scratch, sem, aliases, comm)


def _full(shape):
    nd = len(shape)
    return pl.BlockSpec(shape, lambda *idx: (0,) * nd)


def pre_norm(name, x, mod, g, which, o1=None, comm=None):
    T, D = x.shape
    tr = _div(T, ROW_TILE, SUBLANE)

    def body(*refs):
        if o1 is None:
            x_ref, mod_ref, g_ref, h_ref = refs
            xv = x_ref[...]
        else:
            x_ref, o1_ref, mod_ref, g_ref, h_ref = refs
            xv = x_ref[...] + mod_ref[:, 2 * D:3 * D] * o1_ref[...]
        shift = mod_ref[:, (3 * which) * D:(3 * which + 1) * D]
        scale = mod_ref[:, (3 * which + 1) * D:(3 * which + 2) * D]
        r = lax.rsqrt(_rowmean(xv * xv) + EPS)
        h_ref[...] = ((xv * r) * g_ref[...] * (1.0 + scale) + shift).astype(BF16)

    row = pl.BlockSpec((tr, D), lambda i: (i, 0))
    ins = [x] + ([o1] if o1 is not None else []) + [mod, g]
    specs = [row] * (1 if o1 is None else 2) + [_full(mod.shape), _full(g.shape)]
    return _row_call(name, body, (T // tr,), specs, row, jax.ShapeDtypeStruct((T, D), BF16), sem=("parallel",),
                     comm=comm)(*ins)


def _window_taps(win, taps):
    n = win.shape[0]
    for r in range(SUBLANE):
        group = [(i, tap, off) for i, (tap, off) in enumerate(taps) if off % SUBLANE == r]
        if not group:
            continue
        shifted = win if r == 0 else pltpu.roll(win, n - r, 0)
        for i, tap, off in group:
            assert 0 <= off and off + CONV_HALO <= n
            yield i, tap, shifted[off - r:off - r + CONV_HALO]


def mix_a_fwd(proj, cw, cb, lg, lb, comm=None):
    T = proj.shape[0]
    K, Dc = cw.shape
    tr = _div(T, ROW_TILE, CONV_HALO)
    hb = tr // CONV_HALO

    def body(val_ref, gate_ref, hval_ref, hgate_ref, cw_ref, cb_ref, lg_ref, lb_ref, ac_ref, asw_ref, buf):
        i = pl.program_id(0)
        hist = hval_ref[...] * _sig(hgate_ref[...])
        buf[0:CONV_HALO, :] = jnp.where(i > 0, hist, 0.0)
        buf[CONV_HALO:CONV_HALO + tr, :] = val_ref[...] * _sig(gate_ref[...])
        base = CONV_HALO - (K - 1)
        for c0 in range(0, Dc, LANE):
            lanes = slice(c0, c0 + LANE)

            def step(s, carry):
                r0 = pl.multiple_of(s * CONV_HALO, CONV_HALO)
                win = buf[pl.ds(r0, 2 * CONV_HALO), lanes]
                acc = jnp.zeros((CONV_HALO, LANE), F32)
                for _, k, piece in _window_taps(win, [(k, base + k) for k in range(K)]):
                    acc = acc + piece * cw_ref[k:k + 1, lanes]
                ac_ref[pl.ds(r0, CONV_HALO), lanes] = acc + cb_ref[:, lanes]
                return carry

            lax.fori_loop(0, tr // CONV_HALO, step, 0)
        ac = ac_ref[...]
        mu = _rowmean(ac)
        cen = ac - mu
        y = cen * lax.rsqrt(_rowmean(cen * cen) + EPS)
        aln = y * lg_ref[...] + lb_ref[...]
        asw_ref[...] = (aln * _sig(aln)).astype(BF16)

    def halo(col):
        return pl.BlockSpec((CONV_HALO, Dc), lambda i: (jnp.maximum(i * hb - 1, 0), col))

    row = pl.BlockSpec((tr, Dc), lambda i: (i, 0))
    return _row_call(
        "mix_a_fwd", body, (T // tr,),
        [row, pl.BlockSpec((tr, Dc), lambda i: (i, 1)), halo(0), halo(1),
         _full(cw.shape), _full(cb.shape), _full(lg.shape), _full(lb.shape)],
        [row, row], [jax.ShapeDtypeStruct((T, Dc), F32), jax.ShapeDtypeStruct((T, Dc), BF16)],
        scratch=[pltpu.VMEM((CONV_HALO + tr, Dc), F32)], sem=("parallel",), comm=comm)(proj, proj, proj, proj, cw, cb, lg, lb)


def _spatial_mask():
    t = lax.broadcasted_iota(jnp.int32, (CHUNK, CHUNK), 0)
    s = lax.broadcasted_iota(jnp.int32, (CHUNK, CHUNK), 1)
    return s <= t


def mix_b_fwd(proj, lg, lb, wsp, bsp_t):
    T = proj.shape[0]
    Ds = lg.shape[-1]
    G = wsp.shape[0]
    hd = Ds // G
    tr = _div(T, ROW_TILE, CHUNK)

    def body(u_ref, v_ref, lg_ref, lb_ref, w_ref, b_ref, uv_ref, vs):
        v = _gelu(v_ref[...])
        mu = _rowmean(v)
        cen = v - mu
        vln = (cen * lax.rsqrt(_rowmean(cen * cen) + EPS) * lg_ref[...] + lb_ref[...]).astype(BF16)
        mask = _spatial_mask()
        for g in range(G):
            wg = jnp.where(mask, w_ref[g], 0.0).astype(BF16)
            for n in range(tr // CHUNK):
                rows, cols = slice(n * CHUNK, (n + 1) * CHUNK), slice(g * hd, (g + 1) * hd)
                vs[rows, cols] = jnp.dot(wg, vln[rows, cols], preferred_element_type=F32) + b_ref[:, g:g + 1]
        uv_ref[...] = (_gelu(u_ref[...]) * vs[...]).astype(BF16)

    return _row_call(
        "mix_b_fwd", body, (T // tr,),
        [pl.BlockSpec((tr, Ds), lambda i: (i, 2)), pl.BlockSpec((tr, Ds), lambda i: (i, 3)),
         _full(lg.shape), _full(lb.shape), _full(wsp.shape), _full(bsp_t.shape)],
        pl.BlockSpec((tr, Ds), lambda i: (i, 0)), jax.ShapeDtypeStruct((T, Ds), BF16),
        scratch=[pltpu.VMEM((tr, Ds), F32)], sem=("parallel",))(proj, proj, lg, lb, wsp, bsp_t)


def merge_fwd(proj, y_a, y_b):
    T, D = y_a.shape
    tr = _div(T, ROW_TILE, SUBLANE)

    def body(g_ref, ya_ref, yb_ref, o_ref):
        o_ref[...] = (_sig(g_ref[:, 0:D]) * ya_ref[...].astype(F32)
                      + _sig(g_ref[:, D:2 * D]) * yb_ref[...].astype(F32)).astype(BF16)

    row = pl.BlockSpec((tr, D), lambda i: (i, 0))
    return _row_call("merge_fwd", body, (T // tr,), [pl.BlockSpec((tr, 2 * D), lambda i: (i, 1)), row, row], row,
                     jax.ShapeDtypeStruct((T, D), BF16), sem=("parallel",))(proj, y_a, y_b)


def _fold(v):
    acc = v[0:SUBLANE]
    for r in range(SUBLANE, v.shape[0], SUBLANE):
        acc = acc + v[r:r + SUBLANE]
    return acc


def _conv3(prev, cur, w):
    win = jnp.concatenate([prev, cur], axis=0)
    n = win.shape[0]
    x1 = pltpu.roll(win, 1, 0)[FFN_HALO:n]
    x2 = pltpu.roll(win, 2, 0)[FFN_HALO:n]
    return x2 * w[0] + x1 * w[1] + cur * w[2], (x2, x1, cur)


def _strip_taps(w_ref, b_ref, lanes):
    w = [[w_ref[h, k:k + 1, lanes] for k in range(3)] for h in range(2)]
    b = [b_ref[h, :, lanes] for h in range(2)]
    return w, b


def ffn_act_fwd(upre, fw, fb, comm=None):
    _, T, F = upre.shape
    tr = _div(T, FFN_ROW_TILE, STRIP)
    cb = _div(F, MM_WIDE)
    hb = tr // FFN_HALO
    ns = tr // STRIP

    def body(x_ref, h_ref, w_ref, b_ref, f_ref, u_ref):
        i = pl.program_id(0)
        for c0 in range(0, cb, LANE):
            lanes = slice(c0, c0 + LANE)
            w, b = _strip_taps(w_ref, b_ref, lanes)

            def strip(r0, prev):
                up = [_conv3(prev(h), x_ref[h, pl.ds(r0, STRIP), lanes], w[h])[0] + b[h] for h in range(2)]
                f_ref[pl.ds(r0, STRIP), lanes] = (up[1] * _sig(up[1]) * up[0]).astype(BF16)
                for h in range(2):
                    u_ref[h, pl.ds(r0, STRIP), lanes] = up[h]

            strip(0, lambda h: jnp.where(i > 0, h_ref[h, :, lanes], 0.0))

            def step(s, carry):
                r0 = pl.multiple_of(s * STRIP, STRIP)
                strip(r0, lambda h: x_ref[h, pl.ds(pl.multiple_of(r0 - FFN_HALO, FFN_HALO), FFN_HALO), lanes])
                return carry

            lax.fori_loop(1, ns, step, 0, unroll=STRIP_UNROLL)

    return _row_call(
        "ffn_act_fwd", body, (T // tr, F // cb),
        [pl.BlockSpec((2, tr, cb), lambda i, j: (0, i, j)),
         pl.BlockSpec((2, FFN_HALO, cb), lambda i, j: (0, jnp.maximum(i * hb - 1, 0), j)),
         pl.BlockSpec((2, 3, cb), lambda i, j: (0, 0, j)), pl.BlockSpec((2, 1, cb), lambda i, j: (0, 0, j))],
        [pl.BlockSpec((tr, cb), lambda i, j: (i, j)), pl.BlockSpec((2, tr, cb), lambda i, j: (0, i, j))],
        [jax.ShapeDtypeStruct((T, F), BF16), jax.ShapeDtypeStruct((2, T, F), F32)],
        sem=("parallel", "parallel"), comm=comm)(upre, upre, fw, fb)


def final_fwd_bwd(x, o1, o2, mod, gf, target):
    T, D = x.shape
    tr = _div(T, ROW_TILE, SUBLANE)
    nt = T // tr

    def body(x_ref, o1_ref, o2_ref, mod_ref, gf_ref, t_ref, dx3_ref, do2_ref, st_ref):
        i = pl.program_id(0)
        gate1, gate2 = mod_ref[:, 2 * D:3 * D], mod_ref[:, 5 * D:6 * D]
        o2v = o2_ref[...]
        x3 = x_ref[...] + gate1 * o1_ref[...] + gate2 * o2v
        r = lax.rsqrt(_rowmean(x3 * x3) + EPS)
        xn = x3 * r
        err = xn * gf_ref[...] - t_ref[...]
        dy = err * (1.0 / D)
        dxn = dy * gf_ref[...]
        dx3 = r * (dxn - xn * _rowmean(dxn * xn))
        dx3_ref[...] = dx3
        do2_ref[...] = (dx3 * gate2).astype(BF16)

        @pl.when(i == 0)
        def _():
            st_ref[...] = jnp.zeros_like(st_ref)

        st_ref[0:1, :] += _colsum(dy * xn)
        st_ref[1:2, :] += _colsum(dx3 * o2v)
        st_ref[2:3, :] += _colsum(err * err) * (0.5 / D)

        @pl.when(i == nt - 1)
        def _():
            st_ref[3:4, :] = jnp.zeros((1, D), F32) + jnp.sum(st_ref[2:3, :])

    row = pl.BlockSpec((tr, D), lambda i: (i, 0))
    return _row_call(
        "final_fwd_bwd", body, (nt,), [row, row, row, _full(mod.shape), _full(gf.shape), row],
        [row, row, _full((8, D))],
        [jax.ShapeDtypeStruct((T, D), F32), jax.ShapeDtypeStruct((T, D), BF16), jax.ShapeDtypeStruct((8, D), F32)],
    )(x, o1, o2, mod, gf, target)


def ffn_act_bwd(upre, up, df, fw):
    _, T, F = upre.shape
    tr = _div(T, FFN_ROW_TILE, STRIP)
    cb = _div(F, MM_WIDE)
    nt = T // tr
    ns = tr // STRIP

    def body(x_ref, u_ref, df_ref, w_ref, dpre_ref, dw_ref, db_ref, carry):
        i = pl.program_id(1)

        @pl.when(i == 0)
        def _():
            carry[...] = jnp.zeros_like(carry)
            dw_ref[...] = jnp.zeros_like(dw_ref)
            db_ref[...] = jnp.zeros_like(db_ref)

        for c0 in range(0, cb, LANE):
            lanes = slice(c0, c0 + LANE)
            w = [[w_ref[h, k:k + 1, lanes] for k in range(3)] for h in range(2)]

            def step(s, state):
                later, db, dw = state
                r0 = pl.multiple_of((ns - 1 - s) * STRIP, STRIP)
                rows = pl.ds(r0, STRIP)
                val, gt = u_ref[0, rows, lanes], u_ref[1, rows, lanes]
                sg = _sig(gt)
                dfv = df_ref[rows, lanes].astype(F32)
                dup = (dfv * (gt * sg), dfv * val * (sg * (1.0 + gt * (1.0 - sg))))
                new_db, new_dw = [], []
                for h in range(2):
                    dwin = jnp.concatenate([dup[h], later[h]], axis=0)
                    n = dwin.shape[0]
                    d1 = pltpu.roll(dwin, n - 1, 0)[0:STRIP]
                    d2 = pltpu.roll(dwin, n - 2, 0)[0:STRIP]
                    xs = x_ref[h, rows, lanes]
                    new_db.append(db[h] + _fold(dup[h]))
                    new_dw.append((dw[h][0] + _fold(d2 * xs), dw[h][1] + _fold(d1 * xs), dw[h][2] + _fold(dup[h] * xs)))
                    dpre_ref[h, rows, lanes] = (dup[h] * w[h][2] + d1 * w[h][1] + d2 * w[h][0]).astype(BF16)
                return tuple(dup[h][0:FFN_HALO] for h in range(2)), tuple(new_db), tuple(new_dw)

            zero = jnp.zeros((SUBLANE, LANE), F32)
            state = ((carry[0, :, lanes], carry[1, :, lanes]), (zero, zero), ((zero,) * 3,) * 2)
            later, db, dw = lax.fori_loop(0, ns, step, state, unroll=STRIP_UNROLL)
            for h in range(2):
                carry[h, :, lanes] = later[h]
                db_ref[h, :, lanes] += _colsum(db[h])
                for k in range(3):
                    dw_ref[h, k:k + 1, lanes] += _colsum(dw[h][k])

    tile = pl.BlockSpec((2, tr, cb), lambda j, i: (0, nt - 1 - i, j))
    return _row_call(
        "ffn_act_bwd", body, (F // cb, nt),
        [tile, tile, pl.BlockSpec((tr, cb), lambda j, i: (nt - 1 - i, j)), pl.BlockSpec((2, 3, cb), lambda j, i: (0, 0, j))],
        [tile, pl.BlockSpec((2, 3, cb), lambda j, i: (0, 0, j)), pl.BlockSpec((2, 1, cb), lambda j, i: (0, 0, j))],
        [jax.ShapeDtypeStruct((2, T, F), BF16), jax.ShapeDtypeStruct((2, 3, F), F32), jax.ShapeDtypeStruct((2, 1, F), F32)],
        scratch=[pltpu.VMEM((2, FFN_HALO, cb), F32)],
        sem=("parallel", "arbitrary"))(upre, up, df, fw)


def norm2_bwd(dh2, x, o1, dx3, mod, g2, comm=None):
    T, D = x.shape
    tr = _div(T, ROW_TILE, SUBLANE)

    def body(dh_ref, x_ref, o1_ref, dx3_ref, mod_ref, g_ref, dx2_ref, do1_ref, st_ref):
        i = pl.program_id(0)
        gate1, scale = mod_ref[:, 2 * D:3 * D], mod_ref[:, 4 * D:5 * D]
        o1v = o1_ref[...]
        x2 = x_ref[...] + gate1 * o1v
        r = lax.rsqrt(_rowmean(x2 * x2) + EPS)
        xn = x2 * r
        dh = dh_ref[...].astype(F32)
        dxn = dh * (g_ref[...] * (1.0 + scale))
        dx2 = r * (dxn - xn * _rowmean(dxn * xn)) + dx3_ref[...]
        dx2_ref[...] = dx2
        do1_ref[...] = (dx2 * gate1).astype(BF16)

        @pl.when(i == 0)
        def _():
            st_ref[...] = jnp.zeros_like(st_ref)

        st_ref[0:1, :] += _colsum(dh)
        st_ref[1:2, :] += _colsum(dh * xn) * g_ref[...]
        st_ref[2:3, :] += _colsum(dh * xn) * (1.0 + scale)
        st_ref[3:4, :] += _colsum(dx2 * o1v)

    row = pl.BlockSpec((tr, D), lambda i: (i, 0))
    return _row_call(
        "norm2_bwd", body, (T // tr,), [row, row, row, row, _full(mod.shape), _full(g2.shape)],
        [row, row, _full((8, D))],
        [jax.ShapeDtypeStruct((T, D), F32), jax.ShapeDtypeStruct((T, D), BF16), jax.ShapeDtypeStruct((8, D), F32)],
        comm=comm)(dh2, x, o1, dx3, mod, g2)


def merge_bwd(dmerged, proj, y_a, y_b):
    T, D = y_a.shape
    tr = _div(T, ROW_TILE, SUBLANE)

    def body(dm_ref, g_ref, ya_ref, yb_ref, dya_ref, dyb_ref, dp_ref, db_ref):
        i = pl.program_id(0)
        dm = dm_ref[...].astype(F32)
        sa, sb = _sig(g_ref[:, 0:D]), _sig(g_ref[:, D:2 * D])
        dya_ref[...] = (dm * sa).astype(BF16)
        dyb_ref[...] = (dm * sb).astype(BF16)
        dga = dm * ya_ref[...].astype(F32) * (sa * (1.0 - sa))
        dgb = dm * yb_ref[...].astype(F32) * (sb * (1.0 - sb))
        dp_ref[:, 0:D] = dga.astype(BF16)
        dp_ref[:, D:2 * D] = dgb.astype(BF16)

        @pl.when(i == 0)
        def _():
            db_ref[...] = jnp.zeros_like(db_ref)

        db_ref[:, 0:D] += _colsum(dga)
        db_ref[:, D:2 * D] += _colsum(dgb)

    row = pl.BlockSpec((tr, D), lambda i: (i, 0))
    wide = pl.BlockSpec((tr, 2 * D), lambda i: (i, 1))
    return _row_call(
        "merge_bwd", body, (T // tr,), [row, wide, row, row], [row, row, wide, _full((1, 2 * D))],
        [jax.ShapeDtypeStruct((T, D), BF16), jax.ShapeDtypeStruct((T, D), BF16),
         jax.ShapeDtypeStruct((T, 4 * D), BF16), jax.ShapeDtypeStruct((1, 2 * D), F32)],
    )(dmerged, proj, y_a, y_b)


def mix_a_bwd(dasw, ac, proj, dproj, cw, lg, lb, comm=None):
    T, Dc = ac.shape
    K = cw.shape[0]
    tr = _div(T, ROW_TILE, CONV_HALO)
    nt = T // tr

    def body(dasw_ref, ac_ref, in_ref, dp_hbm, cw_ref, lg_ref, lb_ref,
             dp_ref, st_ref, dcw_ref, db_ref, abuf, dbuf, carry, da_buf):
        del dp_hbm
        i = pl.program_id(0)
        acv = ac_ref[...]
        mu = _rowmean(acv)
        cen = acv - mu
        rstd = lax.rsqrt(_rowmean(cen * cen) + EPS)
        y = cen * rstd
        aln = y * lg_ref[...] + lb_ref[...]
        sg = _sig(aln)
        daln = dasw_ref[...].astype(F32) * (sg * (1.0 + aln * (1.0 - sg)))
        dy = daln * lg_ref[...]
        dac = rstd * (dy - _rowmean(dy) - y * _rowmean(dy * y))

        @pl.when(i == 0)
        def _():
            carry[...] = jnp.zeros_like(carry)
            st_ref[...] = jnp.zeros_like(st_ref)
            dcw_ref[...] = jnp.zeros_like(dcw_ref)
            db_ref[...] = jnp.zeros_like(db_ref)

        st_ref[0:1, :] += _colsum(daln * y)
        st_ref[1:2, :] += _colsum(daln)
        st_ref[2:3, :] += _colsum(dac)
        val, gate = in_ref[:, 0:Dc], in_ref[:, Dc:2 * Dc]
        sgg = _sig(gate)
        abuf[...] = val * sgg
        dbuf[0:tr, :] = dac
        dbuf[tr:tr + CONV_HALO, :] = carry[...]
        carry[...] = dac[0:CONV_HALO, :]
        zero = jnp.zeros((SUBLANE, LANE), F32)
        for c0 in range(0, Dc, LANE):
            lanes = slice(c0, c0 + LANE)

            def step(s, dws):
                r0 = pl.multiple_of(s * CONV_HALO, CONV_HALO)
                dwin = dbuf[pl.ds(r0, 2 * CONV_HALO), lanes]
                dws = list(dws)
                a_piece = abuf[pl.ds(r0, CONV_HALO), lanes]
                acc = jnp.zeros((CONV_HALO, LANE), F32)
                for _, k, piece in _window_taps(dwin, [(k, K - 1 - k) for k in range(K)]):
                    acc = acc + piece * cw_ref[k:k + 1, lanes]
                    dws[k] = dws[k] + _fold(piece * a_piece)
                da_buf[pl.ds(r0, CONV_HALO), lanes] = acc
                return tuple(dws)

            dws = lax.fori_loop(0, tr // CONV_HALO, step, (zero,) * K)
            for k in range(K):
                dcw_ref[k:k + 1, lanes] += _colsum(dws[k])
        da = da_buf[...]
        dval = da * sgg
        dgate = da * val * (sgg * (1.0 - sgg))
        dp_ref[:, 0:Dc] = dval.astype(BF16)
        dp_ref[:, Dc:2 * Dc] = dgate.astype(BF16)
        db_ref[:, 0:Dc] += _colsum(dval)
        db_ref[:, Dc:2 * Dc] += _colsum(dgate)

    row = pl.BlockSpec((tr, Dc), lambda i: (nt - 1 - i, 0))
    wide = pl.BlockSpec((tr, 2 * Dc), lambda i: (nt - 1 - i, 0))
    return _row_call(
        "mix_a_bwd", body, (nt,),
        [row, row, wide, pl.BlockSpec(memory_space=pl.ANY), _full(cw.shape), _full(lg.shape), _full(lb.shape)],
        [wide, _full((8, Dc)), _full((CONV_HALO, Dc)), _full((1, 2 * Dc))],
        [jax.ShapeDtypeStruct(dproj.shape, BF16), jax.ShapeDtypeStruct((8, Dc), F32),
         jax.ShapeDtypeStruct((CONV_HALO, Dc), F32), jax.ShapeDtypeStruct((1, 2 * Dc), F32)],
        scratch=[pltpu.VMEM((tr, Dc), F32), pltpu.VMEM((tr + CONV_HALO, Dc), F32),
                 pltpu.VMEM((CONV_HALO, Dc), F32), pltpu.VMEM((tr, Dc), F32)],
        aliases={3: 0}, comm=comm)(dasw, ac, proj, dproj, cw, lg, lb)


def mix_b_bwd(duv, proj, dproj, lg, lb, wsp, bsp_t, comm=None):
    T, Ds = duv.shape
    G = wsp.shape[0]
    hd = Ds // G
    tr = _div(T, ROW_TILE, CHUNK)
    nt = T // tr

    def body(duv_ref, s_ref, dp_hbm, lg_ref, lb_ref, w_ref, b_ref,
             dp_ref, st_ref, dws_ref, dbs_ref, db_ref, vs, dvln):
        del dp_hbm
        i = pl.program_id(0)

        @pl.when(i == 0)
        def _():
            st_ref[...] = jnp.zeros_like(st_ref)
            dws_ref[...] = jnp.zeros_like(dws_ref)
            dbs_ref[...] = jnp.zeros_like(dbs_ref)
            db_ref[...] = jnp.zeros_like(db_ref)

        upre, vpre = s_ref[:, 0:Ds], s_ref[:, Ds:2 * Ds]
        u, v = _gelu(upre), _gelu(vpre)
        mu = _rowmean(v)
        cen = v - mu
        rstd = lax.rsqrt(_rowmean(cen * cen) + EPS)
        yv = cen * rstd
        vln = (yv * lg_ref[...] + lb_ref[...]).astype(BF16)
        duvv = duv_ref[...].astype(F32)
        dvs = duvv * u
        dvs_b = dvs.astype(BF16)
        mask = _spatial_mask()
        for g in range(G):
            wg = jnp.where(mask, w_ref[g], 0.0).astype(BF16)
            cols = slice(g * hd, (g + 1) * hd)
            dws = jnp.zeros((CHUNK, CHUNK), F32)
            dbs = jnp.zeros((CHUNK, 1), F32)
            for n in range(tr // CHUNK):
                rows = slice(n * CHUNK, (n + 1) * CHUNK)
                vs[rows, cols] = jnp.dot(wg, vln[rows, cols], preferred_element_type=F32) + b_ref[:, g:g + 1]
                dvln[rows, cols] = lax.dot_general(wg, dvs_b[rows, cols], (((0,), (0,)), ((), ())),
                                                   preferred_element_type=F32)
                dws = dws + lax.dot_general(dvs_b[rows, cols], vln[rows, cols], (((1,), (1,)), ((), ())),
                                            preferred_element_type=F32)
                dbs = dbs + jnp.sum(dvs[rows, cols], axis=1, keepdims=True)
            dws_ref[g] += jnp.where(mask, dws, 0.0)
            dbs_ref[:, g:g + 1] += dbs
        dvl = dvln[...]
        st_ref[0:1, :] += _colsum(dvl * yv)
        st_ref[1:2, :] += _colsum(dvl)
        dyv = dvl * lg_ref[...]
        dv = rstd * (dyv - _rowmean(dyv) - yv * _rowmean(dyv * yv))
        dupre = duvv * vs[...] * _gelu_grad(upre)
        dvpre = dv * _gelu_grad(vpre)
        dp_ref[:, 0:Ds] = dupre.astype(BF16)
        dp_ref[:, Ds:2 * Ds] = dvpre.astype(BF16)
        db_ref[:, 0:Ds] += _colsum(dupre)
        db_ref[:, Ds:2 * Ds] += _colsum(dvpre)

    wide = pl.BlockSpec((tr, 2 * Ds), lambda i: (i, 1))
    return _row_call(
        "mix_b_bwd", body, (nt,),
        [pl.BlockSpec((tr, Ds), lambda i: (i, 0)), wide, pl.BlockSpec(memory_space=pl.ANY),
         _full(lg.shape), _full(lb.shape), _full(wsp.shape), _full(bsp_t.shape)],
        [wide, _full((8, Ds)), _full(wsp.shape), _full(bsp_t.shape), _full((1, 2 * Ds))],
        [jax.ShapeDtypeStruct(dproj.shape, BF16), jax.ShapeDtypeStruct((8, Ds), F32),
         jax.ShapeDtypeStruct(wsp.shape, F32), jax.ShapeDtypeStruct(bsp_t.shape, F32),
         jax.ShapeDtypeStruct((1, 2 * Ds), F32)],
        scratch=[pltpu.VMEM((tr, Ds), F32), pltpu.VMEM((tr, Ds), F32)],
        aliases={2: 0}, comm=comm)(duv, proj, dproj, lg, lb, wsp, bsp_t)


def norm1_bwd(dh1, x, dx2, mod, g1):
    T, D = x.shape
    tr = _div(T, ROW_TILE, SUBLANE)

    def body(dh_ref, x_ref, dx2_ref, mod_ref, g_ref, gx_ref, st_ref):
        i = pl.program_id(0)
        scale = mod_ref[:, D:2 * D]
        xv = x_ref[...]
        r = lax.rsqrt(_rowmean(xv * xv) + EPS)
        xn = xv * r
        dh = dh_ref[...].astype(F32)
        dxn = dh * (g_ref[...] * (1.0 + scale))
        gx_ref[...] = r * (dxn - xn * _rowmean(dxn * xn)) + dx2_ref[...]

        @pl.when(i == 0)
        def _():
            st_ref[...] = jnp.zeros_like(st_ref)

        st_ref[0:1, :] += _colsum(dh)
        st_ref[1:2, :] += _colsum(dh * xn) * g_ref[...]
        st_ref[2:3, :] += _colsum(dh * xn) * (1.0 + scale)

    row = pl.BlockSpec((tr, D), lambda i: (i, 0))
    return _row_call(
        "norm1_bwd", body, (T // tr,), [row, row, row, _full(mod.shape), _full(g1.shape)], [row, _full((8, D))],
        [jax.ShapeDtypeStruct((T, D), F32), jax.ShapeDtypeStruct((8, D), F32)])(dh1, x, dx2, mod, g1)


def ada_fwd_local(c_all, w_ada, b_cols):
    B, D = c_all.shape
    Na = w_ada.shape[1]
    tn = _div(Na, 512)

    def body(c_ref, w_ref, b_ref, o_ref):
        cv = c_ref[...]
        act = (cv * _sig(cv)).astype(BF16)
        o_ref[...] = jnp.dot(act, w_ref[...].astype(BF16), preferred_element_type=F32) + b_ref[...]

    return _row_call(
        "ada_fwd_local", body, (Na // tn,),
        [_full(c_all.shape), pl.BlockSpec((D, tn), lambda j: (0, j)), pl.BlockSpec((1, tn), lambda j: (0, j))],
        pl.BlockSpec((B, tn), lambda j: (0, j)), jax.ShapeDtypeStruct((B, Na), F32), sem=("parallel",))(c_all, w_ada, b_cols)


def ada_bwd_local(c_all_t, dmod_all):
    D, B = c_all_t.shape
    Na = dmod_all.shape[1]
    tr = _div(D, 512, SUBLANE)

    def body(c_ref, d_ref, o_ref):
        cv = c_ref[...]
        act = cv * _sig(cv)
        acc = act[:, 0:1] * d_ref[0:1, :]
        for b in range(1, B):
            acc = acc + act[:, b:b + 1] * d_ref[b:b + 1, :]
        o_ref[...] = acc

    return _row_call(
        "ada_bwd_local", body, (D // tr,), [pl.BlockSpec((tr, B), lambda i: (i, 0)), _full(dmod_all.shape)],
        pl.BlockSpec((tr, Na), lambda i: (i, 0)), jax.ShapeDtypeStruct((D, Na), F32), sem=("parallel",))(c_all_t, dmod_all)


def _adam_update(w, m, v, g):
    mn = ADAM_B1 * m + (1.0 - ADAM_B1) * g
    vn = ADAM_B2 * v + (1.0 - ADAM_B2) * (g * g)
    bc1, bc2 = 1.0 - ADAM_B1 ** ADAM_STEP, 1.0 - ADAM_B2 ** ADAM_STEP
    return g, -ADAM_LR * ((mn / bc1) / (jnp.sqrt(vn / bc2) + ADAM_EPS) + ADAM_WD * w), mn, vn


def adamw(name, w, m, v, parts):
    R, C = w.shape
    tr = _div(R, max(SUBLANE, (1 << 18) // C // SUBLANE * SUBLANE), SUBLANE)
    n = len(parts)

    def body(*refs):
        w_ref, m_ref, v_ref = refs[:3]
        g_ref, d_ref, nm_ref, nv_ref = refs[3 + n:]
        g = refs[3][...].astype(F32)
        for p in refs[4:3 + n]:
            g = g + p[...].astype(F32)
        g_ref[...], d_ref[...], nm_ref[...], nv_ref[...] = _adam_update(w_ref[...], m_ref[...], v_ref[...], g)

    row = pl.BlockSpec((tr, C), lambda i: (i, 0))
    pspecs = [row if lead is None else pl.BlockSpec((None, tr, C), lambda i, lead=lead: (lead, i, 0)) for _, lead in parts]
    out = jax.ShapeDtypeStruct((R, C), F32)
    return _row_call(name, body, (R // tr,), [row, row, row] + pspecs, [row] * 4, [out] * 4, sem=("parallel",))(
        w, m, v, *[a for a, _ in parts])


def pair_add(name, g, r, idx):
    _, R, C = g.shape
    tr = _div(R, max(SUBLANE, (1 << 17) // C // SUBLANE * SUBLANE), SUBLANE)

    def body(idx_ref, g0, g1, g2, g3, r_ref, own_ref, tr_ref):
        del idx_ref
        own_ref[...] = g0[...].astype(F32) + r_ref[0].astype(F32)
        for m, gm in ((1, g1), (2, g2), (3, g3)):
            tr_ref[m - 1] = (gm[...].astype(F32) + r_ref[m].astype(F32)).astype(BF16)

    def gspec(m):
        return pl.BlockSpec((None, tr, C), lambda i, idx_ref: (idx_ref[m], i, 0))

    return pl.pallas_call(
        body, name=name,
        grid_spec=pltpu.PrefetchScalarGridSpec(
            num_scalar_prefetch=1, grid=(R // tr,),
            in_specs=[gspec(0), gspec(1), gspec(2), gspec(3), pl.BlockSpec((4, tr, C), lambda i, idx_ref: (0, i, 0))],
            out_specs=[pl.BlockSpec((tr, C), lambda i, idx_ref: (i, 0)),
                       pl.BlockSpec((3, tr, C), lambda i, idx_ref: (0, i, 0))]),
        out_shape=[jax.ShapeDtypeStruct((R, C), F32), jax.ShapeDtypeStruct((3, R, C), BF16)],
        compiler_params=_cp(("parallel",)))(idx, g, g, g, g, r)


def adamw_small(me, rows, g_rows, wsp, g_wsp, cwp, g_cw, fwp, g_fw):
    sizes = [w.shape[1] for w, _, _ in rows]
    offs = [sum(sizes[:i]) for i in range(len(rows))]
    k_cw, n_cw = cwp[0].shape
    n_fw = fwp[0].shape[1]
    per_half = g_fw.shape[-1] // n_fw
    groups = list(rows) + [wsp, cwp, fwp]
    n_p, n_r = len(groups), len(rows)

    def body(me_ref, g_rows_ref, g_wsp_ref, g_cw_ref, g_fw_ref, *refs):
        del me_ref
        wmv, outs = refs[:3 * n_p], refs[3 * n_p:]
        for p in range(n_p):
            if p < n_r:
                pick = lambda d, p=p: g_rows_ref[d, :, offs[p]:offs[p] + sizes[p]]
            elif p == n_r:
                pick = lambda d: g_wsp_ref[d]
            elif p == n_r + 1:
                pick = lambda d: g_cw_ref[d, 0:k_cw, :]
            else:
                pick = lambda d: g_fw_ref[d]
            g = pick(0)
            for d in range(1, NDEV):
                g = g + pick(d)
            res = _adam_update(wmv[3 * p][...], wmv[3 * p + 1][...], wmv[3 * p + 2][...], g)
            for slot in range(4):
                outs[4 * p + slot][...] = res[slot]

    def full(a):
        nd = len(a.shape)
        return pl.BlockSpec(a.shape, lambda i, me_ref: (0,) * nd)

    in_specs = [full(g_rows), full(g_wsp),
                pl.BlockSpec((NDEV, g_cw.shape[1], n_cw), lambda i, me_ref: (0, 0, me_ref[0])),
                pl.BlockSpec((NDEV, None, 3, n_fw), lambda i, me_ref: (0, me_ref[0] // per_half, 0, me_ref[0] % per_half))]
    in_specs += [full(a) for grp in groups for a in grp]
    out_shape = [jax.ShapeDtypeStruct(grp[0].shape, F32) for grp in groups for _ in range(4)]
    flat = pl.pallas_call(
        body, name="adamw_small",
        grid_spec=pltpu.PrefetchScalarGridSpec(num_scalar_prefetch=1, grid=(1,), in_specs=in_specs,
                                               out_specs=[full(s) for s in out_shape]),
        out_shape=out_shape, compiler_params=_cp(("arbitrary",)))(
            me, g_rows, g_wsp, g_cw, g_fw, *[a for grp in groups for a in grp])
    return [tuple(flat[4 * p:4 * p + 4]) for p in range(n_p)]


def _coords():
    return lax.axis_index("x"), lax.axis_index("y"), lax.axis_index("c")


def _flip(v, bit):
    return 1 - v if bit else v


def _comm_call(name, body, ins, out_shapes, n_sems):
    any_spec = pl.BlockSpec(memory_space=pl.ANY)
    return pl.pallas_call(
        body, name=name, in_specs=[any_spec] * len(ins), out_specs=[any_spec] * len(out_shapes), out_shape=out_shapes,
        scratch_shapes=[pltpu.SemaphoreType.DMA((s,)) for s in n_sems],
        compiler_params=pltpu.CompilerParams(has_side_effects=True))(*ins)


def gather_all(name, tensors):
    L = len(tensors)

    def body(*refs):
        ins, outs = refs[:L], refs[L:2 * L]
        send_sems, recv_sems, local_sems = refs[2 * L:]
        x, y, c = _coords()
        me = 4 * x + 2 * y + c
        local = [pltpu.make_async_copy(ins[l], outs[l].at[me], local_sems.at[l]) for l in range(L)]
        copies = []
        for l in range(L):
            for k in range(1, NDEV):
                peer = (_flip(x, k & 4), _flip(y, k & 2), _flip(c, k & 1))
                copies.append(pltpu.make_async_remote_copy(
                    src_ref=ins[l], dst_ref=outs[l].at[me], send_sem=send_sems.at[7 * l + k - 1],
                    recv_sem=recv_sems.at[7 * l + k - 1], device_id=peer, device_id_type=MESH))
        for cp in local + copies:
            cp.start()
        for cp in copies:
            cp.wait_recv()
        for cp in copies:
            cp.wait_send()
        for cp in local:
            cp.wait()

    outs = [jax.ShapeDtypeStruct((NDEV,) + t.shape, t.dtype) for t in tensors]
    return _comm_call(name, body, list(tensors), outs, (7 * L, 7 * L, L))


def exchange_rows(name, slabs):
    def body(in_ref, out_ref, send_sems, recv_sems, local_sem):
        x, y, c = _coords()
        me = 4 * x + 2 * y + c
        mine = pltpu.make_async_copy(in_ref.at[me], out_ref.at[me], local_sem.at[0])
        mine.start()
        copies = []
        for k in range(1, NDEV):
            px, py, pc = _flip(x, k & 4), _flip(y, k & 2), _flip(c, k & 1)
            copies.append(pltpu.make_async_remote_copy(
                src_ref=in_ref.at[4 * px + 2 * py + pc], dst_ref=out_ref.at[me], send_sem=send_sems.at[k - 1],
                recv_sem=recv_sems.at[k - 1], device_id=(px, py, pc), device_id_type=MESH))
        for cp in copies:
            cp.start()
        for cp in copies:
            cp.wait_recv()
        for cp in copies:
            cp.wait_send()
        mine.wait()

    return _comm_call(name, body, [slabs], [jax.ShapeDtypeStruct(slabs.shape, slabs.dtype)], (NDEV - 1, NDEV - 1, 1))[0]


def _run_comm(name, comm):
    n_i, n_o = len(comm.ins), len(comm.out_shapes)

    def body(*refs):
        start, finish = comm.plan(refs[:n_i], refs[n_i:n_i + n_o], refs[n_i + n_o:])
        start()
        finish()

    return _comm_call(name, body, comm.ins, comm.out_shapes, comm.n_sems)


def gather_comm(shards):
    L = len(shards)

    def plan(ins, outs, sems):
        send_sems, recv_sems, local_sems = sems
        x, y, c = _coords()
        sibling = (x, y, 1 - c)
        chips = [(_flip(x, m & 2), _flip(y, m & 1)) for m in (1, 2, 3)]

        def slab(px, py, pc):
            return 4 * px + 2 * py + pc

        def copy(l, k, block, to, src=None):
            dst = outs[l].at[slab(*block)]
            return pltpu.make_async_remote_copy(
                src_ref=dst if src is None else src, dst_ref=dst, send_sem=send_sems.at[7 * l + k],
                recv_sem=recv_sems.at[7 * l + k], device_id=to, device_id_type=MESH)

        local = [pltpu.make_async_copy(ins[l], outs[l].at[slab(x, y, c)], local_sems.at[l]) for l in range(L)]
        first = []
        for l in range(L):
            first.append(copy(l, 0, (x, y, c), sibling, src=ins[l]))
            first += [copy(l, 1 + j, (x, y, c), (*chip, c), src=ins[l]) for j, chip in enumerate(chips)]

        def start():
            for cp in local + first:
                cp.start()

        def finish():
            passed = []
            for j, chip in enumerate(chips):
                for l in range(L):
                    copy(l, 1 + j, (*chip, c), (x, y, c)).wait_recv()
                    fwd = copy(l, 4 + j, (*chip, c), sibling)
                    fwd.start()
                    passed.append(fwd)
            for l in range(L):
                copy(l, 0, sibling, (x, y, c)).wait_recv()
                for j, chip in enumerate(chips):
                    copy(l, 4 + j, (*chip, 1 - c), (x, y, c)).wait_recv()
            for cp in first + passed:
                cp.wait_send()
            for cp in local:
                cp.wait()

        return start, finish

    outs = [jax.ShapeDtypeStruct((NDEV,) + s.shape, s.dtype) for s in shards]
    return Comm(plan, shards, outs, (7 * L, 7 * L, L))


def _all_at_once(copies):
    def start():
        for cp in copies:
            cp.start()

    def finish():
        for cp in copies:
            cp.wait_recv()
        for cp in copies:
            cp.wait_send()

    return start, finish


def sibling_comm(grads):
    L = len(grads)

    def plan(ins, outs, sems):
        send_sems, recv_sems = sems
        x, y, c = _coords()
        copies = []
        for l in range(L):
            for m in range(4):
                qm = 2 * _flip(x, m & 2) + _flip(y, m & 1)
                copies.append(pltpu.make_async_remote_copy(
                    src_ref=ins[l].at[2 * qm + (1 - c)], dst_ref=outs[l].at[m], send_sem=send_sems.at[4 * l + m],
                    recv_sem=recv_sems.at[4 * l + m], device_id=(x, y, 1 - c), device_id_type=MESH))
        return _all_at_once(copies)

    outs = [jax.ShapeDtypeStruct((4,) + g.shape[1:], g.dtype) for g in grads]
    return Comm(plan, grads, outs, (4 * L, 4 * L))


def chip_comm(transits):
    L = len(transits)

    def plan(ins, outs, sems):
        send_sems, recv_sems = sems
        x, y, c = _coords()
        copies = []
        for l in range(L):
            for m in (1, 2, 3):
                copies.append(pltpu.make_async_remote_copy(
                    src_ref=ins[l].at[m - 1], dst_ref=outs[l].at[m - 1], send_sem=send_sems.at[3 * l + m - 1],
                    recv_sem=recv_sems.at[3 * l + m - 1], device_id=(_flip(x, m & 2), _flip(y, m & 1), c),
                    device_id_type=MESH))
        return _all_at_once(copies)

    outs = [jax.ShapeDtypeStruct(t.shape, t.dtype) for t in transits]
    return Comm(plan, transits, outs, (3 * L, 3 * L))


_SMALL_ROWS = ("b_ada", "norm1_g", "b_in", "conv_dw_b", "conv_ln_g", "conv_ln_b", "sgu_ln_g", "sgu_ln_b", "b_spatial",
               "norm2_g", "ffn_dw_b", "final_g")
_BIG = ("w_in", "w_conv_out", "w_sgu_out", "w_out", "w_up", "w_down")
_ORDER = ("w_ada", "b_ada", "norm1_g", "w_in", "b_in", "conv_dw_w", "conv_dw_b", "conv_ln_g", "conv_ln_b", "w_conv_out",
          "sgu_ln_g", "sgu_ln_b", "w_spatial", "b_spatial", "w_sgu_out", "w_out", "norm2_g", "w_up", "ffn_dw_w", "ffn_dw_b",
          "w_down", "final_g")


def kernel(x, c, w_ada, b_ada, norm1_g, w_in, b_in, conv_dw_w, conv_dw_b, conv_ln_g, conv_ln_b, w_conv_out, sgu_ln_g, sgu_ln_b, w_spatial, b_spatial, w_sgu_out, w_out, norm2_g, w_up, ffn_dw_w, ffn_dw_b, w_down, final_g, loss_target, m_w_ada, m_b_ada, m_norm1_g, m_w_in, m_b_in, m_conv_dw_w, m_conv_dw_b, m_conv_ln_g, m_conv_ln_b, m_w_conv_out, m_sgu_ln_g, m_sgu_ln_b, m_w_spatial, m_b_spatial, m_w_sgu_out, m_w_out, m_norm2_g, m_w_up, m_ffn_dw_w, m_ffn_dw_b, m_w_down, m_final_g, v_w_ada, v_b_ada, v_norm1_g, v_w_in, v_b_in, v_conv_dw_w, v_conv_dw_b, v_conv_ln_g, v_conv_ln_b, v_w_conv_out, v_sgu_ln_g, v_sgu_ln_b, v_w_spatial, v_b_spatial, v_w_sgu_out, v_w_out, v_norm2_g, v_w_up, v_ffn_dw_w, v_ffn_dw_b, v_w_down, v_final_g):
    W = dict(w_ada=w_ada, b_ada=b_ada, norm1_g=norm1_g, w_in=w_in, b_in=b_in, conv_dw_w=conv_dw_w, conv_dw_b=conv_dw_b,
             conv_ln_g=conv_ln_g, conv_ln_b=conv_ln_b, w_conv_out=w_conv_out, sgu_ln_g=sgu_ln_g, sgu_ln_b=sgu_ln_b,
             w_spatial=w_spatial, b_spatial=b_spatial, w_sgu_out=w_sgu_out, w_out=w_out, norm2_g=norm2_g, w_up=w_up,
             ffn_dw_w=ffn_dw_w, ffn_dw_b=ffn_dw_b, w_down=w_down, final_g=final_g)
    M = dict(w_ada=m_w_ada, b_ada=m_b_ada, norm1_g=m_norm1_g, w_in=m_w_in, b_in=m_b_in, conv_dw_w=m_conv_dw_w,
             conv_dw_b=m_conv_dw_b, conv_ln_g=m_conv_ln_g, conv_ln_b=m_conv_ln_b, w_conv_out=m_w_conv_out,
             sgu_ln_g=m_sgu_ln_g, sgu_ln_b=m_sgu_ln_b, w_spatial=m_w_spatial, b_spatial=m_b_spatial,
             w_sgu_out=m_w_sgu_out, w_out=m_w_out, norm2_g=m_norm2_g, w_up=m_w_up, ffn_dw_w=m_ffn_dw_w,
             ffn_dw_b=m_ffn_dw_b, w_down=m_w_down, final_g=m_final_g)
    V = dict(w_ada=v_w_ada, b_ada=v_b_ada, norm1_g=v_norm1_g, w_in=v_w_in, b_in=v_b_in, conv_dw_w=v_conv_dw_w,
             conv_dw_b=v_conv_dw_b, conv_ln_g=v_conv_ln_g, conv_ln_b=v_conv_ln_b, w_conv_out=v_w_conv_out,
             sgu_ln_g=v_sgu_ln_g, sgu_ln_b=v_sgu_ln_b, w_spatial=v_w_spatial, b_spatial=v_b_spatial,
             w_sgu_out=v_w_sgu_out, w_out=v_w_out, norm2_g=v_norm2_g, w_up=v_w_up, ffn_dw_w=v_ffn_dw_w,
             ffn_dw_b=v_ffn_dw_b, w_down=v_w_down, final_g=v_final_g)

    xs, tgt = x[0], loss_target[0]
    T, D = xs.shape
    Dc = conv_dw_w.shape[-1] * NDEV
    F = w_down.shape[1] * NDEV
    K31 = conv_dw_w.shape[1]
    G = w_spatial.shape[1]
    assert D == 2 * Dc and sgu_ln_g.shape[-1] == Dc and T % CHUNK == 0
    me = 4 * lax.axis_index("x") + 2 * lax.axis_index("y") + lax.axis_index("c")

    na = w_ada.shape[-1]
    c_all = gather_all("gather_c", [c])[0].reshape(NDEV, D)
    b_cols = lax.dynamic_slice(b_ada, (0, me * na), (1, na))
    mod_cols = ada_fwd_local(c_all, w_ada[0], b_cols)
    mod = exchange_rows("exchange_mod", mod_cols.reshape(NDEV, 1, na)).reshape(1, NDEV * na)

    wbf = {k: W[k][0].astype(BF16) for k in _BIG}
    fb = ffn_dw_b.reshape(2, 1, F)
    bsp_t = jnp.transpose(b_spatial[0])
    wsp = w_spatial[0]

    def plain(wb):
        return jnp.transpose(wb, (1, 0, 2)).reshape(1, wb.shape[1], NDEV * wb.shape[2])

    h1, (wb_in,) = pre_norm("pre_norm1", xs, mod, norm1_g, 0, comm=gather_comm([wbf["w_in"]]))
    proj, (wb_co, wb_so, wb_up, cw_g, fw_g) = mm_nn(
        "proj", h1, wb_in, F32, bias=b_in,
        comm=gather_comm([wbf["w_conv_out"], wbf["w_sgu_out"], wbf["w_up"], conv_dw_w[0], ffn_dw_w[0]]))
    cw = jnp.transpose(cw_g, (1, 0, 2)).reshape(K31, Dc)
    fw = jnp.transpose(jnp.transpose(fw_g, (1, 0, 2)).reshape(3, 2, F), (1, 0, 2))
    wp_co, wp_so = plain(wb_co), plain(wb_so)
    wp_in, wp_up = to_plain("plain_w_in", wb_in), to_plain("plain_w_up", wb_up)
    (ac, asw), (wb_out,) = mix_a_fwd(proj, cw, conv_dw_b, conv_ln_g, conv_ln_b, comm=gather_comm([wbf["w_out"]]))
    wb_out = wb_out.reshape(1, D, D)
    uv = mix_b_fwd(proj, sgu_ln_g, sgu_ln_b, wsp, bsp_t)
    y_a = mm_nn("y_a", asw, wp_co, ACT)
    y_b = mm_nn("y_b", uv, wp_so, ACT)
    merged = merge_fwd(proj, y_a, y_b)
    o1 = mm_nn("o1", merged, wb_out, F32)
    h2 = pre_norm("pre_norm2", xs, mod, norm2_g, 1, o1=o1)
    upre = mm_nn("upre", h2, wb_up, F32, out_halves=True, tn_pref=MM_WIDE)
    (f, up), (wb_down,) = ffn_act_fwd(upre, fw, fb, comm=gather_comm([wbf["w_down"]]))
    wb_down = wb_down.reshape(1, F, D)
    o2 = mm_nn("o2", f, wb_down, F32)
    dx3, do2, st_f = final_fwd_bwd(xs, o1, o2, mod, final_g.reshape(1, D), tgt)
    loss = lax.psum(st_f[3, 0], MESH_AXES)

    xq, yq, cq = lax.axis_index("x"), lax.axis_index("y"), lax.axis_index("c")
    idx = jnp.stack([2 * (2 * _flip(xq, m & 2) + _flip(yq, m & 1)) + cq for m in range(4)]).astype(jnp.int32)
    own, transit, arrived = {}, {}, {}

    def add_pairs(keys, full, from_sibling):
        for k, g_full, r in zip(keys, full, from_sibling):
            own[k], transit[k] = pair_add("pair_add_" + k, g_full, r, idx)

    df = mm_nt("df", do2, wb_down, ACT, tko_pref=MM_WIDE)
    g_down = mm_tn("g_down", f, do2, 1, tko_pref=MM_WIDE).reshape(NDEV, F // NDEV, D)
    dupre, g_fw, g_fb = ffn_act_bwd(upre, up, df, fw)
    dh2 = mm_nt("dh2", dupre, wp_up, ACT, a_halves=True)
    g_up = mm_tn("g_up", h2, dupre, NDEV, g_halves=True, tn_pref=MM_WIDE)
    (dx2, do1, st_2), sib = norm2_bwd(dh2, xs, o1, dx3, mod, norm2_g, comm=sibling_comm([g_down, g_up]))
    add_pairs(("w_down", "w_up"), (g_down, g_up), sib)
    dmerged = mm_nt("dmerged", do1, wb_out, ACT)
    g_out = mm_tn("g_out", merged, do1, 1).reshape(NDEV, D // NDEV, D)
    dy_a, dy_b, dproj, db_g = merge_bwd(dmerged, proj, y_a, y_b)
    def blocked(g):
        return jnp.transpose(g.reshape(g.shape[1], NDEV, g.shape[2] // NDEV), (1, 0, 2))

    dasw = mm_nt("dasw", dy_a, wp_co, ACT)
    g_co = blocked(mm_tn("g_co", asw, dy_a, 1))
    duv = mm_nt("duv", dy_b, wp_so, ACT)
    g_so = blocked(mm_tn("g_so", uv, dy_b, 1))
    (dproj, st_a, g_cw, db_a), (arrived["w_up"],) = mix_a_bwd(
        dasw, ac, proj, dproj, cw, conv_ln_g, conv_ln_b, comm=chip_comm([transit["w_up"]]))
    (dproj, st_b, g_wsp, g_bsp_t, db_s), sib = mix_b_bwd(
        duv, proj, dproj, sgu_ln_g, sgu_ln_b, wsp, bsp_t, comm=sibling_comm([g_out, g_co, g_so]))
    add_pairs(("w_out", "w_conv_out", "w_sgu_out"), (g_out, g_co, g_so), sib)
    g_in, (arrived["w_down"],) = mm_tn("g_in", h1, dproj, NDEV, comm=chip_comm([transit["w_down"]]))
    add_pairs(("w_in",), (g_in,), _run_comm("sibling_w_in", sibling_comm([g_in])))
    late = ("w_out", "w_conv_out", "w_sgu_out", "w_in")
    dh1, got = mm_nt("dh1", dproj, wp_in, ACT, comm=chip_comm([transit[k] for k in late]))
    arrived.update(zip(late, got))
    grad_x, st_1 = norm1_bwd(dh1, xs, dx2, mod, norm1_g)

    dmod = jnp.concatenate([st_1[0], st_1[1], st_2[3], st_2[0], st_2[1], st_f[1]]).reshape(1, NDEV * na)
    dmod_all = exchange_rows("exchange_dmod", dmod.reshape(NDEV, 1, na)).reshape(NDEV, na)
    g_ada = ada_bwd_local(jnp.transpose(c_all), dmod_all)

    res = {}
    for k in _BIG:
        q = arrived[k]
        res[k] = adamw("adamw_" + k, W[k][0], M[k][0], V[k][0], [(own[k], None), (q, 0), (q, 1), (q, 2)])
    res["w_ada"] = adamw("adamw_w_ada", w_ada[0], m_w_ada[0], v_w_ada[0], [(g_ada, None)])

    g_row = dict(
        b_ada=dmod, norm1_g=st_1[2], b_in=jnp.concatenate([db_a, db_s, db_g], axis=1), conv_dw_b=st_a[2],
        conv_ln_g=st_a[0], conv_ln_b=st_a[1], sgu_ln_g=st_b[0], sgu_ln_b=st_b[1], b_spatial=jnp.transpose(g_bsp_t),
        norm2_g=st_2[2], ffn_dw_b=g_fb, final_g=st_f[0])
    packed = jnp.concatenate([g_row[k].reshape(1, -1) for k in _SMALL_ROWS], axis=1)
    g_rows, g_wsp_all, g_cw_all, g_fw_all = gather_all("gather_small", [packed, g_wsp.reshape(-1, CHUNK), g_cw, g_fw])
    rows = [tuple(S[k].reshape(1, -1) for S in (W, M, V)) for k in _SMALL_ROWS]
    wsp3 = tuple(S["w_spatial"].reshape(-1, CHUNK) for S in (W, M, V))
    cw3, fw3 = (tuple(S[k][0] for S in (W, M, V)) for k in ("conv_dw_w", "ffn_dw_w"))
    small = adamw_small(jnp.reshape(me, (1,)).astype(jnp.int32), rows, g_rows, wsp3, g_wsp_all, cw3, g_cw_all, fw3, g_fw_all)
    res.update(zip(_SMALL_ROWS + ("w_spatial", "conv_dw_w", "ffn_dw_w"), small))

    outs = [[], [], [], []]
    for k in _ORDER:
        for slot in range(4):
            outs[slot].append(res[k][slot].reshape(W[k].shape))
    return (loss, grad_x[None], *outs[0], *outs[1], *outs[2], *outs[3])
```

```python
import functools

import jax
import jax.numpy as jnp
from jax import lax
from jax.experimental import pallas as pl
from jax.experimental.pallas import tpu as pltpu

F32, BF16 = jnp.float32, jnp.bfloat16
ACT = BF16
NDEV = 8
MESH_AXES = ("x", "y", "c")
MESH = pl.DeviceIdType.MESH
EPS = 1e-6
CHUNK = 128
CONV_HALO = 32
FFN_HALO = 8
LANE, SUBLANE = 128, 8
ROW_TILE = 256
FFN_ROW_TILE = 512
STRIP = 32
STRIP_UNROLL = 2
MM_TILE = 1024
MM_WIDE = 1408
MM_DEEP = 2816
VMEM_LIMIT = 56 * 1024 * 1024
ADAM_LR, ADAM_B1, ADAM_B2, ADAM_EPS, ADAM_WD, ADAM_STEP = 0.001, 0.9, 0.999, 1e-08, 0.01, 10
SQRT_HALF = 0.7071067811865476
INV_SQRT_2PI = 0.3989422804014327


def _div(n, pref, mult=LANE):
    if n <= pref:
        return n
    for d in range(pref - pref % mult, 0, -mult):
        if n % d == 0:
            return d
    return n


def _cp(sem):
    return pltpu.CompilerParams(dimension_semantics=sem, vmem_limit_bytes=VMEM_LIMIT)


def _sig(v):
    return jax.nn.sigmoid(v)


def _gelu(v):
    return 0.5 * v * (1.0 + lax.erf(v * SQRT_HALF))


def _gelu_grad(v):
    return 0.5 * (1.0 + lax.erf(v * SQRT_HALF)) + v * (INV_SQRT_2PI * jnp.exp(-0.5 * v * v))


def _colsum(v):
    return jnp.sum(v, axis=0, keepdims=True)


def _rowmean(v):
    return jnp.mean(v, axis=-1, keepdims=True)


class Comm:
    def __init__(self, plan, ins, out_shapes, n_sems):
        self.plan, self.ins, self.out_shapes, self.n_sems = plan, list(ins), list(out_shapes), tuple(n_sems)


def _pcall(name, body, grid, in_specs, out_specs, out_shape, scratch=(), sem=None, aliases=None, comm=None):
    if comm is None:
        return pl.pallas_call(
            body, name=name, grid=grid, in_specs=in_specs, out_specs=out_specs, out_shape=out_shape,
            scratch_shapes=list(scratch), input_output_aliases=aliases or {},
            compiler_params=_cp(sem or ("arbitrary",) * len(grid)))
    single = not isinstance(out_shape, (list, tuple))
    own_specs, own_shapes = ([out_specs], [out_shape]) if single else (list(out_specs), list(out_shape))
    n_in, n_out, n_scr = len(in_specs), len(own_shapes), len(scratch)
    n_ci, n_co = len(comm.ins), len(comm.out_shapes)
    any_spec = pl.BlockSpec(memory_space=pl.ANY)

    def fused(*refs):
        ins, cins = refs[:n_in], refs[n_in:n_in + n_ci]
        outs = refs[n_in + n_ci:n_in + n_ci + n_out]
        couts = refs[n_in + n_ci + n_out:n_in + n_ci + n_out + n_co]
        scr = refs[n_in + n_ci + n_out + n_co:n_in + n_ci + n_out + n_co + n_scr]
        sems = refs[n_in + n_ci + n_out + n_co + n_scr:]
        first = functools.reduce(jnp.logical_and, [pl.program_id(d) == 0 for d in range(len(grid))])
        last = functools.reduce(jnp.logical_and, [pl.program_id(d) == grid[d] - 1 for d in range(len(grid))])

        @pl.when(first)
        def _():
            comm.plan(cins, couts, sems)[0]()

        body(*ins, *outs, *scr)

        @pl.when(last)
        def _():
            comm.plan(cins, couts, sems)[1]()

    call = pl.pallas_call(
        fused, name=name, grid=grid, in_specs=list(in_specs) + [any_spec] * n_ci,
        out_specs=own_specs + [any_spec] * n_co, out_shape=own_shapes + comm.out_shapes,
        scratch_shapes=list(scratch) + [pltpu.SemaphoreType.DMA((s,)) for s in comm.n_sems],
        input_output_aliases=aliases or {},
        compiler_params=pltpu.CompilerParams(dimension_semantics=("arbitrary",) * len(grid),
                                             vmem_limit_bytes=VMEM_LIMIT, has_side_effects=True))

    def run(*args):
        res = call(*args, *comm.ins)
        own = res[:n_out]
        return (own[0] if single else list(own)), list(res[n_out:])

    return run


def _matmul(name, a, b, *, grid, a_spec, b_spec, o_spec, out_shape, dims, acc_shape, bias=None, bias_spec=None, comm=None):
    nk = grid[2]

    def body(*refs):
        if bias is None:
            a_ref, b_ref, o_ref, *scr = refs
            bias_ref = None
        else:
            a_ref, b_ref, bias_ref, o_ref, *scr = refs
        part = lax.dot_general(a_ref[...], b_ref[...], (dims, ((), ())), preferred_element_type=F32)

        def finish(total):
            if bias_ref is not None:
                total = total + bias_ref[...]
            o_ref[...] = total.astype(o_ref.dtype)

        if nk == 1:
            finish(part)
        else:
            acc = scr[0]
            k = pl.program_id(2)

            @pl.when(k == 0)
            def _():
                acc[...] = part

            @pl.when(k > 0)
            def _():
                acc[...] += part

            @pl.when(k == nk - 1)
            def _():
                finish(acc[...])

    in_specs = [a_spec, b_spec] + ([bias_spec] if bias is not None else [])
    args = (a, b) + ((bias,) if bias is not None else ())
    return _pcall(name, body, grid, in_specs, o_spec, out_shape,
                  scratch=[pltpu.VMEM(acc_shape, F32)] if nk > 1 else [],
                  sem=("parallel", "parallel", "arbitrary"), comm=comm)(*args)


def mm_nn(name, a, wb, out_dtype, *, bias=None, out_halves=False, tn_pref=MM_TILE, comm=None):
    T, K = a.shape
    NB, _, Ns = wb.shape
    N = NB * Ns
    tm, tn, tk = _div(T, MM_TILE), _div(Ns, tn_pref), _div(K, MM_DEEP)
    npb, nj, nk = Ns // tn, N // tn, K // tk
    if out_halves:
        o_spec = pl.BlockSpec((None, tm, tn), lambda i, j, k: (j // (nj // 2), i, j % (nj // 2)))
        out_shape = jax.ShapeDtypeStruct((2, T, N // 2), out_dtype)
    else:
        o_spec = pl.BlockSpec((tm, tn), lambda i, j, k: (i, j))
        out_shape = jax.ShapeDtypeStruct((T, N), out_dtype)
    return _matmul(
        name, a, wb, grid=(T // tm, nj, nk),
        a_spec=pl.BlockSpec((tm, tk), lambda i, j, k: (i, k)),
        b_spec=pl.BlockSpec((None, tk, tn), lambda i, j, k: (j // npb, k, j % npb)),
        o_spec=o_spec, out_shape=out_shape, dims=((1,), (0,)), acc_shape=(tm, tn),
        bias=bias, bias_spec=pl.BlockSpec((1, tn), lambda i, j, k: (0, j)), comm=comm)


def to_plain(name, wb):
    NB, K, Ns = wb.shape
    tk = _div(K, MM_TILE, 2 * SUBLANE)

    def body(i_ref, o_ref):
        o_ref[...] = i_ref[...]

    return _pcall(name, body, (NB, K // tk), [pl.BlockSpec((None, tk, Ns), lambda b, k: (b, k, 0))],
                  pl.BlockSpec((None, tk, Ns), lambda b, k: (0, k, b)),
                  jax.ShapeDtypeStruct((1, K, NB * Ns), wb.dtype), sem=("parallel", "parallel"))(wb)


def mm_nt(name, a, wb, out_dtype, *, a_halves=False, tko_pref=MM_TILE, tc_pref=MM_DEEP, comm=None):
    NB, K, Ns = wb.shape
    T = a.shape[-2]
    span = Ns // 2 if a_halves and NB == 1 else Ns
    tm, tko, tc = _div(T, MM_TILE), _div(K, tko_pref), _div(span, tc_pref)
    cpb = Ns // tc
    nkk = NB * cpb
    if a_halves:
        a_spec = pl.BlockSpec((None, tm, tc), lambda i, j, k: (k // (nkk // 2), i, k % (nkk // 2)))
    else:
        a_spec = pl.BlockSpec((tm, tc), lambda i, j, k: (i, k))
    return _matmul(
        name, a, wb, grid=(T // tm, K // tko, nkk), a_spec=a_spec,
        b_spec=pl.BlockSpec((None, tko, tc), lambda i, j, k: (k // cpb, j, k % cpb)),
        o_spec=pl.BlockSpec((tm, tko), lambda i, j, k: (i, j)),
        out_shape=jax.ShapeDtypeStruct((T, K), out_dtype), dims=((1,), (1,)), acc_shape=(tm, tko), comm=comm)


def mm_tn(name, a, g, nb, *, g_halves=False, tko_pref=MM_TILE, tn_pref=MM_TILE, out_dtype=BF16, comm=None):
    T, K = a.shape
    N = g.shape[-1] * (2 if g_halves else 1)
    Ns = N // nb
    tt, tko, tn = _div(T, 2 * MM_TILE), _div(K, tko_pref), _div(Ns, tn_pref)
    npb, nj = Ns // tn, N // tn
    if g_halves:
        g_spec = pl.BlockSpec((None, tt, tn), lambda i, j, t: (j // (nj // 2), t, j % (nj // 2)))
    else:
        g_spec = pl.BlockSpec((tt, tn), lambda i, j, t: (t, j))
    return _matmul(
        name, a, g, grid=(K // tko, nj, T // tt),
        a_spec=pl.BlockSpec((tt, tko), lambda i, j, t: (t, i)), b_spec=g_spec,
        o_spec=pl.BlockSpec((None, tko, tn), lambda i, j, t: (j // npb, i, j % npb)),
        out_shape=jax.ShapeDtypeStruct((nb, K, Ns), out_dtype), dims=((0,), (0,)), acc_shape=(tko, tn), comm=comm)


def _row_call(name, body, grid, in_specs, out_specs, out_shape, scratch=(), sem=None, aliases=None, comm=None):
    return _pcall(name, body, grid, in_specs, out_specs, out_shape, scratch, sem, aliases, comm)


def _full(shape):
    nd = len(shape)
    return pl.BlockSpec(shape, lambda *idx: (0,) * nd)


def pre_norm(name, x, mod, g, which, o1=None, comm=None):
    T, D = x.shape
    tr = _div(T, ROW_TILE, SUBLANE)

    def body(*refs):
        if o1 is None:
            x_ref, mod_ref, g_ref, h_ref = refs
            xv = x_ref[...]
        else:
            x_ref, o1_ref, mod_ref, g_ref, h_ref = refs
            xv = x_ref[...] + mod_ref[:, 2 * D:3 * D] * o1_ref[...]
        shift = mod_ref[:, (3 * which) * D:(3 * which + 1) * D]
        scale = mod_ref[:, (3 * which + 1) * D:(3 * which + 2) * D]
        r = lax.rsqrt(_rowmean(xv * xv) + EPS)
        h_ref[...] = ((xv * r) * g_ref[...] * (1.0 + scale) + shift).astype(BF16)

    row = pl.BlockSpec((tr, D), lambda i: (i, 0))
    ins = [x] + ([o1] if o1 is not None else []) + [mod, g]
    specs = [row] * (1 if o1 is None else 2) + [_full(mod.shape), _full(g.shape)]
    return _row_call(name, body, (T // tr,), specs, row, jax.ShapeDtypeStruct((T, D), BF16), sem=("parallel",),
                     comm=comm)(*ins)


def _window_taps(win, taps):
    n = win.shape[0]
    for r in range(SUBLANE):
        group = [(i, tap, off) for i, (tap, off) in enumerate(taps) if off % SUBLANE == r]
        if not group:
            continue
        shifted = win if r == 0 else pltpu.roll(win, n - r, 0)
        for i, tap, off in group:
            assert 0 <= off and off + CONV_HALO <= n
            yield i, tap, shifted[off - r:off - r + CONV_HALO]


def mix_a_fwd(proj, cw, cb, lg, lb, comm=None):
    T = proj.shape[0]
    K, Dc = cw.shape
    tr = _div(T, ROW_TILE, CONV_HALO)
    hb = tr // CONV_HALO

    def body(val_ref, gate_ref, hval_ref, hgate_ref, cw_ref, cb_ref, lg_ref, lb_ref, ac_ref, asw_ref, buf):
        i = pl.program_id(0)
        hist = hval_ref[...] * _sig(hgate_ref[...])
        buf[0:CONV_HALO, :] = jnp.where(i > 0, hist, 0.0)
        buf[CONV_HALO:CONV_HALO + tr, :] = val_ref[...] * _sig(gate_ref[...])
        base = CONV_HALO - (K - 1)
        for c0 in range(0, Dc, LANE):
            lanes = slice(c0, c0 + LANE)

            def step(s, carry):
                r0 = pl.multiple_of(s * CONV_HALO, CONV_HALO)
                win = buf[pl.ds(r0, 2 * CONV_HALO), lanes]
                acc = jnp.zeros((CONV_HALO, LANE), F32)
                for _, k, piece in _window_taps(win, [(k, base + k) for k in range(K)]):
                    acc = acc + piece * cw_ref[k:k + 1, lanes]
                ac_ref[pl.ds(r0, CONV_HALO), lanes] = acc + cb_ref[:, lanes]
                return carry

            lax.fori_loop(0, tr // CONV_HALO, step, 0)
        ac = ac_ref[...]
        mu = _rowmean(ac)
        cen = ac - mu
        y = cen * lax.rsqrt(_rowmean(cen * cen) + EPS)
        aln = y * lg_ref[...] + lb_ref[...]
        asw_ref[...] = (aln * _sig(aln)).astype(BF16)

    def halo(col):
        return pl.BlockSpec((CONV_HALO, Dc), lambda i: (jnp.maximum(i * hb - 1, 0), col))

    row = pl.BlockSpec((tr, Dc), lambda i: (i, 0))
    return _row_call(
        "mix_a_fwd", body, (T // tr,),
        [row, pl.BlockSpec((tr, Dc), lambda i: (i, 1)), halo(0), halo(1),
         _full(cw.shape), _full(cb.shape), _full(lg.shape), _full(lb.shape)],
        [row, row], [jax.ShapeDtypeStruct((T, Dc), F32), jax.ShapeDtypeStruct((T, Dc), BF16)],
        scratch=[pltpu.VMEM((CONV_HALO + tr, Dc), F32)], sem=("parallel",), comm=comm)(proj, proj, proj, proj, cw, cb, lg, lb)


def _spatial_mask():
    t = lax.broadcasted_iota(jnp.int32, (CHUNK, CHUNK), 0)
    s = lax.broadcasted_iota(jnp.int32, (CHUNK, CHUNK), 1)
    return s <= t


def mix_b_fwd(proj, lg, lb, wsp, bsp_t):
    T = proj.shape[0]
    Ds = lg.shape[-1]
    G = wsp.shape[0]
    hd = Ds // G
    tr = _div(T, ROW_TILE, CHUNK)

    def body(u_ref, v_ref, lg_ref, lb_ref, w_ref, b_ref, uv_ref, vs):
        v = _gelu(v_ref[...])
        mu = _rowmean(v)
        cen = v - mu
        vln = (cen * lax.rsqrt(_rowmean(cen * cen) + EPS) * lg_ref[...] + lb_ref[...]).astype(BF16)
        mask = _spatial_mask()
        for g in range(G):
            wg = jnp.where(mask, w_ref[g], 0.0).astype(BF16)
            for n in range(tr // CHUNK):
                rows, cols = slice(n * CHUNK, (n + 1) * CHUNK), slice(g * hd, (g + 1) * hd)
                vs[rows, cols] = jnp.dot(wg, vln[rows, cols], preferred_element_type=F32) + b_ref[:, g:g + 1]
        uv_ref[...] = (_gelu(u_ref[...]) * vs[...]).astype(BF16)

    return _row_call(
        "mix_b_fwd", body, (T // tr,),
        [pl.BlockSpec((tr, Ds), lambda i: (i, 2)), pl.BlockSpec((tr, Ds), lambda i: (i, 3)),
         _full(lg.shape), _full(lb.shape), _full(wsp.shape), _full(bsp_t.shape)],
        pl.BlockSpec((tr, Ds), lambda i: (i, 0)), jax.ShapeDtypeStruct((T, Ds), BF16),
        scratch=[pltpu.VMEM((tr, Ds), F32)], sem=("parallel",))(proj, proj, lg, lb, wsp, bsp_t)


def merge_fwd(proj, y_a, y_b):
    T, D = y_a.shape
    tr = _div(T, ROW_TILE, SUBLANE)

    def body(g_ref, ya_ref, yb_ref, o_ref):
        o_ref[...] = (_sig(g_ref[:, 0:D]) * ya_ref[...].astype(F32)
                      + _sig(g_ref[:, D:2 * D]) * yb_ref[...].astype(F32)).astype(BF16)

    row = pl.BlockSpec((tr, D), lambda i: (i, 0))
    return _row_call("merge_fwd", body, (T // tr,), [pl.BlockSpec((tr, 2 * D), lambda i: (i, 1)), row, row], row,
                     jax.ShapeDtypeStruct((T, D), BF16), sem=("parallel",))(proj, y_a, y_b)


def _fold(v):
    acc = v[0:SUBLANE]
    for r in range(SUBLANE, v.shape[0], SUBLANE):
        acc = acc + v[r:r + SUBLANE]
    return acc


def _conv3(prev, cur, w):
    win = jnp.concatenate([prev, cur], axis=0)
    n = win.shape[0]
    x1 = pltpu.roll(win, 1, 0)[FFN_HALO:n]
    x2 = pltpu.roll(win, 2, 0)[FFN_HALO:n]
    return x2 * w[0] + x1 * w[1] + cur * w[2], (x2, x1, cur)


def _strip_taps(w_ref, b_ref, lanes):
    w = [[w_ref[h, k:k + 1, lanes] for k in range(3)] for h in range(2)]
    b = [b_ref[h, :, lanes] for h in range(2)]
    return w, b


def ffn_act_fwd(upre, fw, fb, comm=None):
    _, T, F = upre.shape
    tr = _div(T, FFN_ROW_TILE, STRIP)
    cb = _div(F, MM_WIDE)
    hb = tr // FFN_HALO
    ns = tr // STRIP

    def body(x_ref, h_ref, w_ref, b_ref, f_ref, u_ref):
        i = pl.program_id(0)
        for c0 in range(0, cb, LANE):
            lanes = slice(c0, c0 + LANE)
            w, b = _strip_taps(w_ref, b_ref, lanes)

            def strip(r0, prev):
                up = [_conv3(prev(h), x_ref[h, pl.ds(r0, STRIP), lanes], w[h])[0] + b[h] for h in range(2)]
                f_ref[pl.ds(r0, STRIP), lanes] = (up[1] * _sig(up[1]) * up[0]).astype(BF16)
                for h in range(2):
                    u_ref[h, pl.ds(r0, STRIP), lanes] = up[h]

            strip(0, lambda h: jnp.where(i > 0, h_ref[h, :, lanes], 0.0))

            def step(s, carry):
                r0 = pl.multiple_of(s * STRIP, STRIP)
                strip(r0, lambda h: x_ref[h, pl.ds(pl.multiple_of(r0 - FFN_HALO, FFN_HALO), FFN_HALO), lanes])
                return carry

            lax.fori_loop(1, ns, step, 0, unroll=STRIP_UNROLL)

    return _row_call(
        "ffn_act_fwd", body, (T // tr, F // cb),
        [pl.BlockSpec((2, tr, cb), lambda i, j: (0, i, j)),
         pl.BlockSpec((2, FFN_HALO, cb), lambda i, j: (0, jnp.maximum(i * hb - 1, 0), j)),
         pl.BlockSpec((2, 3, cb), lambda i, j: (0, 0, j)), pl.BlockSpec((2, 1, cb), lambda i, j: (0, 0, j))],
        [pl.BlockSpec((tr, cb), lambda i, j: (i, j)), pl.BlockSpec((2, tr, cb), lambda i, j: (0, i, j))],
        [jax.ShapeDtypeStruct((T, F), BF16), jax.ShapeDtypeStruct((2, T, F), F32)],
        sem=("parallel", "parallel"), comm=comm)(upre, upre, fw, fb)


def final_fwd_bwd(x, o1, o2, mod, gf, target):
    T, D = x.shape
    tr = _div(T, ROW_TILE, SUBLANE)
    nt = T // tr

    def body(x_ref, o1_ref, o2_ref, mod_ref, gf_ref, t_ref, dx3_ref, do2_ref, st_ref):
        i = pl.program_id(0)
        gate1, gate2 = mod_ref[:, 2 * D:3 * D], mod_ref[:, 5 * D:6 * D]
        o2v = o2_ref[...]
        x3 = x_ref[...] + gate1 * o1_ref[...] + gate2 * o2v
        r = lax.rsqrt(_rowmean(x3 * x3) + EPS)
        xn = x3 * r
        err = xn * gf_ref[...] - t_ref[...]
        dy = err * (1.0 / D)
        dxn = dy * gf_ref[...]
        dx3 = r * (dxn - xn * _rowmean(dxn * xn))
        dx3_ref[...] = dx3
        do2_ref[...] = (dx3 * gate2).astype(BF16)

        @pl.when(i == 0)
        def _():
            st_ref[...] = jnp.zeros_like(st_ref)

        st_ref[0:1, :] += _colsum(dy * xn)
        st_ref[1:2, :] += _colsum(dx3 * o2v)
        st_ref[2:3, :] += _colsum(err * err) * (0.5 / D)

        @pl.when(i == nt - 1)
        def _():
            st_ref[3:4, :] = jnp.zeros((1, D), F32) + jnp.sum(st_ref[2:3, :])

    row = pl.BlockSpec((tr, D), lambda i: (i, 0))
    return _row_call(
        "final_fwd_bwd", body, (nt,), [row, row, row, _full(mod.shape), _full(gf.shape), row],
        [row, row, _full((8, D))],
        [jax.ShapeDtypeStruct((T, D), F32), jax.ShapeDtypeStruct((T, D), BF16), jax.ShapeDtypeStruct((8, D), F32)],
    )(x, o1, o2, mod, gf, target)


def ffn_act_bwd(upre, up, df, fw):
    _, T, F = upre.shape
    tr = _div(T, FFN_ROW_TILE, STRIP)
    cb = _div(F, MM_WIDE)
    nt = T // tr
    ns = tr // STRIP

    def body(x_ref, u_ref, df_ref, w_ref, dpre_ref, dw_ref, db_ref, carry):
        i = pl.program_id(1)

        @pl.when(i == 0)
        def _():
            carry[...] = jnp.zeros_like(carry)
            dw_ref[...] = jnp.zeros_like(dw_ref)
            db_ref[...] = jnp.zeros_like(db_ref)

        for c0 in range(0, cb, LANE):
            lanes = slice(c0, c0 + LANE)
            w = [[w_ref[h, k:k + 1, lanes] for k in range(3)] for h in range(2)]

            def step(s, state):
                later, db, dw = state
                r0 = pl.multiple_of((ns - 1 - s) * STRIP, STRIP)
                rows = pl.ds(r0, STRIP)
                val, gt = u_ref[0, rows, lanes], u_ref[1, rows, lanes]
                sg = _sig(gt)
                dfv = df_ref[rows, lanes].astype(F32)
                dup = (dfv * (gt * sg), dfv * val * (sg * (1.0 + gt * (1.0 - sg))))
                new_db, new_dw = [], []
                for h in range(2):
                    dwin = jnp.concatenate([dup[h], later[h]], axis=0)
                    n = dwin.shape[0]
                    d1 = pltpu.roll(dwin, n - 1, 0)[0:STRIP]
                    d2 = pltpu.roll(dwin, n - 2, 0)[0:STRIP]
                    xs = x_ref[h, rows, lanes]
                    new_db.append(db[h] + _fold(dup[h]))
                    new_dw.append((dw[h][0] + _fold(d2 * xs), dw[h][1] + _fold(d1 * xs), dw[h][2] + _fold(dup[h] * xs)))
                    dpre_ref[h, rows, lanes] = (dup[h] * w[h][2] + d1 * w[h][1] + d2 * w[h][0]).astype(BF16)
                return tuple(dup[h][0:FFN_HALO] for h in range(2)), tuple(new_db), tuple(new_dw)

            zero = jnp.zeros((SUBLANE, LANE), F32)
            state = ((carry[0, :, lanes], carry[1, :, lanes]), (zero, zero), ((zero,) * 3,) * 2)
            later, db, dw = lax.fori_loop(0, ns, step, state, unroll=STRIP_UNROLL)
            for h in range(2):
                carry[h, :, lanes] = later[h]
                db_ref[h, :, lanes] += _colsum(db[h])
                for k in range(3):
                    dw_ref[h, k:k + 1, lanes] += _colsum(dw[h][k])

    tile = pl.BlockSpec((2, tr, cb), lambda j, i: (0, nt - 1 - i, j))
    return _row_call(
        "ffn_act_bwd", body, (F // cb, nt),
        [tile, tile, pl.BlockSpec((tr, cb), lambda j, i: (nt - 1 - i, j)), pl.BlockSpec((2, 3, cb), lambda j, i: (0, 0, j))],
        [tile, pl.BlockSpec((2, 3, cb), lambda j, i: (0, 0, j)), pl.BlockSpec((2, 1, cb), lambda j, i: (0, 0, j))],
        [jax.ShapeDtypeStruct((2, T, F), BF16), jax.ShapeDtypeStruct((2, 3, F), F32), jax.ShapeDtypeStruct((2, 1, F), F32)],
        scratch=[pltpu.VMEM((2, FFN_HALO, cb), F32)],
        sem=("parallel", "arbitrary"))(upre, up, df, fw)


def norm2_bwd(dh2, x, o1, dx3, mod, g2, comm=None):
    T, D = x.shape
    tr = _div(T, ROW_TILE, SUBLANE)

    def body(dh_ref, x_ref, o1_ref, dx3_ref, mod_ref, g_ref, dx2_ref, do1_ref, st_ref):
        i = pl.program_id(0)
        gate1, scale = mod_ref[:, 2 * D:3 * D], mod_ref[:, 4 * D:5 * D]
        o1v = o1_ref[...]
        x2 = x_ref[...] + gate1 * o1v
        r = lax.rsqrt(_rowmean(x2 * x2) + EPS)
        xn = x2 * r
        dh = dh_ref[...].astype(F32)
        dxn = dh * (g_ref[...] * (1.0 + scale))
        dx2 = r * (dxn - xn * _rowmean(dxn * xn)) + dx3_ref[...]
        dx2_ref[...] = dx2
        do1_ref[...] = (dx2 * gate1).astype(BF16)

        @pl.when(i == 0)
        def _():
            st_ref[...] = jnp.zeros_like(st_ref)

        st_ref[0:1, :] += _colsum(dh)
        st_ref[1:2, :] += _colsum(dh * xn) * g_ref[...]
        st_ref[2:3, :] += _colsum(dh * xn) * (1.0 + scale)
        st_ref[3:4, :] += _colsum(dx2 * o1v)

    row = pl.BlockSpec((tr, D), lambda i: (i, 0))
    return _row_call(
        "norm2_bwd", body, (T // tr,), [row, row, row, row, _full(mod.shape), _full(g2.shape)],
        [row, row, _full((8, D))],
        [jax.ShapeDtypeStruct((T, D), F32), jax.ShapeDtypeStruct((T, D), BF16), jax.ShapeDtypeStruct((8, D), F32)],
        comm=comm)(dh2, x, o1, dx3, mod, g2)


def merge_bwd(dmerged, proj, y_a, y_b):
    T, D = y_a.shape
    tr = _div(T, ROW_TILE, SUBLANE)

    def body(dm_ref, g_ref, ya_ref, yb_ref, dya_ref, dyb_ref, dp_ref, db_ref):
        i = pl.program_id(0)
        dm = dm_ref[...].astype(F32)
        sa, sb = _sig(g_ref[:, 0:D]), _sig(g_ref[:, D:2 * D])
        dya_ref[...] = (dm * sa).astype(BF16)
        dyb_ref[...] = (dm * sb).astype(BF16)
        dga = dm * ya_ref[...].astype(F32) * (sa * (1.0 - sa))
        dgb = dm * yb_ref[...].astype(F32) * (sb * (1.0 - sb))
        dp_ref[:, 0:D] = dga.astype(BF16)
        dp_ref[:, D:2 * D] = dgb.astype(BF16)

        @pl.when(i == 0)
        def _():
            db_ref[...] = jnp.zeros_like(db_ref)

        db_ref[:, 0:D] += _colsum(dga)
        db_ref[:, D:2 * D] += _colsum(dgb)

    row = pl.BlockSpec((tr, D), lambda i: (i, 0))
    wide = pl.BlockSpec((tr, 2 * D), lambda i: (i, 1))
    return _row_call(
        "merge_bwd", body, (T // tr,), [row, wide, row, row], [row, row, wide, _full((1, 2 * D))],
        [jax.ShapeDtypeStruct((T, D), BF16), jax.ShapeDtypeStruct((T, D), BF16),
         jax.ShapeDtypeStruct((T, 4 * D), BF16), jax.ShapeDtypeStruct((1, 2 * D), F32)],
    )(dmerged, proj, y_a, y_b)


def mix_a_bwd(dasw, ac, proj, dproj, cw, lg, lb, comm=None):
    T, Dc = ac.shape
    K = cw.shape[0]
    tr = _div(T, ROW_TILE, CONV_HALO)
    nt = T // tr

    def body(dasw_ref, ac_ref, in_ref, dp_hbm, cw_ref, lg_ref, lb_ref,
             dp_ref, st_ref, dcw_ref, db_ref, abuf, dbuf, carry, da_buf):
        del dp_hbm
        i = pl.program_id(0)
        acv = ac_ref[...]
        mu = _rowmean(acv)
        cen = acv - mu
        rstd = lax.rsqrt(_rowmean(cen * cen) + EPS)
        y = cen * rstd
        aln = y * lg_ref[...] + lb_ref[...]
        sg = _sig(aln)
        daln = dasw_ref[...].astype(F32) * (sg * (1.0 + aln * (1.0 - sg)))
        dy = daln * lg_ref[...]
        dac = rstd * (dy - _rowmean(dy) - y * _rowmean(dy * y))

        @pl.when(i == 0)
        def _():
            carry[...] = jnp.zeros_like(carry)
            st_ref[...] = jnp.zeros_like(st_ref)
            dcw_ref[...] = jnp.zeros_like(dcw_ref)
            db_ref[...] = jnp.zeros_like(db_ref)

        st_ref[0:1, :] += _colsum(daln * y)
        st_ref[1:2, :] += _colsum(daln)
        st_ref[2:3, :] += _colsum(dac)
        val, gate = in_ref[:, 0:Dc], in_ref[:, Dc:2 * Dc]
        sgg = _sig(gate)
        abuf[...] = val * sgg
        dbuf[0:tr, :] = dac
        dbuf[tr:tr + CONV_HALO, :] = carry[...]
        carry[...] = dac[0:CONV_HALO, :]
        zero = jnp.zeros((SUBLANE, LANE), F32)
        for c0 in range(0, Dc, LANE):
            lanes = slice(c0, c0 + LANE)

            def step(s, dws):
                r0 = pl.multiple_of(s * CONV_HALO, CONV_HALO)
                dwin = dbuf[pl.ds(r0, 2 * CONV_HALO), lanes]
                dws = list(dws)
                a_piece = abuf[pl.ds(r0, CONV_HALO), lanes]
                acc = jnp.zeros((CONV_HALO, LANE), F32)
                for _, k, piece in _window_taps(dwin, [(k, K - 1 - k) for k in range(K)]):
                    acc = acc + piece * cw_ref[k:k + 1, lanes]
                    dws[k] = dws[k] + _fold(piece * a_piece)
                da_buf[pl.ds(r0, CONV_HALO), lanes] = acc
                return tuple(dws)

            dws = lax.fori_loop(0, tr // CONV_HALO, step, (zero,) * K)
            for k in range(K):
                dcw_ref[k:k + 1, lanes] += _colsum(dws[k])
        da = da_buf[...]
        dval = da * sgg
        dgate = da * val * (sgg * (1.0 - sgg))
        dp_ref[:, 0:Dc] = dval.astype(BF16)
        dp_ref[:, Dc:2 * Dc] = dgate.astype(BF16)
        db_ref[:, 0:Dc] += _colsum(dval)
        db_ref[:, Dc:2 * Dc] += _colsum(dgate)

    row = pl.BlockSpec((tr, Dc), lambda i: (nt - 1 - i, 0))
    wide = pl.BlockSpec((tr, 2 * Dc), lambda i: (nt - 1 - i, 0))
    return _row_call(
        "mix_a_bwd", body, (nt,),
        [row, row, wide, pl.BlockSpec(memory_space=pl.ANY), _full(cw.shape), _full(lg.shape), _full(lb.shape)],
        [wide, _full((8, Dc)), _full((CONV_HALO, Dc)), _full((1, 2 * Dc))],
        [jax.ShapeDtypeStruct(dproj.shape, BF16), jax.ShapeDtypeStruct((8, Dc), F32),
         jax.ShapeDtypeStruct((CONV_HALO, Dc), F32), jax.ShapeDtypeStruct((1, 2 * Dc), F32)],
        scratch=[pltpu.VMEM((tr, Dc), F32), pltpu.VMEM((tr + CONV_HALO, Dc), F32),
                 pltpu.VMEM((CONV_HALO, Dc), F32), pltpu.VMEM((tr, Dc), F32)],
        aliases={3: 0}, comm=comm)(dasw, ac, proj, dproj, cw, lg, lb)


def mix_b_bwd(duv, proj, dproj, lg, lb, wsp, bsp_t, comm=None):
    T, Ds = duv.shape
    G = wsp.shape[0]
    hd = Ds // G
    tr = _div(T, ROW_TILE, CHUNK)
    nt = T // tr

    def body(duv_ref, s_ref, dp_hbm, lg_ref, lb_ref, w_ref, b_ref,
             dp_ref, st_ref, dws_ref, dbs_ref, db_ref, vs, dvln):
        del dp_hbm
        i = pl.program_id(0)

        @pl.when(i == 0)
        def _():
            st_ref[...] = jnp.zeros_like(st_ref)
            dws_ref[...] = jnp.zeros_like(dws_ref)
            dbs_ref[...] = jnp.zeros_like(dbs_ref)
            db_ref[...] = jnp.zeros_like(db_ref)

        upre, vpre = s_ref[:, 0:Ds], s_ref[:, Ds:2 * Ds]
        u, v = _gelu(upre), _gelu(vpre)
        mu = _rowmean(v)
        cen = v - mu
        rstd = lax.rsqrt(_rowmean(cen * cen) + EPS)
        yv = cen * rstd
        vln = (yv * lg_ref[...] + lb_ref[...]).astype(BF16)
        duvv = duv_ref[...].astype(F32)
        dvs = duvv * u
        dvs_b = dvs.astype(BF16)
        mask = _spatial_mask()
        for g in range(G):
            wg = jnp.where(mask, w_ref[g], 0.0).astype(BF16)
            cols = slice(g * hd, (g + 1) * hd)
            dws = jnp.zeros((CHUNK, CHUNK), F32)
            dbs = jnp.zeros((CHUNK, 1), F32)
            for n in range(tr // CHUNK):
                rows = slice(n * CHUNK, (n + 1) * CHUNK)
                vs[rows, cols] = jnp.dot(wg, vln[rows, cols], preferred_element_type=F32) + b_ref[:, g:g + 1]
                dvln[rows, cols] = lax.dot_general(wg, dvs_b[rows, cols], (((0,), (0,)), ((), ())),
                                                   preferred_element_type=F32)
                dws = dws + lax.dot_general(dvs_b[rows, cols], vln[rows, cols], (((1,), (1,)), ((), ())),
                                            preferred_element_type=F32)
                dbs = dbs + jnp.sum(dvs[rows, cols], axis=1, keepdims=True)
            dws_ref[g] += jnp.where(mask, dws, 0.0)
            dbs_ref[:, g:g + 1] += dbs
        dvl = dvln[...]
        st_ref[0:1, :] += _colsum(dvl * yv)
        st_ref[1:2, :] += _colsum(dvl)
        dyv = dvl * lg_ref[...]
        dv = rstd * (dyv - _rowmean(dyv) - yv * _rowmean(dyv * yv))
        dupre = duvv * vs[...] * _gelu_grad(upre)
        dvpre = dv * _gelu_grad(vpre)
        dp_ref[:, 0:Ds] = dupre.astype(BF16)
        dp_ref[:, Ds:2 * Ds] = dvpre.astype(BF16)
        db_ref[:, 0:Ds] += _colsum(dupre)
        db_ref[:, Ds:2 * Ds] += _colsum(dvpre)

    wide = pl.BlockSpec((tr, 2 * Ds), lambda i: (i, 1))
    return _row_call(
        "mix_b_bwd", body, (nt,),
        [pl.BlockSpec((tr, Ds), lambda i: (i, 0)), wide, pl.BlockSpec(memory_space=pl.ANY),
         _full(lg.shape), _full(lb.shape), _full(wsp.shape), _full(bsp_t.shape)],
        [wide, _full((8, Ds)), _full(wsp.shape), _full(bsp_t.shape), _full((1, 2 * Ds))],
        [jax.ShapeDtypeStruct(dproj.shape, BF16), jax.ShapeDtypeStruct((8, Ds), F32),
         jax.ShapeDtypeStruct(wsp.shape, F32), jax.ShapeDtypeStruct(bsp_t.shape, F32),
         jax.ShapeDtypeStruct((1, 2 * Ds), F32)],
        scratch=[pltpu.VMEM((tr, Ds), F32), pltpu.VMEM((tr, Ds), F32)],
        aliases={2: 0}, comm=comm)(duv, proj, dproj, lg, lb, wsp, bsp_t)


def norm1_bwd(dh1, x, dx2, mod, g1):
    T, D = x.shape
    tr = _div(T, ROW_TILE, SUBLANE)

    def body(dh_ref, x_ref, dx2_ref, mod_ref, g_ref, gx_ref, st_ref):
        i = pl.program_id(0)
        scale = mod_ref[:, D:2 * D]
        xv = x_ref[...]
        r = lax.rsqrt(_rowmean(xv * xv) + EPS)
        xn = xv * r
        dh = dh_ref[...].astype(F32)
        dxn = dh * (g_ref[...] * (1.0 + scale))
        gx_ref[...] = r * (dxn - xn * _rowmean(dxn * xn)) + dx2_ref[...]

        @pl.when(i == 0)
        def _():
            st_ref[...] = jnp.zeros_like(st_ref)

        st_ref[0:1, :] += _colsum(dh)
        st_ref[1:2, :] += _colsum(dh * xn) * g_ref[...]
        st_ref[2:3, :] += _colsum(dh * xn) * (1.0 + scale)

    row = pl.BlockSpec((tr, D), lambda i: (i, 0))
    return _row_call(
        "norm1_bwd", body, (T // tr,), [row, row, row, _full(mod.shape), _full(g1.shape)], [row, _full((8, D))],
        [jax.ShapeDtypeStruct((T, D), F32), jax.ShapeDtypeStruct((8, D), F32)])(dh1, x, dx2, mod, g1)


def ada_fwd_local(c_all, w_ada, b_cols):
    B, D = c_all.shape
    Na = w_ada.shape[1]
    tn = _div(Na, 512)

    def body(c_ref, w_ref, b_ref, o_ref):
        cv = c_ref[...]
        act = (cv * _sig(cv)).astype(BF16)
        o_ref[...] = jnp.dot(act, w_ref[...].astype(BF16), preferred_element_type=F32) + b_ref[...]

    return _row_call(
        "ada_fwd_local", body, (Na // tn,),
        [_full(c_all.shape), pl.BlockSpec((D, tn), lambda j: (0, j)), pl.BlockSpec((1, tn), lambda j: (0, j))],
        pl.BlockSpec((B, tn), lambda j: (0, j)), jax.ShapeDtypeStruct((B, Na), F32), sem=("parallel",))(c_all, w_ada, b_cols)


def ada_bwd_local(c_all_t, dmod_all):
    D, B = c_all_t.shape
    Na = dmod_all.shape[1]
    tr = _div(D, 512, SUBLANE)

    def body(c_ref, d_ref, o_ref):
        cv = c_ref[...]
        act = cv * _sig(cv)
        acc = act[:, 0:1] * d_ref[0:1, :]
        for b in range(1, B):
            acc = acc + act[:, b:b + 1] * d_ref[b:b + 1, :]
        o_ref[...] = acc

    return _row_call(
        "ada_bwd_local", body, (D // tr,), [pl.BlockSpec((tr, B), lambda i: (i, 0)), _full(dmod_all.shape)],
        pl.BlockSpec((tr, Na), lambda i: (i, 0)), jax.ShapeDtypeStruct((D, Na), F32), sem=("parallel",))(c_all_t, dmod_all)


def _adam_update(w, m, v, g):
    mn = ADAM_B1 * m + (1.0 - ADAM_B1) * g
    vn = ADAM_B2 * v + (1.0 - ADAM_B2) * (g * g)
    bc1, bc2 = 1.0 - ADAM_B1 ** ADAM_STEP, 1.0 - ADAM_B2 ** ADAM_STEP
    return g, -ADAM_LR * ((mn / bc1) / (jnp.sqrt(vn / bc2) + ADAM_EPS) + ADAM_WD * w), mn, vn


def adamw(name, w, m, v, parts):
    R, C = w.shape
    tr = _div(R, max(SUBLANE, (1 << 18) // C // SUBLANE * SUBLANE), SUBLANE)
    n = len(parts)

    def body(*refs):
        w_ref, m_ref, v_ref = refs[:3]
        g_ref, d_ref, nm_ref, nv_ref = refs[3 + n:]
        g = refs[3][...].astype(F32)
        for p in refs[4:3 + n]:
            g = g + p[...].astype(F32)
        g_ref[...], d_ref[...], nm_ref[...], nv_ref[...] = _adam_update(w_ref[...], m_ref[...], v_ref[...], g)

    row = pl.BlockSpec((tr, C), lambda i: (i, 0))
    pspecs = [row if lead is None else pl.BlockSpec((None, tr, C), lambda i, lead=lead: (lead, i, 0)) for _, lead in parts]
    out = jax.ShapeDtypeStruct((R, C), F32)
    return _row_call(name, body, (R // tr,), [row, row, row] + pspecs, [row] * 4, [out] * 4, sem=("parallel",))(
        w, m, v, *[a for a, _ in parts])


def pair_add(name, g, r, idx):
    _, R, C = g.shape
    tr = _div(R, max(SUBLANE, (1 << 17) // C // SUBLANE * SUBLANE), SUBLANE)

    def body(idx_ref, g0, g1, g2, g3, r_ref, own_ref, tr_ref):
        del idx_ref
        own_ref[...] = g0[...].astype(F32) + r_ref[0].astype(F32)
        for m, gm in ((1, g1), (2, g2), (3, g3)):
            tr_ref[m - 1] = (gm[...].astype(F32) + r_ref[m].astype(F32)).astype(BF16)

    def gspec(m):
        return pl.BlockSpec((None, tr, C), lambda i, idx_ref: (idx_ref[m], i, 0))

    return pl.pallas_call(
        body, name=name,
        grid_spec=pltpu.PrefetchScalarGridSpec(
            num_scalar_prefetch=1, grid=(R // tr,),
            in_specs=[gspec(0), gspec(1), gspec(2), gspec(3), pl.BlockSpec((4, tr, C), lambda i, idx_ref: (0, i, 0))],
            out_specs=[pl.BlockSpec((tr, C), lambda i, idx_ref: (i, 0)),
                       pl.BlockSpec((3, tr, C), lambda i, idx_ref: (0, i, 0))]),
        out_shape=[jax.ShapeDtypeStruct((R, C), F32), jax.ShapeDtypeStruct((3, R, C), BF16)],
        compiler_params=_cp(("parallel",)))(idx, g, g, g, g, r)


def adamw_small(me, rows, g_rows, wsp, g_wsp, cwp, g_cw, fwp, g_fw):
    sizes = [w.shape[1] for w, _, _ in rows]
    offs = [sum(sizes[:i]) for i in range(len(rows))]
    k_cw, n_cw = cwp[0].shape
    n_fw = fwp[0].shape[1]
    per_half = g_fw.shape[-1] // n_fw
    groups = list(rows) + [wsp, cwp, fwp]
    n_p, n_r = len(groups), len(rows)

    def body(me_ref, g_rows_ref, g_wsp_ref, g_cw_ref, g_fw_ref, *refs):
        del me_ref
        wmv, outs = refs[:3 * n_p], refs[3 * n_p:]
        for p in range(n_p):
            if p < n_r:
                pick = lambda d, p=p: g_rows_ref[d, :, offs[p]:offs[p] + sizes[p]]
            elif p == n_r:
                pick = lambda d: g_wsp_ref[d]
            elif p == n_r + 1:
                pick = lambda d: g_cw_ref[d, 0:k_cw, :]
            else:
                pick = lambda d: g_fw_ref[d]
            g = pick(0)
            for d in range(1, NDEV):
                g = g + pick(d)
            res = _adam_update(wmv[3 * p][...], wmv[3 * p + 1][...], wmv[3 * p + 2][...], g)
            for slot in range(4):
                outs[4 * p + slot][...] = res[slot]

    def full(a):
        nd = len(a.shape)
        return pl.BlockSpec(a.shape, lambda i, me_ref: (0,) * nd)

    in_specs = [full(g_rows), full(g_wsp),
                pl.BlockSpec((NDEV, g_cw.shape[1], n_cw), lambda i, me_ref: (0, 0, me_ref[0])),
                pl.BlockSpec((NDEV, None, 3, n_fw), lambda i, me_ref: (0, me_ref[0] // per_half, 0, me_ref[0] % per_half))]
    in_specs += [full(a) for grp in groups for a in grp]
    out_shape = [jax.ShapeDtypeStruct(grp[0].shape, F32) for grp in groups for _ in range(4)]
    flat = pl.pallas_call(
        body, name="adamw_small",
        grid_spec=pltpu.PrefetchScalarGridSpec(num_scalar_prefetch=1, grid=(1,), in_specs=in_specs,
                                               out_specs=[full(s) for s in out_shape]),
        out_shape=out_shape, compiler_params=_cp(("arbitrary",)))(
            me, g_rows, g_wsp, g_cw, g_fw, *[a for grp in groups for a in grp])
    return [tuple(flat[4 * p:4 * p + 4]) for p in range(n_p)]


def _coords():
    return lax.axis_index("x"), lax.axis_index("y"), lax.axis_index("c")


def _flip(v, bit):
    return 1 - v if bit else v


def _comm_call(name, body, ins, out_shapes, n_sems):
    any_spec = pl.BlockSpec(memory_space=pl.ANY)
    return pl.pallas_call(
        body, name=name, in_specs=[any_spec] * len(ins), out_specs=[any_spec] * len(out_shapes), out_shape=out_shapes,
        scratch_shapes=[pltpu.SemaphoreType.DMA((s,)) for s in n_sems],
        compiler_params=pltpu.CompilerParams(has_side_effects=True))(*ins)


def gather_all(name, tensors):
    L = len(tensors)

    def body(*refs):
        ins, outs = refs[:L], refs[L:2 * L]
        send_sems, recv_sems, local_sems = refs[2 * L:]
        x, y, c = _coords()
        me = 4 * x + 2 * y + c
        local = [pltpu.make_async_copy(ins[l], outs[l].at[me], local_sems.at[l]) for l in range(L)]
        copies = []
        for l in range(L):
            for k in range(1, NDEV):
                peer = (_flip(x, k & 4), _flip(y, k & 2), _flip(c, k & 1))
                copies.append(pltpu.make_async_remote_copy(
                    src_ref=ins[l], dst_ref=outs[l].at[me], send_sem=send_sems.at[7 * l + k - 1],
                    recv_sem=recv_sems.at[7 * l + k - 1], device_id=peer, device_id_type=MESH))
        for cp in local + copies:
            cp.start()
        for cp in copies:
            cp.wait_recv()
        for cp in copies:
            cp.wait_send()
        for cp in local:
            cp.wait()

    outs = [jax.ShapeDtypeStruct((NDEV,) + t.shape, t.dtype) for t in tensors]
    return _comm_call(name, body, list(tensors), outs, (7 * L, 7 * L, L))


def exchange_rows(name, slabs):
    def body(in_ref, out_ref, send_sems, recv_sems, local_sem):
        x, y, c = _coords()
        me = 4 * x + 2 * y + c
        mine = pltpu.make_async_copy(in_ref.at[me], out_ref.at[me], local_sem.at[0])
        mine.start()
        copies = []
        for k in range(1, NDEV):
            px, py, pc = _flip(x, k & 4), _flip(y, k & 2), _flip(c, k & 1)
            copies.append(pltpu.make_async_remote_copy(
                src_ref=in_ref.at[4 * px + 2 * py + pc], dst_ref=out_ref.at[me], send_sem=send_sems.at[k - 1],
                recv_sem=recv_sems.at[k - 1], device_id=(px, py, pc), device_id_type=MESH))
        for cp in copies:
            cp.start()
        for cp in copies:
            cp.wait_recv()
        for cp in copies:
            cp.wait_send()
        mine.wait()

    return _comm_call(name, body, [slabs], [jax.ShapeDtypeStruct(slabs.shape, slabs.dtype)], (NDEV - 1, NDEV - 1, 1))[0]


def _run_comm(name, comm):
    n_i, n_o = len(comm.ins), len(comm.out_shapes)

    def body(*refs):
        start, finish = comm.plan(refs[:n_i], refs[n_i:n_i + n_o], refs[n_i + n_o:])
        start()
        finish()

    return _comm_call(name, body, comm.ins, comm.out_shapes, comm.n_sems)


def gather_comm(shards):
    L = len(shards)

    def plan(ins, outs, sems):
        send_sems, recv_sems, local_sems = sems
        x, y, c = _coords()
        sibling = (x, y, 1 - c)
        chips = [(_flip(x, m & 2), _flip(y, m & 1)) for m in (1, 2, 3)]

        def slab(px, py, pc):
            return 4 * px + 2 * py + pc

        def copy(l, k, block, to, src=None):
            dst = outs[l].at[slab(*block)]
            return pltpu.make_async_remote_copy(
                src_ref=dst if src is None else src, dst_ref=dst, send_sem=send_sems.at[7 * l + k],
                recv_sem=recv_sems.at[7 * l + k], device_id=to, device_id_type=MESH)

        local = [pltpu.make_async_copy(ins[l], outs[l].at[slab(x, y, c)], local_sems.at[l]) for l in range(L)]
        first = []
        for l in range(L):
            first.append(copy(l, 0, (x, y, c), sibling, src=ins[l]))
            first += [copy(l, 1 + j, (x, y, c), (*chip, c), src=ins[l]) for j, chip in enumerate(chips)]

        def start():
            for cp in local + first:
                cp.start()

        def finish():
            passed = []
            for j, chip in enumerate(chips):
                for l in range(L):
                    copy(l, 1 + j, (*chip, c), (x, y, c)).wait_recv()
                    fwd = copy(l, 4 + j, (*chip, c), sibling)
                    fwd.start()
                    passed.append(fwd)
            for l in range(L):
                copy(l, 0, sibling, (x, y, c)).wait_recv()
                for j, chip in enumerate(chips):
                    copy(l, 4 + j, (*chip, 1 - c), (x, y, c)).wait_recv()
            for cp in first + passed:
                cp.wait_send()
            for cp in local:
                cp.wait()

        return start, finish

    outs = [jax.ShapeDtypeStruct((NDEV,) + s.shape, s.dtype) for s in shards]
    return Comm(plan, shards, outs, (7 * L, 7 * L, L))


def _all_at_once(copies):
    def start():
        for cp in copies:
            cp.start()

    def finish():
        for cp in copies:
            cp.wait_recv()
        for cp in copies:
            cp.wait_send()

    return start, finish


def sibling_comm(grads):
    L = len(grads)

    def plan(ins, outs, sems):
        send_sems, recv_sems = sems
        x, y, c = _coords()
        copies = []
        for l in range(L):
            for m in range(4):
                qm = 2 * _flip(x, m & 2) + _flip(y, m & 1)
                copies.append(pltpu.make_async_remote_copy(
                    src_ref=ins[l].at[2 * qm + (1 - c)], dst_ref=outs[l].at[m], send_sem=send_sems.at[4 * l + m],
                    recv_sem=recv_sems.at[4 * l + m], device_id=(x, y, 1 - c), device_id_type=MESH))
        return _all_at_once(copies)

    outs = [jax.ShapeDtypeStruct((4,) + g.shape[1:], g.dtype) for g in grads]
    return Comm(plan, grads, outs, (4 * L, 4 * L))


def chip_comm(transits):
    L = len(transits)

    def plan(ins, outs, sems):
        send_sems, recv_sems = sems
        x, y, c = _coords()
        copies = []
        for l in range(L):
            for m in (1, 2, 3):
                copies.append(pltpu.make_async_remote_copy(
                    src_ref=ins[l].at[m - 1], dst_ref=outs[l].at[m - 1], send_sem=send_sems.at[3 * l + m - 1],
                    recv_sem=recv_sems.at[3 * l + m - 1], device_id=(_flip(x, m & 2), _flip(y, m & 1), c),
                    device_id_type=MESH))
        return _all_at_once(copies)

    outs = [jax.ShapeDtypeStruct(t.shape, t.dtype) for t in transits]
    return Comm(plan, transits, outs, (3 * L, 3 * L))


_SMALL_ROWS = ("b_ada", "norm1_g", "b_in", "conv_dw_b", "conv_ln_g", "conv_ln_b", "sgu_ln_g", "sgu_ln_b", "b_spatial",
               "norm2_g", "ffn_dw_b", "final_g")
_BIG = ("w_in", "w_conv_out", "w_sgu_out", "w_out", "w_up", "w_down")
_ORDER = ("w_ada", "b_ada", "norm1_g", "w_in", "b_in", "conv_dw_w", "conv_dw_b", "conv_ln_g", "conv_ln_b", "w_conv_out",
          "sgu_ln_g", "sgu_ln_b", "w_spatial", "b_spatial", "w_sgu_out", "w_out", "norm2_g", "w_up", "ffn_dw_w", "ffn_dw_b",
          "w_down", "final_g")


def kernel(x, c, w_ada, b_ada, norm1_g, w_in, b_in, conv_dw_w, conv_dw_b, conv_ln_g, conv_ln_b, w_conv_out, sgu_ln_g, sgu_ln_b, w_spatial, b_spatial, w_sgu_out, w_out, norm2_g, w_up, ffn_dw_w, ffn_dw_b, w_down, final_g, loss_target, m_w_ada, m_b_ada, m_norm1_g, m_w_in, m_b_in, m_conv_dw_w, m_conv_dw_b, m_conv_ln_g, m_conv_ln_b, m_w_conv_out, m_sgu_ln_g, m_sgu_ln_b, m_w_spatial, m_b_spatial, m_w_sgu_out, m_w_out, m_norm2_g, m_w_up, m_ffn_dw_w, m_ffn_dw_b, m_w_down, m_final_g, v_w_ada, v_b_ada, v_norm1_g, v_w_in, v_b_in, v_conv_dw_w, v_conv_dw_b, v_conv_ln_g, v_conv_ln_b, v_w_conv_out, v_sgu_ln_g, v_sgu_ln_b, v_w_spatial, v_b_spatial, v_w_sgu_out, v_w_out, v_norm2_g, v_w_up, v_ffn_dw_w, v_ffn_dw_b, v_w_down, v_final_g):
    W = dict(w_ada=w_ada, b_ada=b_ada, norm1_g=norm1_g, w_in=w_in, b_in=b_in, conv_dw_w=conv_dw_w, conv_dw_b=conv_dw_b,
             conv_ln_g=conv_ln_g, conv_ln_b=conv_ln_b, w_conv_out=w_conv_out, sgu_ln_g=sgu_ln_g, sgu_ln_b=sgu_ln_b,
             w_spatial=w_spatial, b_spatial=b_spatial, w_sgu_out=w_sgu_out, w_out=w_out, norm2_g=norm2_g, w_up=w_up,
             ffn_dw_w=ffn_dw_w, ffn_dw_b=ffn_dw_b, w_down=w_down, final_g=final_g)
    M = dict(w_ada=m_w_ada, b_ada=m_b_ada, norm1_g=m_norm1_g, w_in=m_w_in, b_in=m_b_in, conv_dw_w=m_conv_dw_w,
             conv_dw_b=m_conv_dw_b, conv_ln_g=m_conv_ln_g, conv_ln_b=m_conv_ln_b, w_conv_out=m_w_conv_out,
             sgu_ln_g=m_sgu_ln_g, sgu_ln_b=m_sgu_ln_b, w_spatial=m_w_spatial, b_spatial=m_b_spatial,
             w_sgu_out=m_w_sgu_out, w_out=m_w_out, norm2_g=m_norm2_g, w_up=m_w_up, ffn_dw_w=m_ffn_dw_w,
             ffn_dw_b=m_ffn_dw_b, w_down=m_w_down, final_g=m_final_g)
    V = dict(w_ada=v_w_ada, b_ada=v_b_ada, norm1_g=v_norm1_g, w_in=v_w_in, b_in=v_b_in, conv_dw_w=v_conv_dw_w,
             conv_dw_b=v_conv_dw_b, conv_ln_g=v_conv_ln_g, conv_ln_b=v_conv_ln_b, w_conv_out=v_w_conv_out,
             sgu_ln_g=v_sgu_ln_g, sgu_ln_b=v_sgu_ln_b, w_spatial=v_w_spatial, b_spatial=v_b_spatial,
             w_sgu_out=v_w_sgu_out, w_out=v_w_out, norm2_g=v_norm2_g, w_up=v_w_up, ffn_dw_w=v_ffn_dw_w,
             ffn_dw_b=v_ffn_dw_b, w_down=v_w_down, final_g=v_final_g)

    xs, tgt = x[0], loss_target[0]
    T, D = xs.shape
    Dc = conv_dw_w.shape[-1] * NDEV
    F = w_down.shape[1] * NDEV
    K31 = conv_dw_w.shape[1]
    G = w_spatial.shape[1]
    assert D == 2 * Dc and sgu_ln_g.shape[-1] == Dc and T % CHUNK == 0
    me = 4 * lax.axis_index("x") + 2 * lax.axis_index("y") + lax.axis_index("c")

    na = w_ada.shape[-1]
    c_all = gather_all("gather_c", [c])[0].reshape(NDEV, D)
    b_cols = lax.dynamic_slice(b_ada, (0, me * na), (1, na))
    mod_cols = ada_fwd_local(c_all, w_ada[0], b_cols)
    mod = exchange_rows("exchange_mod", mod_cols.reshape(NDEV, 1, na)).reshape(1, NDEV * na)

    wbf = {k: W[k][0].astype(BF16) for k in _BIG}
    fb = ffn_dw_b.reshape(2, 1, F)
    bsp_t = jnp.transpose(b_spatial[0])
    wsp = w_spatial[0]

    def plain(wb):
        return jnp.transpose(wb, (1, 0, 2)).reshape(1, wb.shape[1], NDEV * wb.shape[2])

    h1, (wb_in,) = pre_norm("pre_norm1", xs, mod, norm1_g, 0, comm=gather_comm([wbf["w_in"]]))
    proj, (wb_up, cw_g, fw_g) = mm_nn(
        "proj", h1, wb_in, F32, bias=b_in, comm=gather_comm([wbf["w_up"], conv_dw_w[0], ffn_dw_w[0]]))
    cw = jnp.transpose(cw_g, (1, 0, 2)).reshape(K31, Dc)
    fw = jnp.transpose(jnp.transpose(fw_g, (1, 0, 2)).reshape(3, 2, F), (1, 0, 2))
    wp_in, wp_up = to_plain("plain_w_in", wb_in), to_plain("plain_w_up", wb_up)
    (ac, asw), (wb_out, wb_co, wb_so) = mix_a_fwd(
        proj, cw, conv_dw_b, conv_ln_g, conv_ln_b,
        comm=gather_comm([wbf["w_out"], wbf["w_conv_out"], wbf["w_sgu_out"]]))
    wb_out = wb_out.reshape(1, D, D)
    wp_co, wp_so = plain(wb_co), plain(wb_so)
    uv = mix_b_fwd(proj, sgu_ln_g, sgu_ln_b, wsp, bsp_t)
    y_a = mm_nn("y_a", asw, wp_co, ACT)
    y_b = mm_nn("y_b", uv, wp_so, ACT)
    merged = merge_fwd(proj, y_a, y_b)
    o1 = mm_nn("o1", merged, wb_out, F32)
    h2 = pre_norm("pre_norm2", xs, mod, norm2_g, 1, o1=o1)
    upre, (wb_down,) = mm_nn("upre", h2, wb_up, F32, out_halves=True, tn_pref=MM_WIDE,
                             comm=gather_comm([wbf["w_down"]]))
    wb_down = wb_down.reshape(1, F, D)
    f, up = ffn_act_fwd(upre, fw, fb)
    o2 = mm_nn("o2", f, wb_down, F32)
    dx3, do2, st_f = final_fwd_bwd(xs, o1, o2, mod, final_g.reshape(1, D), tgt)
    loss = lax.psum(st_f[3, 0], MESH_AXES)

    xq, yq, cq = lax.axis_index("x"), lax.axis_index("y"), lax.axis_index("c")
    idx = jnp.stack([2 * (2 * _flip(xq, m & 2) + _flip(yq, m & 1)) + cq for m in range(4)]).astype(jnp.int32)
    own, transit, arrived = {}, {}, {}

    def add_pairs(keys, full, from_sibling):
        for k, g_full, r in zip(keys, full, from_sibling):
            own[k], transit[k] = pair_add("pair_add_" + k, g_full, r, idx)

    df = mm_nt("df", do2, wb_down, ACT, tko_pref=MM_WIDE)
    g_down = mm_tn("g_down", f, do2, 1, tko_pref=MM_WIDE).reshape(NDEV, F // NDEV, D)
    dupre, g_fw, g_fb = ffn_act_bwd(upre, up, df, fw)
    dh2 = mm_nt("dh2", dupre, wp_up, ACT, a_halves=True)
    g_up = mm_tn("g_up", h2, dupre, NDEV, g_halves=True, tn_pref=MM_WIDE)
    dx2, do1, st_2 = norm2_bwd(dh2, xs, o1, dx3, mod, norm2_g)
    dmerged, sib = mm_nt("dmerged", do1, wb_out, ACT, comm=sibling_comm([g_down, g_up]))
    add_pairs(("w_down", "w_up"), (g_down, g_up), sib)
    g_out = mm_tn("g_out", merged, do1, 1).reshape(NDEV, D // NDEV, D)
    dy_a, dy_b, dproj, db_g = merge_bwd(dmerged, proj, y_a, y_b)
    def blocked(g):
        return jnp.transpose(g.reshape(g.shape[1], NDEV, g.shape[2] // NDEV), (1, 0, 2))

    dasw = mm_nt("dasw", dy_a, wp_co, ACT)
    g_co = blocked(mm_tn("g_co", asw, dy_a, 1))
    duv = mm_nt("duv", dy_b, wp_so, ACT)
    g_so = blocked(mm_tn("g_so", uv, dy_b, 1))
    (dproj, st_a, g_cw, db_a), (arrived["w_up"],) = mix_a_bwd(
        dasw, ac, proj, dproj, cw, conv_ln_g, conv_ln_b, comm=chip_comm([transit["w_up"]]))
    (dproj, st_b, g_wsp, g_bsp_t, db_s), sib = mix_b_bwd(
        duv, proj, dproj, sgu_ln_g, sgu_ln_b, wsp, bsp_t, comm=sibling_comm([g_out, g_co, g_so]))
    add_pairs(("w_out", "w_conv_out", "w_sgu_out"), (g_out, g_co, g_so), sib)
    g_in, (arrived["w_down"],) = mm_tn("g_in", h1, dproj, NDEV, comm=chip_comm([transit["w_down"]]))
    add_pairs(("w_in",), (g_in,), _run_comm("sibling_w_in", sibling_comm([g_in])))
    late = ("w_out", "w_conv_out", "w_sgu_out", "w_in")
    dh1, got = mm_nt("dh1", dproj, wp_in, ACT, comm=chip_comm([transit[k] for k in late]))
    arrived.update(zip(late, got))
    grad_x, st_1 = norm1_bwd(dh1, xs, dx2, mod, norm1_g)

    dmod = jnp.concatenate([st_1[0], st_1[1], st_2[3], st_2[0], st_2[1], st_f[1]]).reshape(1, NDEV * na)
    dmod_all = exchange_rows("exchange_dmod", dmod.reshape(NDEV, 1, na)).reshape(NDEV, na)
    g_ada = ada_bwd_local(jnp.transpose(c_all), dmod_all)

    res = {}
    for k in _BIG:
        q = arrived[k]
        res[k] = adamw("adamw_" + k, W[k][0], M[k][0], V[k][0], [(own[k], None), (q, 0), (q, 1), (q, 2)])
    res["w_ada"] = adamw("adamw_w_ada", w_ada[0], m_w_ada[0], v_w_ada[0], [(g_ada, None)])

    g_row = dict(
        b_ada=dmod, norm1_g=st_1[2], b_in=jnp.concatenate([db_a, db_s, db_g], axis=1), conv_dw_b=st_a[2],
        conv_ln_g=st_a[0], conv_ln_b=st_a[1], sgu_ln_g=st_b[0], sgu_ln_b=st_b[1], b_spatial=jnp.transpose(g_bsp_t),
        norm2_g=st_2[2], ffn_dw_b=g_fb, final_g=st_f[0])
    packed = jnp.concatenate([g_row[k].reshape(1, -1) for k in _SMALL_ROWS], axis=1)
    g_rows, g_wsp_all, g_cw_all, g_fw_all = gather_all("gather_small", [packed, g_wsp.reshape(-1, CHUNK), g_cw, g_fw])
    rows = [tuple(S[k].reshape(1, -1) for S in (W, M, V)) for k in _SMALL_ROWS]
    wsp3 = tuple(S["w_spatial"].reshape(-1, CHUNK) for S in (W, M, V))
    cw3, fw3 = (tuple(S[k][0] for S in (W, M, V)) for k in ("conv_dw_w", "ffn_dw_w"))
    small = adamw_small(jnp.reshape(me, (1,)).astype(jnp.int32), rows, g_rows, wsp3, g_wsp_all, cw3, g_cw_all, fw3, g_fw_all)
    res.update(zip(_SMALL_ROWS + ("w_spatial", "conv_dw_w", "ffn_dw_w"), small))

    outs = [[], [], [], []]
    for k in _ORDER:
        for slot in range(4):
            outs[slot].append(res[k][slot].reshape(W[k].shape))
    return (loss, grad_x[None], *outs[0], *outs[1], *outs[2], *outs[3])
```

```python
import functools

import jax
import jax.numpy as jnp
from jax import lax
from jax.experimental import pallas as pl
from jax.experimental.pallas import tpu as pltpu

F32, BF16 = jnp.float32, jnp.bfloat16
ACT = BF16
NDEV = 8
MESH_AXES = ("x", "y", "c")
MESH = pl.DeviceIdType.MESH
EPS = 1e-6
CHUNK = 128
CONV_HALO = 32
FFN_HALO = 8
LANE, SUBLANE = 128, 8
ROW_TILE = 256
FFN_ROW_TILE = 512
STRIP = 32
STRIP_UNROLL = 2
MM_TILE = 1024
MM_WIDE = 1408
MM_DEEP = 2816
VMEM_LIMIT = 56 * 1024 * 1024
ADAM_LR, ADAM_B1, ADAM_B2, ADAM_EPS, ADAM_WD, ADAM_STEP = 0.001, 0.9, 0.999, 1e-08, 0.01, 10
SQRT_HALF = 0.7071067811865476
INV_SQRT_2PI = 0.3989422804014327


def _div(n, pref, mult=LANE):
    if n <= pref:
        return n
    for d in range(pref - pref % mult, 0, -mult):
        if n % d == 0:
            return d
    return n


def _cp(sem):
    return pltpu.CompilerParams(dimension_semantics=sem, vmem_limit_bytes=VMEM_LIMIT)


def _sig(v):
    return jax.nn.sigmoid(v)


def _gelu(v):
    return 0.5 * v * (1.0 + lax.erf(v * SQRT_HALF))


def _gelu_grad(v):
    return 0.5 * (1.0 + lax.erf(v * SQRT_HALF)) + v * (INV_SQRT_2PI * jnp.exp(-0.5 * v * v))


def _colsum(v):
    return jnp.sum(v, axis=0, keepdims=True)


def _rowmean(v):
    return jnp.mean(v, axis=-1, keepdims=True)


class Comm:
    def __init__(self, plan, ins, out_shapes, n_sems):
        self.plan, self.ins, self.out_shapes, self.n_sems = plan, list(ins), list(out_shapes), tuple(n_sems)


def _pcall(name, body, grid, in_specs, out_specs, out_shape, scratch=(), sem=None, aliases=None, comm=None):
    if comm is None:
        return pl.pallas_call(
            body, name=name, grid=grid, in_specs=in_specs, out_specs=out_specs, out_shape=out_shape,
            scratch_shapes=list(scratch), input_output_aliases=aliases or {},
            compiler_params=_cp(sem or ("arbitrary",) * len(grid)))
    single = not isinstance(out_shape, (list, tuple))
    own_specs, own_shapes = ([out_specs], [out_shape]) if single else (list(out_specs), list(out_shape))
    n_in, n_out, n_scr = len(in_specs), len(own_shapes), len(scratch)
    n_ci, n_co = len(comm.ins), len(comm.out_shapes)
    any_spec = pl.BlockSpec(memory_space=pl.ANY)

    def fused(*refs):
        ins, cins = refs[:n_in], refs[n_in:n_in + n_ci]
        outs = refs[n_in + n_ci:n_in + n_ci + n_out]
        couts = refs[n_in + n_ci + n_out:n_in + n_ci + n_out + n_co]
        scr = refs[n_in + n_ci + n_out + n_co:n_in + n_ci + n_out + n_co + n_scr]
        sems = refs[n_in + n_ci + n_out + n_co + n_scr:]
        first = functools.reduce(jnp.logical_and, [pl.program_id(d) == 0 for d in range(len(grid))])
        last = functools.reduce(jnp.logical_and, [pl.program_id(d) == grid[d] - 1 for d in range(len(grid))])

        @pl.when(first)
        def _():
            comm.plan(cins, couts, sems)[0]()

        body(*ins, *outs, *scr)

        @pl.when(last)
        def _():
            comm.plan(cins, couts, sems)[1]()

    call = pl.pallas_call(
        fused, name=name, grid=grid, in_specs=list(in_specs) + [any_spec] * n_ci,
        out_specs=own_specs + [any_spec] * n_co, out_shape=own_shapes + comm.out_shapes,
        scratch_shapes=list(scratch) + [pltpu.SemaphoreType.DMA((s,)) for s in comm.n_sems],
        input_output_aliases=aliases or {},
        compiler_params=pltpu.CompilerParams(dimension_semantics=("arbitrary",) * len(grid),
                                             vmem_limit_bytes=VMEM_LIMIT, has_side_effects=True))

    def run(*args):
        res = call(*args, *comm.ins)
        own = res[:n_out]
        return (own[0] if single else list(own)), list(res[n_out:])

    return run


def _matmul(name, a, b, *, grid, a_spec, b_spec, o_spec, out_shape, dims, acc_shape, bias=None, bias_spec=None, comm=None):
    nk = grid[2]

    def body(*refs):
        if bias is None:
            a_ref, b_ref, o_ref, *scr = refs
            bias_ref = None
        else:
            a_ref, b_ref, bias_ref, o_ref, *scr = refs
        part = lax.dot_general(a_ref[...], b_ref[...], (dims, ((), ())), preferred_element_type=F32)

        def finish(total):
            if bias_ref is not None:
                total = total + bias_ref[...]
            o_ref[...] = total.astype(o_ref.dtype)

        if nk == 1:
            finish(part)
        else:
            acc = scr[0]
            k = pl.program_id(2)

            @pl.when(k == 0)
            def _():
                acc[...] = part

            @pl.when(k > 0)
            def _():
                acc[...] += part

            @pl.when(k == nk - 1)
            def _():
                finish(acc[...])

    in_specs = [a_spec, b_spec] + ([bias_spec] if bias is not None else [])
    args = (a, b) + ((bias,) if bias is not None else ())
    return _pcall(name, body, grid, in_specs, o_spec, out_shape,
                  scratch=[pltpu.VMEM(acc_shape, F32)] if nk > 1 else [],
                  sem=("parallel", "parallel", "arbitrary"), comm=comm)(*args)


def mm_nn(name, a, wb, out_dtype, *, bias=None, out_halves=False, tn_pref=MM_TILE, comm=None):
    T, K = a.shape
    NB, _, Ns = wb.shape
    N = NB * Ns
    tm, tn, tk = _div(T, MM_TILE), _div(Ns, tn_pref), _div(K, MM_DEEP)
    npb, nj, nk = Ns // tn, N // tn, K // tk
    if out_halves:
        o_spec = pl.BlockSpec((None, tm, tn), lambda i, j, k: (j // (nj // 2), i, j % (nj // 2)))
        out_shape = jax.ShapeDtypeStruct((2, T, N // 2), out_dtype)
    else:
        o_spec = pl.BlockSpec((tm, tn), lambda i, j, k: (i, j))
        out_shape = jax.ShapeDtypeStruct((T, N), out_dtype)
    return _matmul(
        name, a, wb, grid=(T // tm, nj, nk),
        a_spec=pl.BlockSpec((tm, tk), lambda i, j, k: (i, k)),
        b_spec=pl.BlockSpec((None, tk, tn), lambda i, j, k: (j // npb, k, j % npb)),
        o_spec=o_spec, out_shape=out_shape, dims=((1,), (0,)), acc_shape=(tm, tn),
        bias=bias, bias_spec=pl.BlockSpec((1, tn), lambda i, j, k: (0, j)), comm=comm)


def to_plain(name, wb):
    NB, K, Ns = wb.shape
    tk = _div(K, MM_TILE, 2 * SUBLANE)

    def body(i_ref, o_ref):
        o_ref[...] = i_ref[...]

    return _pcall(name, body, (NB, K // tk), [pl.BlockSpec((None, tk, Ns), lambda b, k: (b, k, 0))],
                  pl.BlockSpec((None, tk, Ns), lambda b, k: (0, k, b)),
                  jax.ShapeDtypeStruct((1, K, NB * Ns), wb.dtype), sem=("parallel", "parallel"))(wb)


def mm_nt(name, a, wb, out_dtype, *, a_halves=False, tko_pref=MM_TILE, tc_pref=MM_DEEP, comm=None):
    NB, K, Ns = wb.shape
    T = a.shape[-2]
    span = Ns // 2 if a_halves and NB == 1 else Ns
    tm, tko, tc = _div(T, MM_TILE), _div(K, tko_pref), _div(span, tc_pref)
    cpb = Ns // tc
    nkk = NB * cpb
    if a_halves:
        a_spec = pl.BlockSpec((None, tm, tc), lambda i, j, k: (k // (nkk // 2), i, k % (nkk // 2)))
    else:
        a_spec = pl.BlockSpec((tm, tc), lambda i, j, k: (i, k))
    return _matmul(
        name, a, wb, grid=(T // tm, K // tko, nkk), a_spec=a_spec,
        b_spec=pl.BlockSpec((None, tko, tc), lambda i, j, k: (k // cpb, j, k % cpb)),
        o_spec=pl.BlockSpec((tm, tko), lambda i, j, k: (i, j)),
        out_shape=jax.ShapeDtypeStruct((T, K), out_dtype), dims=((1,), (1,)), acc_shape=(tm, tko), comm=comm)


def mm_tn(name, a, g, nb, *, g_halves=False, tko_pref=MM_TILE, tn_pref=MM_TILE, out_dtype=BF16, comm=None):
    T, K = a.shape
    N = g.shape[-1] * (2 if g_halves else 1)
    Ns = N // nb
    tt, tko, tn = _div(T, 2 * MM_TILE), _div(K, tko_pref), _div(Ns, tn_pref)
    npb, nj = Ns // tn, N // tn
    if g_halves:
        g_spec = pl.BlockSpec((None, tt, tn), lambda i, j, t: (j // (nj // 2), t, j % (nj // 2)))
    else:
        g_spec = pl.BlockSpec((tt, tn), lambda i, j, t: (t, j))
    return _matmul(
        name, a, g, grid=(K // tko, nj, T // tt),
        a_spec=pl.BlockSpec((tt, tko), lambda i, j, t: (t, i)), b_spec=g_spec,
        o_spec=pl.BlockSpec((None, tko, tn), lambda i, j, t: (j // npb, i, j % npb)),
        out_shape=jax.ShapeDtypeStruct((nb, K, Ns), out_dtype), dims=((0,), (0,)), acc_shape=(tko, tn), comm=comm)


def _row_call(name, body, grid, in_specs, out_specs, out_shape, scratch=(), sem=None, aliases=None, comm=None):
    return _pcall(name, body, grid, in_specs, out_specs, out_shape, scratch, sem, aliases, comm)


def _full(shape):
    nd = len(shape)
    return pl.BlockSpec(shape, lambda *idx: (0,) * nd)


def pre_norm(name, x, mod, g, which, o1=None, comm=None):
    T, D = x.shape
    tr = _div(T, ROW_TILE, SUBLANE)

    def body(*refs):
        if o1 is None:
            x_ref, mod_ref, g_ref, h_ref = refs
            xv = x_ref[...]
        else:
            x_ref, o1_ref, mod_ref, g_ref, h_ref = refs
            xv = x_ref[...] + mod_ref[:, 2 * D:3 * D] * o1_ref[...]
        shift = mod_ref[:, (3 * which) * D:(3 * which + 1) * D]
        scale = mod_ref[:, (3 * which + 1) * D:(3 * which + 2) * D]
        r = lax.rsqrt(_rowmean(xv * xv) + EPS)
        h_ref[...] = ((xv * r) * g_ref[...] * (1.0 + scale) + shift).astype(BF16)

    row = pl.BlockSpec((tr, D), lambda i: (i, 0))
    ins = [x] + ([o1] if o1 is not None else []) + [mod, g]
    specs = [row] * (1 if o1 is None else 2) + [_full(mod.shape), _full(g.shape)]
    return _row_call(name, body, (T // tr,), specs, row, jax.ShapeDtypeStruct((T, D), BF16), sem=("parallel",),
                     comm=comm)(*ins)


def _window_taps(win, taps):
    n = win.shape[0]
    for r in range(SUBLANE):
        group = [(i, tap, off) for i, (tap, off) in enumerate(taps) if off % SUBLANE == r]
        if not group:
            continue
        shifted = win if r == 0 else pltpu.roll(win, n - r, 0)
        for i, tap, off in group:
            assert 0 <= off and off + CONV_HALO <= n
            yield i, tap, shifted[off - r:off - r + CONV_HALO]


def mix_a_fwd(proj, cw, cb, lg, lb, comm=None):
    T = proj.shape[0]
    K, Dc = cw.shape
    tr = _div(T, ROW_TILE, CONV_HALO)
    hb = tr // CONV_HALO

    def body(val_ref, gate_ref, hval_ref, hgate_ref, cw_ref, cb_ref, lg_ref, lb_ref, ac_ref, asw_ref, buf):
        i = pl.program_id(0)
        hist = hval_ref[...] * _sig(hgate_ref[...])
        buf[0:CONV_HALO, :] = jnp.where(i > 0, hist, 0.0)
        buf[CONV_HALO:CONV_HALO + tr, :] = val_ref[...] * _sig(gate_ref[...])
        base = CONV_HALO - (K - 1)
        for c0 in range(0, Dc, LANE):
            lanes = slice(c0, c0 + LANE)

            def step(s, carry):
                r0 = pl.multiple_of(s * CONV_HALO, CONV_HALO)
                win = buf[pl.ds(r0, 2 * CONV_HALO), lanes]
                acc = jnp.zeros((CONV_HALO, LANE), F32)
                for _, k, piece in _window_taps(win, [(k, base + k) for k in range(K)]):
                    acc = acc + piece * cw_ref[k:k + 1, lanes]
                ac_ref[pl.ds(r0, CONV_HALO), lanes] = acc + cb_ref[:, lanes]
                return carry

            lax.fori_loop(0, tr // CONV_HALO, step, 0)
        ac = ac_ref[...]
        mu = _rowmean(ac)
        cen = ac - mu
        y = cen * lax.rsqrt(_rowmean(cen * cen) + EPS)
        aln = y * lg_ref[...] + lb_ref[...]
        asw_ref[...] = (aln * _sig(aln)).astype(BF16)

    def halo(col):
        return pl.BlockSpec((CONV_HALO, Dc), lambda i: (jnp.maximum(i * hb - 1, 0), col))

    row = pl.BlockSpec((tr, Dc), lambda i: (i, 0))
    return _row_call(
        "mix_a_fwd", body, (T // tr,),
        [row, pl.BlockSpec((tr, Dc), lambda i: (i, 1)), halo(0), halo(1),
         _full(cw.shape), _full(cb.shape), _full(lg.shape), _full(lb.shape)],
        [row, row], [jax.ShapeDtypeStruct((T, Dc), F32), jax.ShapeDtypeStruct((T, Dc), BF16)],
        scratch=[pltpu.VMEM((CONV_HALO + tr, Dc), F32)], sem=("parallel",), comm=comm)(proj, proj, proj, proj, cw, cb, lg, lb)


def _spatial_mask():
    t = lax.broadcasted_iota(jnp.int32, (CHUNK, CHUNK), 0)
    s = lax.broadcasted_iota(jnp.int32, (CHUNK, CHUNK), 1)
    return s <= t


def mix_b_fwd(proj, lg, lb, wsp, bsp_t):
    T = proj.shape[0]
    Ds = lg.shape[-1]
    G = wsp.shape[0]
    hd = Ds // G
    tr = _div(T, ROW_TILE, CHUNK)

    def body(u_ref, v_ref, lg_ref, lb_ref, w_ref, b_ref, uv_ref, vs):
        v = _gelu(v_ref[...])
        mu = _rowmean(v)
        cen = v - mu
        vln = (cen * lax.rsqrt(_rowmean(cen * cen) + EPS) * lg_ref[...] + lb_ref[...]).astype(BF16)
        mask = _spatial_mask()
        for g in range(G):
            wg = jnp.where(mask, w_ref[g], 0.0).astype(BF16)
            for n in range(tr // CHUNK):
                rows, cols = slice(n * CHUNK, (n + 1) * CHUNK), slice(g * hd, (g + 1) * hd)
                vs[rows, cols] = jnp.dot(wg, vln[rows, cols], preferred_element_type=F32) + b_ref[:, g:g + 1]
        uv_ref[...] = (_gelu(u_ref[...]) * vs[...]).astype(BF16)

    return _row_call(
        "mix_b_fwd", body, (T // tr,),
        [pl.BlockSpec((tr, Ds), lambda i: (i, 2)), pl.BlockSpec((tr, Ds), lambda i: (i, 3)),
         _full(lg.shape), _full(lb.shape), _full(wsp.shape), _full(bsp_t.shape)],
        pl.BlockSpec((tr, Ds), lambda i: (i, 0)), jax.ShapeDtypeStruct((T, Ds), BF16),
        scratch=[pltpu.VMEM((tr, Ds), F32)], sem=("parallel",))(proj, proj, lg, lb, wsp, bsp_t)


def merge_fwd(proj, y_a, y_b):
    T, D = y_a.shape
    tr = _div(T, ROW_TILE, SUBLANE)

    def body(g_ref, ya_ref, yb_ref, o_ref):
        o_ref[...] = (_sig(g_ref[:, 0:D]) * ya_ref[...].astype(F32)
                      + _sig(g_ref[:, D:2 * D]) * yb_ref[...].astype(F32)).astype(BF16)

    row = pl.BlockSpec((tr, D), lambda i: (i, 0))
    return _row_call("merge_fwd", body, (T // tr,), [pl.BlockSpec((tr, 2 * D), lambda i: (i, 1)), row, row], row,
                     jax.ShapeDtypeStruct((T, D), BF16), sem=("parallel",))(proj, y_a, y_b)


def _fold(v):
    acc = v[0:SUBLANE]
    for r in range(SUBLANE, v.shape[0], SUBLANE):
        acc = acc + v[r:r + SUBLANE]
    return acc


def _conv3(prev, cur, w):
    win = jnp.concatenate([prev, cur], axis=0)
    n = win.shape[0]
    x1 = pltpu.roll(win, 1, 0)[FFN_HALO:n]
    x2 = pltpu.roll(win, 2, 0)[FFN_HALO:n]
    return x2 * w[0] + x1 * w[1] + cur * w[2], (x2, x1, cur)


def _strip_taps(w_ref, b_ref, lanes):
    w = [[w_ref[h, k:k + 1, lanes] for k in range(3)] for h in range(2)]
    b = [b_ref[h, :, lanes] for h in range(2)]
    return w, b


def ffn_act_fwd(upre, fw, fb, comm=None):
    _, T, F = upre.shape
    tr = _div(T, FFN_ROW_TILE, STRIP)
    cb = _div(F, MM_WIDE)
    hb = tr // FFN_HALO
    ns = tr // STRIP

    def body(x_ref, h_ref, w_ref, b_ref, f_ref, u_ref):
        i = pl.program_id(0)
        for c0 in range(0, cb, LANE):
            lanes = slice(c0, c0 + LANE)
            w, b = _strip_taps(w_ref, b_ref, lanes)

            def strip(r0, prev):
                up = [_conv3(prev(h), x_ref[h, pl.ds(r0, STRIP), lanes], w[h])[0] + b[h] for h in range(2)]
                f_ref[pl.ds(r0, STRIP), lanes] = (up[1] * _sig(up[1]) * up[0]).astype(BF16)
                for h in range(2):
                    u_ref[h, pl.ds(r0, STRIP), lanes] = up[h]

            strip(0, lambda h: jnp.where(i > 0, h_ref[h, :, lanes], 0.0))

            def step(s, carry):
                r0 = pl.multiple_of(s * STRIP, STRIP)
                strip(r0, lambda h: x_ref[h, pl.ds(pl.multiple_of(r0 - FFN_HALO, FFN_HALO), FFN_HALO), lanes])
                return carry

            lax.fori_loop(1, ns, step, 0, unroll=STRIP_UNROLL)

    return _row_call(
        "ffn_act_fwd", body, (T // tr, F // cb),
        [pl.BlockSpec((2, tr, cb), lambda i, j: (0, i, j)),
         pl.BlockSpec((2, FFN_HALO, cb), lambda i, j: (0, jnp.maximum(i * hb - 1, 0), j)),
         pl.BlockSpec((2, 3, cb), lambda i, j: (0, 0, j)), pl.BlockSpec((2, 1, cb), lambda i, j: (0, 0, j))],
        [pl.BlockSpec((tr, cb), lambda i, j: (i, j)), pl.BlockSpec((2, tr, cb), lambda i, j: (0, i, j))],
        [jax.ShapeDtypeStruct((T, F), BF16), jax.ShapeDtypeStruct((2, T, F), F32)],
        sem=("parallel", "parallel"), comm=comm)(upre, upre, fw, fb)


def final_fwd_bwd(x, o1, o2, mod, gf, target):
    T, D = x.shape
    tr = _div(T, ROW_TILE, SUBLANE)
    nt = T // tr

    def body(x_ref, o1_ref, o2_ref, mod_ref, gf_ref, t_ref, dx3_ref, do2_ref, st_ref):
        i = pl.program_id(0)
        gate1, gate2 = mod_ref[:, 2 * D:3 * D], mod_ref[:, 5 * D:6 * D]
        o2v = o2_ref[...]
        x3 = x_ref[...] + gate1 * o1_ref[...] + gate2 * o2v
        r = lax.rsqrt(_rowmean(x3 * x3) + EPS)
        xn = x3 * r
        err = xn * gf_ref[...] - t_ref[...]
        dy = err * (1.0 / D)
        dxn = dy * gf_ref[...]
        dx3 = r * (dxn - xn * _rowmean(dxn * xn))
        dx3_ref[...] = dx3
        do2_ref[...] = (dx3 * gate2).astype(BF16)

        @pl.when(i == 0)
        def _():
            st_ref[...] = jnp.zeros_like(st_ref)

        st_ref[0:1, :] += _colsum(dy * xn)
        st_ref[1:2, :] += _colsum(dx3 * o2v)
        st_ref[2:3, :] += _colsum(err * err) * (0.5 / D)

        @pl.when(i == nt - 1)
        def _():
            st_ref[3:4, :] = jnp.zeros((1, D), F32) + jnp.sum(st_ref[2:3, :])

    row = pl.BlockSpec((tr, D), lambda i: (i, 0))
    return _row_call(
        "final_fwd_bwd", body, (nt,), [row, row, row, _full(mod.shape), _full(gf.shape), row],
        [row, row, _full((8, D))],
        [jax.ShapeDtypeStruct((T, D), F32), jax.ShapeDtypeStruct((T, D), BF16), jax.ShapeDtypeStruct((8, D), F32)],
    )(x, o1, o2, mod, gf, target)


def ffn_act_bwd(upre, up, df, fw):
    _, T, F = upre.shape
    tr = _div(T, FFN_ROW_TILE, STRIP)
    cb = _div(F, MM_WIDE)
    nt = T // tr
    ns = tr // STRIP

    def body(x_ref, u_ref, df_ref, w_ref, dpre_ref, dw_ref, db_ref, carry):
        i = pl.program_id(1)

        @pl.when(i == 0)
        def _():
            carry[...] = jnp.zeros_like(carry)
            dw_ref[...] = jnp.zeros_like(dw_ref)
            db_ref[...] = jnp.zeros_like(db_ref)

        for c0 in range(0, cb, 2 * LANE):
            wd = min(2 * LANE, cb - c0)
            lanes = slice(c0, c0 + wd)
            w = [[w_ref[h, k:k + 1, lanes] for k in range(3)] for h in range(2)]

            def step(s, state):
                later, db, dw = state
                r0 = pl.multiple_of((ns - 1 - s) * STRIP, STRIP)
                rows = pl.ds(r0, STRIP)
                val, gt = u_ref[0, rows, lanes], u_ref[1, rows, lanes]
                sg = _sig(gt)
                dfv = df_ref[rows, lanes].astype(F32)
                dup = (dfv * (gt * sg), dfv * val * (sg * (1.0 + gt * (1.0 - sg))))
                new_db, new_dw = [], []
                for h in range(2):
                    dwin = jnp.concatenate([dup[h], later[h]], axis=0)
                    n = dwin.shape[0]
                    d1 = pltpu.roll(dwin, n - 1, 0)[0:STRIP]
                    d2 = pltpu.roll(dwin, n - 2, 0)[0:STRIP]
                    xs = x_ref[h, rows, lanes]
                    new_db.append(db[h] + _fold(dup[h]))
                    new_dw.append((dw[h][0] + _fold(d2 * xs), dw[h][1] + _fold(d1 * xs), dw[h][2] + _fold(dup[h] * xs)))
                    dpre_ref[h, rows, lanes] = (dup[h] * w[h][2] + d1 * w[h][1] + d2 * w[h][0]).astype(BF16)
                return tuple(dup[h][0:FFN_HALO] for h in range(2)), tuple(new_db), tuple(new_dw)

            zero = jnp.zeros((SUBLANE, wd), F32)
            state = ((carry[0, :, lanes], carry[1, :, lanes]), (zero, zero), ((zero,) * 3,) * 2)
            later, db, dw = lax.fori_loop(0, ns, step, state, unroll=STRIP_UNROLL)
            for h in range(2):
                carry[h, :, lanes] = later[h]
                db_ref[h, :, lanes] += _colsum(db[h])
                for k in range(3):
                    dw_ref[h, k:k + 1, lanes] += _colsum(dw[h][k])

    tile = pl.BlockSpec((2, tr, cb), lambda j, i: (0, nt - 1 - i, j))
    return _row_call(
        "ffn_act_bwd", body, (F // cb, nt),
        [tile, tile, pl.BlockSpec((tr, cb), lambda j, i: (nt - 1 - i, j)), pl.BlockSpec((2, 3, cb), lambda j, i: (0, 0, j))],
        [tile, pl.BlockSpec((2, 3, cb), lambda j, i: (0, 0, j)), pl.BlockSpec((2, 1, cb), lambda j, i: (0, 0, j))],
        [jax.ShapeDtypeStruct((2, T, F), BF16), jax.ShapeDtypeStruct((2, 3, F), F32), jax.ShapeDtypeStruct((2, 1, F), F32)],
        scratch=[pltpu.VMEM((2, FFN_HALO, cb), F32)],
        sem=("parallel", "arbitrary"))(upre, up, df, fw)


def norm2_bwd(dh2, x, o1, dx3, mod, g2, comm=None):
    T, D = x.shape
    tr = _div(T, ROW_TILE, SUBLANE)

    def body(dh_ref, x_ref, o1_ref, dx3_ref, mod_ref, g_ref, dx2_ref, do1_ref, st_ref):
        i = pl.program_id(0)
        gate1, scale = mod_ref[:, 2 * D:3 * D], mod_ref[:, 4 * D:5 * D]
        o1v = o1_ref[...]
        x2 = x_ref[...] + gate1 * o1v
        r = lax.rsqrt(_rowmean(x2 * x2) + EPS)
        xn = x2 * r
        dh = dh_ref[...].astype(F32)
        dxn = dh * (g_ref[...] * (1.0 + scale))
        dx2 = r * (dxn - xn * _rowmean(dxn * xn)) + dx3_ref[...]
        dx2_ref[...] = dx2
        do1_ref[...] = (dx2 * gate1).astype(BF16)

        @pl.when(i == 0)
        def _():
            st_ref[...] = jnp.zeros_like(st_ref)

        st_ref[0:1, :] += _colsum(dh)
        st_ref[1:2, :] += _colsum(dh * xn) * g_ref[...]
        st_ref[2:3, :] += _colsum(dh * xn) * (1.0 + scale)
        st_ref[3:4, :] += _colsum(dx2 * o1v)

    row = pl.BlockSpec((tr, D), lambda i: (i, 0))
    return _row_call(
        "norm2_bwd", body, (T // tr,), [row, row, row, row, _full(mod.shape), _full(g2.shape)],
        [row, row, _full((8, D))],
        [jax.ShapeDtypeStruct((T, D), F32), jax.ShapeDtypeStruct((T, D), BF16), jax.ShapeDtypeStruct((8, D), F32)],
        comm=comm)(dh2, x, o1, dx3, mod, g2)


def merge_bwd(dmerged, proj, y_a, y_b):
    T, D = y_a.shape
    tr = _div(T, ROW_TILE, SUBLANE)

    def body(dm_ref, g_ref, ya_ref, yb_ref, dya_ref, dyb_ref, dp_ref, db_ref):
        i = pl.program_id(0)
        dm = dm_ref[...].astype(F32)
        sa, sb = _sig(g_ref[:, 0:D]), _sig(g_ref[:, D:2 * D])
        dya_ref[...] = (dm * sa).astype(BF16)
        dyb_ref[...] = (dm * sb).astype(BF16)
        dga = dm * ya_ref[...].astype(F32) * (sa * (1.0 - sa))
        dgb = dm * yb_ref[...].astype(F32) * (sb * (1.0 - sb))
        dp_ref[:, 0:D] = dga.astype(BF16)
        dp_ref[:, D:2 * D] = dgb.astype(BF16)

        @pl.when(i == 0)
        def _():
            db_ref[...] = jnp.zeros_like(db_ref)

        db_ref[:, 0:D] += _colsum(dga)
        db_ref[:, D:2 * D] += _colsum(dgb)

    row = pl.BlockSpec((tr, D), lambda i: (i, 0))
    wide = pl.BlockSpec((tr, 2 * D), lambda i: (i, 1))
    return _row_call(
        "merge_bwd", body, (T // tr,), [row, wide, row, row], [row, row, wide, _full((1, 2 * D))],
        [jax.ShapeDtypeStruct((T, D), BF16), jax.ShapeDtypeStruct((T, D), BF16),
         jax.ShapeDtypeStruct((T, 4 * D), BF16), jax.ShapeDtypeStruct((1, 2 * D), F32)],
    )(dmerged, proj, y_a, y_b)


def mix_a_bwd(dasw, ac, proj, dproj, cw, lg, lb, comm=None):
    T, Dc = ac.shape
    K = cw.shape[0]
    tr = _div(T, ROW_TILE, CONV_HALO)
    nt = T // tr

    def body(dasw_ref, ac_ref, in_ref, dp_hbm, cw_ref, lg_ref, lb_ref,
             dp_ref, st_ref, dcw_ref, db_ref, abuf, dbuf, carry, da_buf):
        del dp_hbm
        i = pl.program_id(0)
        acv = ac_ref[...]
        mu = _rowmean(acv)
        cen = acv - mu
        rstd = lax.rsqrt(_rowmean(cen * cen) + EPS)
        y = cen * rstd
        aln = y * lg_ref[...] + lb_ref[...]
        sg = _sig(aln)
        daln = dasw_ref[...].astype(F32) * (sg * (1.0 + aln * (1.0 - sg)))
        dy = daln * lg_ref[...]
        dac = rstd * (dy - _rowmean(dy) - y * _rowmean(dy * y))

        @pl.when(i == 0)
        def _():
            carry[...] = jnp.zeros_like(carry)
            st_ref[...] = jnp.zeros_like(st_ref)
            dcw_ref[...] = jnp.zeros_like(dcw_ref)
            db_ref[...] = jnp.zeros_like(db_ref)

        st_ref[0:1, :] += _colsum(daln * y)
        st_ref[1:2, :] += _colsum(daln)
        st_ref[2:3, :] += _colsum(dac)
        val, gate = in_ref[:, 0:Dc], in_ref[:, Dc:2 * Dc]
        sgg = _sig(gate)
        abuf[...] = val * sgg
        dbuf[0:tr, :] = dac
        dbuf[tr:tr + CONV_HALO, :] = carry[...]
        carry[...] = dac[0:CONV_HALO, :]
        zero = jnp.zeros((SUBLANE, LANE), F32)
        for c0 in range(0, Dc, LANE):
            lanes = slice(c0, c0 + LANE)

            def step(s, dws):
                r0 = pl.multiple_of(s * CONV_HALO, CONV_HALO)
                dwin = dbuf[pl.ds(r0, 2 * CONV_HALO), lanes]
                dws = list(dws)
                a_piece = abuf[pl.ds(r0, CONV_HALO), lanes]
                acc = jnp.zeros((CONV_HALO, LANE), F32)
                for _, k, piece in _window_taps(dwin, [(k, K - 1 - k) for k in range(K)]):
                    acc = acc + piece * cw_ref[k:k + 1, lanes]
                    dws[k] = dws[k] + _fold(piece * a_piece)
                da_buf[pl.ds(r0, CONV_HALO), lanes] = acc
                return tuple(dws)

            dws = lax.fori_loop(0, tr // CONV_HALO, step, (zero,) * K)
            for k in range(K):
                dcw_ref[k:k + 1, lanes] += _colsum(dws[k])
        da = da_buf[...]
        dval = da * sgg
        dgate = da * val * (sgg * (1.0 - sgg))
        dp_ref[:, 0:Dc] = dval.astype(BF16)
        dp_ref[:, Dc:2 * Dc] = dgate.astype(BF16)
        db_ref[:, 0:Dc] += _colsum(dval)
        db_ref[:, Dc:2 * Dc] += _colsum(dgate)

    row = pl.BlockSpec((tr, Dc), lambda i: (nt - 1 - i, 0))
    wide = pl.BlockSpec((tr, 2 * Dc), lambda i: (nt - 1 - i, 0))
    return _row_call(
        "mix_a_bwd", body, (nt,),
        [row, row, wide, pl.BlockSpec(memory_space=pl.ANY), _full(cw.shape), _full(lg.shape), _full(lb.shape)],
        [wide, _full((8, Dc)), _full((CONV_HALO, Dc)), _full((1, 2 * Dc))],
        [jax.ShapeDtypeStruct(dproj.shape, BF16), jax.ShapeDtypeStruct((8, Dc), F32),
         jax.ShapeDtypeStruct((CONV_HALO, Dc), F32), jax.ShapeDtypeStruct((1, 2 * Dc), F32)],
        scratch=[pltpu.VMEM((tr, Dc), F32), pltpu.VMEM((tr + CONV_HALO, Dc), F32),
                 pltpu.VMEM((CONV_HALO, Dc), F32), pltpu.VMEM((tr, Dc), F32)],
        aliases={3: 0}, comm=comm)(dasw, ac, proj, dproj, cw, lg, lb)


def mix_b_bwd(duv, proj, dproj, lg, lb, wsp, bsp_t, comm=None):
    T, Ds = duv.shape
    G = wsp.shape[0]
    hd = Ds // G
    tr = _div(T, ROW_TILE, CHUNK)
    nt = T // tr

    def body(duv_ref, s_ref, dp_hbm, lg_ref, lb_ref, w_ref, b_ref,
             dp_ref, st_ref, dws_ref, dbs_ref, db_ref, vs, dvln):
        del dp_hbm
        i = pl.program_id(0)

        @pl.when(i == 0)
        def _():
            st_ref[...] = jnp.zeros_like(st_ref)
            dws_ref[...] = jnp.zeros_like(dws_ref)
            dbs_ref[...] = jnp.zeros_like(dbs_ref)
            db_ref[...] = jnp.zeros_like(db_ref)

        upre, vpre = s_ref[:, 0:Ds], s_ref[:, Ds:2 * Ds]
        u, v = _gelu(upre), _gelu(vpre)
        mu = _rowmean(v)
        cen = v - mu
        rstd = lax.rsqrt(_rowmean(cen * cen) + EPS)
        yv = cen * rstd
        vln = (yv * lg_ref[...] + lb_ref[...]).astype(BF16)
        duvv = duv_ref[...].astype(F32)
        dvs = duvv * u
        dvs_b = dvs.astype(BF16)
        mask = _spatial_mask()
        for g in range(G):
            wg = jnp.where(mask, w_ref[g], 0.0).astype(BF16)
            cols = slice(g * hd, (g + 1) * hd)
            dws = jnp.zeros((CHUNK, CHUNK), F32)
            dbs = jnp.zeros((CHUNK, 1), F32)
            for n in range(tr // CHUNK):
                rows = slice(n * CHUNK, (n + 1) * CHUNK)
                vs[rows, cols] = jnp.dot(wg, vln[rows, cols], preferred_element_type=F32) + b_ref[:, g:g + 1]
                dvln[rows, cols] = lax.dot_general(wg, dvs_b[rows, cols], (((0,), (0,)), ((), ())),
                                                   preferred_element_type=F32)
                dws = dws + lax.dot_general(dvs_b[rows, cols], vln[rows, cols], (((1,), (1,)), ((), ())),
                                            preferred_element_type=F32)
                dbs = dbs + jnp.sum(dvs[rows, cols], axis=1, keepdims=True)
            dws_ref[g] += jnp.where(mask, dws, 0.0)
            dbs_ref[:, g:g + 1] += dbs
        dvl = dvln[...]
        st_ref[0:1, :] += _colsum(dvl * yv)
        st_ref[1:2, :] += _colsum(dvl)
        dyv = dvl * lg_ref[...]
        dv = rstd * (dyv - _rowmean(dyv) - yv * _rowmean(dyv * yv))
        dupre = duvv * vs[...] * _gelu_grad(upre)
        dvpre = dv * _gelu_grad(vpre)
        dp_ref[:, 0:Ds] = dupre.astype(BF16)
        dp_ref[:, Ds:2 * Ds] = dvpre.astype(BF16)
        db_ref[:, 0:Ds] += _colsum(dupre)
        db_ref[:, Ds:2 * Ds] += _colsum(dvpre)

    wide = pl.BlockSpec((tr, 2 * Ds), lambda i: (i, 1))
    return _row_call(
        "mix_b_bwd", body, (nt,),
        [pl.BlockSpec((tr, Ds), lambda i: (i, 0)), wide, pl.BlockSpec(memory_space=pl.ANY),
         _full(lg.shape), _full(lb.shape), _full(wsp.shape), _full(bsp_t.shape)],
        [wide, _full((8, Ds)), _full(wsp.shape), _full(bsp_t.shape), _full((1, 2 * Ds))],
        [jax.ShapeDtypeStruct(dproj.shape, BF16), jax.ShapeDtypeStruct((8, Ds), F32),
         jax.ShapeDtypeStruct(wsp.shape, F32), jax.ShapeDtypeStruct(bsp_t.shape, F32),
         jax.ShapeDtypeStruct((1, 2 * Ds), F32)],
        scratch=[pltpu.VMEM((tr, Ds), F32), pltpu.VMEM((tr, Ds), F32)],
        aliases={2: 0}, comm=comm)(duv, proj, dproj, lg, lb, wsp, bsp_t)


def norm1_bwd(dh1, x, dx2, mod, g1):
    T, D = x.shape
    tr = _div(T, ROW_TILE, SUBLANE)

    def body(dh_ref, x_ref, dx2_ref, mod_ref, g_ref, gx_ref, st_ref):
        i = pl.program_id(0)
        scale = mod_ref[:, D:2 * D]
        xv = x_ref[...]
        r = lax.rsqrt(_rowmean(xv * xv) + EPS)
        xn = xv * r
        dh = dh_ref[...].astype(F32)
        dxn = dh * (g_ref[...] * (1.0 + scale))
        gx_ref[...] = r * (dxn - xn * _rowmean(dxn * xn)) + dx2_ref[...]

        @pl.when(i == 0)
        def _():
            st_ref[...] = jnp.zeros_like(st_ref)

        st_ref[0:1, :] += _colsum(dh)
        st_ref[1:2, :] += _colsum(dh * xn) * g_ref[...]
        st_ref[2:3, :] += _colsum(dh * xn) * (1.0 + scale)

    row = pl.BlockSpec((tr, D), lambda i: (i, 0))
    return _row_call(
        "norm1_bwd", body, (T // tr,), [row, row, row, _full(mod.shape), _full(g1.shape)], [row, _full((8, D))],
        [jax.ShapeDtypeStruct((T, D), F32), jax.ShapeDtypeStruct((8, D), F32)])(dh1, x, dx2, mod, g1)


def ada_fwd_local(c_all, w_ada, b_cols):
    B, D = c_all.shape
    Na = w_ada.shape[1]
    tn = _div(Na, 512)

    def body(c_ref, w_ref, b_ref, o_ref):
        cv = c_ref[...]
        act = (cv * _sig(cv)).astype(BF16)
        o_ref[...] = jnp.dot(act, w_ref[...].astype(BF16), preferred_element_type=F32) + b_ref[...]

    return _row_call(
        "ada_fwd_local", body, (Na // tn,),
        [_full(c_all.shape), pl.BlockSpec((D, tn), lambda j: (0, j)), pl.BlockSpec((1, tn), lambda j: (0, j))],
        pl.BlockSpec((B, tn), lambda j: (0, j)), jax.ShapeDtypeStruct((B, Na), F32), sem=("parallel",))(c_all, w_ada, b_cols)


def ada_bwd_local(c_all_t, dmod_all):
    D, B = c_all_t.shape
    Na = dmod_all.shape[1]
    tr = _div(D, 512, SUBLANE)

    def body(c_ref, d_ref, o_ref):
        cv = c_ref[...]
        act = cv * _sig(cv)
        acc = act[:, 0:1] * d_ref[0:1, :]
        for b in range(1, B):
            acc = acc + act[:, b:b + 1] * d_ref[b:b + 1, :]
        o_ref[...] = acc

    return _row_call(
        "ada_bwd_local", body, (D // tr,), [pl.BlockSpec((tr, B), lambda i: (i, 0)), _full(dmod_all.shape)],
        pl.BlockSpec((tr, Na), lambda i: (i, 0)), jax.ShapeDtypeStruct((D, Na), F32), sem=("parallel",))(c_all_t, dmod_all)


def _adam_update(w, m, v, g):
    mn = ADAM_B1 * m + (1.0 - ADAM_B1) * g
    vn = ADAM_B2 * v + (1.0 - ADAM_B2) * (g * g)
    bc1, bc2 = 1.0 - ADAM_B1 ** ADAM_STEP, 1.0 - ADAM_B2 ** ADAM_STEP
    return g, -ADAM_LR * ((mn / bc1) / (jnp.sqrt(vn / bc2) + ADAM_EPS) + ADAM_WD * w), mn, vn


def adamw(name, w, m, v, parts):
    R, C = w.shape
    tr = _div(R, max(SUBLANE, (1 << 18) // C // SUBLANE * SUBLANE), SUBLANE)
    n = len(parts)

    def body(*refs):
        w_ref, m_ref, v_ref = refs[:3]
        g_ref, d_ref, nm_ref, nv_ref = refs[3 + n:]
        g = refs[3][...].astype(F32)
        for p in refs[4:3 + n]:
            g = g + p[...].astype(F32)
        g_ref[...], d_ref[...], nm_ref[...], nv_ref[...] = _adam_update(w_ref[...], m_ref[...], v_ref[...], g)

    row = pl.BlockSpec((tr, C), lambda i: (i, 0))
    pspecs = [row if lead is None else pl.BlockSpec((None, tr, C), lambda i, lead=lead: (lead, i, 0)) for _, lead in parts]
    out = jax.ShapeDtypeStruct((R, C), F32)
    return _row_call(name, body, (R // tr,), [row, row, row] + pspecs, [row] * 4, [out] * 4, sem=("parallel",))(
        w, m, v, *[a for a, _ in parts])


def pair_add(name, g, r, idx):
    _, R, C = g.shape
    tr = _div(R, max(SUBLANE, (1 << 17) // C // SUBLANE * SUBLANE), SUBLANE)

    def body(idx_ref, g0, g1, g2, g3, r_ref, own_ref, tr_ref):
        del idx_ref
        own_ref[...] = g0[...].astype(F32) + r_ref[0].astype(F32)
        for m, gm in ((1, g1), (2, g2), (3, g3)):
            tr_ref[m - 1] = (gm[...].astype(F32) + r_ref[m].astype(F32)).astype(BF16)

    def gspec(m):
        return pl.BlockSpec((None, tr, C), lambda i, idx_ref: (idx_ref[m], i, 0))

    return pl.pallas_call(
        body, name=name,
        grid_spec=pltpu.PrefetchScalarGridSpec(
            num_scalar_prefetch=1, grid=(R // tr,),
            in_specs=[gspec(0), gspec(1), gspec(2), gspec(3), pl.BlockSpec((4, tr, C), lambda i, idx_ref: (0, i, 0))],
            out_specs=[pl.BlockSpec((tr, C), lambda i, idx_ref: (i, 0)),
                       pl.BlockSpec((3, tr, C), lambda i, idx_ref: (0, i, 0))]),
        out_shape=[jax.ShapeDtypeStruct((R, C), F32), jax.ShapeDtypeStruct((3, R, C), BF16)],
        compiler_params=_cp(("parallel",)))(idx, g, g, g, g, r)


def adamw_small(me, rows, g_rows, wsp, g_wsp, cwp, g_cw, fwp, g_fw):
    sizes = [w.shape[1] for w, _, _ in rows]
    offs = [sum(sizes[:i]) for i in range(len(rows))]
    k_cw, n_cw = cwp[0].shape
    n_fw = fwp[0].shape[1]
    per_half = g_fw.shape[-1] // n_fw
    groups = list(rows) + [wsp, cwp, fwp]
    n_p, n_r = len(groups), len(rows)

    def body(me_ref, g_rows_ref, g_wsp_ref, g_cw_ref, g_fw_ref, *refs):
        del me_ref
        wmv, outs = refs[:3 * n_p], refs[3 * n_p:]
        for p in range(n_p):
            if p < n_r:
                pick = lambda d, p=p: g_rows_ref[d, :, offs[p]:offs[p] + sizes[p]]
            elif p == n_r:
                pick = lambda d: g_wsp_ref[d]
            elif p == n_r + 1:
                pick = lambda d: g_cw_ref[d, 0:k_cw, :]
            else:
                pick = lambda d: g_fw_ref[d]
            g = pick(0)
            for d in range(1, NDEV):
                g = g + pick(d)
            res = _adam_update(wmv[3 * p][...], wmv[3 * p + 1][...], wmv[3 * p + 2][...], g)
            for slot in range(4):
                outs[4 * p + slot][...] = res[slot]

    def full(a):
        nd = len(a.shape)
        return pl.BlockSpec(a.shape, lambda i, me_ref: (0,) * nd)

    in_specs = [full(g_rows), full(g_wsp),
                pl.BlockSpec((NDEV, g_cw.shape[1], n_cw), lambda i, me_ref: (0, 0, me_ref[0])),
                pl.BlockSpec((NDEV, None, 3, n_fw), lambda i, me_ref: (0, me_ref[0] // per_half, 0, me_ref[0] % per_half))]
    in_specs += [full(a) for grp in groups for a in grp]
    out_shape = [jax.ShapeDtypeStruct(grp[0].shape, F32) for grp in groups for _ in range(4)]
    flat = pl.pallas_call(
        body, name="adamw_small",
        grid_spec=pltpu.PrefetchScalarGridSpec(num_scalar_prefetch=1, grid=(1,), in_specs=in_specs,
                                               out_specs=[full(s) for s in out_shape]),
        out_shape=out_shape, compiler_params=_cp(("arbitrary",)))(
            me, g_rows, g_wsp, g_cw, g_fw, *[a for grp in groups for a in grp])
    return [tuple(flat[4 * p:4 * p + 4]) for p in range(n_p)]


def _coords():
    return lax.axis_index("x"), lax.axis_index("y"), lax.axis_index("c")


def _flip(v, bit):
    return 1 - v if bit else v


def _comm_call(name, body, ins, out_shapes, n_sems):
    any_spec = pl.BlockSpec(memory_space=pl.ANY)
    return pl.pallas_call(
        body, name=name, in_specs=[any_spec] * len(ins), out_specs=[any_spec] * len(out_shapes), out_shape=out_shapes,
        scratch_shapes=[pltpu.SemaphoreType.DMA((s,)) for s in n_sems],
        compiler_params=pltpu.CompilerParams(has_side_effects=True))(*ins)


def gather_all(name, tensors):
    L = len(tensors)

    def body(*refs):
        ins, outs = refs[:L], refs[L:2 * L]
        send_sems, recv_sems, local_sems = refs[2 * L:]
        x, y, c = _coords()
        me = 4 * x + 2 * y + c
        local = [pltpu.make_async_copy(ins[l], outs[l].at[me], local_sems.at[l]) for l in range(L)]
        copies = []
        for l in range(L):
            for k in range(1, NDEV):
                peer = (_flip(x, k & 4), _flip(y, k & 2), _flip(c, k & 1))
                copies.append(pltpu.make_async_remote_copy(
                    src_ref=ins[l], dst_ref=outs[l].at[me], send_sem=send_sems.at[7 * l + k - 1],
                    recv_sem=recv_sems.at[7 * l + k - 1], device_id=peer, device_id_type=MESH))
        for cp in local + copies:
            cp.start()
        for cp in copies:
            cp.wait_recv()
        for cp in copies:
            cp.wait_send()
        for cp in local:
            cp.wait()

    outs = [jax.ShapeDtypeStruct((NDEV,) + t.shape, t.dtype) for t in tensors]
    return _comm_call(name, body, list(tensors), outs, (7 * L, 7 * L, L))


def exchange_rows(name, slabs):
    def body(in_ref, out_ref, send_sems, recv_sems, local_sem):
        x, y, c = _coords()
        me = 4 * x + 2 * y + c
        mine = pltpu.make_async_copy(in_ref.at[me], out_ref.at[me], local_sem.at[0])
        mine.start()
        copies = []
        for k in range(1, NDEV):
            px, py, pc = _flip(x, k & 4), _flip(y, k & 2), _flip(c, k & 1)
            copies.append(pltpu.make_async_remote_copy(
                src_ref=in_ref.at[4 * px + 2 * py + pc], dst_ref=out_ref.at[me], send_sem=send_sems.at[k - 1],
                recv_sem=recv_sems.at[k - 1], device_id=(px, py, pc), device_id_type=MESH))
        for cp in copies:
            cp.start()
        for cp in copies:
            cp.wait_recv()
        for cp in copies:
            cp.wait_send()
        mine.wait()

    return _comm_call(name, body, [slabs], [jax.ShapeDtypeStruct(slabs.shape, slabs.dtype)], (NDEV - 1, NDEV - 1, 1))[0]


def _run_comm(name, comm):
    n_i, n_o = len(comm.ins), len(comm.out_shapes)

    def body(*refs):
        start, finish = comm.plan(refs[:n_i], refs[n_i:n_i + n_o], refs[n_i + n_o:])
        start()
        finish()

    return _comm_call(name, body, comm.ins, comm.out_shapes, comm.n_sems)


def gather_comm(shards):
    L = len(shards)

    def plan(ins, outs, sems):
        send_sems, recv_sems, local_sems = sems
        x, y, c = _coords()
        sibling = (x, y, 1 - c)
        chips = [(_flip(x, m & 2), _flip(y, m & 1)) for m in (1, 2, 3)]

        def slab(px, py, pc):
            return 4 * px + 2 * py + pc

        def copy(l, k, block, to, src=None):
            dst = outs[l].at[slab(*block)]
            return pltpu.make_async_remote_copy(
                src_ref=dst if src is None else src, dst_ref=dst, send_sem=send_sems.at[7 * l + k],
                recv_sem=recv_sems.at[7 * l + k], device_id=to, device_id_type=MESH)

        local = [pltpu.make_async_copy(ins[l], outs[l].at[slab(x, y, c)], local_sems.at[l]) for l in range(L)]
        first = []
        for l in range(L):
            first.append(copy(l, 0, (x, y, c), sibling, src=ins[l]))
            first += [copy(l, 1 + j, (x, y, c), (*chip, c), src=ins[l]) for j, chip in enumerate(chips)]

        def start():
            for cp in local + first:
                cp.start()

        def finish():
            passed = []
            for j, chip in enumerate(chips):
                for l in range(L):
                    copy(l, 1 + j, (*chip, c), (x, y, c)).wait_recv()
                    fwd = copy(l, 4 + j, (*chip, c), sibling)
                    fwd.start()
                    passed.append(fwd)
            for l in range(L):
                copy(l, 0, sibling, (x, y, c)).wait_recv()
                for j, chip in enumerate(chips):
                    copy(l, 4 + j, (*chip, 1 - c), (x, y, c)).wait_recv()
            for cp in first + passed:
                cp.wait_send()
            for cp in local:
                cp.wait()

        return start, finish

    outs = [jax.ShapeDtypeStruct((NDEV,) + s.shape, s.dtype) for s in shards]
    return Comm(plan, shards, outs, (7 * L, 7 * L, L))


def _all_at_once(copies):
    def start():
        for cp in copies:
            cp.start()

    def finish():
        for cp in copies:
            cp.wait_recv()
        for cp in copies:
            cp.wait_send()

    return start, finish


def sibling_comm(grads):
    L = len(grads)

    def plan(ins, outs, sems):
        send_sems, recv_sems = sems
        x, y, c = _coords()
        copies = []
        for l in range(L):
            for m in range(4):
                qm = 2 * _flip(x, m & 2) + _flip(y, m & 1)
                copies.append(pltpu.make_async_remote_copy(
                    src_ref=ins[l].at[2 * qm + (1 - c)], dst_ref=outs[l].at[m], send_sem=send_sems.at[4 * l + m],
                    recv_sem=recv_sems.at[4 * l + m], device_id=(x, y, 1 - c), device_id_type=MESH))
        return _all_at_once(copies)

    outs = [jax.ShapeDtypeStruct((4,) + g.shape[1:], g.dtype) for g in grads]
    return Comm(plan, grads, outs, (4 * L, 4 * L))


def chip_comm(transits):
    L = len(transits)

    def plan(ins, outs, sems):
        send_sems, recv_sems = sems
        x, y, c = _coords()
        copies = []
        for l in range(L):
            for m in (1, 2, 3):
                copies.append(pltpu.make_async_remote_copy(
                    src_ref=ins[l].at[m - 1], dst_ref=outs[l].at[m - 1], send_sem=send_sems.at[3 * l + m - 1],
                    recv_sem=recv_sems.at[3 * l + m - 1], device_id=(_flip(x, m & 2), _flip(y, m & 1), c),
                    device_id_type=MESH))
        return _all_at_once(copies)

    outs = [jax.ShapeDtypeStruct(t.shape, t.dtype) for t in transits]
    return Comm(plan, transits, outs, (3 * L, 3 * L))


_SMALL_ROWS = ("b_ada", "norm1_g", "b_in", "conv_dw_b", "conv_ln_g", "conv_ln_b", "sgu_ln_g", "sgu_ln_b", "b_spatial",
               "norm2_g", "ffn_dw_b", "final_g")
_BIG = ("w_in", "w_conv_out", "w_sgu_out", "w_out", "w_up", "w_down")
_ORDER = ("w_ada", "b_ada", "norm1_g", "w_in", "b_in", "conv_dw_w", "conv_dw_b", "conv_ln_g", "conv_ln_b", "w_conv_out",
          "sgu_ln_g", "sgu_ln_b", "w_spatial", "b_spatial", "w_sgu_out", "w_out", "norm2_g", "w_up", "ffn_dw_w", "ffn_dw_b",
          "w_down", "final_g")


def kernel(x, c, w_ada, b_ada, norm1_g, w_in, b_in, conv_dw_w, conv_dw_b, conv_ln_g, conv_ln_b, w_conv_out, sgu_ln_g, sgu_ln_b, w_spatial, b_spatial, w_sgu_out, w_out, norm2_g, w_up, ffn_dw_w, ffn_dw_b, w_down, final_g, loss_target, m_w_ada, m_b_ada, m_norm1_g, m_w_in, m_b_in, m_conv_dw_w, m_conv_dw_b, m_conv_ln_g, m_conv_ln_b, m_w_conv_out, m_sgu_ln_g, m_sgu_ln_b, m_w_spatial, m_b_spatial, m_w_sgu_out, m_w_out, m_norm2_g, m_w_up, m_ffn_dw_w, m_ffn_dw_b, m_w_down, m_final_g, v_w_ada, v_b_ada, v_norm1_g, v_w_in, v_b_in, v_conv_dw_w, v_conv_dw_b, v_conv_ln_g, v_conv_ln_b, v_w_conv_out, v_sgu_ln_g, v_sgu_ln_b, v_w_spatial, v_b_spatial, v_w_sgu_out, v_w_out, v_norm2_g, v_w_up, v_ffn_dw_w, v_ffn_dw_b, v_w_down, v_final_g):
    W = dict(w_ada=w_ada, b_ada=b_ada, norm1_g=norm1_g, w_in=w_in, b_in=b_in, conv_dw_w=conv_dw_w, conv_dw_b=conv_dw_b,
             conv_ln_g=conv_ln_g, conv_ln_b=conv_ln_b, w_conv_out=w_conv_out, sgu_ln_g=sgu_ln_g, sgu_ln_b=sgu_ln_b,
             w_spatial=w_spatial, b_spatial=b_spatial, w_sgu_out=w_sgu_out, w_out=w_out, norm2_g=norm2_g, w_up=w_up,
             ffn_dw_w=ffn_dw_w, ffn_dw_b=ffn_dw_b, w_down=w_down, final_g=final_g)
    M = dict(w_ada=m_w_ada, b_ada=m_b_ada, norm1_g=m_norm1_g, w_in=m_w_in, b_in=m_b_in, conv_dw_w=m_conv_dw_w,
             conv_dw_b=m_conv_dw_b, conv_ln_g=m_conv_ln_g, conv_ln_b=m_conv_ln_b, w_conv_out=m_w_conv_out,
             sgu_ln_g=m_sgu_ln_g, sgu_ln_b=m_sgu_ln_b, w_spatial=m_w_spatial, b_spatial=m_b_spatial,
             w_sgu_out=m_w_sgu_out, w_out=m_w_out, norm2_g=m_norm2_g, w_up=m_w_up, ffn_dw_w=m_ffn_dw_w,
             ffn_dw_b=m_ffn_dw_b, w_down=m_w_down, final_g=m_final_g)
    V = dict(w_ada=v_w_ada, b_ada=v_b_ada, norm1_g=v_norm1_g, w_in=v_w_in, b_in=v_b_in, conv_dw_w=v_conv_dw_w,
             conv_dw_b=v_conv_dw_b, conv_ln_g=v_conv_ln_g, conv_ln_b=v_conv_ln_b, w_conv_out=v_w_conv_out,
             sgu_ln_g=v_sgu_ln_g, sgu_ln_b=v_sgu_ln_b, w_spatial=v_w_spatial, b_spatial=v_b_spatial,
             w_sgu_out=v_w_sgu_out, w_out=v_w_out, norm2_g=v_norm2_g, w_up=v_w_up, ffn_dw_w=v_ffn_dw_w,
             ffn_dw_b=v_ffn_dw_b, w_down=v_w_down, final_g=v_final_g)

    xs, tgt = x[0], loss_target[0]
    T, D = xs.shape
    Dc = conv_dw_w.shape[-1] * NDEV
    F = w_down.shape[1] * NDEV
    K31 = conv_dw_w.shape[1]
    G = w_spatial.shape[1]
    assert D == 2 * Dc and sgu_ln_g.shape[-1] == Dc and T % CHUNK == 0
    me = 4 * lax.axis_index("x") + 2 * lax.axis_index("y") + lax.axis_index("c")

    na = w_ada.shape[-1]
    c_all = gather_all("gather_c", [c])[0].reshape(NDEV, D)
    b_cols = lax.dynamic_slice(b_ada, (0, me * na), (1, na))
    mod_cols = ada_fwd_local(c_all, w_ada[0], b_cols)
    mod = exchange_rows("exchange_mod", mod_cols.reshape(NDEV, 1, na)).reshape(1, NDEV * na)

    wbf = {k: W[k][0].astype(BF16) for k in _BIG}
    fb = ffn_dw_b.reshape(2, 1, F)
    bsp_t = jnp.transpose(b_spatial[0])
    wsp = w_spatial[0]

    def plain(wb):
        return jnp.transpose(wb, (1, 0, 2)).reshape(1, wb.shape[1], NDEV * wb.shape[2])

    h1, (wb_in,) = pre_norm("pre_norm1", xs, mod, norm1_g, 0, comm=gather_comm([wbf["w_in"]]))
    proj, (wb_up, cw_g, fw_g) = mm_nn(
        "proj", h1, wb_in, F32, bias=b_in, comm=gather_comm([wbf["w_up"], conv_dw_w[0], ffn_dw_w[0]]))
    cw = jnp.transpose(cw_g, (1, 0, 2)).reshape(K31, Dc)
    fw = jnp.transpose(jnp.transpose(fw_g, (1, 0, 2)).reshape(3, 2, F), (1, 0, 2))
    wp_in, wp_up = to_plain("plain_w_in", wb_in), to_plain("plain_w_up", wb_up)
    (ac, asw), (wb_out, wb_co, wb_so) = mix_a_fwd(
        proj, cw, conv_dw_b, conv_ln_g, conv_ln_b,
        comm=gather_comm([wbf["w_out"], wbf["w_conv_out"], wbf["w_sgu_out"]]))
    wb_out = wb_out.reshape(1, D, D)
    wp_co, wp_so = plain(wb_co), plain(wb_so)
    uv = mix_b_fwd(proj, sgu_ln_g, sgu_ln_b, wsp, bsp_t)
    y_a = mm_nn("y_a", asw, wp_co, ACT)
    y_b = mm_nn("y_b", uv, wp_so, ACT)
    merged = merge_fwd(proj, y_a, y_b)
    o1 = mm_nn("o1", merged, wb_out, F32)
    h2 = pre_norm("pre_norm2", xs, mod, norm2_g, 1, o1=o1)
    upre, (wb_down,) = mm_nn("upre", h2, wb_up, F32, out_halves=True, tn_pref=MM_WIDE,
                             comm=gather_comm([wbf["w_down"]]))
    wb_down = wb_down.reshape(1, F, D)
    f, up = ffn_act_fwd(upre, fw, fb)
    o2 = mm_nn("o2", f, wb_down, F32)
    dx3, do2, st_f = final_fwd_bwd(xs, o1, o2, mod, final_g.reshape(1, D), tgt)
    loss = lax.psum(st_f[3, 0], MESH_AXES)

    xq, yq, cq = lax.axis_index("x"), lax.axis_index("y"), lax.axis_index("c")
    idx = jnp.stack([2 * (2 * _flip(xq, m & 2) + _flip(yq, m & 1)) + cq for m in range(4)]).astype(jnp.int32)
    own, transit, arrived = {}, {}, {}

    def add_pairs(keys, full, from_sibling):
        for k, g_full, r in zip(keys, full, from_sibling):
            own[k], transit[k] = pair_add("pair_add_" + k, g_full, r, idx)

    df = mm_nt("df", do2, wb_down, ACT, tko_pref=MM_WIDE)
    g_down = mm_tn("g_down", f, do2, 1, tko_pref=MM_WIDE).reshape(NDEV, F // NDEV, D)
    dupre, g_fw, g_fb = ffn_act_bwd(upre, up, df, fw)
    dh2 = mm_nt("dh2", dupre, wp_up, ACT, a_halves=True)
    g_up = mm_tn("g_up", h2, dupre, NDEV, g_halves=True, tn_pref=MM_WIDE)
    dx2, do1, st_2 = norm2_bwd(dh2, xs, o1, dx3, mod, norm2_g)
    dmerged, sib = mm_nt("dmerged", do1, wb_out, ACT, comm=sibling_comm([g_down, g_up]))
    add_pairs(("w_down", "w_up"), (g_down, g_up), sib)
    g_out = mm_tn("g_out", merged, do1, 1).reshape(NDEV, D // NDEV, D)
    dy_a, dy_b, dproj, db_g = merge_bwd(dmerged, proj, y_a, y_b)
    def blocked(g):
        return jnp.transpose(g.reshape(g.shape[1], NDEV, g.shape[2] // NDEV), (1, 0, 2))

    dasw = mm_nt("dasw", dy_a, wp_co, ACT)
    g_co = blocked(mm_tn("g_co", asw, dy_a, 1))
    duv = mm_nt("duv", dy_b, wp_so, ACT)
    g_so = blocked(mm_tn("g_so", uv, dy_b, 1))
    (dproj, st_a, g_cw, db_a), (arrived["w_up"],) = mix_a_bwd(
        dasw, ac, proj, dproj, cw, conv_ln_g, conv_ln_b, comm=chip_comm([transit["w_up"]]))
    (dproj, st_b, g_wsp, g_bsp_t, db_s), sib = mix_b_bwd(
        duv, proj, dproj, sgu_ln_g, sgu_ln_b, wsp, bsp_t, comm=sibling_comm([g_out, g_co, g_so]))
    add_pairs(("w_out", "w_conv_out", "w_sgu_out"), (g_out, g_co, g_so), sib)
    g_in, (arrived["w_down"],) = mm_tn("g_in", h1, dproj, NDEV, comm=chip_comm([transit["w_down"]]))
    add_pairs(("w_in",), (g_in,), _run_comm("sibling_w_in", sibling_comm([g_in])))
    late = ("w_out", "w_conv_out", "w_sgu_out", "w_in")
    dh1, got = mm_nt("dh1", dproj, wp_in, ACT, comm=chip_comm([transit[k] for k in late]))
    arrived.update(zip(late, got))
    grad_x, st_1 = norm1_bwd(dh1, xs, dx2, mod, norm1_g)

    dmod = jnp.concatenate([st_1[0], st_1[1], st_2[3], st_2[0], st_2[1], st_f[1]]).reshape(1, NDEV * na)
    dmod_all = exchange_rows("exchange_dmod", dmod.reshape(NDEV, 1, na)).reshape(NDEV, na)
    g_ada = ada_bwd_local(jnp.transpose(c_all), dmod_all)

    res = {}
    for k in _BIG:
        q = arrived[k]
        res[k] = adamw("adamw_" + k, W[k][0], M[k][0], V[k][0], [(own[k], None), (q, 0), (q, 1), (q, 2)])
    res["w_ada"] = adamw("adamw_w_ada", w_ada[0], m_w_ada[0], v_w_ada[0], [(g_ada, None)])

    g_row = dict(
        b_ada=dmod, norm1_g=st_1[2], b_in=jnp.concatenate([db_a, db_s, db_g], axis=1), conv_dw_b=st_a[2],
        conv_ln_g=st_a[0], conv_ln_b=st_a[1], sgu_ln_g=st_b[0], sgu_ln_b=st_b[1], b_spatial=jnp.transpose(g_bsp_t),
        norm2_g=st_2[2], ffn_dw_b=g_fb, final_g=st_f[0])
    packed = jnp.concatenate([g_row[k].reshape(1, -1) for k in _SMALL_ROWS], axis=1)
    g_rows, g_wsp_all, g_cw_all, g_fw_all = gather_all("gather_small", [packed, g_wsp.reshape(-1, CHUNK), g_cw, g_fw])
    rows = [tuple(S[k].reshape(1, -1) for S in (W, M, V)) for k in _SMALL_ROWS]
    wsp3 = tuple(S["w_spatial"].reshape(-1, CHUNK) for S in (W, M, V))
    cw3, fw3 = (tuple(S[k][0] for S in (W, M, V)) for k in ("conv_dw_w", "ffn_dw_w"))
    small = adamw_small(jnp.reshape(me, (1,)).astype(jnp.int32), rows, g_rows, wsp3, g_wsp_all, cw3, g_cw_all, fw3, g_fw_all)
    res.update(zip(_SMALL_ROWS + ("w_spatial", "conv_dw_w", "ffn_dw_w"), small))

    outs = [[], [], [], []]
    for k in _ORDER:
        for slot in range(4):
            outs[slot].append(res[k][slot].reshape(W[k].shape))
    return (loss, grad_x[None], *outs[0], *outs[1], *outs[2], *outs[3])
```

```python
import functools

import jax
import jax.numpy as jnp
from jax import lax
from jax.experimental import pallas as pl
from jax.experimental.pallas import tpu as pltpu

F32, BF16 = jnp.float32, jnp.bfloat16
ACT = BF16
NDEV = 8
MESH_AXES = ("x", "y", "c")
MESH = pl.DeviceIdType.MESH
EPS = 1e-6
CHUNK = 128
CONV_HALO = 32
FFN_HALO = 8
LANE, SUBLANE = 128, 8
ROW_TILE = 256
FFN_ROW_TILE = 512
STRIP = 32
STRIP_UNROLL = 2
MM_TILE = 1024
MM_WIDE = 1408
MM_DEEP = 2816
VMEM_LIMIT = 56 * 1024 * 1024
ADAM_LR, ADAM_B1, ADAM_B2, ADAM_EPS, ADAM_WD, ADAM_STEP = 0.001, 0.9, 0.999, 1e-08, 0.01, 10
SQRT_HALF = 0.7071067811865476
INV_SQRT_2PI = 0.3989422804014327


def _div(n, pref, mult=LANE):
    if n <= pref:
        return n
    for d in range(pref - pref % mult, 0, -mult):
        if n % d == 0:
            return d
    return n


def _cp(sem):
    return pltpu.CompilerParams(dimension_semantics=sem, vmem_limit_bytes=VMEM_LIMIT)


def _sig(v):
    return jax.nn.sigmoid(v)


def _gelu(v):
    return 0.5 * v * (1.0 + lax.erf(v * SQRT_HALF))


def _gelu_grad(v):
    return 0.5 * (1.0 + lax.erf(v * SQRT_HALF)) + v * (INV_SQRT_2PI * jnp.exp(-0.5 * v * v))


def _colsum(v):
    return jnp.sum(v, axis=0, keepdims=True)


def _rowmean(v):
    return jnp.mean(v, axis=-1, keepdims=True)


class Comm:
    def __init__(self, plan, ins, out_shapes, n_sems):
        self.plan, self.ins, self.out_shapes, self.n_sems = plan, list(ins), list(out_shapes), tuple(n_sems)


def _pcall(name, body, grid, in_specs, out_specs, out_shape, scratch=(), sem=None, aliases=None, comm=None):
    if comm is None:
        return pl.pallas_call(
            body, name=name, grid=grid, in_specs=in_specs, out_specs=out_specs, out_shape=out_shape,
            scratch_shapes=list(scratch), input_output_aliases=aliases or {},
            compiler_params=_cp(sem or ("arbitrary",) * len(grid)))
    single = not isinstance(out_shape, (list, tuple))
    own_specs, own_shapes = ([out_specs], [out_shape]) if single else (list(out_specs), list(out_shape))
    n_in, n_out, n_scr = len(in_specs), len(own_shapes), len(scratch)
    n_ci, n_co = len(comm.ins), len(comm.out_shapes)
    any_spec = pl.BlockSpec(memory_space=pl.ANY)

    def fused(*refs):
        ins, cins = refs[:n_in], refs[n_in:n_in + n_ci]
        outs = refs[n_in + n_ci:n_in + n_ci + n_out]
        couts = refs[n_in + n_ci + n_out:n_in + n_ci + n_out + n_co]
        scr = refs[n_in + n_ci + n_out + n_co:n_in + n_ci + n_out + n_co + n_scr]
        sems = refs[n_in + n_ci + n_out + n_co + n_scr:]
        first = functools.reduce(jnp.logical_and, [pl.program_id(d) == 0 for d in range(len(grid))])
        last = functools.reduce(jnp.logical_and, [pl.program_id(d) == grid[d] - 1 for d in range(len(grid))])

        @pl.when(first)
        def _():
            comm.plan(cins, couts, sems)[0]()

        body(*ins, *outs, *scr)

        @pl.when(last)
        def _():
            comm.plan(cins, couts, sems)[1]()

    call = pl.pallas_call(
        fused, name=name, grid=grid, in_specs=list(in_specs) + [any_spec] * n_ci,
        out_specs=own_specs + [any_spec] * n_co, out_shape=own_shapes + comm.out_shapes,
        scratch_shapes=list(scratch) + [pltpu.SemaphoreType.DMA((s,)) for s in comm.n_sems],
        input_output_aliases=aliases or {},
        compiler_params=pltpu.CompilerParams(dimension_semantics=("arbitrary",) * len(grid),
                                             vmem_limit_bytes=VMEM_LIMIT, has_side_effects=True))

    def run(*args):
        res = call(*args, *comm.ins)
        own = res[:n_out]
        return (own[0] if single else list(own)), list(res[n_out:])

    return run


def _matmul(name, a, b, *, grid, a_spec, b_spec, o_spec, out_shape, dims, acc_shape, bias=None, bias_spec=None, comm=None):
    nk = grid[2]

    def body(*refs):
        if bias is None:
            a_ref, b_ref, o_ref, *scr = refs
            bias_ref = None
        else:
            a_ref, b_ref, bias_ref, o_ref, *scr = refs
        part = lax.dot_general(a_ref[...], b_ref[...], (dims, ((), ())), preferred_element_type=F32)

        def finish(total):
            if bias_ref is not None:
                total = total + bias_ref[...]
            o_ref[...] = total.astype(o_ref.dtype)

        if nk == 1:
            finish(part)
        else:
            acc = scr[0]
            k = pl.program_id(2)

            @pl.when(k == 0)
            def _():
                acc[...] = part

            @pl.when(k > 0)
            def _():
                acc[...] += part

            @pl.when(k == nk - 1)
            def _():
                finish(acc[...])

    in_specs = [a_spec, b_spec] + ([bias_spec] if bias is not None else [])
    args = (a, b) + ((bias,) if bias is not None else ())
    return _pcall(name, body, grid, in_specs, o_spec, out_shape,
                  scratch=[pltpu.VMEM(acc_shape, F32)] if nk > 1 else [],
                  sem=("parallel", "parallel", "arbitrary"), comm=comm)(*args)


def mm_nn(name, a, wb, out_dtype, *, bias=None, out_halves=False, tn_pref=MM_TILE, comm=None):
    T, K = a.shape
    NB, _, Ns = wb.shape
    N = NB * Ns
    tm, tn, tk = _div(T, MM_TILE), _div(Ns, tn_pref), _div(K, MM_DEEP)
    npb, nj, nk = Ns // tn, N // tn, K // tk
    if out_halves:
        o_spec = pl.BlockSpec((None, tm, tn), lambda i, j, k: (j // (nj // 2), i, j % (nj // 2)))
        out_shape = jax.ShapeDtypeStruct((2, T, N // 2), out_dtype)
    else:
        o_spec = pl.BlockSpec((tm, tn), lambda i, j, k: (i, j))
        out_shape = jax.ShapeDtypeStruct((T, N), out_dtype)
    return _matmul(
        name, a, wb, grid=(T // tm, nj, nk),
        a_spec=pl.BlockSpec((tm, tk), lambda i, j, k: (i, k)),
        b_spec=pl.BlockSpec((None, tk, tn), lambda i, j, k: (j // npb, k, j % npb)),
        o_spec=o_spec, out_shape=out_shape, dims=((1,), (0,)), acc_shape=(tm, tn),
        bias=bias, bias_spec=pl.BlockSpec((1, tn), lambda i, j, k: (0, j)), comm=comm)


def to_plain(name, wb):
    NB, K, Ns = wb.shape
    tk = _div(K, MM_TILE, 2 * SUBLANE)

    def body(i_ref, o_ref):
        o_ref[...] = i_ref[...]

    return _pcall(name, body, (NB, K // tk), [pl.BlockSpec((None, tk, Ns), lambda b, k: (b, k, 0))],
                  pl.BlockSpec((None, tk, Ns), lambda b, k: (0, k, b)),
                  jax.ShapeDtypeStruct((1, K, NB * Ns), wb.dtype), sem=("parallel", "parallel"))(wb)


def mm_nt(name, a, wb, out_dtype, *, a_halves=False, tko_pref=MM_TILE, tc_pref=MM_DEEP, comm=None):
    NB, K, Ns = wb.shape
    T = a.shape[-2]
    span = Ns // 2 if a_halves and NB == 1 else Ns
    tm, tko, tc = _div(T, MM_TILE), _div(K, tko_pref), _div(span, tc_pref)
    cpb = Ns // tc
    nkk = NB * cpb
    if a_halves:
        a_spec = pl.BlockSpec((None, tm, tc), lambda i, j, k: (k // (nkk // 2), i, k % (nkk // 2)))
    else:
        a_spec = pl.BlockSpec((tm, tc), lambda i, j, k: (i, k))
    return _matmul(
        name, a, wb, grid=(T // tm, K // tko, nkk), a_spec=a_spec,
        b_spec=pl.BlockSpec((None, tko, tc), lambda i, j, k: (k // cpb, j, k % cpb)),
        o_spec=pl.BlockSpec((tm, tko), lambda i, j, k: (i, j)),
        out_shape=jax.ShapeDtypeStruct((T, K), out_dtype), dims=((1,), (1,)), acc_shape=(tm, tko), comm=comm)


def mm_tn(name, a, g, nb, *, g_halves=False, tko_pref=MM_TILE, tn_pref=MM_TILE, out_dtype=BF16, comm=None):
    T, K = a.shape
    N = g.shape[-1] * (2 if g_halves else 1)
    Ns = N // nb
    tt, tko, tn = _div(T, 2 * MM_TILE), _div(K, tko_pref), _div(Ns, tn_pref)
    npb, nj = Ns // tn, N // tn
    if g_halves:
        g_spec = pl.BlockSpec((None, tt, tn), lambda i, j, t: (j // (nj // 2), t, j % (nj // 2)))
    else:
        g_spec = pl.BlockSpec((tt, tn), lambda i, j, t: (t, j))
    return _matmul(
        name, a, g, grid=(K // tko, nj, T // tt),
        a_spec=pl.BlockSpec((tt, tko), lambda i, j, t: (t, i)), b_spec=g_spec,
        o_spec=pl.BlockSpec((None, tko, tn), lambda i, j, t: (j // npb, i, j % npb)),
        out_shape=jax.ShapeDtypeStruct((nb, K, Ns), out_dtype), dims=((0,), (0,)), acc_shape=(tko, tn), comm=comm)


def _row_call(name, body, grid, in_specs, out_specs, out_shape, scratch=(), sem=None, aliases=None, comm=None):
    return _pcall(name, body, grid, in_specs, out_specs, out_shape, scratch, sem, aliases, comm)


def _full(shape):
    nd = len(shape)
    return pl.BlockSpec(shape, lambda *idx: (0,) * nd)


def pre_norm(name, x, mod, g, which, o1=None, comm=None):
    T, D = x.shape
    tr = _div(T, ROW_TILE, SUBLANE)

    def body(*refs):
        if o1 is None:
            x_ref, mod_ref, g_ref, h_ref = refs
            xv = x_ref[...]
        else:
            x_ref, o1_ref, mod_ref, g_ref, h_ref = refs
            xv = x_ref[...] + mod_ref[:, 2 * D:3 * D] * o1_ref[...]
        shift = mod_ref[:, (3 * which) * D:(3 * which + 1) * D]
        scale = mod_ref[:, (3 * which + 1) * D:(3 * which + 2) * D]
        r = lax.rsqrt(_rowmean(xv * xv) + EPS)
        h_ref[...] = ((xv * r) * g_ref[...] * (1.0 + scale) + shift).astype(BF16)

    row = pl.BlockSpec((tr, D), lambda i: (i, 0))
    ins = [x] + ([o1] if o1 is not None else []) + [mod, g]
    specs = [row] * (1 if o1 is None else 2) + [_full(mod.shape), _full(g.shape)]
    return _row_call(name, body, (T // tr,), specs, row, jax.ShapeDtypeStruct((T, D), BF16), sem=("parallel",),
                     comm=comm)(*ins)


def _window_taps(win, taps):
    n = win.shape[0]
    for r in range(SUBLANE):
        group = [(i, tap, off) for i, (tap, off) in enumerate(taps) if off % SUBLANE == r]
        if not group:
            continue
        shifted = win if r == 0 else pltpu.roll(win, n - r, 0)
        for i, tap, off in group:
            assert 0 <= off and off + CONV_HALO <= n
            yield i, tap, shifted[off - r:off - r + CONV_HALO]


def mix_a_fwd(proj, cw, cb, lg, lb, comm=None):
    T = proj.shape[0]
    K, Dc = cw.shape
    tr = _div(T, ROW_TILE, CONV_HALO)
    hb = tr // CONV_HALO

    def body(val_ref, gate_ref, hval_ref, hgate_ref, cw_ref, cb_ref, lg_ref, lb_ref, ac_ref, asw_ref, buf):
        i = pl.program_id(0)
        hist = hval_ref[...] * _sig(hgate_ref[...])
        buf[0:CONV_HALO, :] = jnp.where(i > 0, hist, 0.0)
        buf[CONV_HALO:CONV_HALO + tr, :] = val_ref[...] * _sig(gate_ref[...])
        base = CONV_HALO - (K - 1)
        for c0 in range(0, Dc, LANE):
            lanes = slice(c0, c0 + LANE)

            def step(s, carry):
                r0 = pl.multiple_of(s * CONV_HALO, CONV_HALO)
                win = buf[pl.ds(r0, 2 * CONV_HALO), lanes]
                acc = jnp.zeros((CONV_HALO, LANE), F32)
                for _, k, piece in _window_taps(win, [(k, base + k) for k in range(K)]):
                    acc = acc + piece * cw_ref[k:k + 1, lanes]
                ac_ref[pl.ds(r0, CONV_HALO), lanes] = acc + cb_ref[:, lanes]
                return carry

            lax.fori_loop(0, tr // CONV_HALO, step, 0)
        ac = ac_ref[...]
        mu = _rowmean(ac)
        cen = ac - mu
        y = cen * lax.rsqrt(_rowmean(cen * cen) + EPS)
        aln = y * lg_ref[...] + lb_ref[...]
        asw_ref[...] = (aln * _sig(aln)).astype(BF16)

    def halo(col):
        return pl.BlockSpec((CONV_HALO, Dc), lambda i: (jnp.maximum(i * hb - 1, 0), col))

    row = pl.BlockSpec((tr, Dc), lambda i: (i, 0))
    return _row_call(
        "mix_a_fwd", body, (T // tr,),
        [row, pl.BlockSpec((tr, Dc), lambda i: (i, 1)), halo(0), halo(1),
         _full(cw.shape), _full(cb.shape), _full(lg.shape), _full(lb.shape)],
        [row, row], [jax.ShapeDtypeStruct((T, Dc), F32), jax.ShapeDtypeStruct((T, Dc), BF16)],
        scratch=[pltpu.VMEM((CONV_HALO + tr, Dc), F32)], sem=("parallel",), comm=comm)(proj, proj, proj, proj, cw, cb, lg, lb)


def _spatial_mask():
    t = lax.broadcasted_iota(jnp.int32, (CHUNK, CHUNK), 0)
    s = lax.broadcasted_iota(jnp.int32, (CHUNK, CHUNK), 1)
    return s <= t


def mix_b_fwd(proj, lg, lb, wsp, bsp_t):
    T = proj.shape[0]
    Ds = lg.shape[-1]
    G = wsp.shape[0]
    hd = Ds // G
    tr = _div(T, ROW_TILE, CHUNK)

    def body(u_ref, v_ref, lg_ref, lb_ref, w_ref, b_ref, uv_ref, vs):
        v = _gelu(v_ref[...])
        mu = _rowmean(v)
        cen = v - mu
        vln = (cen * lax.rsqrt(_rowmean(cen * cen) + EPS) * lg_ref[...] + lb_ref[...]).astype(BF16)
        mask = _spatial_mask()
        for g in range(G):
            wg = jnp.where(mask, w_ref[g], 0.0).astype(BF16)
            for n in range(tr // CHUNK):
                rows, cols = slice(n * CHUNK, (n + 1) * CHUNK), slice(g * hd, (g + 1) * hd)
                vs[rows, cols] = jnp.dot(wg, vln[rows, cols], preferred_element_type=F32) + b_ref[:, g:g + 1]
        uv_ref[...] = (_gelu(u_ref[...]) * vs[...]).astype(BF16)

    return _row_call(
        "mix_b_fwd", body, (T // tr,),
        [pl.BlockSpec((tr, Ds), lambda i: (i, 2)), pl.BlockSpec((tr, Ds), lambda i: (i, 3)),
         _full(lg.shape), _full(lb.shape), _full(wsp.shape), _full(bsp_t.shape)],
        pl.BlockSpec((tr, Ds), lambda i: (i, 0)), jax.ShapeDtypeStruct((T, Ds), BF16),
        scratch=[pltpu.VMEM((tr, Ds), F32)], sem=("parallel",))(proj, proj, lg, lb, wsp, bsp_t)


def merge_fwd(proj, y_a, y_b):
    T, D = y_a.shape
    tr = _div(T, ROW_TILE, SUBLANE)

    def body(g_ref, ya_ref, yb_ref, o_ref):
        o_ref[...] = (_sig(g_ref[:, 0:D]) * ya_ref[...].astype(F32)
                      + _sig(g_ref[:, D:2 * D]) * yb_ref[...].astype(F32)).astype(BF16)

    row = pl.BlockSpec((tr, D), lambda i: (i, 0))
    return _row_call("merge_fwd", body, (T // tr,), [pl.BlockSpec((tr, 2 * D), lambda i: (i, 1)), row, row], row,
                     jax.ShapeDtypeStruct((T, D), BF16), sem=("parallel",))(proj, y_a, y_b)


def _fold(v):
    acc = v[0:SUBLANE]
    for r in range(SUBLANE, v.shape[0], SUBLANE):
        acc = acc + v[r:r + SUBLANE]
    return acc


def _conv3(prev, cur, w):
    win = jnp.concatenate([prev, cur], axis=0)
    n = win.shape[0]
    x1 = pltpu.roll(win, 1, 0)[FFN_HALO:n]
    x2 = pltpu.roll(win, 2, 0)[FFN_HALO:n]
    return x2 * w[0] + x1 * w[1] + cur * w[2], (x2, x1, cur)


def _strip_taps(w_ref, b_ref, lanes):
    w = [[w_ref[h, k:k + 1, lanes] for k in range(3)] for h in range(2)]
    b = [b_ref[h, :, lanes] for h in range(2)]
    return w, b


def ffn_act_fwd(upre, fw, fb, comm=None):
    _, T, F = upre.shape
    tr = _div(T, FFN_ROW_TILE, STRIP)
    cb = _div(F, MM_WIDE)
    hb = tr // FFN_HALO
    ns = tr // STRIP

    def body(x_ref, h_ref, w_ref, b_ref, f_ref, u_ref):
        i = pl.program_id(0)
        for c0 in range(0, cb, 2 * LANE):
            lanes = slice(c0, c0 + min(2 * LANE, cb - c0))
            w, b = _strip_taps(w_ref, b_ref, lanes)

            def strip(r0, prev):
                up = [_conv3(prev(h), x_ref[h, pl.ds(r0, STRIP), lanes], w[h])[0] + b[h] for h in range(2)]
                f_ref[pl.ds(r0, STRIP), lanes] = (up[1] * _sig(up[1]) * up[0]).astype(BF16)
                for h in range(2):
                    u_ref[h, pl.ds(r0, STRIP), lanes] = up[h]

            strip(0, lambda h: jnp.where(i > 0, h_ref[h, :, lanes], 0.0))

            def step(s, carry):
                r0 = pl.multiple_of(s * STRIP, STRIP)
                strip(r0, lambda h: x_ref[h, pl.ds(pl.multiple_of(r0 - FFN_HALO, FFN_HALO), FFN_HALO), lanes])
                return carry

            lax.fori_loop(1, ns, step, 0, unroll=STRIP_UNROLL)

    return _row_call(
        "ffn_act_fwd", body, (T // tr, F // cb),
        [pl.BlockSpec((2, tr, cb), lambda i, j: (0, i, j)),
         pl.BlockSpec((2, FFN_HALO, cb), lambda i, j: (0, jnp.maximum(i * hb - 1, 0), j)),
         pl.BlockSpec((2, 3, cb), lambda i, j: (0, 0, j)), pl.BlockSpec((2, 1, cb), lambda i, j: (0, 0, j))],
        [pl.BlockSpec((tr, cb), lambda i, j: (i, j)), pl.BlockSpec((2, tr, cb), lambda i, j: (0, i, j))],
        [jax.ShapeDtypeStruct((T, F), BF16), jax.ShapeDtypeStruct((2, T, F), F32)],
        sem=("parallel", "parallel"), comm=comm)(upre, upre, fw, fb)


def final_fwd_bwd(x, o1, o2, mod, gf, target):
    T, D = x.shape
    tr = _div(T, ROW_TILE, SUBLANE)
    nt = T // tr

    def body(x_ref, o1_ref, o2_ref, mod_ref, gf_ref, t_ref, dx3_ref, do2_ref, st_ref):
        i = pl.program_id(0)
        gate1, gate2 = mod_ref[:, 2 * D:3 * D], mod_ref[:, 5 * D:6 * D]
        o2v = o2_ref[...]
        x3 = x_ref[...] + gate1 * o1_ref[...] + gate2 * o2v
        r = lax.rsqrt(_rowmean(x3 * x3) + EPS)
        xn = x3 * r
        err = xn * gf_ref[...] - t_ref[...]
        dy = err * (1.0 / D)
        dxn = dy * gf_ref[...]
        dx3 = r * (dxn - xn * _rowmean(dxn * xn))
        dx3_ref[...] = dx3
        do2_ref[...] = (dx3 * gate2).astype(BF16)

        @pl.when(i == 0)
        def _():
            st_ref[...] = jnp.zeros_like(st_ref)

        st_ref[0:1, :] += _colsum(dy * xn)
        st_ref[1:2, :] += _colsum(dx3 * o2v)
        st_ref[2:3, :] += _colsum(err * err) * (0.5 / D)

        @pl.when(i == nt - 1)
        def _():
            st_ref[3:4, :] = jnp.zeros((1, D), F32) + jnp.sum(st_ref[2:3, :])

    row = pl.BlockSpec((tr, D), lambda i: (i, 0))
    return _row_call(
        "final_fwd_bwd", body, (nt,), [row, row, row, _full(mod.shape), _full(gf.shape), row],
        [row, row, _full((8, D))],
        [jax.ShapeDtypeStruct((T, D), F32), jax.ShapeDtypeStruct((T, D), BF16), jax.ShapeDtypeStruct((8, D), F32)],
    )(x, o1, o2, mod, gf, target)


def ffn_act_bwd(upre, up, df, fw):
    _, T, F = upre.shape
    tr = _div(T, FFN_ROW_TILE, STRIP)
    cb = _div(F, MM_WIDE)
    nt = T // tr
    ns = tr // STRIP

    def body(x_ref, u_ref, df_ref, w_ref, dpre_ref, dw_ref, db_ref, carry):
        i = pl.program_id(1)

        @pl.when(i == 0)
        def _():
            carry[...] = jnp.zeros_like(carry)
            dw_ref[...] = jnp.zeros_like(dw_ref)
            db_ref[...] = jnp.zeros_like(db_ref)

        for c0 in range(0, cb, 2 * LANE):
            wd = min(2 * LANE, cb - c0)
            lanes = slice(c0, c0 + wd)
            w = [[w_ref[h, k:k + 1, lanes] for k in range(3)] for h in range(2)]

            def step(s, state):
                later, db, dw = state
                r0 = pl.multiple_of((ns - 1 - s) * STRIP, STRIP)
                rows = pl.ds(r0, STRIP)
                val, gt = u_ref[0, rows, lanes], u_ref[1, rows, lanes]
                sg = _sig(gt)
                dfv = df_ref[rows, lanes].astype(F32)
                dup = (dfv * (gt * sg), dfv * val * (sg * (1.0 + gt * (1.0 - sg))))
                new_db, new_dw = [], []
                for h in range(2):
                    dwin = jnp.concatenate([dup[h], later[h]], axis=0)
                    n = dwin.shape[0]
                    d1 = pltpu.roll(dwin, n - 1, 0)[0:STRIP]
                    d2 = pltpu.roll(dwin, n - 2, 0)[0:STRIP]
                    xs = x_ref[h, rows, lanes]
                    new_db.append(db[h] + _fold(dup[h]))
                    new_dw.append((dw[h][0] + _fold(d2 * xs), dw[h][1] + _fold(d1 * xs), dw[h][2] + _fold(dup[h] * xs)))
                    dpre_ref[h, rows, lanes] = (dup[h] * w[h][2] + d1 * w[h][1] + d2 * w[h][0]).astype(BF16)
                return tuple(dup[h][0:FFN_HALO] for h in range(2)), tuple(new_db), tuple(new_dw)

            zero = jnp.zeros((SUBLANE, wd), F32)
            state = ((carry[0, :, lanes], carry[1, :, lanes]), (zero, zero), ((zero,) * 3,) * 2)
            later, db, dw = lax.fori_loop(0, ns, step, state, unroll=STRIP_UNROLL)
            for h in range(2):
                carry[h, :, lanes] = later[h]
                db_ref[h, :, lanes] += _colsum(db[h])
                for k in range(3):
                    dw_ref[h, k:k + 1, lanes] += _colsum(dw[h][k])

    tile = pl.BlockSpec((2, tr, cb), lambda j, i: (0, nt - 1 - i, j))
    return _row_call(
        "ffn_act_bwd", body, (F // cb, nt),
        [tile, tile, pl.BlockSpec((tr, cb), lambda j, i: (nt - 1 - i, j)), pl.BlockSpec((2, 3, cb), lambda j, i: (0, 0, j))],
        [tile, pl.BlockSpec((2, 3, cb), lambda j, i: (0, 0, j)), pl.BlockSpec((2, 1, cb), lambda j, i: (0, 0, j))],
        [jax.ShapeDtypeStruct((2, T, F), BF16), jax.ShapeDtypeStruct((2, 3, F), F32), jax.ShapeDtypeStruct((2, 1, F), F32)],
        scratch=[pltpu.VMEM((2, FFN_HALO, cb), F32)],
        sem=("parallel", "arbitrary"))(upre, up, df, fw)


def norm2_bwd(dh2, x, o1, dx3, mod, g2, comm=None):
    T, D = x.shape
    tr = _div(T, ROW_TILE, SUBLANE)

    def body(dh_ref, x_ref, o1_ref, dx3_ref, mod_ref, g_ref, dx2_ref, do1_ref, st_ref):
        i = pl.program_id(0)
        gate1, scale = mod_ref[:, 2 * D:3 * D], mod_ref[:, 4 * D:5 * D]
        o1v = o1_ref[...]
        x2 = x_ref[...] + gate1 * o1v
        r = lax.rsqrt(_rowmean(x2 * x2) + EPS)
        xn = x2 * r
        dh = dh_ref[...].astype(F32)
        dxn = dh * (g_ref[...] * (1.0 + scale))
        dx2 = r * (dxn - xn * _rowmean(dxn * xn)) + dx3_ref[...]
        dx2_ref[...] = dx2
        do1_ref[...] = (dx2 * gate1).astype(BF16)

        @pl.when(i == 0)
        def _():
            st_ref[...] = jnp.zeros_like(st_ref)

        st_ref[0:1, :] += _colsum(dh)
        st_ref[1:2, :] += _colsum(dh * xn) * g_ref[...]
        st_ref[2:3, :] += _colsum(dh * xn) * (1.0 + scale)
        st_ref[3:4, :] += _colsum(dx2 * o1v)

    row = pl.BlockSpec((tr, D), lambda i: (i, 0))
    return _row_call(
        "norm2_bwd", body, (T // tr,), [row, row, row, row, _full(mod.shape), _full(g2.shape)],
        [row, row, _full((8, D))],
        [jax.ShapeDtypeStruct((T, D), F32), jax.ShapeDtypeStruct((T, D), BF16), jax.ShapeDtypeStruct((8, D), F32)],
        comm=comm)(dh2, x, o1, dx3, mod, g2)


def merge_bwd(dmerged, proj, y_a, y_b):
    T, D = y_a.shape
    tr = _div(T, ROW_TILE, SUBLANE)

    def body(dm_ref, g_ref, ya_ref, yb_ref, dya_ref, dyb_ref, dp_ref, db_ref):
        i = pl.program_id(0)
        dm = dm_ref[...].astype(F32)
        sa, sb = _sig(g_ref[:, 0:D]), _sig(g_ref[:, D:2 * D])
        dya_ref[...] = (dm * sa).astype(BF16)
        dyb_ref[...] = (dm * sb).astype(BF16)
        dga = dm * ya_ref[...].astype(F32) * (sa * (1.0 - sa))
        dgb = dm * yb_ref[...].astype(F32) * (sb * (1.0 - sb))
        dp_ref[:, 0:D] = dga.astype(BF16)
        dp_ref[:, D:2 * D] = dgb.astype(BF16)

        @pl.when(i == 0)
        def _():
            db_ref[...] = jnp.zeros_like(db_ref)

        db_ref[:, 0:D] += _colsum(dga)
        db_ref[:, D:2 * D] += _colsum(dgb)

    row = pl.BlockSpec((tr, D), lambda i: (i, 0))
    wide = pl.BlockSpec((tr, 2 * D), lambda i: (i, 1))
    return _row_call(
        "merge_bwd", body, (T // tr,), [row, wide, row, row], [row, row, wide, _full((1, 2 * D))],
        [jax.ShapeDtypeStruct((T, D), BF16), jax.ShapeDtypeStruct((T, D), BF16),
         jax.ShapeDtypeStruct((T, 4 * D), BF16), jax.ShapeDtypeStruct((1, 2 * D), F32)],
    )(dmerged, proj, y_a, y_b)


def mix_a_bwd(dasw, ac, proj, dproj, cw, lg, lb, comm=None):
    T, Dc = ac.shape
    K = cw.shape[0]
    tr = _div(T, ROW_TILE, CONV_HALO)
    nt = T // tr

    def body(dasw_ref, ac_ref, in_ref, dp_hbm, cw_ref, lg_ref, lb_ref,
             dp_ref, st_ref, dcw_ref, db_ref, abuf, dbuf, carry, da_buf):
        del dp_hbm
        i = pl.program_id(0)
        acv = ac_ref[...]
        mu = _rowmean(acv)
        cen = acv - mu
        rstd = lax.rsqrt(_rowmean(cen * cen) + EPS)
        y = cen * rstd
        aln = y * lg_ref[...] + lb_ref[...]
        sg = _sig(aln)
        daln = dasw_ref[...].astype(F32) * (sg * (1.0 + aln * (1.0 - sg)))
        dy = daln * lg_ref[...]
        dac = rstd * (dy - _rowmean(dy) - y * _rowmean(dy * y))

        @pl.when(i == 0)
        def _():
            carry[...] = jnp.zeros_like(carry)
            st_ref[...] = jnp.zeros_like(st_ref)
            dcw_ref[...] = jnp.zeros_like(dcw_ref)
            db_ref[...] = jnp.zeros_like(db_ref)

        st_ref[0:1, :] += _colsum(daln * y)
        st_ref[1:2, :] += _colsum(daln)
        st_ref[2:3, :] += _colsum(dac)
        val, gate = in_ref[:, 0:Dc], in_ref[:, Dc:2 * Dc]
        sgg = _sig(gate)
        abuf[...] = val * sgg
        dbuf[0:tr, :] = dac
        dbuf[tr:tr + CONV_HALO, :] = carry[...]
        carry[...] = dac[0:CONV_HALO, :]
        zero = jnp.zeros((SUBLANE, LANE), F32)
        for c0 in range(0, Dc, LANE):
            lanes = slice(c0, c0 + LANE)

            def step(s, dws):
                r0 = pl.multiple_of(s * CONV_HALO, CONV_HALO)
                dwin = dbuf[pl.ds(r0, 2 * CONV_HALO), lanes]
                dws = list(dws)
                a_piece = abuf[pl.ds(r0, CONV_HALO), lanes]
                acc = jnp.zeros((CONV_HALO, LANE), F32)
                for _, k, piece in _window_taps(dwin, [(k, K - 1 - k) for k in range(K)]):
                    acc = acc + piece * cw_ref[k:k + 1, lanes]
                    dws[k] = dws[k] + _fold(piece * a_piece)
                da_buf[pl.ds(r0, CONV_HALO), lanes] = acc
                return tuple(dws)

            dws = lax.fori_loop(0, tr // CONV_HALO, step, (zero,) * K)
            for k in range(K):
                dcw_ref[k:k + 1, lanes] += _colsum(dws[k])
        da = da_buf[...]
        dval = da * sgg
        dgate = da * val * (sgg * (1.0 - sgg))
        dp_ref[:, 0:Dc] = dval.astype(BF16)
        dp_ref[:, Dc:2 * Dc] = dgate.astype(BF16)
        db_ref[:, 0:Dc] += _colsum(dval)
        db_ref[:, Dc:2 * Dc] += _colsum(dgate)

    row = pl.BlockSpec((tr, Dc), lambda i: (nt - 1 - i, 0))
    wide = pl.BlockSpec((tr, 2 * Dc), lambda i: (nt - 1 - i, 0))
    return _row_call(
        "mix_a_bwd", body, (nt,),
        [row, row, wide, pl.BlockSpec(memory_space=pl.ANY), _full(cw.shape), _full(lg.shape), _full(lb.shape)],
        [wide, _full((8, Dc)), _full((CONV_HALO, Dc)), _full((1, 2 * Dc))],
        [jax.ShapeDtypeStruct(dproj.shape, BF16), jax.ShapeDtypeStruct((8, Dc), F32),
         jax.ShapeDtypeStruct((CONV_HALO, Dc), F32), jax.ShapeDtypeStruct((1, 2 * Dc), F32)],
        scratch=[pltpu.VMEM((tr, Dc), F32), pltpu.VMEM((tr + CONV_HALO, Dc), F32),
                 pltpu.VMEM((CONV_HALO, Dc), F32), pltpu.VMEM((tr, Dc), F32)],
        aliases={3: 0}, comm=comm)(dasw, ac, proj, dproj, cw, lg, lb)


def mix_b_bwd(duv, proj, dproj, lg, lb, wsp, bsp_t, comm=None):
    T, Ds = duv.shape
    G = wsp.shape[0]
    hd = Ds // G
    tr = _div(T, ROW_TILE, CHUNK)
    nt = T // tr

    def body(duv_ref, s_ref, dp_hbm, lg_ref, lb_ref, w_ref, b_ref,
             dp_ref, st_ref, dws_ref, dbs_ref, db_ref, vs, dvln):
        del dp_hbm
        i = pl.program_id(0)

        @pl.when(i == 0)
        def _():
            st_ref[...] = jnp.zeros_like(st_ref)
            dws_ref[...] = jnp.zeros_like(dws_ref)
            dbs_ref[...] = jnp.zeros_like(dbs_ref)
            db_ref[...] = jnp.zeros_like(db_ref)

        upre, vpre = s_ref[:, 0:Ds], s_ref[:, Ds:2 * Ds]
        u, v = _gelu(upre), _gelu(vpre)
        mu = _rowmean(v)
        cen = v - mu
        rstd = lax.rsqrt(_rowmean(cen * cen) + EPS)
        yv = cen * rstd
        vln = (yv * lg_ref[...] + lb_ref[...]).astype(BF16)
        duvv = duv_ref[...].astype(F32)
        dvs = duvv * u
        dvs_b = dvs.astype(BF16)
        mask = _spatial_mask()
        for g in range(G):
            wg = jnp.where(mask, w_ref[g], 0.0).astype(BF16)
            cols = slice(g * hd, (g + 1) * hd)
            dws = jnp.zeros((CHUNK, CHUNK), F32)
            dbs = jnp.zeros((CHUNK, 1), F32)
            for n in range(tr // CHUNK):
                rows = slice(n * CHUNK, (n + 1) * CHUNK)
                vs[rows, cols] = jnp.dot(wg, vln[rows, cols], preferred_element_type=F32) + b_ref[:, g:g + 1]
                dvln[rows, cols] = lax.dot_general(wg, dvs_b[rows, cols], (((0,), (0,)), ((), ())),
                                                   preferred_element_type=F32)
                dws = dws + lax.dot_general(dvs_b[rows, cols], vln[rows, cols], (((1,), (1,)), ((), ())),
                                            preferred_element_type=F32)
                dbs = dbs + jnp.sum(dvs[rows, cols], axis=1, keepdims=True)
            dws_ref[g] += jnp.where(mask, dws, 0.0)
            dbs_ref[:, g:g + 1] += dbs
        dvl = dvln[...]
        st_ref[0:1, :] += _colsum(dvl * yv)
        st_ref[1:2, :] += _colsum(dvl)
        dyv = dvl * lg_ref[...]
        dv = rstd * (dyv - _rowmean(dyv) - yv * _rowmean(dyv * yv))
        dupre = duvv * vs[...] * _gelu_grad(upre)
        dvpre = dv * _gelu_grad(vpre)
        dp_ref[:, 0:Ds] = dupre.astype(BF16)
        dp_ref[:, Ds:2 * Ds] = dvpre.astype(BF16)
        db_ref[:, 0:Ds] += _colsum(dupre)
        db_ref[:, Ds:2 * Ds] += _colsum(dvpre)

    wide = pl.BlockSpec((tr, 2 * Ds), lambda i: (i, 1))
    return _row_call(
        "mix_b_bwd", body, (nt,),
        [pl.BlockSpec((tr, Ds), lambda i: (i, 0)), wide, pl.BlockSpec(memory_space=pl.ANY),
         _full(lg.shape), _full(lb.shape), _full(wsp.shape), _full(bsp_t.shape)],
        [wide, _full((8, Ds)), _full(wsp.shape), _full(bsp_t.shape), _full((1, 2 * Ds))],
        [jax.ShapeDtypeStruct(dproj.shape, BF16), jax.ShapeDtypeStruct((8, Ds), F32),
         jax.ShapeDtypeStruct(wsp.shape, F32), jax.ShapeDtypeStruct(bsp_t.shape, F32),
         jax.ShapeDtypeStruct((1, 2 * Ds), F32)],
        scratch=[pltpu.VMEM((tr, Ds), F32), pltpu.VMEM((tr, Ds), F32)],
        aliases={2: 0}, comm=comm)(duv, proj, dproj, lg, lb, wsp, bsp_t)


def norm1_bwd(dh1, x, dx2, mod, g1):
    T, D = x.shape
    tr = _div(T, ROW_TILE, SUBLANE)

    def body(dh_ref, x_ref, dx2_ref, mod_ref, g_ref, gx_ref, st_ref):
        i = pl.program_id(0)
        scale = mod_ref[:, D:2 * D]
        xv = x_ref[...]
        r = lax.rsqrt(_rowmean(xv * xv) + EPS)
        xn = xv * r
        dh = dh_ref[...].astype(F32)
        dxn = dh * (g_ref[...] * (1.0 + scale))
        gx_ref[...] = r * (dxn - xn * _rowmean(dxn * xn)) + dx2_ref[...]

        @pl.when(i == 0)
        def _():
            st_ref[...] = jnp.zeros_like(st_ref)

        st_ref[0:1, :] += _colsum(dh)
        st_ref[1:2, :] += _colsum(dh * xn) * g_ref[...]
        st_ref[2:3, :] += _colsum(dh * xn) * (1.0 + scale)

    row = pl.BlockSpec((tr, D), lambda i: (i, 0))
    return _row_call(
        "norm1_bwd", body, (T // tr,), [row, row, row, _full(mod.shape), _full(g1.shape)], [row, _full((8, D))],
        [jax.ShapeDtypeStruct((T, D), F32), jax.ShapeDtypeStruct((8, D), F32)])(dh1, x, dx2, mod, g1)


def ada_fwd_local(c_all, w_ada, b_cols):
    B, D = c_all.shape
    Na = w_ada.shape[1]
    tn = _div(Na, 512)

    def body(c_ref, w_ref, b_ref, o_ref):
        cv = c_ref[...]
        act = (cv * _sig(cv)).astype(BF16)
        o_ref[...] = jnp.dot(act, w_ref[...].astype(BF16), preferred_element_type=F32) + b_ref[...]

    return _row_call(
        "ada_fwd_local", body, (Na // tn,),
        [_full(c_all.shape), pl.BlockSpec((D, tn), lambda j: (0, j)), pl.BlockSpec((1, tn), lambda j: (0, j))],
        pl.BlockSpec((B, tn), lambda j: (0, j)), jax.ShapeDtypeStruct((B, Na), F32), sem=("parallel",))(c_all, w_ada, b_cols)


def ada_bwd_local(c_all_t, dmod_all):
    D, B = c_all_t.shape
    Na = dmod_all.shape[1]
    tr = _div(D, 512, SUBLANE)

    def body(c_ref, d_ref, o_ref):
        cv = c_ref[...]
        act = cv * _sig(cv)
        acc = act[:, 0:1] * d_ref[0:1, :]
        for b in range(1, B):
            acc = acc + act[:, b:b + 1] * d_ref[b:b + 1, :]
        o_ref[...] = acc

    return _row_call(
        "ada_bwd_local", body, (D // tr,), [pl.BlockSpec((tr, B), lambda i: (i, 0)), _full(dmod_all.shape)],
        pl.BlockSpec((tr, Na), lambda i: (i, 0)), jax.ShapeDtypeStruct((D, Na), F32), sem=("parallel",))(c_all_t, dmod_all)


def _adam_update(w, m, v, g):
    mn = ADAM_B1 * m + (1.0 - ADAM_B1) * g
    vn = ADAM_B2 * v + (1.0 - ADAM_B2) * (g * g)
    bc1, bc2 = 1.0 - ADAM_B1 ** ADAM_STEP, 1.0 - ADAM_B2 ** ADAM_STEP
    return g, -ADAM_LR * ((mn / bc1) / (jnp.sqrt(vn / bc2) + ADAM_EPS) + ADAM_WD * w), mn, vn


def adamw(name, w, m, v, parts):
    R, C = w.shape
    tr = _div(R, max(SUBLANE, (1 << 18) // C // SUBLANE * SUBLANE), SUBLANE)
    n = len(parts)

    def body(*refs):
        w_ref, m_ref, v_ref = refs[:3]
        g_ref, d_ref, nm_ref, nv_ref = refs[3 + n:]
        g = refs[3][...].astype(F32)
        for p in refs[4:3 + n]:
            g = g + p[...].astype(F32)
        g_ref[...], d_ref[...], nm_ref[...], nv_ref[...] = _adam_update(w_ref[...], m_ref[...], v_ref[...], g)

    row = pl.BlockSpec((tr, C), lambda i: (i, 0))
    pspecs = [row if lead is None else pl.BlockSpec((None, tr, C), lambda i, lead=lead: (lead, i, 0)) for _, lead in parts]
    out = jax.ShapeDtypeStruct((R, C), F32)
    return _row_call(name, body, (R // tr,), [row, row, row] + pspecs, [row] * 4, [out] * 4, sem=("parallel",))(
        w, m, v, *[a for a, _ in parts])


def pair_add(name, g, r, idx):
    _, R, C = g.shape
    tr = _div(R, max(SUBLANE, (1 << 17) // C // SUBLANE * SUBLANE), SUBLANE)

    def body(idx_ref, g0, g1, g2, g3, r_ref, own_ref, tr_ref):
        del idx_ref
        own_ref[...] = g0[...].astype(F32) + r_ref[0].astype(F32)
        for m, gm in ((1, g1), (2, g2), (3, g3)):
            tr_ref[m - 1] = (gm[...].astype(F32) + r_ref[m].astype(F32)).astype(BF16)

    def gspec(m):
        return pl.BlockSpec((None, tr, C), lambda i, idx_ref: (idx_ref[m], i, 0))

    return pl.pallas_call(
        body, name=name,
        grid_spec=pltpu.PrefetchScalarGridSpec(
            num_scalar_prefetch=1, grid=(R // tr,),
            in_specs=[gspec(0), gspec(1), gspec(2), gspec(3), pl.BlockSpec((4, tr, C), lambda i, idx_ref: (0, i, 0))],
            out_specs=[pl.BlockSpec((tr, C), lambda i, idx_ref: (i, 0)),
                       pl.BlockSpec((3, tr, C), lambda i, idx_ref: (0, i, 0))]),
        out_shape=[jax.ShapeDtypeStruct((R, C), F32), jax.ShapeDtypeStruct((3, R, C), BF16)],
        compiler_params=_cp(("parallel",)))(idx, g, g, g, g, r)


def adamw_small(me, rows, g_rows, wsp, g_wsp, cwp, g_cw, fwp, g_fw):
    sizes = [w.shape[1] for w, _, _ in rows]
    offs = [sum(sizes[:i]) for i in range(len(rows))]
    k_cw, n_cw = cwp[0].shape
    n_fw = fwp[0].shape[1]
    per_half = g_fw.shape[-1] // n_fw
    groups = list(rows) + [wsp, cwp, fwp]
    n_p, n_r = len(groups), len(rows)

    def body(me_ref, g_rows_ref, g_wsp_ref, g_cw_ref, g_fw_ref, *refs):
        del me_ref
        wmv, outs = refs[:3 * n_p], refs[3 * n_p:]
        for p in range(n_p):
            if p < n_r:
                pick = lambda d, p=p: g_rows_ref[d, :, offs[p]:offs[p] + sizes[p]]
            elif p == n_r:
                pick = lambda d: g_wsp_ref[d]
            elif p == n_r + 1:
                pick = lambda d: g_cw_ref[d, 0:k_cw, :]
            else:
                pick = lambda d: g_fw_ref[d]
            g = pick(0)
            for d in range(1, NDEV):
                g = g + pick(d)
            res = _adam_update(wmv[3 * p][...], wmv[3 * p + 1][...], wmv[3 * p + 2][...], g)
            for slot in range(4):
                outs[4 * p + slot][...] = res[slot]

    def full(a):
        nd = len(a.shape)
        return pl.BlockSpec(a.shape, lambda i, me_ref: (0,) * nd)

    in_specs = [full(g_rows), full(g_wsp),
                pl.BlockSpec((NDEV, g_cw.shape[1], n_cw), lambda i, me_ref: (0, 0, me_ref[0])),
                pl.BlockSpec((NDEV, None, 3, n_fw), lambda i, me_ref: (0, me_ref[0] // per_half, 0, me_ref[0] % per_half))]
    in_specs += [full(a) for grp in groups for a in grp]
    out_shape = [jax.ShapeDtypeStruct(grp[0].shape, F32) for grp in groups for _ in range(4)]
    flat = pl.pallas_call(
        body, name="adamw_small",
        grid_spec=pltpu.PrefetchScalarGridSpec(num_scalar_prefetch=1, grid=(1,), in_specs=in_specs,
                                               out_specs=[full(s) for s in out_shape]),
        out_shape=out_shape, compiler_params=_cp(("arbitrary",)))(
            me, g_rows, g_wsp, g_cw, g_fw, *[a for grp in groups for a in grp])
    return [tuple(flat[4 * p:4 * p + 4]) for p in range(n_p)]


def _coords():
    return lax.axis_index("x"), lax.axis_index("y"), lax.axis_index("c")


def _flip(v, bit):
    return 1 - v if bit else v


def _comm_call(name, body, ins, out_shapes, n_sems):
    any_spec = pl.BlockSpec(memory_space=pl.ANY)
    return pl.pallas_call(
        body, name=name, in_specs=[any_spec] * len(ins), out_specs=[any_spec] * len(out_shapes), out_shape=out_shapes,
        scratch_shapes=[pltpu.SemaphoreType.DMA((s,)) for s in n_sems],
        compiler_params=pltpu.CompilerParams(has_side_effects=True))(*ins)


def gather_all(name, tensors):
    L = len(tensors)

    def body(*refs):
        ins, outs = refs[:L], refs[L:2 * L]
        send_sems, recv_sems, local_sems = refs[2 * L:]
        x, y, c = _coords()
        me = 4 * x + 2 * y + c
        local = [pltpu.make_async_copy(ins[l], outs[l].at[me], local_sems.at[l]) for l in range(L)]
        copies = []
        for l in range(L):
            for k in range(1, NDEV):
                peer = (_flip(x, k & 4), _flip(y, k & 2), _flip(c, k & 1))
                copies.append(pltpu.make_async_remote_copy(
                    src_ref=ins[l], dst_ref=outs[l].at[me], send_sem=send_sems.at[7 * l + k - 1],
                    recv_sem=recv_sems.at[7 * l + k - 1], device_id=peer, device_id_type=MESH))
        for cp in local + copies:
            cp.start()
        for cp in copies:
            cp.wait_recv()
        for cp in copies:
            cp.wait_send()
        for cp in local:
            cp.wait()

    outs = [jax.ShapeDtypeStruct((NDEV,) + t.shape, t.dtype) for t in tensors]
    return _comm_call(name, body, list(tensors), outs, (7 * L, 7 * L, L))


def exchange_rows(name, slabs):
    def body(in_ref, out_ref, send_sems, recv_sems, local_sem):
        x, y, c = _coords()
        me = 4 * x + 2 * y + c
        mine = pltpu.make_async_copy(in_ref.at[me], out_ref.at[me], local_sem.at[0])
        mine.start()
        copies = []
        for k in range(1, NDEV):
            px, py, pc = _flip(x, k & 4), _flip(y, k & 2), _flip(c, k & 1)
            copies.append(pltpu.make_async_remote_copy(
                src_ref=in_ref.at[4 * px + 2 * py + pc], dst_ref=out_ref.at[me], send_sem=send_sems.at[k - 1],
                recv_sem=recv_sems.at[k - 1], device_id=(px, py, pc), device_id_type=MESH))
        for cp in copies:
            cp.start()
        for cp in copies:
            cp.wait_recv()
        for cp in copies:
            cp.wait_send()
        mine.wait()

    return _comm_call(name, body, [slabs], [jax.ShapeDtypeStruct(slabs.shape, slabs.dtype)], (NDEV - 1, NDEV - 1, 1))[0]


def _run_comm(name, comm):
    n_i, n_o = len(comm.ins), len(comm.out_shapes)

    def body(*refs):
        start, finish = comm.plan(refs[:n_i], refs[n_i:n_i + n_o], refs[n_i + n_o:])
        start()
        finish()

    return _comm_call(name, body, comm.ins, comm.out_shapes, comm.n_sems)


def gather_comm(shards):
    L = len(shards)

    def plan(ins, outs, sems):
        send_sems, recv_sems, local_sems = sems
        x, y, c = _coords()
        sibling = (x, y, 1 - c)
        chips = [(_flip(x, m & 2), _flip(y, m & 1)) for m in (1, 2, 3)]

        def slab(px, py, pc):
            return 4 * px + 2 * py + pc

        def copy(l, k, block, to, src=None):
            dst = outs[l].at[slab(*block)]
            return pltpu.make_async_remote_copy(
                src_ref=dst if src is None else src, dst_ref=dst, send_sem=send_sems.at[7 * l + k],
                recv_sem=recv_sems.at[7 * l + k], device_id=to, device_id_type=MESH)

        local = [pltpu.make_async_copy(ins[l], outs[l].at[slab(x, y, c)], local_sems.at[l]) for l in range(L)]
        first = []
        for l in range(L):
            first.append(copy(l, 0, (x, y, c), sibling, src=ins[l]))
            first += [copy(l, 1 + j, (x, y, c), (*chip, c), src=ins[l]) for j, chip in enumerate(chips)]

        def start():
            for cp in local + first:
                cp.start()

        def finish():
            passed = []
            for j, chip in enumerate(chips):
                for l in range(L):
                    copy(l, 1 + j, (*chip, c), (x, y, c)).wait_recv()
                    fwd = copy(l, 4 + j, (*chip, c), sibling)
                    fwd.start()
                    passed.append(fwd)
            for l in range(L):
                copy(l, 0, sibling, (x, y, c)).wait_recv()
                for j, chip in enumerate(chips):
                    copy(l, 4 + j, (*chip, 1 - c), (x, y, c)).wait_recv()
            for cp in first + passed:
                cp.wait_send()
            for cp in local:
                cp.wait()

        return start, finish

    outs = [jax.ShapeDtypeStruct((NDEV,) + s.shape, s.dtype) for s in shards]
    return Comm(plan, shards, outs, (7 * L, 7 * L, L))


def _all_at_once(copies):
    def start():
        for cp in copies:
            cp.start()

    def finish():
        for cp in copies:
            cp.wait_recv()
        for cp in copies:
            cp.wait_send()

    return start, finish


def sibling_comm(grads):
    L = len(grads)

    def plan(ins, outs, sems):
        send_sems, recv_sems = sems
        x, y, c = _coords()
        copies = []
        for l in range(L):
            for m in range(4):
                qm = 2 * _flip(x, m & 2) + _flip(y, m & 1)
                copies.append(pltpu.make_async_remote_copy(
                    src_ref=ins[l].at[2 * qm + (1 - c)], dst_ref=outs[l].at[m], send_sem=send_sems.at[4 * l + m],
                    recv_sem=recv_sems.at[4 * l + m], device_id=(x, y, 1 - c), device_id_type=MESH))
        return _all_at_once(copies)

    outs = [jax.ShapeDtypeStruct((4,) + g.shape[1:], g.dtype) for g in grads]
    return Comm(plan, grads, outs, (4 * L, 4 * L))


def chip_comm(transits):
    L = len(transits)

    def plan(ins, outs, sems):
        send_sems, recv_sems = sems
        x, y, c = _coords()
        copies = []
        for l in range(L):
            for m in (1, 2, 3):
                copies.append(pltpu.make_async_remote_copy(
                    src_ref=ins[l].at[m - 1], dst_ref=outs[l].at[m - 1], send_sem=send_sems.at[3 * l + m - 1],
                    recv_sem=recv_sems.at[3 * l + m - 1], device_id=(_flip(x, m & 2), _flip(y, m & 1), c),
                    device_id_type=MESH))
        return _all_at_once(copies)

    outs = [jax.ShapeDtypeStruct(t.shape, t.dtype) for t in transits]
    return Comm(plan, transits, outs, (3 * L, 3 * L))


_SMALL_ROWS = ("b_ada", "norm1_g", "b_in", "conv_dw_b", "conv_ln_g", "conv_ln_b", "sgu_ln_g", "sgu_ln_b", "b_spatial",
               "norm2_g", "ffn_dw_b", "final_g")
_BIG = ("w_in", "w_conv_out", "w_sgu_out", "w_out", "w_up", "w_down")
_ORDER = ("w_ada", "b_ada", "norm1_g", "w_in", "b_in", "conv_dw_w", "conv_dw_b", "conv_ln_g", "conv_ln_b", "w_conv_out",
          "sgu_ln_g", "sgu_ln_b", "w_spatial", "b_spatial", "w_sgu_out", "w_out", "norm2_g", "w_up", "ffn_dw_w", "ffn_dw_b",
          "w_down", "final_g")


def kernel(x, c, w_ada, b_ada, norm1_g, w_in, b_in, conv_dw_w, conv_dw_b, conv_ln_g, conv_ln_b, w_conv_out, sgu_ln_g, sgu_ln_b, w_spatial, b_spatial, w_sgu_out, w_out, norm2_g, w_up, ffn_dw_w, ffn_dw_b, w_down, final_g, loss_target, m_w_ada, m_b_ada, m_norm1_g, m_w_in, m_b_in, m_conv_dw_w, m_conv_dw_b, m_conv_ln_g, m_conv_ln_b, m_w_conv_out, m_sgu_ln_g, m_sgu_ln_b, m_w_spatial, m_b_spatial, m_w_sgu_out, m_w_out, m_norm2_g, m_w_up, m_ffn_dw_w, m_ffn_dw_b, m_w_down, m_final_g, v_w_ada, v_b_ada, v_norm1_g, v_w_in, v_b_in, v_conv_dw_w, v_conv_dw_b, v_conv_ln_g, v_conv_ln_b, v_w_conv_out, v_sgu_ln_g, v_sgu_ln_b, v_w_spatial, v_b_spatial, v_w_sgu_out, v_w_out, v_norm2_g, v_w_up, v_ffn_dw_w, v_ffn_dw_b, v_w_down, v_final_g):
    W = dict(w_ada=w_ada, b_ada=b_ada, norm1_g=norm1_g, w_in=w_in, b_in=b_in, conv_dw_w=conv_dw_w, conv_dw_b=conv_dw_b,
             conv_ln_g=conv_ln_g, conv_ln_b=conv_ln_b, w_conv_out=w_conv_out, sgu_ln_g=sgu_ln_g, sgu_ln_b=sgu_ln_b,
             w_spatial=w_spatial, b_spatial=b_spatial, w_sgu_out=w_sgu_out, w_out=w_out, norm2_g=norm2_g, w_up=w_up,
             ffn_dw_w=ffn_dw_w, ffn_dw_b=ffn_dw_b, w_down=w_down, final_g=final_g)
    M = dict(w_ada=m_w_ada, b_ada=m_b_ada, norm1_g=m_norm1_g, w_in=m_w_in, b_in=m_b_in, conv_dw_w=m_conv_dw_w,
             conv_dw_b=m_conv_dw_b, conv_ln_g=m_conv_ln_g, conv_ln_b=m_conv_ln_b, w_conv_out=m_w_conv_out,
             sgu_ln_g=m_sgu_ln_g, sgu_ln_b=m_sgu_ln_b, w_spatial=m_w_spatial, b_spatial=m_b_spatial,
             w_sgu_out=m_w_sgu_out, w_out=m_w_out, norm2_g=m_norm2_g, w_up=m_w_up, ffn_dw_w=m_ffn_dw_w,
             ffn_dw_b=m_ffn_dw_b, w_down=m_w_down, final_g=m_final_g)
    V = dict(w_ada=v_w_ada, b_ada=v_b_ada, norm1_g=v_norm1_g, w_in=v_w_in, b_in=v_b_in, conv_dw_w=v_conv_dw_w,
             conv_dw_b=v_conv_dw_b, conv_ln_g=v_conv_ln_g, conv_ln_b=v_conv_ln_b, w_conv_out=v_w_conv_out,
             sgu_ln_g=v_sgu_ln_g, sgu_ln_b=v_sgu_ln_b, w_spatial=v_w_spatial, b_spatial=v_b_spatial,
             w_sgu_out=v_w_sgu_out, w_out=v_w_out, norm2_g=v_norm2_g, w_up=v_w_up, ffn_dw_w=v_ffn_dw_w,
             ffn_dw_b=v_ffn_dw_b, w_down=v_w_down, final_g=v_final_g)

    xs, tgt = x[0], loss_target[0]
    T, D = xs.shape
    Dc = conv_dw_w.shape[-1] * NDEV
    F = w_down.shape[1] * NDEV
    K31 = conv_dw_w.shape[1]
    G = w_spatial.shape[1]
    assert D == 2 * Dc and sgu_ln_g.shape[-1] == Dc and T % CHUNK == 0
    me = 4 * lax.axis_index("x") + 2 * lax.axis_index("y") + lax.axis_index("c")

    na = w_ada.shape[-1]
    c_all = gather_all("gather_c", [c])[0].reshape(NDEV, D)
    b_cols = lax.dynamic_slice(b_ada, (0, me * na), (1, na))
    mod_cols = ada_fwd_local(c_all, w_ada[0], b_cols)
    mod = exchange_rows("exchange_mod", mod_cols.reshape(NDEV, 1, na)).reshape(1, NDEV * na)

    wbf = {k: W[k][0].astype(BF16) for k in _BIG}
    fb = ffn_dw_b.reshape(2, 1, F)
    bsp_t = jnp.transpose(b_spatial[0])
    wsp = w_spatial[0]

    def plain(wb):
        return jnp.transpose(wb, (1, 0, 2)).reshape(1, wb.shape[1], NDEV * wb.shape[2])

    h1, (wb_in,) = pre_norm("pre_norm1", xs, mod, norm1_g, 0, comm=gather_comm([wbf["w_in"]]))
    proj, (wb_up, cw_g, fw_g) = mm_nn(
        "proj", h1, wb_in, F32, bias=b_in, comm=gather_comm([wbf["w_up"], conv_dw_w[0], ffn_dw_w[0]]))
    cw = jnp.transpose(cw_g, (1, 0, 2)).reshape(K31, Dc)
    fw = jnp.transpose(jnp.transpose(fw_g, (1, 0, 2)).reshape(3, 2, F), (1, 0, 2))
    wp_in, wp_up = to_plain("plain_w_in", wb_in), to_plain("plain_w_up", wb_up)
    (ac, asw), (wb_out, wb_co, wb_so) = mix_a_fwd(
        proj, cw, conv_dw_b, conv_ln_g, conv_ln_b,
        comm=gather_comm([wbf["w_out"], wbf["w_conv_out"], wbf["w_sgu_out"]]))
    wb_out = wb_out.reshape(1, D, D)
    wp_co, wp_so = plain(wb_co), plain(wb_so)
    uv = mix_b_fwd(proj, sgu_ln_g, sgu_ln_b, wsp, bsp_t)
    y_a = mm_nn("y_a", asw, wp_co, ACT)
    y_b = mm_nn("y_b", uv, wp_so, ACT)
    merged = merge_fwd(proj, y_a, y_b)
    o1 = mm_nn("o1", merged, wb_out, F32)
    h2 = pre_norm("pre_norm2", xs, mod, norm2_g, 1, o1=o1)
    upre, (wb_down,) = mm_nn("upre", h2, wb_up, F32, out_halves=True, tn_pref=MM_WIDE,
                             comm=gather_comm([wbf["w_down"]]))
    wb_down = wb_down.reshape(1, F, D)
    f, up = ffn_act_fwd(upre, fw, fb)
    o2 = mm_nn("o2", f, wb_down, F32)
    dx3, do2, st_f = final_fwd_bwd(xs, o1, o2, mod, final_g.reshape(1, D), tgt)
    loss = lax.psum(st_f[3, 0], MESH_AXES)

    xq, yq, cq = lax.axis_index("x"), lax.axis_index("y"), lax.axis_index("c")
    idx = jnp.stack([2 * (2 * _flip(xq, m & 2) + _flip(yq, m & 1)) + cq for m in range(4)]).astype(jnp.int32)
    own, transit, arrived = {}, {}, {}

    def add_pairs(keys, full, from_sibling):
        for k, g_full, r in zip(keys, full, from_sibling):
            own[k], transit[k] = pair_add("pair_add_" + k, g_full, r, idx)

    df = mm_nt("df", do2, wb_down, ACT, tko_pref=MM_WIDE)
    g_down = mm_tn("g_down", f, do2, 1, tko_pref=MM_WIDE).reshape(NDEV, F // NDEV, D)
    dupre, g_fw, g_fb = ffn_act_bwd(upre, up, df, fw)
    dh2 = mm_nt("dh2", dupre, wp_up, ACT, a_halves=True)
    g_up = mm_tn("g_up", h2, dupre, NDEV, g_halves=True, tn_pref=MM_WIDE)
    dx2, do1, st_2 = norm2_bwd(dh2, xs, o1, dx3, mod, norm2_g)
    dmerged, sib = mm_nt("dmerged", do1, wb_out, ACT, comm=sibling_comm([g_down, g_up]))
    add_pairs(("w_down", "w_up"), (g_down, g_up), sib)
    g_out = mm_tn("g_out", merged, do1, 1).reshape(NDEV, D // NDEV, D)
    dy_a, dy_b, dproj, db_g = merge_bwd(dmerged, proj, y_a, y_b)
    def blocked(g):
        return jnp.transpose(g.reshape(g.shape[1], NDEV, g.shape[2] // NDEV), (1, 0, 2))

    dasw = mm_nt("dasw", dy_a, wp_co, ACT)
    g_co = blocked(mm_tn("g_co", asw, dy_a, 1))
    duv = mm_nt("duv", dy_b, wp_so, ACT)
    g_so = blocked(mm_tn("g_so", uv, dy_b, 1))
    (dproj, st_a, g_cw, db_a), (arrived["w_up"],) = mix_a_bwd(
        dasw, ac, proj, dproj, cw, conv_ln_g, conv_ln_b, comm=chip_comm([transit["w_up"]]))
    (dproj, st_b, g_wsp, g_bsp_t, db_s), sib = mix_b_bwd(
        duv, proj, dproj, sgu_ln_g, sgu_ln_b, wsp, bsp_t, comm=sibling_comm([g_out, g_co, g_so]))
    add_pairs(("w_out", "w_conv_out", "w_sgu_out"), (g_out, g_co, g_so), sib)
    g_in, (arrived["w_down"],) = mm_tn("g_in", h1, dproj, NDEV, comm=chip_comm([transit["w_down"]]))
    add_pairs(("w_in",), (g_in,), _run_comm("sibling_w_in", sibling_comm([g_in])))
    late = ("w_out", "w_conv_out", "w_sgu_out", "w_in")
    dh1, got = mm_nt("dh1", dproj, wp_in, ACT, comm=chip_comm([transit[k] for k in late]))
    arrived.update(zip(late, got))
    grad_x, st_1 = norm1_bwd(dh1, xs, dx2, mod, norm1_g)

    dmod = jnp.concatenate([st_1[0], st_1[1], st_2[3], st_2[0], st_2[1], st_f[1]]).reshape(1, NDEV * na)
    dmod_all = exchange_rows("exchange_dmod", dmod.reshape(NDEV, 1, na)).reshape(NDEV, na)
    g_ada = ada_bwd_local(jnp.transpose(c_all), dmod_all)

    res = {}
    for k in _BIG:
        q = arrived[k]
        res[k] = adamw("adamw_" + k, W[k][0], M[k][0], V[k][0], [(own[k], None), (q, 0), (q, 1), (q, 2)])
    res["w_ada"] = adamw("adamw_w_ada", w_ada[0], m_w_ada[0], v_w_ada[0], [(g_ada, None)])

    g_row = dict(
        b_ada=dmod, norm1_g=st_1[2], b_in=jnp.concatenate([db_a, db_s, db_g], axis=1), conv_dw_b=st_a[2],
        conv_ln_g=st_a[0], conv_ln_b=st_a[1], sgu_ln_g=st_b[0], sgu_ln_b=st_b[1], b_spatial=jnp.transpose(g_bsp_t),
        norm2_g=st_2[2], ffn_dw_b=g_fb, final_g=st_f[0])
    packed = jnp.concatenate([g_row[k].reshape(1, -1) for k in _SMALL_ROWS], axis=1)
    g_rows, g_wsp_all, g_cw_all, g_fw_all = gather_all("gather_small", [packed, g_wsp.reshape(-1, CHUNK), g_cw, g_fw])
    rows = [tuple(S[k].reshape(1, -1) for S in (W, M, V)) for k in _SMALL_ROWS]
    wsp3 = tuple(S["w_spatial"].reshape(-1, CHUNK) for S in (W, M, V))
    cw3, fw3 = (tuple(S[k][0] for S in (W, M, V)) for k in ("conv_dw_w", "ffn_dw_w"))
    small = adamw_small(jnp.reshape(me, (1,)).astype(jnp.int32), rows, g_rows, wsp3, g_wsp_all, cw3, g_cw_all, fw3, g_fw_all)
    res.update(zip(_SMALL_ROWS + ("w_spatial", "conv_dw_w", "ffn_dw_w"), small))

    outs = [[], [], [], []]
    for k in _ORDER:
        for slot in range(4):
            outs[slot].append(res[k][slot].reshape(W[k].shape))
    return (loss, grad_x[None], *outs[0], *outs[1], *outs[2], *outs[3])
```

```python
import functools

import jax
import jax.numpy as jnp
from jax import lax
from jax.experimental import pallas as pl
from jax.experimental.pallas import tpu as pltpu

F32, BF16 = jnp.float32, jnp.bfloat16
ACT = BF16
NDEV = 8
MESH_AXES = ("x", "y", "c")
MESH = pl.DeviceIdType.MESH
EPS = 1e-6
CHUNK = 128
CONV_HALO = 32
FFN_HALO = 8
LANE, SUBLANE = 128, 8
ROW_TILE = 256
FFN_ROW_TILE = 512
STRIP = 32
STRIP_UNROLL = 2
MM_TILE = 1024
MM_WIDE = 1408
MM_DEEP = 2816
VMEM_LIMIT = 56 * 1024 * 1024
ADAM_LR, ADAM_B1, ADAM_B2, ADAM_EPS, ADAM_WD, ADAM_STEP = 0.001, 0.9, 0.999, 1e-08, 0.01, 10
SQRT_HALF = 0.7071067811865476
INV_SQRT_2PI = 0.3989422804014327


def _div(n, pref, mult=LANE):
    if n <= pref:
        return n
    for d in range(pref - pref % mult, 0, -mult):
        if n % d == 0:
            return d
    return n


def _cp(sem):
    return pltpu.CompilerParams(dimension_semantics=sem, vmem_limit_bytes=VMEM_LIMIT)


def _sig(v):
    return jax.nn.sigmoid(v)


def _gelu(v):
    return 0.5 * v * (1.0 + lax.erf(v * SQRT_HALF))


def _gelu_grad(v):
    return 0.5 * (1.0 + lax.erf(v * SQRT_HALF)) + v * (INV_SQRT_2PI * jnp.exp(-0.5 * v * v))


def _colsum(v):
    return jnp.sum(v, axis=0, keepdims=True)


def _rowmean(v):
    return jnp.mean(v, axis=-1, keepdims=True)


class Comm:
    def __init__(self, plan, ins, out_shapes, n_sems):
        self.plan, self.ins, self.out_shapes, self.n_sems = plan, list(ins), list(out_shapes), tuple(n_sems)


def _pcall(name, body, grid, in_specs, out_specs, out_shape, scratch=(), sem=None, aliases=None, comm=None):
    if comm is None:
        return pl.pallas_call(
            body, name=name, grid=grid, in_specs=in_specs, out_specs=out_specs, out_shape=out_shape,
            scratch_shapes=list(scratch), input_output_aliases=aliases or {},
            compiler_params=_cp(sem or ("arbitrary",) * len(grid)))
    single = not isinstance(out_shape, (list, tuple))
    own_specs, own_shapes = ([out_specs], [out_shape]) if single else (list(out_specs), list(out_shape))
    n_in, n_out, n_scr = len(in_specs), len(own_shapes), len(scratch)
    n_ci, n_co = len(comm.ins), len(comm.out_shapes)
    any_spec = pl.BlockSpec(memory_space=pl.ANY)

    def fused(*refs):
        ins, cins = refs[:n_in], refs[n_in:n_in + n_ci]
        outs = refs[n_in + n_ci:n_in + n_ci + n_out]
        couts = refs[n_in + n_ci + n_out:n_in + n_ci + n_out + n_co]
        scr = refs[n_in + n_ci + n_out + n_co:n_in + n_ci + n_out + n_co + n_scr]
        sems = refs[n_in + n_ci + n_out + n_co + n_scr:]
        first = functools.reduce(jnp.logical_and, [pl.program_id(d) == 0 for d in range(len(grid))])
        last = functools.reduce(jnp.logical_and, [pl.program_id(d) == grid[d] - 1 for d in range(len(grid))])

        @pl.when(first)
        def _():
            comm.plan(cins, couts, sems)[0]()

        body(*ins, *outs, *scr)

        @pl.when(last)
        def _():
            comm.plan(cins, couts, sems)[1]()

    call = pl.pallas_call(
        fused, name=name, grid=grid, in_specs=list(in_specs) + [any_spec] * n_ci,
        out_specs=own_specs + [any_spec] * n_co, out_shape=own_shapes + comm.out_shapes,
        scratch_shapes=list(scratch) + [pltpu.SemaphoreType.DMA((s,)) for s in comm.n_sems],
        input_output_aliases=aliases or {},
        compiler_params=pltpu.CompilerParams(dimension_semantics=("arbitrary",) * len(grid),
                                             vmem_limit_bytes=VMEM_LIMIT, has_side_effects=True))

    def run(*args):
        res = call(*args, *comm.ins)
        own = res[:n_out]
        return (own[0] if single else list(own)), list(res[n_out:])

    return run


def _matmul(name, a, b, *, grid, a_spec, b_spec, o_spec, out_shape, dims, acc_shape, bias=None, bias_spec=None, comm=None):
    nk = grid[2]

    def body(*refs):
        if bias is None:
            a_ref, b_ref, o_ref, *scr = refs
            bias_ref = None
        else:
            a_ref, b_ref, bias_ref, o_ref, *scr = refs
        part = lax.dot_general(a_ref[...], b_ref[...], (dims, ((), ())), preferred_element_type=F32)

        def finish(total):
            if bias_ref is not None:
                total = total + bias_ref[...]
            o_ref[...] = total.astype(o_ref.dtype)

        if nk == 1:
            finish(part)
        else:
            acc = scr[0]
            k = pl.program_id(2)

            @pl.when(k == 0)
            def _():
                acc[...] = part

            @pl.when(k > 0)
            def _():
                acc[...] += part

            @pl.when(k == nk - 1)
            def _():
                finish(acc[...])

    in_specs = [a_spec, b_spec] + ([bias_spec] if bias is not None else [])
    args = (a, b) + ((bias,) if bias is not None else ())
    return _pcall(name, body, grid, in_specs, o_spec, out_shape,
                  scratch=[pltpu.VMEM(acc_shape, F32)] if nk > 1 else [],
                  sem=("parallel", "parallel", "arbitrary"), comm=comm)(*args)


def mm_nn(name, a, wb, out_dtype, *, bias=None, out_halves=False, tn_pref=MM_TILE, comm=None):
    T, K = a.shape
    NB, _, Ns = wb.shape
    N = NB * Ns
    tm, tn, tk = _div(T, MM_TILE), _div(Ns, tn_pref), _div(K, MM_DEEP)
    npb, nj, nk = Ns // tn, N // tn, K // tk
    if out_halves:
        o_spec = pl.BlockSpec((None, tm, tn), lambda i, j, k: (j // (nj // 2), i, j % (nj // 2)))
        out_shape = jax.ShapeDtypeStruct((2, T, N // 2), out_dtype)
    else:
        o_spec = pl.BlockSpec((tm, tn), lambda i, j, k: (i, j))
        out_shape = jax.ShapeDtypeStruct((T, N), out_dtype)
    return _matmul(
        name, a, wb, grid=(T // tm, nj, nk),
        a_spec=pl.BlockSpec((tm, tk), lambda i, j, k: (i, k)),
        b_spec=pl.BlockSpec((None, tk, tn), lambda i, j, k: (j // npb, k, j % npb)),
        o_spec=o_spec, out_shape=out_shape, dims=((1,), (0,)), acc_shape=(tm, tn),
        bias=bias, bias_spec=pl.BlockSpec((1, tn), lambda i, j, k: (0, j)), comm=comm)


def to_plain(name, wb):
    NB, K, Ns = wb.shape
    tk = _div(K, MM_TILE, 2 * SUBLANE)

    def body(i_ref, o_ref):
        o_ref[...] = i_ref[...]

    return _pcall(name, body, (NB, K // tk), [pl.BlockSpec((None, tk, Ns), lambda b, k: (b, k, 0))],
                  pl.BlockSpec((None, tk, Ns), lambda b, k: (0, k, b)),
                  jax.ShapeDtypeStruct((1, K, NB * Ns), wb.dtype), sem=("parallel", "parallel"))(wb)


def mm_nt(name, a, wb, out_dtype, *, a_halves=False, tko_pref=MM_TILE, tc_pref=MM_DEEP, comm=None):
    NB, K, Ns = wb.shape
    T = a.shape[-2]
    span = Ns // 2 if a_halves and NB == 1 else Ns
    tm, tko, tc = _div(T, MM_TILE), _div(K, tko_pref), _div(span, tc_pref)
    cpb = Ns // tc
    nkk = NB * cpb
    if a_halves:
        a_spec = pl.BlockSpec((None, tm, tc), lambda i, j, k: (k // (nkk // 2), i, k % (nkk // 2)))
    else:
        a_spec = pl.BlockSpec((tm, tc), lambda i, j, k: (i, k))
    return _matmul(
        name, a, wb, grid=(T // tm, K // tko, nkk), a_spec=a_spec,
        b_spec=pl.BlockSpec((None, tko, tc), lambda i, j, k: (k // cpb, j, k % cpb)),
        o_spec=pl.BlockSpec((tm, tko), lambda i, j, k: (i, j)),
        out_shape=jax.ShapeDtypeStruct((T, K), out_dtype), dims=((1,), (1,)), acc_shape=(tm, tko), comm=comm)


def mm_tn(name, a, g, nb, *, g_halves=False, tko_pref=MM_TILE, tn_pref=MM_TILE, out_dtype=BF16, comm=None):
    T, K = a.shape
    N = g.shape[-1] * (2 if g_halves else 1)
    Ns = N // nb
    tt, tko, tn = _div(T, 2 * MM_TILE), _div(K, tko_pref), _div(Ns, tn_pref)
    npb, nj = Ns // tn, N // tn
    if g_halves:
        g_spec = pl.BlockSpec((None, tt, tn), lambda i, j, t: (j // (nj // 2), t, j % (nj // 2)))
    else:
        g_spec = pl.BlockSpec((tt, tn), lambda i, j, t: (t, j))
    return _matmul(
        name, a, g, grid=(K // tko, nj, T // tt),
        a_spec=pl.BlockSpec((tt, tko), lambda i, j, t: (t, i)), b_spec=g_spec,
        o_spec=pl.BlockSpec((None, tko, tn), lambda i, j, t: (j // npb, i, j % npb)),
        out_shape=jax.ShapeDtypeStruct((nb, K, Ns), out_dtype), dims=((0,), (0,)), acc_shape=(tko, tn), comm=comm)


def _row_call(name, body, grid, in_specs, out_specs, out_shape, scratch=(), sem=None, aliases=None, comm=None):
    return _pcall(name, body, grid, in_specs, out_specs, out_shape, scratch, sem, aliases, comm)


def _full(shape):
    nd = len(shape)
    return pl.BlockSpec(shape, lambda *idx: (0,) * nd)


def pre_norm(name, x, mod, g, which, o1=None, comm=None):
    T, D = x.shape
    tr = _div(T, ROW_TILE, SUBLANE)

    def body(*refs):
        if o1 is None:
            x_ref, mod_ref, g_ref, h_ref = refs
            xv = x_ref[...]
        else:
            x_ref, o1_ref, mod_ref, g_ref, h_ref = refs
            xv = x_ref[...] + mod_ref[:, 2 * D:3 * D] * o1_ref[...]
        shift = mod_ref[:, (3 * which) * D:(3 * which + 1) * D]
        scale = mod_ref[:, (3 * which + 1) * D:(3 * which + 2) * D]
        r = lax.rsqrt(_rowmean(xv * xv) + EPS)
        h_ref[...] = ((xv * r) * g_ref[...] * (1.0 + scale) + shift).astype(BF16)

    row = pl.BlockSpec((tr, D), lambda i: (i, 0))
    ins = [x] + ([o1] if o1 is not None else []) + [mod, g]
    specs = [row] * (1 if o1 is None else 2) + [_full(mod.shape), _full(g.shape)]
    return _row_call(name, body, (T // tr,), specs, row, jax.ShapeDtypeStruct((T, D), BF16), sem=("parallel",),
                     comm=comm)(*ins)


def _window_taps(win, taps):
    n = win.shape[0]
    for r in range(SUBLANE):
        group = [(i, tap, off) for i, (tap, off) in enumerate(taps) if off % SUBLANE == r]
        if not group:
            continue
        shifted = win if r == 0 else pltpu.roll(win, n - r, 0)
        for i, tap, off in group:
            assert 0 <= off and off + CONV_HALO <= n
            yield i, tap, shifted[off - r:off - r + CONV_HALO]


def mix_a_fwd(proj, cw, cb, lg, lb, comm=None):
    T = proj.shape[0]
    K, Dc = cw.shape
    tr = _div(T, ROW_TILE, CONV_HALO)
    hb = tr // CONV_HALO

    def body(val_ref, gate_ref, hval_ref, hgate_ref, cw_ref, cb_ref, lg_ref, lb_ref, ac_ref, asw_ref, buf):
        i = pl.program_id(0)
        hist = hval_ref[...] * _sig(hgate_ref[...])
        buf[0:CONV_HALO, :] = jnp.where(i > 0, hist, 0.0)
        buf[CONV_HALO:CONV_HALO + tr, :] = val_ref[...] * _sig(gate_ref[...])
        base = CONV_HALO - (K - 1)
        for c0 in range(0, Dc, LANE):
            lanes = slice(c0, c0 + LANE)

            def step(s, carry):
                r0 = pl.multiple_of(s * CONV_HALO, CONV_HALO)
                win = buf[pl.ds(r0, 2 * CONV_HALO), lanes]
                acc = jnp.zeros((CONV_HALO, LANE), F32)
                for _, k, piece in _window_taps(win, [(k, base + k) for k in range(K)]):
                    acc = acc + piece * cw_ref[k:k + 1, lanes]
                ac_ref[pl.ds(r0, CONV_HALO), lanes] = acc + cb_ref[:, lanes]
                return carry

            lax.fori_loop(0, tr // CONV_HALO, step, 0, unroll=STRIP_UNROLL)
        ac = ac_ref[...]
        mu = _rowmean(ac)
        cen = ac - mu
        y = cen * lax.rsqrt(_rowmean(cen * cen) + EPS)
        aln = y * lg_ref[...] + lb_ref[...]
        asw_ref[...] = (aln * _sig(aln)).astype(BF16)

    def halo(col):
        return pl.BlockSpec((CONV_HALO, Dc), lambda i: (jnp.maximum(i * hb - 1, 0), col))

    row = pl.BlockSpec((tr, Dc), lambda i: (i, 0))
    return _row_call(
        "mix_a_fwd", body, (T // tr,),
        [row, pl.BlockSpec((tr, Dc), lambda i: (i, 1)), halo(0), halo(1),
         _full(cw.shape), _full(cb.shape), _full(lg.shape), _full(lb.shape)],
        [row, row], [jax.ShapeDtypeStruct((T, Dc), F32), jax.ShapeDtypeStruct((T, Dc), BF16)],
        scratch=[pltpu.VMEM((CONV_HALO + tr, Dc), F32)], sem=("parallel",), comm=comm)(proj, proj, proj, proj, cw, cb, lg, lb)


def _spatial_mask():
    t = lax.broadcasted_iota(jnp.int32, (CHUNK, CHUNK), 0)
    s = lax.broadcasted_iota(jnp.int32, (CHUNK, CHUNK), 1)
    return s <= t


def mix_b_fwd(proj, lg, lb, wsp, bsp_t):
    T = proj.shape[0]
    Ds = lg.shape[-1]
    G = wsp.shape[0]
    hd = Ds // G
    tr = _div(T, ROW_TILE, CHUNK)

    def body(u_ref, v_ref, lg_ref, lb_ref, w_ref, b_ref, uv_ref, vs):
        v = _gelu(v_ref[...])
        mu = _rowmean(v)
        cen = v - mu
        vln = (cen * lax.rsqrt(_rowmean(cen * cen) + EPS) * lg_ref[...] + lb_ref[...]).astype(BF16)
        mask = _spatial_mask()
        for g in range(G):
            wg = jnp.where(mask, w_ref[g], 0.0).astype(BF16)
            for n in range(tr // CHUNK):
                rows, cols = slice(n * CHUNK, (n + 1) * CHUNK), slice(g * hd, (g + 1) * hd)
                vs[rows, cols] = jnp.dot(wg, vln[rows, cols], preferred_element_type=F32) + b_ref[:, g:g + 1]
        uv_ref[...] = (_gelu(u_ref[...]) * vs[...]).astype(BF16)

    return _row_call(
        "mix_b_fwd", body, (T // tr,),
        [pl.BlockSpec((tr, Ds), lambda i: (i, 2)), pl.BlockSpec((tr, Ds), lambda i: (i, 3)),
         _full(lg.shape), _full(lb.shape), _full(wsp.shape), _full(bsp_t.shape)],
        pl.BlockSpec((tr, Ds), lambda i: (i, 0)), jax.ShapeDtypeStruct((T, Ds), BF16),
        scratch=[pltpu.VMEM((tr, Ds), F32)], sem=("parallel",))(proj, proj, lg, lb, wsp, bsp_t)


def merge_fwd(proj, y_a, y_b):
    T, D = y_a.shape
    tr = _div(T, ROW_TILE, SUBLANE)

    def body(g_ref, ya_ref, yb_ref, o_ref):
        o_ref[...] = (_sig(g_ref[:, 0:D]) * ya_ref[...].astype(F32)
                      + _sig(g_ref[:, D:2 * D]) * yb_ref[...].astype(F32)).astype(BF16)

    row = pl.BlockSpec((tr, D), lambda i: (i, 0))
    return _row_call("merge_fwd", body, (T // tr,), [pl.BlockSpec((tr, 2 * D), lambda i: (i, 1)), row, row], row,
                     jax.ShapeDtypeStruct((T, D), BF16), sem=("parallel",))(proj, y_a, y_b)


def _fold(v):
    acc = v[0:SUBLANE]
    for r in range(SUBLANE, v.shape[0], SUBLANE):
        acc = acc + v[r:r + SUBLANE]
    return acc


def _conv3(prev, cur, w):
    win = jnp.concatenate([prev, cur], axis=0)
    n = win.shape[0]
    x1 = pltpu.roll(win, 1, 0)[FFN_HALO:n]
    x2 = pltpu.roll(win, 2, 0)[FFN_HALO:n]
    return x2 * w[0] + x1 * w[1] + cur * w[2], (x2, x1, cur)


def _strip_taps(w_ref, b_ref, lanes):
    w = [[w_ref[h, k:k + 1, lanes] for k in range(3)] for h in range(2)]
    b = [b_ref[h, :, lanes] for h in range(2)]
    return w, b


def ffn_act_fwd(upre, fw, fb, comm=None):
    _, T, F = upre.shape
    tr = _div(T, FFN_ROW_TILE, STRIP)
    cb = _div(F, MM_WIDE)
    hb = tr // FFN_HALO
    ns = tr // STRIP

    def body(x_ref, h_ref, w_ref, b_ref, f_ref, u_ref):
        i = pl.program_id(0)
        for c0 in range(0, cb, 2 * LANE):
            lanes = slice(c0, c0 + min(2 * LANE, cb - c0))
            w, b = _strip_taps(w_ref, b_ref, lanes)

            def strip(r0, prev):
                up = [_conv3(prev(h), x_ref[h, pl.ds(r0, STRIP), lanes], w[h])[0] + b[h] for h in range(2)]
                f_ref[pl.ds(r0, STRIP), lanes] = (up[1] * _sig(up[1]) * up[0]).astype(BF16)
                for h in range(2):
                    u_ref[h, pl.ds(r0, STRIP), lanes] = up[h]

            strip(0, lambda h: jnp.where(i > 0, h_ref[h, :, lanes], 0.0))

            def step(s, carry):
                r0 = pl.multiple_of(s * STRIP, STRIP)
                strip(r0, lambda h: x_ref[h, pl.ds(pl.multiple_of(r0 - FFN_HALO, FFN_HALO), FFN_HALO), lanes])
                return carry

            lax.fori_loop(1, ns, step, 0, unroll=STRIP_UNROLL)

    return _row_call(
        "ffn_act_fwd", body, (T // tr, F // cb),
        [pl.BlockSpec((2, tr, cb), lambda i, j: (0, i, j)),
         pl.BlockSpec((2, FFN_HALO, cb), lambda i, j: (0, jnp.maximum(i * hb - 1, 0), j)),
         pl.BlockSpec((2, 3, cb), lambda i, j: (0, 0, j)), pl.BlockSpec((2, 1, cb), lambda i, j: (0, 0, j))],
        [pl.BlockSpec((tr, cb), lambda i, j: (i, j)), pl.BlockSpec((2, tr, cb), lambda i, j: (0, i, j))],
        [jax.ShapeDtypeStruct((T, F), BF16), jax.ShapeDtypeStruct((2, T, F), F32)],
        sem=("parallel", "parallel"), comm=comm)(upre, upre, fw, fb)


def final_fwd_bwd(x, o1, o2, mod, gf, target):
    T, D = x.shape
    tr = _div(T, ROW_TILE, SUBLANE)
    nt = T // tr

    def body(x_ref, o1_ref, o2_ref, mod_ref, gf_ref, t_ref, dx3_ref, do2_ref, st_ref):
        i = pl.program_id(0)
        gate1, gate2 = mod_ref[:, 2 * D:3 * D], mod_ref[:, 5 * D:6 * D]
        o2v = o2_ref[...]
        x3 = x_ref[...] + gate1 * o1_ref[...] + gate2 * o2v
        r = lax.rsqrt(_rowmean(x3 * x3) + EPS)
        xn = x3 * r
        err = xn * gf_ref[...] - t_ref[...]
        dy = err * (1.0 / D)
        dxn = dy * gf_ref[...]
        dx3 = r * (dxn - xn * _rowmean(dxn * xn))
        dx3_ref[...] = dx3
        do2_ref[...] = (dx3 * gate2).astype(BF16)

        @pl.when(i == 0)
        def _():
            st_ref[...] = jnp.zeros_like(st_ref)

        st_ref[0:1, :] += _colsum(dy * xn)
        st_ref[1:2, :] += _colsum(dx3 * o2v)
        st_ref[2:3, :] += _colsum(err * err) * (0.5 / D)

        @pl.when(i == nt - 1)
        def _():
            st_ref[3:4, :] = jnp.zeros((1, D), F32) + jnp.sum(st_ref[2:3, :])

    row = pl.BlockSpec((tr, D), lambda i: (i, 0))
    return _row_call(
        "final_fwd_bwd", body, (nt,), [row, row, row, _full(mod.shape), _full(gf.shape), row],
        [row, row, _full((8, D))],
        [jax.ShapeDtypeStruct((T, D), F32), jax.ShapeDtypeStruct((T, D), BF16), jax.ShapeDtypeStruct((8, D), F32)],
    )(x, o1, o2, mod, gf, target)


def ffn_act_bwd(upre, up, df, fw):
    _, T, F = upre.shape
    tr = _div(T, FFN_ROW_TILE, STRIP)
    cb = _div(F, MM_WIDE)
    nt = T // tr
    ns = tr // STRIP

    def body(x_ref, u_ref, df_ref, w_ref, dpre_ref, dw_ref, db_ref, carry):
        i = pl.program_id(1)

        @pl.when(i == 0)
        def _():
            carry[...] = jnp.zeros_like(carry)
            dw_ref[...] = jnp.zeros_like(dw_ref)
            db_ref[...] = jnp.zeros_like(db_ref)

        for c0 in range(0, cb, 2 * LANE):
            wd = min(2 * LANE, cb - c0)
            lanes = slice(c0, c0 + wd)
            w = [[w_ref[h, k:k + 1, lanes] for k in range(3)] for h in range(2)]

            def step(s, state):
                later, db, dw = state
                r0 = pl.multiple_of((ns - 1 - s) * STRIP, STRIP)
                rows = pl.ds(r0, STRIP)
                val, gt = u_ref[0, rows, lanes], u_ref[1, rows, lanes]
                sg = _sig(gt)
                dfv = df_ref[rows, lanes].astype(F32)
                dup = (dfv * (gt * sg), dfv * val * (sg * (1.0 + gt * (1.0 - sg))))
                new_db, new_dw = [], []
                for h in range(2):
                    dwin = jnp.concatenate([dup[h], later[h]], axis=0)
                    n = dwin.shape[0]
                    d1 = pltpu.roll(dwin, n - 1, 0)[0:STRIP]
                    d2 = pltpu.roll(dwin, n - 2, 0)[0:STRIP]
                    xs = x_ref[h, rows, lanes]
                    new_db.append(db[h] + _fold(dup[h]))
                    new_dw.append((dw[h][0] + _fold(d2 * xs), dw[h][1] + _fold(d1 * xs), dw[h][2] + _fold(dup[h] * xs)))
                    dpre_ref[h, rows, lanes] = (dup[h] * w[h][2] + d1 * w[h][1] + d2 * w[h][0]).astype(BF16)
                return tuple(dup[h][0:FFN_HALO] for h in range(2)), tuple(new_db), tuple(new_dw)

            zero = jnp.zeros((SUBLANE, wd), F32)
            state = ((carry[0, :, lanes], carry[1, :, lanes]), (zero, zero), ((zero,) * 3,) * 2)
            later, db, dw = lax.fori_loop(0, ns, step, state, unroll=STRIP_UNROLL)
            for h in range(2):
                carry[h, :, lanes] = later[h]
                db_ref[h, :, lanes] += _colsum(db[h])
                for k in range(3):
                    dw_ref[h, k:k + 1, lanes] += _colsum(dw[h][k])

    tile = pl.BlockSpec((2, tr, cb), lambda j, i: (0, nt - 1 - i, j))
    return _row_call(
        "ffn_act_bwd", body, (F // cb, nt),
        [tile, tile, pl.BlockSpec((tr, cb), lambda j, i: (nt - 1 - i, j)), pl.BlockSpec((2, 3, cb), lambda j, i: (0, 0, j))],
        [tile, pl.BlockSpec((2, 3, cb), lambda j, i: (0, 0, j)), pl.BlockSpec((2, 1, cb), lambda j, i: (0, 0, j))],
        [jax.ShapeDtypeStruct((2, T, F), BF16), jax.ShapeDtypeStruct((2, 3, F), F32), jax.ShapeDtypeStruct((2, 1, F), F32)],
        scratch=[pltpu.VMEM((2, FFN_HALO, cb), F32)],
        sem=("parallel", "arbitrary"))(upre, up, df, fw)


def norm2_bwd(dh2, x, o1, dx3, mod, g2, comm=None):
    T, D = x.shape
    tr = _div(T, ROW_TILE, SUBLANE)

    def body(dh_ref, x_ref, o1_ref, dx3_ref, mod_ref, g_ref, dx2_ref, do1_ref, st_ref):
        i = pl.program_id(0)
        gate1, scale = mod_ref[:, 2 * D:3 * D], mod_ref[:, 4 * D:5 * D]
        o1v = o1_ref[...]
        x2 = x_ref[...] + gate1 * o1v
        r = lax.rsqrt(_rowmean(x2 * x2) + EPS)
        xn = x2 * r
        dh = dh_ref[...].astype(F32)
        dxn = dh * (g_ref[...] * (1.0 + scale))
        dx2 = r * (dxn - xn * _rowmean(dxn * xn)) + dx3_ref[...]
        dx2_ref[...] = dx2
        do1_ref[...] = (dx2 * gate1).astype(BF16)

        @pl.when(i == 0)
        def _():
            st_ref[...] = jnp.zeros_like(st_ref)

        st_ref[0:1, :] += _colsum(dh)
        st_ref[1:2, :] += _colsum(dh * xn) * g_ref[...]
        st_ref[2:3, :] += _colsum(dh * xn) * (1.0 + scale)
        st_ref[3:4, :] += _colsum(dx2 * o1v)

    row = pl.BlockSpec((tr, D), lambda i: (i, 0))
    return _row_call(
        "norm2_bwd", body, (T // tr,), [row, row, row, row, _full(mod.shape), _full(g2.shape)],
        [row, row, _full((8, D))],
        [jax.ShapeDtypeStruct((T, D), F32), jax.ShapeDtypeStruct((T, D), BF16), jax.ShapeDtypeStruct((8, D), F32)],
        comm=comm)(dh2, x, o1, dx3, mod, g2)


def merge_bwd(dmerged, proj, y_a, y_b):
    T, D = y_a.shape
    tr = _div(T, ROW_TILE, SUBLANE)

    def body(dm_ref, g_ref, ya_ref, yb_ref, dya_ref, dyb_ref, dp_ref, db_ref):
        i = pl.program_id(0)
        dm = dm_ref[...].astype(F32)
        sa, sb = _sig(g_ref[:, 0:D]), _sig(g_ref[:, D:2 * D])
        dya_ref[...] = (dm * sa).astype(BF16)
        dyb_ref[...] = (dm * sb).astype(BF16)
        dga = dm * ya_ref[...].astype(F32) * (sa * (1.0 - sa))
        dgb = dm * yb_ref[...].astype(F32) * (sb * (1.0 - sb))
        dp_ref[:, 0:D] = dga.astype(BF16)
        dp_ref[:, D:2 * D] = dgb.astype(BF16)

        @pl.when(i == 0)
        def _():
            db_ref[...] = jnp.zeros_like(db_ref)

        db_ref[:, 0:D] += _colsum(dga)
        db_ref[:, D:2 * D] += _colsum(dgb)

    row = pl.BlockSpec((tr, D), lambda i: (i, 0))
    wide = pl.BlockSpec((tr, 2 * D), lambda i: (i, 1))
    return _row_call(
        "merge_bwd", body, (T // tr,), [row, wide, row, row], [row, row, wide, _full((1, 2 * D))],
        [jax.ShapeDtypeStruct((T, D), BF16), jax.ShapeDtypeStruct((T, D), BF16),
         jax.ShapeDtypeStruct((T, 4 * D), BF16), jax.ShapeDtypeStruct((1, 2 * D), F32)],
    )(dmerged, proj, y_a, y_b)


def mix_a_bwd(dasw, ac, proj, dproj, cw, lg, lb, comm=None):
    T, Dc = ac.shape
    K = cw.shape[0]
    tr = _div(T, ROW_TILE, CONV_HALO)
    nt = T // tr

    def body(dasw_ref, ac_ref, in_ref, dp_hbm, cw_ref, lg_ref, lb_ref,
             dp_ref, st_ref, dcw_ref, db_ref, abuf, dbuf, carry, da_buf):
        del dp_hbm
        i = pl.program_id(0)
        acv = ac_ref[...]
        mu = _rowmean(acv)
        cen = acv - mu
        rstd = lax.rsqrt(_rowmean(cen * cen) + EPS)
        y = cen * rstd
        aln = y * lg_ref[...] + lb_ref[...]
        sg = _sig(aln)
        daln = dasw_ref[...].astype(F32) * (sg * (1.0 + aln * (1.0 - sg)))
        dy = daln * lg_ref[...]
        dac = rstd * (dy - _rowmean(dy) - y * _rowmean(dy * y))

        @pl.when(i == 0)
        def _():
            carry[...] = jnp.zeros_like(carry)
            st_ref[...] = jnp.zeros_like(st_ref)
            dcw_ref[...] = jnp.zeros_like(dcw_ref)
            db_ref[...] = jnp.zeros_like(db_ref)

        st_ref[0:1, :] += _colsum(daln * y)
        st_ref[1:2, :] += _colsum(daln)
        st_ref[2:3, :] += _colsum(dac)
        val, gate = in_ref[:, 0:Dc], in_ref[:, Dc:2 * Dc]
        sgg = _sig(gate)
        abuf[...] = val * sgg
        dbuf[0:tr, :] = dac
        dbuf[tr:tr + CONV_HALO, :] = carry[...]
        carry[...] = dac[0:CONV_HALO, :]
        zero = jnp.zeros((SUBLANE, LANE), F32)
        for c0 in range(0, Dc, LANE):
            lanes = slice(c0, c0 + LANE)

            def step(s, dws):
                r0 = pl.multiple_of(s * CONV_HALO, CONV_HALO)
                dwin = dbuf[pl.ds(r0, 2 * CONV_HALO), lanes]
                dws = list(dws)
                a_piece = abuf[pl.ds(r0, CONV_HALO), lanes]
                acc = jnp.zeros((CONV_HALO, LANE), F32)
                for _, k, piece in _window_taps(dwin, [(k, K - 1 - k) for k in range(K)]):
                    acc = acc + piece * cw_ref[k:k + 1, lanes]
                    dws[k] = dws[k] + _fold(piece * a_piece)
                da_buf[pl.ds(r0, CONV_HALO), lanes] = acc
                return tuple(dws)

            dws = lax.fori_loop(0, tr // CONV_HALO, step, (zero,) * K)
            for k in range(K):
                dcw_ref[k:k + 1, lanes] += _colsum(dws[k])
        da = da_buf[...]
        dval = da * sgg
        dgate = da * val * (sgg * (1.0 - sgg))
        dp_ref[:, 0:Dc] = dval.astype(BF16)
        dp_ref[:, Dc:2 * Dc] = dgate.astype(BF16)
        db_ref[:, 0:Dc] += _colsum(dval)
        db_ref[:, Dc:2 * Dc] += _colsum(dgate)

    row = pl.BlockSpec((tr, Dc), lambda i: (nt - 1 - i, 0))
    wide = pl.BlockSpec((tr, 2 * Dc), lambda i: (nt - 1 - i, 0))
    return _row_call(
        "mix_a_bwd", body, (nt,),
        [row, row, wide, pl.BlockSpec(memory_space=pl.ANY), _full(cw.shape), _full(lg.shape), _full(lb.shape)],
        [wide, _full((8, Dc)), _full((CONV_HALO, Dc)), _full((1, 2 * Dc))],
        [jax.ShapeDtypeStruct(dproj.shape, BF16), jax.ShapeDtypeStruct((8, Dc), F32),
         jax.ShapeDtypeStruct((CONV_HALO, Dc), F32), jax.ShapeDtypeStruct((1, 2 * Dc), F32)],
        scratch=[pltpu.VMEM((tr, Dc), F32), pltpu.VMEM((tr + CONV_HALO, Dc), F32),
                 pltpu.VMEM((CONV_HALO, Dc), F32), pltpu.VMEM((tr, Dc), F32)],
        aliases={3: 0}, comm=comm)(dasw, ac, proj, dproj, cw, lg, lb)


def mix_b_bwd(duv, proj, dproj, lg, lb, wsp, bsp_t, comm=None):
    T, Ds = duv.shape
    G = wsp.shape[0]
    hd = Ds // G
    tr = _div(T, ROW_TILE, CHUNK)
    nt = T // tr

    def body(duv_ref, s_ref, dp_hbm, lg_ref, lb_ref, w_ref, b_ref,
             dp_ref, st_ref, dws_ref, dbs_ref, db_ref, vs, dvln):
        del dp_hbm
        i = pl.program_id(0)

        @pl.when(i == 0)
        def _():
            st_ref[...] = jnp.zeros_like(st_ref)
            dws_ref[...] = jnp.zeros_like(dws_ref)
            dbs_ref[...] = jnp.zeros_like(dbs_ref)
            db_ref[...] = jnp.zeros_like(db_ref)

        upre, vpre = s_ref[:, 0:Ds], s_ref[:, Ds:2 * Ds]
        u, v = _gelu(upre), _gelu(vpre)
        mu = _rowmean(v)
        cen = v - mu
        rstd = lax.rsqrt(_rowmean(cen * cen) + EPS)
        yv = cen * rstd
        vln = (yv * lg_ref[...] + lb_ref[...]).astype(BF16)
        duvv = duv_ref[...].astype(F32)
        dvs = duvv * u
        dvs_b = dvs.astype(BF16)
        mask = _spatial_mask()
        for g in range(G):
            wg = jnp.where(mask, w_ref[g], 0.0).astype(BF16)
            cols = slice(g * hd, (g + 1) * hd)
            dws = jnp.zeros((CHUNK, CHUNK), F32)
            dbs = jnp.zeros((CHUNK, 1), F32)
            for n in range(tr // CHUNK):
                rows = slice(n * CHUNK, (n + 1) * CHUNK)
                vs[rows, cols] = jnp.dot(wg, vln[rows, cols], preferred_element_type=F32) + b_ref[:, g:g + 1]
                dvln[rows, cols] = lax.dot_general(wg, dvs_b[rows, cols], (((0,), (0,)), ((), ())),
                                                   preferred_element_type=F32)
                dws = dws + lax.dot_general(dvs_b[rows, cols], vln[rows, cols], (((1,), (1,)), ((), ())),
                                            preferred_element_type=F32)
                dbs = dbs + jnp.sum(dvs[rows, cols], axis=1, keepdims=True)
            dws_ref[g] += jnp.where(mask, dws, 0.0)
            dbs_ref[:, g:g + 1] += dbs
        dvl = dvln[...]
        st_ref[0:1, :] += _colsum(dvl * yv)
        st_ref[1:2, :] += _colsum(dvl)
        dyv = dvl * lg_ref[...]
        dv = rstd * (dyv - _rowmean(dyv) - yv * _rowmean(dyv * yv))
        dupre = duvv * vs[...] * _gelu_grad(upre)
        dvpre = dv * _gelu_grad(vpre)
        dp_ref[:, 0:Ds] = dupre.astype(BF16)
        dp_ref[:, Ds:2 * Ds] = dvpre.astype(BF16)
        db_ref[:, 0:Ds] += _colsum(dupre)
        db_ref[:, Ds:2 * Ds] += _colsum(dvpre)

    wide = pl.BlockSpec((tr, 2 * Ds), lambda i: (i, 1))
    return _row_call(
        "mix_b_bwd", body, (nt,),
        [pl.BlockSpec((tr, Ds), lambda i: (i, 0)), wide, pl.BlockSpec(memory_space=pl.ANY),
         _full(lg.shape), _full(lb.shape), _full(wsp.shape), _full(bsp_t.shape)],
        [wide, _full((8, Ds)), _full(wsp.shape), _full(bsp_t.shape), _full((1, 2 * Ds))],
        [jax.ShapeDtypeStruct(dproj.shape, BF16), jax.ShapeDtypeStruct((8, Ds), F32),
         jax.ShapeDtypeStruct(wsp.shape, F32), jax.ShapeDtypeStruct(bsp_t.shape, F32),
         jax.ShapeDtypeStruct((1, 2 * Ds), F32)],
        scratch=[pltpu.VMEM((tr, Ds), F32), pltpu.VMEM((tr, Ds), F32)],
        aliases={2: 0}, comm=comm)(duv, proj, dproj, lg, lb, wsp, bsp_t)


def norm1_bwd(dh1, x, dx2, mod, g1):
    T, D = x.shape
    tr = _div(T, ROW_TILE, SUBLANE)

    def body(dh_ref, x_ref, dx2_ref, mod_ref, g_ref, gx_ref, st_ref):
        i = pl.program_id(0)
        scale = mod_ref[:, D:2 * D]
        xv = x_ref[...]
        r = lax.rsqrt(_rowmean(xv * xv) + EPS)
        xn = xv * r
        dh = dh_ref[...].astype(F32)
        dxn = dh * (g_ref[...] * (1.0 + scale))
        gx_ref[...] = r * (dxn - xn * _rowmean(dxn * xn)) + dx2_ref[...]

        @pl.when(i == 0)
        def _():
            st_ref[...] = jnp.zeros_like(st_ref)

        st_ref[0:1, :] += _colsum(dh)
        st_ref[1:2, :] += _colsum(dh * xn) * g_ref[...]
        st_ref[2:3, :] += _colsum(dh * xn) * (1.0 + scale)

    row = pl.BlockSpec((tr, D), lambda i: (i, 0))
    return _row_call(
        "norm1_bwd", body, (T // tr,), [row, row, row, _full(mod.shape), _full(g1.shape)], [row, _full((8, D))],
        [jax.ShapeDtypeStruct((T, D), F32), jax.ShapeDtypeStruct((8, D), F32)])(dh1, x, dx2, mod, g1)


def ada_fwd_local(c_all, w_ada, b_cols):
    B, D = c_all.shape
    Na = w_ada.shape[1]
    tn = _div(Na, 512)

    def body(c_ref, w_ref, b_ref, o_ref):
        cv = c_ref[...]
        act = (cv * _sig(cv)).astype(BF16)
        o_ref[...] = jnp.dot(act, w_ref[...].astype(BF16), preferred_element_type=F32) + b_ref[...]

    return _row_call(
        "ada_fwd_local", body, (Na // tn,),
        [_full(c_all.shape), pl.BlockSpec((D, tn), lambda j: (0, j)), pl.BlockSpec((1, tn), lambda j: (0, j))],
        pl.BlockSpec((B, tn), lambda j: (0, j)), jax.ShapeDtypeStruct((B, Na), F32), sem=("parallel",))(c_all, w_ada, b_cols)


def ada_bwd_local(c_all_t, dmod_all):
    D, B = c_all_t.shape
    Na = dmod_all.shape[1]
    tr = _div(D, 512, SUBLANE)

    def body(c_ref, d_ref, o_ref):
        cv = c_ref[...]
        act = cv * _sig(cv)
        acc = act[:, 0:1] * d_ref[0:1, :]
        for b in range(1, B):
            acc = acc + act[:, b:b + 1] * d_ref[b:b + 1, :]
        o_ref[...] = acc

    return _row_call(
        "ada_bwd_local", body, (D // tr,), [pl.BlockSpec((tr, B), lambda i: (i, 0)), _full(dmod_all.shape)],
        pl.BlockSpec((tr, Na), lambda i: (i, 0)), jax.ShapeDtypeStruct((D, Na), F32), sem=("parallel",))(c_all_t, dmod_all)


def _adam_update(w, m, v, g):
    mn = ADAM_B1 * m + (1.0 - ADAM_B1) * g
    vn = ADAM_B2 * v + (1.0 - ADAM_B2) * (g * g)
    bc1, bc2 = 1.0 - ADAM_B1 ** ADAM_STEP, 1.0 - ADAM_B2 ** ADAM_STEP
    return g, -ADAM_LR * ((mn / bc1) / (jnp.sqrt(vn / bc2) + ADAM_EPS) + ADAM_WD * w), mn, vn


def adamw(name, w, m, v, parts):
    R, C = w.shape
    tr = _div(R, max(SUBLANE, (1 << 18) // C // SUBLANE * SUBLANE), SUBLANE)
    n = len(parts)

    def body(*refs):
        w_ref, m_ref, v_ref = refs[:3]
        g_ref, d_ref, nm_ref, nv_ref = refs[3 + n:]
        g = refs[3][...].astype(F32)
        for p in refs[4:3 + n]:
            g = g + p[...].astype(F32)
        g_ref[...], d_ref[...], nm_ref[...], nv_ref[...] = _adam_update(w_ref[...], m_ref[...], v_ref[...], g)

    row = pl.BlockSpec((tr, C), lambda i: (i, 0))
    pspecs = [row if lead is None else pl.BlockSpec((None, tr, C), lambda i, lead=lead: (lead, i, 0)) for _, lead in parts]
    out = jax.ShapeDtypeStruct((R, C), F32)
    return _row_call(name, body, (R // tr,), [row, row, row] + pspecs, [row] * 4, [out] * 4, sem=("parallel",))(
        w, m, v, *[a for a, _ in parts])


def pair_add(name, g, r, idx):
    _, R, C = g.shape
    tr = _div(R, max(SUBLANE, (1 << 17) // C // SUBLANE * SUBLANE), SUBLANE)

    def body(idx_ref, g0, g1, g2, g3, r_ref, own_ref, tr_ref):
        del idx_ref
        own_ref[...] = g0[...].astype(F32) + r_ref[0].astype(F32)
        for m, gm in ((1, g1), (2, g2), (3, g3)):
            tr_ref[m - 1] = (gm[...].astype(F32) + r_ref[m].astype(F32)).astype(BF16)

    def gspec(m):
        return pl.BlockSpec((None, tr, C), lambda i, idx_ref: (idx_ref[m], i, 0))

    return pl.pallas_call(
        body, name=name,
        grid_spec=pltpu.PrefetchScalarGridSpec(
            num_scalar_prefetch=1, grid=(R // tr,),
            in_specs=[gspec(0), gspec(1), gspec(2), gspec(3), pl.BlockSpec((4, tr, C), lambda i, idx_ref: (0, i, 0))],
            out_specs=[pl.BlockSpec((tr, C), lambda i, idx_ref: (i, 0)),
                       pl.BlockSpec((3, tr, C), lambda i, idx_ref: (0, i, 0))]),
        out_shape=[jax.ShapeDtypeStruct((R, C), F32), jax.ShapeDtypeStruct((3, R, C), BF16)],
        compiler_params=_cp(("parallel",)))(idx, g, g, g, g, r)


def adamw_small(me, rows, g_rows, wsp, g_wsp, cwp, g_cw, fwp, g_fw):
    sizes = [w.shape[1] for w, _, _ in rows]
    offs = [sum(sizes[:i]) for i in range(len(rows))]
    k_cw, n_cw = cwp[0].shape
    n_fw = fwp[0].shape[1]
    per_half = g_fw.shape[-1] // n_fw
    groups = list(rows) + [wsp, cwp, fwp]
    n_p, n_r = len(groups), len(rows)

    def body(me_ref, g_rows_ref, g_wsp_ref, g_cw_ref, g_fw_ref, *refs):
        del me_ref
        wmv, outs = refs[:3 * n_p], refs[3 * n_p:]
        for p in range(n_p):
            if p < n_r:
                pick = lambda d, p=p: g_rows_ref[d, :, offs[p]:offs[p] + sizes[p]]
            elif p == n_r:
                pick = lambda d: g_wsp_ref[d]
            elif p == n_r + 1:
                pick = lambda d: g_cw_ref[d, 0:k_cw, :]
            else:
                pick = lambda d: g_fw_ref[d]
            g = pick(0)
            for d in range(1, NDEV):
                g = g + pick(d)
            res = _adam_update(wmv[3 * p][...], wmv[3 * p + 1][...], wmv[3 * p + 2][...], g)
            for slot in range(4):
                outs[4 * p + slot][...] = res[slot]

    def full(a):
        nd = len(a.shape)
        return pl.BlockSpec(a.shape, lambda i, me_ref: (0,) * nd)

    in_specs = [full(g_rows), full(g_wsp),
                pl.BlockSpec((NDEV, g_cw.shape[1], n_cw), lambda i, me_ref: (0, 0, me_ref[0])),
                pl.BlockSpec((NDEV, None, 3, n_fw), lambda i, me_ref: (0, me_ref[0] // per_half, 0, me_ref[0] % per_half))]
    in_specs += [full(a) for grp in groups for a in grp]
    out_shape = [jax.ShapeDtypeStruct(grp[0].shape, F32) for grp in groups for _ in range(4)]
    flat = pl.pallas_call(
        body, name="adamw_small",
        grid_spec=pltpu.PrefetchScalarGridSpec(num_scalar_prefetch=1, grid=(1,), in_specs=in_specs,
                                               out_specs=[full(s) for s in out_shape]),
        out_shape=out_shape, compiler_params=_cp(("arbitrary",)))(
            me, g_rows, g_wsp, g_cw, g_fw, *[a for grp in groups for a in grp])
    return [tuple(flat[4 * p:4 * p + 4]) for p in range(n_p)]


def _coords():
    return lax.axis_index("x"), lax.axis_index("y"), lax.axis_index("c")


def _flip(v, bit):
    return 1 - v if bit else v


def _comm_call(name, body, ins, out_shapes, n_sems):
    any_spec = pl.BlockSpec(memory_space=pl.ANY)
    return pl.pallas_call(
        body, name=name, in_specs=[any_spec] * len(ins), out_specs=[any_spec] * len(out_shapes), out_shape=out_shapes,
        scratch_shapes=[pltpu.SemaphoreType.DMA((s,)) for s in n_sems],
        compiler_params=pltpu.CompilerParams(has_side_effects=True))(*ins)


def gather_all(name, tensors):
    L = len(tensors)

    def body(*refs):
        ins, outs = refs[:L], refs[L:2 * L]
        send_sems, recv_sems, local_sems = refs[2 * L:]
        x, y, c = _coords()
        me = 4 * x + 2 * y + c
        local = [pltpu.make_async_copy(ins[l], outs[l].at[me], local_sems.at[l]) for l in range(L)]
        copies = []
        for l in range(L):
            for k in range(1, NDEV):
                peer = (_flip(x, k & 4), _flip(y, k & 2), _flip(c, k & 1))
                copies.append(pltpu.make_async_remote_copy(
                    src_ref=ins[l], dst_ref=outs[l].at[me], send_sem=send_sems.at[7 * l + k - 1],
                    recv_sem=recv_sems.at[7 * l + k - 1], device_id=peer, device_id_type=MESH))
        for cp in local + copies:
            cp.start()
        for cp in copies:
            cp.wait_recv()
        for cp in copies:
            cp.wait_send()
        for cp in local:
            cp.wait()

    outs = [jax.ShapeDtypeStruct((NDEV,) + t.shape, t.dtype) for t in tensors]
    return _comm_call(name, body, list(tensors), outs, (7 * L, 7 * L, L))


def exchange_rows(name, slabs):
    def body(in_ref, out_ref, send_sems, recv_sems, local_sem):
        x, y, c = _coords()
        me = 4 * x + 2 * y + c
        mine = pltpu.make_async_copy(in_ref.at[me], out_ref.at[me], local_sem.at[0])
        mine.start()
        copies = []
        for k in range(1, NDEV):
            px, py, pc = _flip(x, k & 4), _flip(y, k & 2), _flip(c, k & 1)
            copies.append(pltpu.make_async_remote_copy(
                src_ref=in_ref.at[4 * px + 2 * py + pc], dst_ref=out_ref.at[me], send_sem=send_sems.at[k - 1],
                recv_sem=recv_sems.at[k - 1], device_id=(px, py, pc), device_id_type=MESH))
        for cp in copies:
            cp.start()
        for cp in copies:
            cp.wait_recv()
        for cp in copies:
            cp.wait_send()
        mine.wait()

    return _comm_call(name, body, [slabs], [jax.ShapeDtypeStruct(slabs.shape, slabs.dtype)], (NDEV - 1, NDEV - 1, 1))[0]


def _run_comm(name, comm):
    n_i, n_o = len(comm.ins), len(comm.out_shapes)

    def body(*refs):
        start, finish = comm.plan(refs[:n_i], refs[n_i:n_i + n_o], refs[n_i + n_o:])
        start()
        finish()

    return _comm_call(name, body, comm.ins, comm.out_shapes, comm.n_sems)


def gather_comm(shards):
    L = len(shards)

    def plan(ins, outs, sems):
        send_sems, recv_sems, local_sems = sems
        x, y, c = _coords()
        sibling = (x, y, 1 - c)
        chips = [(_flip(x, m & 2), _flip(y, m & 1)) for m in (1, 2, 3)]

        def slab(px, py, pc):
            return 4 * px + 2 * py + pc

        def copy(l, k, block, to, src=None):
            dst = outs[l].at[slab(*block)]
            return pltpu.make_async_remote_copy(
                src_ref=dst if src is None else src, dst_ref=dst, send_sem=send_sems.at[7 * l + k],
                recv_sem=recv_sems.at[7 * l + k], device_id=to, device_id_type=MESH)

        local = [pltpu.make_async_copy(ins[l], outs[l].at[slab(x, y, c)], local_sems.at[l]) for l in range(L)]
        first = []
        for l in range(L):
            first.append(copy(l, 0, (x, y, c), sibling, src=ins[l]))
            first += [copy(l, 1 + j, (x, y, c), (*chip, c), src=ins[l]) for j, chip in enumerate(chips)]

        def start():
            for cp in local + first:
                cp.start()

        def finish():
            passed = []
            for j, chip in enumerate(chips):
                for l in range(L):
                    copy(l, 1 + j, (*chip, c), (x, y, c)).wait_recv()
                    fwd = copy(l, 4 + j, (*chip, c), sibling)
                    fwd.start()
                    passed.append(fwd)
            for l in range(L):
                copy(l, 0, sibling, (x, y, c)).wait_recv()
                for j, chip in enumerate(chips):
                    copy(l, 4 + j, (*chip, 1 - c), (x, y, c)).wait_recv()
            for cp in first + passed:
                cp.wait_send()
            for cp in local:
                cp.wait()

        return start, finish

    outs = [jax.ShapeDtypeStruct((NDEV,) + s.shape, s.dtype) for s in shards]
    return Comm(plan, shards, outs, (7 * L, 7 * L, L))


def _all_at_once(copies):
    def start():
        for cp in copies:
            cp.start()

    def finish():
        for cp in copies:
            cp.wait_recv()
        for cp in copies:
            cp.wait_send()

    return start, finish


def sibling_comm(grads):
    L = len(grads)

    def plan(ins, outs, sems):
        send_sems, recv_sems = sems
        x, y, c = _coords()
        copies = []
        for l in range(L):
            for m in range(4):
                qm = 2 * _flip(x, m & 2) + _flip(y, m & 1)
                copies.append(pltpu.make_async_remote_copy(
                    src_ref=ins[l].at[2 * qm + (1 - c)], dst_ref=outs[l].at[m], send_sem=send_sems.at[4 * l + m],
                    recv_sem=recv_sems.at[4 * l + m], device_id=(x, y, 1 - c), device_id_type=MESH))
        return _all_at_once(copies)

    outs = [jax.ShapeDtypeStruct((4,) + g.shape[1:], g.dtype) for g in grads]
    return Comm(plan, grads, outs, (4 * L, 4 * L))


def chip_comm(transits):
    L = len(transits)

    def plan(ins, outs, sems):
        send_sems, recv_sems = sems
        x, y, c = _coords()
        copies = []
        for l in range(L):
            for m in (1, 2, 3):
                copies.append(pltpu.make_async_remote_copy(
                    src_ref=ins[l].at[m - 1], dst_ref=outs[l].at[m - 1], send_sem=send_sems.at[3 * l + m - 1],
                    recv_sem=recv_sems.at[3 * l + m - 1], device_id=(_flip(x, m & 2), _flip(y, m & 1), c),
                    device_id_type=MESH))
        return _all_at_once(copies)

    outs = [jax.ShapeDtypeStruct(t.shape, t.dtype) for t in transits]
    return Comm(plan, transits, outs, (3 * L, 3 * L))


_SMALL_ROWS = ("b_ada", "norm1_g", "b_in", "conv_dw_b", "conv_ln_g", "conv_ln_b", "sgu_ln_g", "sgu_ln_b", "b_spatial",
               "norm2_g", "ffn_dw_b", "final_g")
_BIG = ("w_in", "w_conv_out", "w_sgu_out", "w_out", "w_up", "w_down")
_ORDER = ("w_ada", "b_ada", "norm1_g", "w_in", "b_in", "conv_dw_w", "conv_dw_b", "conv_ln_g", "conv_ln_b", "w_conv_out",
          "sgu_ln_g", "sgu_ln_b", "w_spatial", "b_spatial", "w_sgu_out", "w_out", "norm2_g", "w_up", "ffn_dw_w", "ffn_dw_b",
          "w_down", "final_g")


def kernel(x, c, w_ada, b_ada, norm1_g, w_in, b_in, conv_dw_w, conv_dw_b, conv_ln_g, conv_ln_b, w_conv_out, sgu_ln_g, sgu_ln_b, w_spatial, b_spatial, w_sgu_out, w_out, norm2_g, w_up, ffn_dw_w, ffn_dw_b, w_down, final_g, loss_target, m_w_ada, m_b_ada, m_norm1_g, m_w_in, m_b_in, m_conv_dw_w, m_conv_dw_b, m_conv_ln_g, m_conv_ln_b, m_w_conv_out, m_sgu_ln_g, m_sgu_ln_b, m_w_spatial, m_b_spatial, m_w_sgu_out, m_w_out, m_norm2_g, m_w_up, m_ffn_dw_w, m_ffn_dw_b, m_w_down, m_final_g, v_w_ada, v_b_ada, v_norm1_g, v_w_in, v_b_in, v_conv_dw_w, v_conv_dw_b, v_conv_ln_g, v_conv_ln_b, v_w_conv_out, v_sgu_ln_g, v_sgu_ln_b, v_w_spatial, v_b_spatial, v_w_sgu_out, v_w_out, v_norm2_g, v_w_up, v_ffn_dw_w, v_ffn_dw_b, v_w_down, v_final_g):
    W = dict(w_ada=w_ada, b_ada=b_ada, norm1_g=norm1_g, w_in=w_in, b_in=b_in, conv_dw_w=conv_dw_w, conv_dw_b=conv_dw_b,
             conv_ln_g=conv_ln_g, conv_ln_b=conv_ln_b, w_conv_out=w_conv_out, sgu_ln_g=sgu_ln_g, sgu_ln_b=sgu_ln_b,
             w_spatial=w_spatial, b_spatial=b_spatial, w_sgu_out=w_sgu_out, w_out=w_out, norm2_g=norm2_g, w_up=w_up,
             ffn_dw_w=ffn_dw_w, ffn_dw_b=ffn_dw_b, w_down=w_down, final_g=final_g)
    M = dict(w_ada=m_w_ada, b_ada=m_b_ada, norm1_g=m_norm1_g, w_in=m_w_in, b_in=m_b_in, conv_dw_w=m_conv_dw_w,
             conv_dw_b=m_conv_dw_b, conv_ln_g=m_conv_ln_g, conv_ln_b=m_conv_ln_b, w_conv_out=m_w_conv_out,
             sgu_ln_g=m_sgu_ln_g, sgu_ln_b=m_sgu_ln_b, w_spatial=m_w_spatial, b_spatial=m_b_spatial,
             w_sgu_out=m_w_sgu_out, w_out=m_w_out, norm2_g=m_norm2_g, w_up=m_w_up, ffn_dw_w=m_ffn_dw_w,
             ffn_dw_b=m_ffn_dw_b, w_down=m_w_down, final_g=m_final_g)
    V = dict(w_ada=v_w_ada, b_ada=v_b_ada, norm1_g=v_norm1_g, w_in=v_w_in, b_in=v_b_in, conv_dw_w=v_conv_dw_w,
             conv_dw_b=v_conv_dw_b, conv_ln_g=v_conv_ln_g, conv_ln_b=v_conv_ln_b, w_conv_out=v_w_conv_out,
             sgu_ln_g=v_sgu_ln_g, sgu_ln_b=v_sgu_ln_b, w_spatial=v_w_spatial, b_spatial=v_b_spatial,
             w_sgu_out=v_w_sgu_out, w_out=v_w_out, norm2_g=v_norm2_g, w_up=v_w_up, ffn_dw_w=v_ffn_dw_w,
             ffn_dw_b=v_ffn_dw_b, w_down=v_w_down, final_g=v_final_g)

    xs, tgt = x[0], loss_target[0]
    T, D = xs.shape
    Dc = conv_dw_w.shape[-1] * NDEV
    F = w_down.shape[1] * NDEV
    K31 = conv_dw_w.shape[1]
    G = w_spatial.shape[1]
    assert D == 2 * Dc and sgu_ln_g.shape[-1] == Dc and T % CHUNK == 0
    me = 4 * lax.axis_index("x") + 2 * lax.axis_index("y") + lax.axis_index("c")

    na = w_ada.shape[-1]
    c_all = gather_all("gather_c", [c])[0].reshape(NDEV, D)
    b_cols = lax.dynamic_slice(b_ada, (0, me * na), (1, na))
    mod_cols = ada_fwd_local(c_all, w_ada[0], b_cols)
    mod = exchange_rows("exchange_mod", mod_cols.reshape(NDEV, 1, na)).reshape(1, NDEV * na)

    wbf = {k: W[k][0].astype(BF16) for k in _BIG}
    fb = ffn_dw_b.reshape(2, 1, F)
    bsp_t = jnp.transpose(b_spatial[0])
    wsp = w_spatial[0]

    def plain(wb):
        return jnp.transpose(wb, (1, 0, 2)).reshape(1, wb.shape[1], NDEV * wb.shape[2])

    h1, (wb_in,) = pre_norm("pre_norm1", xs, mod, norm1_g, 0, comm=gather_comm([wbf["w_in"]]))
    proj, (wb_up, cw_g, fw_g) = mm_nn(
        "proj", h1, wb_in, F32, bias=b_in, comm=gather_comm([wbf["w_up"], conv_dw_w[0], ffn_dw_w[0]]))
    cw = jnp.transpose(cw_g, (1, 0, 2)).reshape(K31, Dc)
    fw = jnp.transpose(jnp.transpose(fw_g, (1, 0, 2)).reshape(3, 2, F), (1, 0, 2))
    wp_in, wp_up = to_plain("plain_w_in", wb_in), to_plain("plain_w_up", wb_up)
    (ac, asw), (wb_out, wb_co, wb_so) = mix_a_fwd(
        proj, cw, conv_dw_b, conv_ln_g, conv_ln_b,
        comm=gather_comm([wbf["w_out"], wbf["w_conv_out"], wbf["w_sgu_out"]]))
    wb_out = wb_out.reshape(1, D, D)
    wp_co, wp_so = plain(wb_co), plain(wb_so)
    uv = mix_b_fwd(proj, sgu_ln_g, sgu_ln_b, wsp, bsp_t)
    y_a = mm_nn("y_a", asw, wp_co, ACT)
    y_b = mm_nn("y_b", uv, wp_so, ACT)
    merged = merge_fwd(proj, y_a, y_b)
    o1 = mm_nn("o1", merged, wb_out, F32)
    h2 = pre_norm("pre_norm2", xs, mod, norm2_g, 1, o1=o1)
    upre, (wb_down,) = mm_nn("upre", h2, wb_up, F32, out_halves=True, tn_pref=MM_WIDE,
                             comm=gather_comm([wbf["w_down"]]))
    wb_down = wb_down.reshape(1, F, D)
    f, up = ffn_act_fwd(upre, fw, fb)
    o2 = mm_nn("o2", f, wb_down, F32)
    dx3, do2, st_f = final_fwd_bwd(xs, o1, o2, mod, final_g.reshape(1, D), tgt)
    loss = lax.psum(st_f[3, 0], MESH_AXES)

    xq, yq, cq = lax.axis_index("x"), lax.axis_index("y"), lax.axis_index("c")
    idx = jnp.stack([2 * (2 * _flip(xq, m & 2) + _flip(yq, m & 1)) + cq for m in range(4)]).astype(jnp.int32)
    own, transit, arrived = {}, {}, {}

    def add_pairs(keys, full, from_sibling):
        for k, g_full, r in zip(keys, full, from_sibling):
            own[k], transit[k] = pair_add("pair_add_" + k, g_full, r, idx)

    df = mm_nt("df", do2, wb_down, ACT, tko_pref=MM_WIDE)
    g_down = mm_tn("g_down", f, do2, 1, tko_pref=MM_WIDE).reshape(NDEV, F // NDEV, D)
    dupre, g_fw, g_fb = ffn_act_bwd(upre, up, df, fw)
    dh2 = mm_nt("dh2", dupre, wp_up, ACT, a_halves=True)
    g_up = mm_tn("g_up", h2, dupre, NDEV, g_halves=True, tn_pref=MM_WIDE)
    dx2, do1, st_2 = norm2_bwd(dh2, xs, o1, dx3, mod, norm2_g)
    dmerged, sib = mm_nt("dmerged", do1, wb_out, ACT, comm=sibling_comm([g_down, g_up]))
    add_pairs(("w_down", "w_up"), (g_down, g_up), sib)
    g_out = mm_tn("g_out", merged, do1, 1).reshape(NDEV, D // NDEV, D)
    dy_a, dy_b, dproj, db_g = merge_bwd(dmerged, proj, y_a, y_b)
    def blocked(g):
        return jnp.transpose(g.reshape(g.shape[1], NDEV, g.shape[2] // NDEV), (1, 0, 2))

    dasw = mm_nt("dasw", dy_a, wp_co, ACT)
    g_co = blocked(mm_tn("g_co", asw, dy_a, 1))
    duv = mm_nt("duv", dy_b, wp_so, ACT)
    g_so = blocked(mm_tn("g_so", uv, dy_b, 1))
    (dproj, st_a, g_cw, db_a), (arrived["w_up"],) = mix_a_bwd(
        dasw, ac, proj, dproj, cw, conv_ln_g, conv_ln_b, comm=chip_comm([transit["w_up"]]))
    (dproj, st_b, g_wsp, g_bsp_t, db_s), sib = mix_b_bwd(
        duv, proj, dproj, sgu_ln_g, sgu_ln_b, wsp, bsp_t, comm=sibling_comm([g_out, g_co, g_so]))
    add_pairs(("w_out", "w_conv_out", "w_sgu_out"), (g_out, g_co, g_so), sib)
    g_in, (arrived["w_down"],) = mm_tn("g_in", h1, dproj, NDEV, comm=chip_comm([transit["w_down"]]))
    add_pairs(("w_in",), (g_in,), _run_comm("sibling_w_in", sibling_comm([g_in])))
    late = ("w_out", "w_conv_out", "w_sgu_out", "w_in")
    dh1, got = mm_nt("dh1", dproj, wp_in, ACT, comm=chip_comm([transit[k] for k in late]))
    arrived.update(zip(late, got))
    grad_x, st_1 = norm1_bwd(dh1, xs, dx2, mod, norm1_g)

    dmod = jnp.concatenate([st_1[0], st_1[1], st_2[3], st_2[0], st_2[1], st_f[1]]).reshape(1, NDEV * na)
    dmod_all = exchange_rows("exchange_dmod", dmod.reshape(NDEV, 1, na)).reshape(NDEV, na)
    g_ada = ada_bwd_local(jnp.transpose(c_all), dmod_all)

    res = {}
    for k in _BIG:
        q = arrived[k]
        res[k] = adamw("adamw_" + k, W[k][0], M[k][0], V[k][0], [(own[k], None), (q, 0), (q, 1), (q, 2)])
    res["w_ada"] = adamw("adamw_w_ada", w_ada[0], m_w_ada[0], v_w_ada[0], [(g_ada, None)])

    g_row = dict(
        b_ada=dmod, norm1_g=st_1[2], b_in=jnp.concatenate([db_a, db_s, db_g], axis=1), conv_dw_b=st_a[2],
        conv_ln_g=st_a[0], conv_ln_b=st_a[1], sgu_ln_g=st_b[0], sgu_ln_b=st_b[1], b_spatial=jnp.transpose(g_bsp_t),
        norm2_g=st_2[2], ffn_dw_b=g_fb, final_g=st_f[0])
    packed = jnp.concatenate([g_row[k].reshape(1, -1) for k in _SMALL_ROWS], axis=1)
    g_rows, g_wsp_all, g_cw_all, g_fw_all = gather_all("gather_small", [packed, g_wsp.reshape(-1, CHUNK), g_cw, g_fw])
    rows = [tuple(S[k].reshape(1, -1) for S in (W, M, V)) for k in _SMALL_ROWS]
    wsp3 = tuple(S["w_spatial"].reshape(-1, CHUNK) for S in (W, M, V))
    cw3, fw3 = (tuple(S[k][0] for S in (W, M, V)) for k in ("conv_dw_w", "ffn_dw_w"))
    small = adamw_small(jnp.reshape(me, (1,)).astype(jnp.int32), rows, g_rows, wsp3, g_wsp_all, cw3, g_cw_all, fw3, g_fw_all)
    res.update(zip(_SMALL_ROWS + ("w_spatial", "conv_dw_w", "ffn_dw_w"), small))

    outs = [[], [], [], []]
    for k in _ORDER:
        for slot in range(4):
            outs[slot].append(res[k][slot].reshape(W[k].shape))
    return (loss, grad_x[None], *outs[0], *outs[1], *outs[2], *outs[3])
```
